```python
import math
import jax, jax.numpy as jnp
from jax import lax
import numpy as np

D_MODEL = 1024
BATCH = 2
SEQ = 8192
DEPTH = 1
DEC_BATCH = 16
DEC_SEQ = 16
PAST_LEN = 2048

CHUNK = 64
HA = 8
DH = 64
H_IDX = 8
D_IDX = 64
TOPK_MAX = 256
Q_BLOCK = 128
NUM_BUCKETS = 32
MAX_DISTANCE = 128
HB = 4
DK = 128
DV = 128
CONV_B = 4
D_FF = 2816
CONV_F = 3
D_PLE = 256
EPS = 1e-6
NEG = -1e30

WA = HA * DH
WB_K = HB * DK
WB_V = HB * DV
C_CONV_B = 2 * WB_K + WB_V
SPLITS = [WA, WA, WA, H_IDX * D_IDX, D_IDX, H_IDX, WB_K, WB_K, WB_V, WB_V, HB, HB, D_MODEL, D_MODEL]
D_IN = sum(SPLITS)

kernel_name = 'hybrid_dsa_gdn_stream'


def _split_offsets():
    offs, acc = [], 0
    for s in SPLITS[:-1]:
        acc += s
        offs.append(acc)
    return offs


def _rmsnorm(x, g):
    xf = x.astype(jnp.float32)
    y = xf * lax.rsqrt(jnp.mean(xf * xf, axis=-1, keepdims=True) + EPS)
    return (y * g.astype(jnp.float32)).astype(x.dtype)


def _l2norm(x):
    return x * lax.rsqrt(jnp.sum(x * x, axis=-1, keepdims=True) + EPS)


def _causal_dwconv(ext, w):
    width = w.shape[0]
    T = ext.shape[1] - width + 1
    out = ext[:, 0:T] * w[0]
    for j in range(1, width):
        out = out + ext[:, j:j + T] * w[j]
    return out


def _t5_bucket(rel):
    half = NUM_BUCKETS // 2
    max_exact = half // 2
    ret = jnp.where(rel > 0, half, 0)
    n = jnp.abs(rel)
    nf = jnp.maximum(n, 1).astype(jnp.float32)
    large = max_exact + (jnp.log(nf / max_exact) / math.log(MAX_DISTANCE / max_exact)
                         * (half - max_exact)).astype(jnp.int32)
    large = jnp.minimum(large, half - 1)
    return ret + jnp.where(n < max_exact, n, large)


def _dsa_block(q, qi, wi, q_pos, k, v, ki, topk, rel_bias):
    f32 = jnp.float32
    L = k.shape[1]
    k_pos = jnp.arange(L, dtype=jnp.int32)
    s = jnp.einsum('bqhd,bsd->bqhs', qi.astype(f32), ki.astype(f32)) * (D_IDX ** -0.5)
    score = jnp.einsum('bqhs,bqh->bqs', jax.nn.relu(s), wi.astype(f32))
    adm = (k_pos[None, :] // CHUNK) <= (q_pos[:, None] // CHUNK)
    score = jnp.where(adm[None], score, NEG)
    _, idx = lax.top_k(score, topk)
    k_sel = jax.vmap(lambda kb, ib: kb[ib])(k, idx)
    v_sel = jax.vmap(lambda vb, ib: vb[ib])(v, idx)
    valid = (idx // CHUNK) <= (q_pos[None, :, None] // CHUNK)
    bias = jnp.moveaxis(rel_bias[_t5_bucket(idx - q_pos[None, :, None])].astype(f32), -1, 2)
    logits = jnp.einsum('bqhd,bqkhd->bqhk', q.astype(f32), k_sel.astype(f32)) * (DH ** -0.5) + bias
    logits = jnp.where(valid[:, :, None, :], logits, NEG)
    prob = jax.nn.softmax(logits, axis=-1)
    out = jnp.einsum('bqhk,bqkhd->bqhd', prob, v_sel.astype(f32))
    return out.astype(q.dtype)


def _gated_delta_chunked(q, k, v, g, beta, s0):
    B, T = q.shape[0], q.shape[1]
    n = -(-T // CHUNK)
    pad = n * CHUNK - T

    def prep(a):
        a = jnp.pad(a, [(0, 0), (0, pad)] + [(0, 0)] * (a.ndim - 2))
        a = a.reshape((B, n, CHUNK) + a.shape[2:])
        return jnp.moveaxis(a, (1, 3), (0, 2))

    qc, kc, vc, gc, bc = prep(q), prep(k), prep(v), prep(g), prep(beta)
    gcum = jnp.cumsum(gc, axis=-1)
    tri = jnp.tril(jnp.ones((CHUNK, CHUNK), dtype=bool))
    strict = jnp.tril(jnp.ones((CHUNK, CHUNK), dtype=bool), -1)
    diff = gcum[..., :, None] - gcum[..., None, :]
    decay = jnp.where(tri, jnp.exp(jnp.where(tri, diff, 0.0)), 0.0)
    kb = kc * bc[..., None]
    a_mat = jnp.where(strict, jnp.einsum('...id,...jd->...ij', kb, kc) * decay, 0.0) + jnp.eye(CHUNK, dtype=jnp.float32)
    rhs = jnp.concatenate([vc * bc[..., None], kb * jnp.exp(gcum)[..., None]], axis=-1)
    sol = lax.linalg.triangular_solve(a_mat, rhs, left_side=True, lower=True, unit_diagonal=True)
    u, w = sol[..., :DV], sol[..., DV:]
    attn_intra = jnp.where(tri, jnp.einsum('...id,...jd->...ij', qc, kc) * decay, 0.0)

    def step(S, xs):
        qi, ki, ui, wi, gi, ai = xs
        v_new = ui - jnp.einsum('bhcd,bhde->bhce', wi, S)
        o = jnp.einsum('bhcd,bhde->bhce', qi * jnp.exp(gi)[..., None], S) + jnp.einsum('bhij,bhje->bhie', ai, v_new)
        g_last = gi[..., -1:]
        S = S * jnp.exp(g_last)[..., None] + jnp.einsum('bhcd,bhce->bhde', ki * jnp.exp(g_last - gi)[..., None], v_new)
        return S, o

    s_fin, o = lax.scan(step, s0, (qc, kc, u, w, gcum, attn_intra))
    o = jnp.moveaxis(o, (0, 2), (1, 3)).reshape(B, n * CHUNK, HB, DV)[:, :T]
    return o, s_fin


def _layer(x, p, past_k, past_v, past_kidx, s_gdn, conv_b_buf, ffn_buf,
           norm_mix, w_in, conv_b, a_log, dt_bias, norm_gdn, w_proj_a, w_proj_b, w_out,
           norm_ffn, w_up, conv_ffn, w_down, norm_ple, w_ple, w_ple_gate, rel_bias):
    f32 = jnp.float32
    B, T, _ = x.shape
    past = past_k.shape[1]
    L = past + T
    h = _rmsnorm(x, norm_mix)
    qa, ka, va, qi, ki, wi, qb, kb, vb, gb, bb, ab, ga, gbr = jnp.split(h @ w_in, _split_offsets(), axis=-1)

    qa = qa.reshape(B, T, HA, DH)
    ka = ka.reshape(B, T, HA, DH)
    va = va.reshape(B, T, HA, DH)
    qi = qi.reshape(B, T, H_IDX, D_IDX)
    wi = wi * (H_IDX ** -0.5)
    k_all = jnp.concatenate([past_k.astype(ka.dtype), ka], axis=1)
    v_all = jnp.concatenate([past_v.astype(va.dtype), va], axis=1)
    ki_all = jnp.concatenate([past_kidx.astype(ki.dtype), ki], axis=1)
    q_pos = past + jnp.arange(T, dtype=jnp.int32)
    topk = min(TOPK_MAX, L // 4)
    if T > Q_BLOCK and T % Q_BLOCK == 0:
        nb = T // Q_BLOCK

        def to_blocks(a):
            return jnp.moveaxis(a.reshape((B, nb, Q_BLOCK) + a.shape[2:]), 1, 0)

        def blk(args):
            qb_, qib_, wib_, posb_ = args
            return _dsa_block(qb_, qib_, wib_, posb_, k_all, v_all, ki_all, topk, rel_bias)

        oa = lax.map(blk, (to_blocks(qa), to_blocks(qi), to_blocks(wi), q_pos.reshape(nb, Q_BLOCK)))
        oa = jnp.moveaxis(oa, 0, 1).reshape(B, T, HA, DH)
    else:
        oa = _dsa_block(qa, qi, wi, q_pos, k_all, v_all, ki_all, topk, rel_bias)
    ya = oa.reshape(B, T, WA) @ w_proj_a

    conv_in = jnp.concatenate([qb, kb, vb], axis=-1)
    ext = jnp.concatenate([conv_b_buf.astype(conv_in.dtype), conv_in], axis=1)
    new_conv_b = ext[:, T:]
    cb = jax.nn.silu(_causal_dwconv(ext, conv_b))
    qb_c, kb_c, vb_c = jnp.split(cb, [WB_K, 2 * WB_K], axis=-1)
    qh = _l2norm(qb_c.reshape(B, T, HB, DK).astype(f32)) * (DK ** -0.5)
    kh = _l2norm(kb_c.reshape(B, T, HB, DK).astype(f32))
    vh = vb_c.reshape(B, T, HB, DV).astype(f32)
    beta = jax.nn.sigmoid(bb.astype(f32))
    g = -jnp.exp(a_log.astype(f32)) * jax.nn.softplus(ab.astype(f32) + dt_bias.astype(f32))
    ob, s_new = _gated_delta_chunked(qh, kh, vh, g, beta, s_gdn.astype(f32))
    ob = _rmsnorm(ob, norm_gdn) * jax.nn.silu(gb.reshape(B, T, HB, DV).astype(f32))
    yb = ob.reshape(B, T, WB_V).astype(x.dtype) @ w_proj_b

    mix = jax.nn.sigmoid(ga) * ya + jax.nn.sigmoid(gbr) * yb
    x = x + mix @ w_out

    h2 = _rmsnorm(x, norm_ffn)
    u_gate, u_val = jnp.split(h2 @ w_up, [D_FF], axis=-1)
    ext_f = jnp.concatenate([ffn_buf.astype(u_gate.dtype), u_gate], axis=1)
    new_ffn = ext_f[:, T:]
    act = jax.nn.gelu(_causal_dwconv(ext_f, conv_ffn), approximate=True)
    x = x + (act * u_val) @ w_down

    gate_p = jax.nn.sigmoid(_rmsnorm(x, norm_ple) @ w_ple_gate)
    x = x + gate_p * (p @ w_ple)
    return x, ka, va, ki, s_new.astype(x.dtype), new_conv_b, new_ffn


def setup_inputs(seed: int = 0) -> dict:
    key = jax.random.key(seed)
    ks = jax.random.split(key, 32)
    nrm = jax.random.normal
    f32 = jnp.float32
    inp = {}
    inp['x_prompt'] = nrm(ks[0], (BATCH, SEQ, D_MODEL), f32)
    inp['x_sample'] = nrm(ks[1], (DEC_BATCH, DEC_SEQ, D_MODEL), f32)
    inp['p_prompt'] = nrm(ks[2], (DEPTH, BATCH, SEQ, D_PLE), f32)
    inp['p_sample'] = nrm(ks[3], (DEPTH, DEC_BATCH, DEC_SEQ, D_PLE), f32)
    inp['cache_k'] = nrm(ks[4], (DEPTH, DEC_BATCH, PAST_LEN, HA, DH), f32)
    inp['cache_v'] = nrm(ks[5], (DEPTH, DEC_BATCH, PAST_LEN, HA, DH), f32)
    inp['cache_kidx'] = nrm(ks[6], (DEPTH, DEC_BATCH, PAST_LEN, D_IDX), f32)
    inp['state_gdn'] = 0.1 * nrm(ks[7], (DEPTH, DEC_BATCH, HB, DK, DV), f32)
    inp['state_gdn_conv'] = nrm(ks[8], (DEPTH, DEC_BATCH, CONV_B - 1, C_CONV_B), f32)
    inp['state_ffn_conv'] = nrm(ks[9], (DEPTH, DEC_BATCH, CONV_F - 1, D_FF), f32)
    inp['norm_mix'] = 1.0 + 0.05 * nrm(ks[10], (DEPTH, D_MODEL), f32)
    inp['w_in'] = nrm(ks[11], (DEPTH, D_MODEL, D_IN), f32) * D_MODEL ** -0.5
    inp['conv_b'] = nrm(ks[12], (DEPTH, CONV_B, C_CONV_B), f32) * CONV_B ** -0.5
    inp['a_log'] = jnp.log(jax.random.uniform(ks[13], (DEPTH, HB), f32, 1.0, 16.0))
    dt = jax.random.uniform(ks[14], (DEPTH, HB), f32, 0.001, 0.1)
    inp['dt_bias'] = jnp.log(jnp.expm1(dt))
    inp['norm_gdn'] = 1.0 + 0.05 * nrm(ks[15], (DEPTH, DV), f32)
    inp['w_proj_a'] = nrm(ks[16], (DEPTH, WA, D_MODEL), f32) * WA ** -0.5
    inp['w_proj_b'] = nrm(ks[17], (DEPTH, WB_V, D_MODEL), f32) * WB_V ** -0.5
    inp['w_out'] = nrm(ks[18], (DEPTH, D_MODEL, D_MODEL), f32) * D_MODEL ** -0.5
    inp['norm_ffn'] = 1.0 + 0.05 * nrm(ks[19], (DEPTH, D_MODEL), f32)
    inp['w_up'] = nrm(ks[20], (DEPTH, D_MODEL, 2 * D_FF), f32) * D_MODEL ** -0.5
    inp['conv_ffn'] = nrm(ks[21], (DEPTH, CONV_F, D_FF), f32) * CONV_F ** -0.5
    inp['w_down'] = nrm(ks[22], (DEPTH, D_FF, D_MODEL), f32) * D_FF ** -0.5
    inp['norm_ple'] = 1.0 + 0.05 * nrm(ks[23], (DEPTH, D_MODEL), f32)
    inp['w_ple'] = nrm(ks[24], (DEPTH, D_PLE, D_MODEL), f32) * D_PLE ** -0.5
    inp['w_ple_gate'] = nrm(ks[25], (DEPTH, D_MODEL, D_MODEL), f32) * D_MODEL ** -0.5
    inp['rel_bias'] = 0.5 * nrm(ks[26], (NUM_BUCKETS, HA), f32)
    inp['norm_final'] = 1.0 + 0.05 * nrm(ks[27], (D_MODEL,), f32)
    return inp


def reference(x_prompt, x_sample, p_prompt, p_sample, cache_k, cache_v, cache_kidx, state_gdn,
              state_gdn_conv, state_ffn_conv, norm_mix, w_in, conv_b, a_log, dt_bias, norm_gdn,
              w_proj_a, w_proj_b, w_out, norm_ffn, w_up, conv_ffn, w_down, norm_ple, w_ple,
              w_ple_gate, rel_bias, norm_final):
    xp, xs = x_prompt, x_sample
    bp = xp.shape[0]
    dt = xp.dtype
    outs_p = [[] for _ in range(6)]
    outs_s = [[] for _ in range(6)]
    for i in range(DEPTH):
        lw = (norm_mix[i], w_in[i], conv_b[i], a_log[i], dt_bias[i], norm_gdn[i], w_proj_a[i], w_proj_b[i],
              w_out[i], norm_ffn[i], w_up[i], conv_ffn[i], w_down[i], norm_ple[i], w_ple[i], w_ple_gate[i], rel_bias)
        xp, *st_p = _layer(xp, p_prompt[i],
                           jnp.zeros((bp, 0, HA, DH), dt), jnp.zeros((bp, 0, HA, DH), dt),
                           jnp.zeros((bp, 0, D_IDX), dt), jnp.zeros((bp, HB, DK, DV), dt),
                           jnp.zeros((bp, CONV_B - 1, C_CONV_B), dt), jnp.zeros((bp, CONV_F - 1, D_FF), dt),
                           *lw)
        xs, *st_s = _layer(xs, p_sample[i], cache_k[i], cache_v[i], cache_kidx[i], state_gdn[i],
                           state_gdn_conv[i], state_ffn_conv[i], *lw)
        for lst, a in zip(outs_p, st_p):
            lst.append(a)
        for lst, a in zip(outs_s, st_s):
            lst.append(a)
    y_prompt = _rmsnorm(xp, norm_final)
    y_sample = _rmsnorm(xs, norm_final)
    k_p, v_p, kidx_p, gdn_p, gconv_p, fconv_p = [jnp.stack(a, axis=0) for a in outs_p]
    k_s, v_s, kidx_s, gdn_s, gconv_s, fconv_s = [jnp.stack(a, axis=0) for a in outs_s]
    return (y_prompt, y_sample, k_p, v_p, kidx_p, gdn_p, gconv_p, fconv_p,
            k_s, v_s, kidx_s, gdn_s, gconv_s, fconv_s)
```

```python
import functools

import numpy as np
import jax
import jax.numpy as jnp
from jax import lax
from jax.experimental import pallas as pl
from jax.experimental.pallas import tpu as pltpu

F32 = jnp.float32
BF16 = jnp.bfloat16
I32 = jnp.int32

D_MODEL = 1024
CHUNK = 64
HA, DH = 8, 64
H_IDX, D_IDX = 8, 64
TOPK_MAX = 256
NUM_BUCKETS, MAX_DISTANCE = 32, 128
HB, DK, DV = 4, 128, 128
CONV_B = 4
D_FF = 2816
CONV_F = 3
D_PLE = 256
EPS = 1e-6
NEG = -1e30
WA = HA * DH
WB = HB * DK
C_CONV_B = 3 * WB
INT_MIN = -2 ** 31

_O_QA, _O_KA, _O_VA, _O_QI, _O_KI, _O_WI = 0, 512, 1024, 1536, 2048, 2112
_O_QB, _O_GB, _O_BB, _O_AB, _O_GA, _O_GBR = 2120, 3656, 4168, 4172, 4176, 5200
_L_WI, _L_BB, _L_AB = 64, 72, 76

VMEM_LIMIT = 56 * 1024 * 1024


def _cparams(sem):
    return pltpu.CompilerParams(dimension_semantics=sem, vmem_limit_bytes=VMEM_LIMIT)


def _const_spec(shape):
    nd = len(shape)
    return pl.BlockSpec(shape, lambda *_: (0,) * nd, pipeline_mode=pl.Buffered(1))


def _rms(x, g):
    return x * lax.rsqrt(jnp.mean(x * x, axis=-1, keepdims=True) + EPS) * g


def _split(x):
    hi = x.astype(BF16)
    lo = (x - hi.astype(F32)).astype(BF16)
    return hi, lo


def _dot(a, b):
    return jnp.dot(a, b, preferred_element_type=F32)


def _dot3(a, b):
    ah, al = _split(a)
    bh, bl = _split(b)
    return _dot(ah, bh) + (_dot(al, bh) + _dot(ah, bl))


def _sigmoid(x):
    return 1.0 / (1.0 + jnp.exp(-x))


def _silu(x):
    return x * _sigmoid(x)


def _in_proj_kernel(x_ref, g_ref, wm_ref, wih_ref, wil_ref, ph_ref, plo_ref, sc_ref,
                    qa_ref, ka_ref, va_ref, kbf_ref, vbf_ref, conv_ref, gb_ref, ga_ref, gbr_ref,
                    qih_ref, qil_ref, small_ref, ki3_ref):
    h = _rms(x_ref[...], g_ref[...])
    hh, hl = _split(h)

    def main(lo, hi):
        return _dot(hh, wm_ref[:, lo:hi])

    qa_ref[...] = main(0, 512).astype(BF16)
    ka = main(512, 1024)
    ka_ref[...] = ka
    kbf_ref[...] = ka.astype(BF16)
    va = main(1024, 1536)
    va_ref[...] = va
    vbf_ref[...] = va.astype(BF16)
    conv_ref[...] = main(1536, 3072)
    gb_ref[...] = main(3072, 3584)
    ga_ref[...] = main(3584, 4608)
    gbr_ref[...] = main(4608, 5632)

    def idx(lo, hi):
        wh = wih_ref[:, lo:hi]
        return _dot(hh, wh) + (_dot(hl, wh) + _dot(hh, wil_ref[:, lo:hi]))

    qi = idx(0, 512)
    qh, ql = _split(qi)
    qih_ref[...] = qh
    qil_ref[...] = ql
    small = idx(512, 640) * sc_ref[...]
    small_ref[...] = small
    sh, sl = _split(small)
    ki3_ref[...] = (_dot(sh, ph_ref[...]) + _dot(sl, plo_ref[...])).astype(BF16)


def _in_proj(x2d, norm_mix, w_in, tm):
    n = x2d.shape[0]
    assert n % tm == 0
    w = w_in
    wm = jnp.concatenate([w[:, _O_QA:_O_KA] * (DH ** -0.5), w[:, _O_KA:_O_QI],
                          w[:, _O_QB:_O_BB], w[:, _O_GA:]], axis=1).astype(BF16)
    wi = jnp.concatenate([w[:, _O_QI:_O_KI] * (D_IDX ** -0.5), w[:, _O_KI:_O_QB],
                          w[:, _O_BB:_O_GA], jnp.zeros((D_MODEL, 48), F32)], axis=1)
    wih = wi.astype(BF16)
    wil = (wi - wih.astype(F32)).astype(BF16)
    ph = np.zeros((128, 256), np.float32)
    plo = np.zeros((128, 256), np.float32)
    for c in range(64):
        ph[c, c] = 1.0
        ph[c, 64 + c] = 1.0
        plo[c, 128 + c] = 1.0
    sc = np.ones((1, 128), np.float32)
    sc[0, _L_WI:_L_WI + H_IDX] = H_IDX ** -0.5
    row = lambda c: pl.BlockSpec((tm, c), lambda i: (i, 0))
    out_cols = [(512, BF16), (512, F32), (512, F32), (512, BF16), (512, BF16), (1536, F32), (512, F32),
                (1024, F32), (1024, F32), (512, BF16), (512, BF16), (128, F32), (256, BF16)]
    return pl.pallas_call(
        _in_proj_kernel,
        grid=(n // tm,),
        in_specs=[row(D_MODEL), _const_spec((1, D_MODEL)), _const_spec((D_MODEL, 5632)),
                  _const_spec((D_MODEL, 640)), _const_spec((D_MODEL, 640)),
                  _const_spec((128, 256)), _const_spec((128, 256)), _const_spec((1, 128))],
        out_specs=[row(c) for c, _ in out_cols],
        out_shape=[jax.ShapeDtypeStruct((n, c), d) for c, d in out_cols],
        compiler_params=_cparams(("arbitrary",)),
        name="in_proj",
    )(x2d, norm_mix.reshape(1, D_MODEL), wm, wih, wil, jnp.asarray(ph, BF16), jnp.asarray(plo, BF16),
      jnp.asarray(sc))


def _dsa_kernel(qT_ref, qihT_ref, qilT_ref, wiT_ref, k_ref, vT_ref, ki3_ref, bias_ref, adm_ref, ltri_ref,
                o_ref, key_ref, m_ref, l_ref, acc_ref, c_ref, *, tq, tk, nk_static, causal, topk, nvalid):
    nk = (pl.program_id(1) + 1) if causal else nk_static
    wiT = wiT_ref[0]
    adm = adm_ref[...] > 0.5

    zeros64 = jnp.zeros((64, tq), BF16)
    q3 = []
    for h in range(H_IDX):
        hi = qihT_ref[0, h * 64:(h + 1) * 64, :]
        lo = qilT_ref[0, h * 64:(h + 1) * 64, :]
        q3.append(jnp.concatenate([hi, lo, hi, zeros64], axis=0))

    def score_tile(j, is_last):
        kt = ki3_ref[0, j]
        acc = None
        for h in range(H_IDX):
            t = jnp.maximum(_dot(kt, q3[h]), 0.0) * wiT[h:h + 1, :]
            acc = t if acc is None else acc + t
        if is_last:
            acc = jnp.where(adm, acc, NEG)
        bits = pltpu.bitcast(acc, I32)
        bits = jnp.where(bits == INT_MIN, 0, bits)
        key_ref[j] = bits ^ ((bits >> 31) & 0x7FFFFFFF)

    def score_body(j, c):
        score_tile(j, False)
        return c

    lax.fori_loop(0, nk - 1, score_body, 0)
    score_tile(nk - 1, True)

    def count(cand, strict):
        def body(j, acc):
            kk = key_ref[j]
            hit = (kk > cand) if strict else (kk >= cand)
            return acc + jnp.where(hit, 1, 0).astype(I32).reshape(tk // 8, 8, tq).sum(axis=0)

        acc = lax.fori_loop(0, nk, body, jnp.zeros((8, tq), I32))
        return acc.sum(axis=0, keepdims=True)

    def bisect(it, tu):
        cand_u = tu | jnp.left_shift(jnp.int32(1), 31 - it)
        cnt = count(cand_u ^ INT_MIN, False)
        return jnp.where(cnt >= topk, cand_u, tu)

    thr = lax.fori_loop(0, 32, bisect, jnp.zeros((1, tq), I32)) ^ INT_MIN
    cnt_ge = count(thr, False)
    need = (topk - count(thr, True)).astype(F32)
    lane = lax.broadcasted_iota(I32, (1, tq), 1)
    excess = jnp.max(jnp.where(lane < nvalid, cnt_ge - topk, 0))

    m_ref[...] = jnp.full(m_ref.shape, NEG, F32)
    l_ref[...] = jnp.zeros(l_ref.shape, F32)
    acc_ref[...] = jnp.zeros(acc_ref.shape, F32)
    c_ref[...] = jnp.zeros(c_ref.shape, F32)
    qm = []
    for h in range(HA):
        qh = qT_ref[0, h * 64:(h + 1) * 64, :]
        qm.append(jnp.concatenate([qh, zeros64] if h % 2 == 0 else [zeros64, qh], axis=0))

    def attn_tile(j, kind, ties):
        kk = key_ref[j]
        if ties:
            eq = kk == thr
            eqb = jnp.where(eq, 1.0, 0.0).astype(BF16)
            rank = _dot(ltri_ref[...], eqb) + c_ref[0:1, :]
            sel = (kk > thr) | (eq & (rank <= need))
            c_ref[0:1, :] = c_ref[0:1, :] + jnp.sum(eqb.astype(F32), axis=0, keepdims=True)
        else:
            sel = kk >= thr
        if kind == 2:
            sel = sel & adm
        for h in range(HA):
            pr = h // 2
            lg = _dot(k_ref[0, j, :, pr * 128:(pr + 1) * 128], qm[h])
            if kind >= 1:
                lg = lg + bias_ref[kind - 1, h]
            lg = jnp.where(sel, lg, NEG)
            m_old = m_ref[h:h + 1, :]
            m_new = jnp.maximum(m_old, jnp.max(lg, axis=0, keepdims=True))
            p = jnp.exp(lg - m_new)
            alpha = jnp.exp(m_old - m_new)
            l_ref[h:h + 1, :] = alpha * l_ref[h:h + 1, :] + jnp.sum(p, axis=0, keepdims=True)
            pv = _dot(vT_ref[0, j, h * 64:(h + 1) * 64, :], p.astype(BF16))
            acc_ref[h * 64:(h + 1) * 64, :] = alpha * acc_ref[h * 64:(h + 1) * 64, :] + pv
            m_ref[h:h + 1, :] = m_new

    def attend(ties):
        def far(j, c):
            attn_tile(j, 0, ties)
            return c

        lax.fori_loop(0, nk - 2, far, 0)
        if causal:
            @pl.when(nk >= 2)
            def _():
                attn_tile(nk - 2, 1, ties)
        elif nk_static >= 2:
            attn_tile(nk - 2, 1, ties)
        attn_tile(nk - 1, 2, ties)

    @pl.when(excess > 0)
    def _():
        attend(True)

    @pl.when(excess <= 0)
    def _():
        attend(False)

    for h in range(HA):
        o_ref[0, h * 64:(h + 1) * 64, :] = (acc_ref[h * 64:(h + 1) * 64, :] / l_ref[h:h + 1, :]).astype(BF16)


def _t5_bucket_np(rel):
    half = NUM_BUCKETS // 2
    max_exact = half // 2
    out = np.zeros(rel.shape, np.int64)
    flat_rel = rel.reshape(-1)
    flat = out.reshape(-1)
    for a in range(flat_rel.size):
        r = int(flat_rel[a])
        n = abs(r)
        b = n if n < max_exact else min(half - 1, (n * n).bit_length() + 1)
        flat[a] = b + (half if r > 0 else 0)
    return out


def _bias_tables(rel_bias, tk, tq, tq_valid):
    kj = np.arange(tk)[:, None]
    t = np.minimum(np.arange(tq), tq_valid - 1)[None, :]
    rel_to_bucket = _t5_bucket_np(np.arange(-2 * tk - tq, tk + 1))
    lut = lambda rel: rel_to_bucket[rel + 2 * tk + tq]
    idx = np.stack([lut(kj - tk - t), lut(kj - t)], axis=0)
    tab = rel_bias.astype(F32)
    far = tab[NUM_BUCKETS // 2 - 1]
    b = jnp.take(tab, jnp.asarray(idx, I32), axis=0)
    return jnp.moveaxis(b - far, -1, 1)


def _dsa(qT, qihT, qilT, wiT, k4, vT4, ki34, bias, adm, *, nq, causal, topk, nvalid):
    g, _, tq = qT.shape
    b, nk, tk, _ = k4.shape
    assert g == b * nq and tk % 8 == 0
    ltri = jnp.asarray(np.tril(np.ones((tk, tk), np.float32)), BF16)
    qspec = lambda r: pl.BlockSpec((1, r, tq), lambda bi, i: (bi * nq + i, 0, 0))
    kspec = lambda s: pl.BlockSpec((1,) + s, lambda bi, i: (bi, 0, 0, 0), pipeline_mode=pl.Buffered(1))
    kern = functools.partial(_dsa_kernel, tq=tq, tk=tk, nk_static=nk, causal=causal, topk=topk, nvalid=nvalid)
    return pl.pallas_call(
        kern,
        grid=(b, nq),
        in_specs=[qspec(512), qspec(512), qspec(512), qspec(8),
                  kspec((nk, tk, 512)), kspec((nk, 512, tk)), kspec((nk, tk, 256)),
                  _const_spec((2, HA, tk, tq)), _const_spec((tk, tq)), _const_spec((tk, tk))],
        out_specs=qspec(512),
        out_shape=jax.ShapeDtypeStruct((g, 512, tq), BF16),
        scratch_shapes=[pltpu.VMEM((nk, tk, tq), I32), pltpu.VMEM((8, tq), F32), pltpu.VMEM((8, tq), F32),
                        pltpu.VMEM((512, tq), F32), pltpu.VMEM((8, tq), F32)],
        compiler_params=_cparams(("arbitrary", "arbitrary")),
        name="dsa_causal" if causal else "dsa_cached",
    )(qT, qihT, qilT, wiT, k4, vT4, ki34, bias, adm, ltri)


def _gdn_kernel(conv_ref, hist_ref, gb_ref, small_ref, s0_ref, cw_ref, alog_ref, dtb_ref, ng_ref,
                ob_ref, sfin_ref, ext_ref, s_ref, *, nvalid):
    c = CHUNK
    n = HB * c

    @pl.when(pl.program_id(1) == 0)
    def _():
        ext_ref[0:8, :] = hist_ref[0]
        s_ref[...] = s0_ref[0]

    ext_ref[8:8 + c, :] = conv_ref[0]
    cb = ext_ref[5:5 + c, :] * cw_ref[0:1, :]
    for j in range(1, CONV_B):
        cb = cb + ext_ref[5 + j:5 + j + c, :] * cw_ref[j:j + 1, :]
    cb = _silu(cb)
    ext_ref[0:8, :] = ext_ref[c:c + 8, :]

    small = small_ref[0]
    rowv = (lax.broadcasted_iota(I32, (c, 1), 0) < nvalid).astype(F32)
    beta_all = _sigmoid(small) * rowv
    sp = small + dtb_ref[...]
    g_all = -jnp.exp(alog_ref[...]) * (jnp.maximum(sp, 0.0) + jnp.log(1.0 + jnp.exp(-jnp.abs(sp)))) * rowv

    def stack(fn):
        return jnp.concatenate([fn(h) for h in range(HB)], axis=0)

    def l2n(x):
        return x * lax.rsqrt(jnp.sum(x * x, axis=-1, keepdims=True) + EPS)

    q = stack(lambda h: l2n(cb[:, h * DK:(h + 1) * DK]) * (DK ** -0.5))
    k = stack(lambda h: l2n(cb[:, WB + h * DK:WB + (h + 1) * DK]) * rowv)
    v = stack(lambda h: cb[:, 2 * WB + h * DV:2 * WB + (h + 1) * DV] * rowv)
    beta = stack(lambda h: beta_all[:, _L_BB + h:_L_BB + h + 1])
    g = stack(lambda h: g_all[:, _L_AB + h:_L_AB + h + 1])

    ri = lax.broadcasted_iota(I32, (n, n), 0)
    ci = lax.broadcasted_iota(I32, (n, n), 1)
    same = (ri // c) == (ci // c)
    tri = same & (ci <= ri)
    strict = same & (ci < ri)
    eye = (ri == ci).astype(F32)

    gcum = _dot3(tri.astype(F32), jnp.broadcast_to(g, (n, 128)))
    gcum_row = gcum.T[0:1, :]
    gcum_col = gcum[:, 0:1]
    decay = jnp.where(tri, jnp.exp(jnp.where(tri, gcum_col - gcum_row, 0.0)), 0.0)
    kb = k * beta
    kbf = k.astype(BF16)
    nt = (((1,), (1,)), ((), ()))
    kk = lax.dot_general(kb.astype(BF16), kbf, nt, preferred_element_type=F32)
    nmat = jnp.where(strict, kk * decay, 0.0)
    inv = eye - nmat
    pw = nmat
    for _ in range(int(np.log2(c)) - 1):
        pw = _dot3(pw, pw)
        inv = inv + _dot3(inv, pw)
    rhs = jnp.concatenate([v * beta, kb * jnp.exp(gcum_col)], axis=1)
    sol = _dot3(inv, rhs)
    u, w = sol[:, :DV], sol[:, DV:]
    attn = jnp.where(tri, lax.dot_general(q.astype(BF16), kbf, nt, preferred_element_type=F32) * decay, 0.0)

    qg = (q * jnp.exp(gcum_col)).astype(BF16)
    wb = w.astype(BF16)
    v_new = stack(lambda h: u[h * c:(h + 1) * c] - _dot(wb[h * c:(h + 1) * c], s_ref[h].astype(BF16)))
    vnb = v_new.astype(BF16)
    attnb = attn.astype(BF16)
    ng = ng_ref[...]
    tn = (((0,), (0,)), ((), ()))
    for h in range(HB):
        rows = slice(h * c, (h + 1) * c)
        s_old = s_ref[h]
        o = _dot(qg[rows], s_old.astype(BF16)) + _dot(attnb[rows], vnb)
        g_last = gcum_col[(h + 1) * c - 1:(h + 1) * c, :]
        kd = (k[rows] * jnp.exp(g_last - gcum_col[rows])).astype(BF16)
        s_ref[h] = s_old * jnp.exp(g_last) + lax.dot_general(kd, vnb[rows], tn, preferred_element_type=F32)
        gate = gb_ref[0, :, h * DV:(h + 1) * DV]
        ob_ref[0, :, h * DV:(h + 1) * DV] = (_rms(o, ng) * _silu(gate)).astype(BF16)
    sfin_ref[0] = s_ref[...]


def _gdn(conv_in, hist, gb, small, s0, conv_w, a_log, dt_bias, norm_gdn, nvalid):
    b, t, _ = conv_in.shape
    assert t % CHUNK == 0
    alog = jnp.zeros((1, 128), F32).at[0, _L_AB:_L_AB + HB].set(a_log)
    dtb = jnp.zeros((1, 128), F32).at[0, _L_AB:_L_AB + HB].set(dt_bias)
    row = lambda c_: pl.BlockSpec((1, CHUNK, c_), lambda bi, ti: (bi, ti, 0))
    per_b = lambda s: pl.BlockSpec((1,) + s, lambda bi, ti: (bi,) + (0,) * len(s))
    return pl.pallas_call(
        functools.partial(_gdn_kernel, nvalid=nvalid),
        grid=(b, t // CHUNK),
        in_specs=[row(C_CONV_B), per_b((8, C_CONV_B)), row(WB), row(128), per_b((HB, DK, DV)),
                  _const_spec((CONV_B, C_CONV_B)), _const_spec((1, 128)), _const_spec((1, 128)),
                  _const_spec((1, DV))],
        out_specs=[row(WB), per_b((HB, DK, DV))],
        out_shape=[jax.ShapeDtypeStruct((b, t, WB), BF16), jax.ShapeDtypeStruct((b, HB, DK, DV), F32)],
        scratch_shapes=[pltpu.VMEM((8 + CHUNK + 8, C_CONV_B), F32), pltpu.VMEM((HB, DK, DV), F32)],
        compiler_params=_cparams(("arbitrary", "arbitrary")),
        name="gdn",
    )(conv_in, hist, gb, small, s0, conv_w, alog, dtb, norm_gdn.reshape(1, DV))


def _merge_kernel(x_ref, oa_ref, ob_ref, ga_ref, gbr_ref, wa_ref, wb_ref, wo_ref, nf_ref, x1_ref, h2_ref):
    ya = _dot(oa_ref[...], wa_ref[...])
    yb = _dot(ob_ref[...], wb_ref[...])
    mix = _sigmoid(ga_ref[...]) * ya + _sigmoid(gbr_ref[...]) * yb
    x1 = x_ref[...] + _dot(mix.astype(BF16), wo_ref[...])
    x1_ref[...] = x1
    h2_ref[...] = _rms(x1, nf_ref[...]).astype(BF16)


def _merge(x2d, oa, ob, ga, gbr, w_proj_a, w_proj_b, w_out, norm_ffn, tm):
    n = x2d.shape[0]
    row = lambda c: pl.BlockSpec((tm, c), lambda i: (i, 0))
    return pl.pallas_call(
        _merge_kernel,
        grid=(n // tm,),
        in_specs=[row(D_MODEL), row(WA), row(WB), row(D_MODEL), row(D_MODEL),
                  _const_spec((WA, D_MODEL)), _const_spec((WB, D_MODEL)), _const_spec((D_MODEL, D_MODEL)),
                  _const_spec((1, D_MODEL))],
        out_specs=[row(D_MODEL), row(D_MODEL)],
        out_shape=[jax.ShapeDtypeStruct((n, D_MODEL), F32), jax.ShapeDtypeStruct((n, D_MODEL), BF16)],
        compiler_params=_cparams(("arbitrary",)),
        name="merge",
    )(x2d, oa, ob, ga, gbr, w_proj_a.astype(BF16), w_proj_b.astype(BF16), w_out.astype(BF16),
      norm_ffn.reshape(1, D_MODEL))


def _ffn_kernel(x1_ref, h2_ref, p_ref, hist_ref, wup_ref, cw_ref, wdn_ref, npl_ref, wpg_ref, wple_ref, nfin_ref,
                y_ref, tail_ref, ext_ref, *, tm):
    @pl.when(pl.program_id(1) == 0)
    def _():
        ext_ref[0:8, :] = hist_ref[0]

    h2 = h2_ref[0]
    ext_ref[8:8 + tm, :] = _dot(h2, wup_ref[:, 0:D_FF])
    u_val = _dot(h2, wup_ref[:, D_FF:2 * D_FF])
    cv = ext_ref[6:6 + tm, :] * cw_ref[0:1, :]
    for j in range(1, CONV_F):
        cv = cv + ext_ref[6 + j:6 + j + tm, :] * cw_ref[j:j + 1, :]
    tail = ext_ref[tm:tm + 8, :]
    ext_ref[0:8, :] = tail
    tail_ref[0] = tail
    act = 0.5 * cv * (1.0 + jnp.tanh(0.7978845608028654 * (cv + 0.044715 * (cv * cv * cv))))
    x2 = x1_ref[0] + _dot((act * u_val).astype(BF16), wdn_ref[...])
    gate = _sigmoid(_dot(_rms(x2, npl_ref[...]).astype(BF16), wpg_ref[...]))
    x3 = x2 + gate * _dot(p_ref[0].astype(BF16), wple_ref[...])
    y_ref[0] = _rms(x3, nfin_ref[...])


def _ffn_ple(x1, h2, p, hist, w_up, conv_ffn, w_down, norm_ple, w_ple_gate, w_ple, norm_final, tm):
    b, t, _ = x1.shape
    assert t % tm == 0 and tm >= 8
    row = lambda c: pl.BlockSpec((1, tm, c), lambda bi, ti: (bi, ti, 0))
    per_b = pl.BlockSpec((1, 8, D_FF), lambda bi, ti: (bi, 0, 0))
    return pl.pallas_call(
        functools.partial(_ffn_kernel, tm=tm),
        grid=(b, t // tm),
        in_specs=[row(D_MODEL), row(D_MODEL), row(D_PLE), per_b,
                  _const_spec((D_MODEL, 2 * D_FF)), _const_spec((CONV_F, D_FF)), _const_spec((D_FF, D_MODEL)),
                  _const_spec((1, D_MODEL)), _const_spec((D_MODEL, D_MODEL)), _const_spec((D_PLE, D_MODEL)),
                  _const_spec((1, D_MODEL))],
        out_specs=[row(D_MODEL), per_b],
        out_shape=[jax.ShapeDtypeStruct((b, t, D_MODEL), F32), jax.ShapeDtypeStruct((b, 8, D_FF), F32)],
        scratch_shapes=[pltpu.VMEM((8 + tm, D_FF), F32)],
        compiler_params=_cparams(("arbitrary", "arbitrary")),
        name="ffn_ple",
    )(x1, h2, p, hist, w_up.astype(BF16), conv_ffn, w_down.astype(BF16), norm_ple.reshape(1, D_MODEL),
      w_ple_gate.astype(BF16), w_ple.astype(BF16), norm_final.reshape(1, D_MODEL))


def _pad_hist(hist, rows=8):
    b, r, c = hist.shape
    return jnp.concatenate([jnp.zeros((b, rows - r, c), hist.dtype), hist], axis=1)


def _layer(x, p, past_k, past_v, past_kidx, s_gdn, conv_b_hist, ffn_hist, wts, *, tm, tq):
    (norm_mix, w_in, conv_b, a_log, dt_bias, norm_gdn, w_proj_a, w_proj_b, w_out, norm_ffn, w_up, conv_ffn,
     w_down, norm_ple, w_ple, w_ple_gate, rel_bias, norm_final) = wts
    b, t, _ = x.shape
    n = b * t
    past = past_k.shape[1]
    topk = min(TOPK_MAX, (past + t) // 4)
    x2d = x.reshape(n, D_MODEL)
    (qa, ka, va, kbf, vbf, conv_in, gb, ga, gbr, qih, qil, small, ki3) = _in_proj(x2d, norm_mix, w_in, min(tm, n))

    if past == 0:
        assert t % tq == 0 and tq % CHUNK == 0 and tq >= topk
        nq, tk, lanes = t // tq, tq, tq
        to_q = lambda a: jnp.swapaxes(a.reshape(b * nq, tq, a.shape[-1]), 1, 2)
        k4 = kbf.reshape(b, nq, tk, WA)
        vT4 = jnp.swapaxes(vbf.reshape(b, nq, tk, WA), 2, 3)
        ki34 = ki3.reshape(b, nq, tk, 256)
        kj = np.arange(tk)[:, None]
        adm = ((kj // CHUNK) <= (np.arange(tq)[None, :] // CHUNK)).astype(np.float32)
        nvalid, causal = tq, True
    else:
        tk, lanes, nq = 128, 128, 1
        assert past % tk == 0 and t <= tk and t <= lanes
        padq = lambda a: jnp.pad(a.reshape(b, t, a.shape[-1]), ((0, 0), (0, lanes - t), (0, 0)))
        to_q = lambda a: jnp.swapaxes(padq(a), 1, 2)
        padk = lambda new, old: jnp.concatenate(
            [old, new.reshape(b, t, new.shape[-1]), jnp.zeros((b, tk - t, new.shape[-1]), new.dtype)], axis=1)
        nk = past // tk + 1
        k4 = padk(kbf, past_k.reshape(b, past, WA).astype(BF16)).reshape(b, nk, tk, WA)
        vT4 = jnp.swapaxes(padk(vbf, past_v.reshape(b, past, WA).astype(BF16)).reshape(b, nk, tk, WA), 2, 3)
        pk = past_kidx.astype(F32)
        pkh = pk.astype(BF16)
        pkl = (pk - pkh.astype(F32)).astype(BF16)
        past3 = jnp.concatenate([pkh, pkh, pkl, jnp.zeros_like(pkh)], axis=-1)
        ki34 = padk(ki3, past3).reshape(b, nk, tk, 256)
        adm = np.broadcast_to(np.arange(tk)[:, None] < t, (tk, lanes)).astype(np.float32)
        nvalid, causal = t, False
    bias = _bias_tables(rel_bias, tk, lanes, nvalid)
    oT = _dsa(to_q(qa), to_q(qih), to_q(qil), to_q(small[:, _L_WI:_L_WI + H_IDX]), k4, vT4, ki34, bias,
              jnp.asarray(adm), nq=nq, causal=causal, topk=topk, nvalid=nvalid)
    oa = jnp.swapaxes(oT, 1, 2)[:, :min(lanes, t) if past else lanes].reshape(n, WA)

    tp = -(-t // CHUNK) * CHUNK
    padt = lambda a: jnp.pad(a.reshape(b, t, a.shape[-1]), ((0, 0), (0, tp - t), (0, 0)))
    ob, s_new = _gdn(padt(conv_in), _pad_hist(conv_b_hist), padt(gb), padt(small), s_gdn, conv_b, a_log, dt_bias,
                     norm_gdn, nvalid=min(t, CHUNK))
    ob = ob[:, :t].reshape(n, WB)
    new_conv_b = jnp.concatenate([conv_b_hist, conv_in.reshape(b, t, C_CONV_B)], axis=1)[:, t:]

    x1, h2 = _merge(x2d, oa, ob, ga, gbr, w_proj_a, w_proj_b, w_out, norm_ffn, min(tm, n))
    tmf = min(tm, t)
    y, tail = _ffn_ple(x1.reshape(b, t, D_MODEL), h2.reshape(b, t, D_MODEL), p, _pad_hist(ffn_hist), w_up,
                       conv_ffn, w_down, norm_ple, w_ple_gate, w_ple, norm_final, tmf)
    new_ffn = tail[:, 8 - (CONV_F - 1):]
    return (y, ka.reshape(b, t, HA, DH), va.reshape(b, t, HA, DH), small[:, :D_IDX].reshape(b, t, D_IDX),
            s_new, new_conv_b, new_ffn)


def kernel(x_prompt, x_sample, p_prompt, p_sample, cache_k, cache_v, cache_kidx, state_gdn, state_gdn_conv,
           state_ffn_conv, norm_mix, w_in, conv_b, a_log, dt_bias, norm_gdn, w_proj_a, w_proj_b, w_out, norm_ffn,
           w_up, conv_ffn, w_down, norm_ple, w_ple, w_ple_gate, rel_bias, norm_final):
    assert norm_mix.shape[0] == 1
    bp = x_prompt.shape[0]
    dt = x_prompt.dtype
    wts = (norm_mix[0], w_in[0], conv_b[0], a_log[0], dt_bias[0], norm_gdn[0], w_proj_a[0], w_proj_b[0], w_out[0],
           norm_ffn[0], w_up[0], conv_ffn[0], w_down[0], norm_ple[0], w_ple[0], w_ple_gate[0], rel_bias, norm_final)
    outs_p = _layer(x_prompt, p_prompt[0], jnp.zeros((bp, 0, HA, DH), dt), jnp.zeros((bp, 0, HA, DH), dt),
                    jnp.zeros((bp, 0, D_IDX), dt), jnp.zeros((bp, HB, DK, DV), dt),
                    jnp.zeros((bp, CONV_B - 1, C_CONV_B), dt), jnp.zeros((bp, CONV_F - 1, D_FF), dt),
                    wts, tm=256, tq=256)
    outs_s = _layer(x_sample, p_sample[0], cache_k[0], cache_v[0], cache_kidx[0], state_gdn[0],
                    state_gdn_conv[0], state_ffn_conv[0], wts, tm=256, tq=256)
    yp, ys = outs_p[0], outs_s[0]
    return (yp, ys) + tuple(a[None] for a in outs_p[1:]) + tuple(a[None] for a in outs_s[1:])
```

```python
import functools

import numpy as np
import jax
import jax.numpy as jnp
from jax import lax
from jax.experimental import pallas as pl
from jax.experimental.pallas import tpu as pltpu

F32 = jnp.float32
BF16 = jnp.bfloat16
I32 = jnp.int32

D_MODEL = 1024
CHUNK = 64
HA, DH = 8, 64
H_IDX, D_IDX = 8, 64
TOPK_MAX = 256
NUM_BUCKETS, MAX_DISTANCE = 32, 128
HB, DK, DV = 4, 128, 128
CONV_B = 4
D_FF = 2816
CONV_F = 3
D_PLE = 256
EPS = 1e-6
NEG = -1e30
WA = HA * DH
WB = HB * DK
C_CONV_B = 3 * WB
INT_MIN = -2 ** 31
LOG2E = 1.4426950408889634

_O_QA, _O_KA, _O_VA, _O_QI, _O_KI, _O_WI = 0, 512, 1024, 1536, 2048, 2112
_O_QB, _O_GB, _O_BB, _O_AB, _O_GA, _O_GBR = 2120, 3656, 4168, 4172, 4176, 5200
_L_WI, _L_BB, _L_AB = 64, 72, 76

VMEM_LIMIT = 56 * 1024 * 1024


def _cparams(sem):
    return pltpu.CompilerParams(dimension_semantics=sem, vmem_limit_bytes=VMEM_LIMIT)


def _const_spec(shape):
    nd = len(shape)
    return pl.BlockSpec(shape, lambda *_: (0,) * nd, pipeline_mode=pl.Buffered(1))


def _rms(x, g):
    return x * lax.rsqrt(jnp.mean(x * x, axis=-1, keepdims=True) + EPS) * g


def _split(x):
    hi = x.astype(BF16)
    lo = (x - hi.astype(F32)).astype(BF16)
    return hi, lo


def _dot(a, b):
    return jnp.dot(a, b, preferred_element_type=F32)


def _dot3(a, b):
    ah, al = _split(a)
    bh, bl = _split(b)
    return _dot(ah, bh) + (_dot(al, bh) + _dot(ah, bl))


def _sigmoid(x):
    return 1.0 / (1.0 + jnp.exp(-x))


def _silu(x):
    return x * _sigmoid(x)


def _in_proj_kernel(x_ref, g_ref, wm_ref, wih_ref, wil_ref, ph_ref, plo_ref, sc_ref,
                    qa_ref, ka_ref, va_ref, kbf_ref, vbf_ref, conv_ref, gb_ref, ga_ref, gbr_ref,
                    qih_ref, qil_ref, small_ref, ki3_ref):
    h = _rms(x_ref[...], g_ref[...])
    hh, hl = _split(h)

    def main(lo, hi):
        return _dot(hh, wm_ref[:, lo:hi])

    qa_ref[...] = main(0, 512).astype(BF16)
    ka = main(512, 1024)
    ka_ref[...] = ka
    kbf_ref[...] = ka.astype(BF16)
    va = main(1024, 1536)
    va_ref[...] = va
    vbf_ref[...] = va.astype(BF16)
    conv_ref[...] = main(1536, 3072)
    gb_ref[...] = main(3072, 3584)
    ga_ref[...] = main(3584, 4608)
    gbr_ref[...] = main(4608, 5632)

    def idx(lo, hi):
        wh = wih_ref[:, lo:hi]
        return _dot(hh, wh) + (_dot(hl, wh) + _dot(hh, wil_ref[:, lo:hi]))

    qi = idx(0, 512)
    qh, ql = _split(qi)
    qih_ref[...] = qh
    qil_ref[...] = ql
    small = idx(512, 640) * sc_ref[...]
    small_ref[...] = small
    sh, sl = _split(small)
    ki3_ref[...] = (_dot(sh, ph_ref[...]) + _dot(sl, plo_ref[...])).astype(BF16)


def _in_proj(x2d, norm_mix, w_in, tm):
    n = x2d.shape[0]
    assert n % tm == 0
    w = w_in
    wm = jnp.concatenate([w[:, _O_QA:_O_KA] * (DH ** -0.5 * LOG2E), w[:, _O_KA:_O_QI],
                          w[:, _O_QB:_O_BB], w[:, _O_GA:]], axis=1).astype(BF16)
    wi = jnp.concatenate([w[:, _O_QI:_O_KI] * (D_IDX ** -0.5), w[:, _O_KI:_O_QB],
                          w[:, _O_BB:_O_GA], jnp.zeros((D_MODEL, 48), F32)], axis=1)
    wih = wi.astype(BF16)
    wil = (wi - wih.astype(F32)).astype(BF16)
    ph = np.zeros((128, 256), np.float32)
    plo = np.zeros((128, 256), np.float32)
    for c in range(64):
        ph[c, c] = 1.0
        ph[c, 64 + c] = 1.0
        plo[c, 128 + c] = 1.0
    sc = np.ones((1, 128), np.float32)
    sc[0, _L_WI:_L_WI + H_IDX] = H_IDX ** -0.5
    row = lambda c: pl.BlockSpec((tm, c), lambda i: (i, 0))
    out_cols = [(512, BF16), (512, F32), (512, F32), (512, BF16), (512, BF16), (1536, F32), (512, F32),
                (1024, F32), (1024, F32), (512, BF16), (512, BF16), (128, F32), (256, BF16)]
    return pl.pallas_call(
        _in_proj_kernel,
        grid=(n // tm,),
        in_specs=[row(D_MODEL), _const_spec((1, D_MODEL)), _const_spec((D_MODEL, 5632)),
                  _const_spec((D_MODEL, 640)), _const_spec((D_MODEL, 640)),
                  _const_spec((128, 256)), _const_spec((128, 256)), _const_spec((1, 128))],
        out_specs=[row(c) for c, _ in out_cols],
        out_shape=[jax.ShapeDtypeStruct((n, c), d) for c, d in out_cols],
        compiler_params=_cparams(("arbitrary",)),
        name="in_proj",
    )(x2d, norm_mix.reshape(1, D_MODEL), wm, wih, wil, jnp.asarray(ph, BF16), jnp.asarray(plo, BF16),
      jnp.asarray(sc))


def _dsa_kernel(qT_ref, qihT_ref, qilT_ref, wiT_ref, k_ref, vT_ref, ki3_ref, bias_ref, adm_ref, ltri_ref,
                o_ref, key_ref, m_ref, l_ref, acc_ref, c_ref, lg_ref, *, tq, tk, nk_static, causal, topk, nvalid):
    nk = (pl.program_id(1) + 1) if causal else nk_static
    wiT = wiT_ref[0]
    adm = adm_ref[...] > 0.5

    zeros64 = jnp.zeros((64, tq), BF16)
    q3 = []
    for h in range(H_IDX):
        hi = qihT_ref[0, h * 64:(h + 1) * 64, :]
        lo = qilT_ref[0, h * 64:(h + 1) * 64, :]
        q3.append(jnp.concatenate([hi, lo, hi, zeros64], axis=0))

    def score_tile(j, is_last):
        kt = ki3_ref[0, j]
        acc = None
        for h in range(H_IDX):
            t = jnp.maximum(_dot(kt, q3[h]), 0.0) * wiT[h:h + 1, :]
            acc = t if acc is None else acc + t
        if is_last:
            acc = jnp.where(adm, acc, NEG)
        bits = pltpu.bitcast(acc, I32)
        bits = jnp.where(bits == INT_MIN, 0, bits)
        key_ref[j] = bits ^ ((bits >> 31) & 0x7FFFFFFF)

    def score_body(j, c):
        score_tile(j, False)
        return c

    lax.fori_loop(0, nk - 1, score_body, 0)
    score_tile(nk - 1, True)

    def count(cand, strict):
        def body(j, acc):
            kk = key_ref[j]
            hit = (kk > cand) if strict else (kk >= cand)
            return acc + jnp.where(hit, 1, 0).astype(I32).reshape(tk // 8, 8, tq).sum(axis=0)

        acc = lax.fori_loop(0, nk, body, jnp.zeros((8, tq), I32))
        return acc.sum(axis=0, keepdims=True)

    def bisect(it, tu):
        cand_u = tu | jnp.left_shift(jnp.int32(1), 31 - it)
        return jnp.where(count(cand_u ^ INT_MIN, False) >= topk, cand_u, tu)

    thr = lax.fori_loop(0, 32, bisect, jnp.zeros((1, tq), I32)) ^ INT_MIN
    lane_ok = lax.broadcasted_iota(I32, (1, tq), 1) < nvalid
    excess = jnp.max(jnp.where(lane_ok, count(thr, False) - topk, 0))

    m_ref[...] = jnp.full(m_ref.shape, NEG, F32)
    l_ref[...] = jnp.zeros(l_ref.shape, F32)
    acc_ref[...] = jnp.zeros(acc_ref.shape, F32)
    c_ref[...] = jnp.zeros(c_ref.shape, F32)
    qm = []
    for h in range(HA):
        qh = qT_ref[0, h * 64:(h + 1) * 64, :]
        qm.append(jnp.concatenate([qh, zeros64] if h % 2 == 0 else [zeros64, qh], axis=0))

    def attn_tile(j, kind, need):
        kk = key_ref[j]
        if need is not None:
            eq = kk == thr
            eqb = jnp.where(eq, 1.0, 0.0).astype(BF16)
            rank = _dot(ltri_ref[...], eqb) + c_ref[0:1, :]
            sel = (kk > thr) | (eq & (rank <= need))
            c_ref[0:1, :] = c_ref[0:1, :] + jnp.sum(eqb.astype(F32), axis=0, keepdims=True)
        else:
            sel = kk >= thr
        if kind == 2:
            sel = sel & adm
        mx = []
        for h in range(HA):
            pr = h // 2
            lg = _dot(k_ref[0, j, :, pr * 128:(pr + 1) * 128], qm[h])
            if kind >= 1:
                lg = lg + bias_ref[kind - 1, h]
            lg = jnp.where(sel, lg, NEG)
            lg_ref[h] = lg
            mx.append(jnp.max(lg, axis=0, keepdims=True))
        m_old = m_ref[...]
        m_new = jnp.maximum(m_old, jnp.concatenate(mx, axis=0))
        alpha = jnp.exp2(m_old - m_new)
        m_ref[...] = m_new
        ps = []
        for h in range(HA):
            p = jnp.exp2(lg_ref[h] - m_new[h:h + 1, :])
            ps.append(jnp.sum(p, axis=0, keepdims=True))
            pv = _dot(vT_ref[0, j, h * 64:(h + 1) * 64, :], p.astype(BF16))
            acc_ref[h * 64:(h + 1) * 64, :] = alpha[h:h + 1, :] * acc_ref[h * 64:(h + 1) * 64, :] + pv
        l_ref[...] = alpha * l_ref[...] + jnp.concatenate(ps, axis=0)

    def attend(need):
        def far(j, c):
            attn_tile(j, 0, need)
            return c

        lax.fori_loop(0, nk - 2, far, 0)
        if causal:
            @pl.when(nk >= 2)
            def _():
                attn_tile(nk - 2, 1, need)
        elif nk_static >= 2:
            attn_tile(nk - 2, 1, need)
        attn_tile(nk - 1, 2, need)

    @pl.when(excess > 0)
    def _():
        attend((topk - count(thr, True)).astype(F32))

    @pl.when(excess <= 0)
    def _():
        attend(None)

    for h in range(HA):
        o_ref[0, h * 64:(h + 1) * 64, :] = (acc_ref[h * 64:(h + 1) * 64, :] / l_ref[h:h + 1, :]).astype(BF16)


def _t5_bucket_np(rel):
    half = NUM_BUCKETS // 2
    max_exact = half // 2
    out = np.zeros(rel.shape, np.int64)
    flat_rel = rel.reshape(-1)
    flat = out.reshape(-1)
    for a in range(flat_rel.size):
        r = int(flat_rel[a])
        n = abs(r)
        b = n if n < max_exact else min(half - 1, (n * n).bit_length() + 1)
        flat[a] = b + (half if r > 0 else 0)
    return out


def _bias_tables(rel_bias, tk, tq, tq_valid):
    kj = np.arange(tk)[:, None]
    t = np.minimum(np.arange(tq), tq_valid - 1)[None, :]
    rel_to_bucket = _t5_bucket_np(np.arange(-2 * tk - tq, tk + 1))
    lut = lambda rel: rel_to_bucket[rel + 2 * tk + tq]
    idx = np.stack([lut(kj - tk - t), lut(kj - t)], axis=0)
    tab = rel_bias.astype(F32) * LOG2E
    far = tab[NUM_BUCKETS // 2 - 1]
    onehot = jnp.asarray(idx[..., None] == np.arange(NUM_BUCKETS), F32)
    return jnp.einsum("ktqb,bh->khtq", onehot, tab - far, precision=lax.Precision.HIGHEST)


def _dsa(qT, qihT, qilT, wiT, k4, vT4, ki34, bias, adm, *, nq, causal, topk, nvalid):
    g, _, tq = qT.shape
    b, nk, tk, _ = k4.shape
    assert g == b * nq and tk % 8 == 0
    ltri = jnp.asarray(np.tril(np.ones((tk, tk), np.float32)), BF16)
    qspec = lambda r: pl.BlockSpec((1, r, tq), lambda bi, i: (bi * nq + i, 0, 0))
    kspec = lambda s: pl.BlockSpec((1,) + s, lambda bi, i: (bi, 0, 0, 0), pipeline_mode=pl.Buffered(1))
    kern = functools.partial(_dsa_kernel, tq=tq, tk=tk, nk_static=nk, causal=causal, topk=topk, nvalid=nvalid)
    return pl.pallas_call(
        kern,
        grid=(b, nq),
        in_specs=[qspec(512), qspec(512), qspec(512), qspec(8),
                  kspec((nk, tk, 512)), kspec((nk, 512, tk)), kspec((nk, tk, 256)),
                  _const_spec((2, HA, tk, tq)), _const_spec((tk, tq)), _const_spec((tk, tk))],
        out_specs=qspec(512),
        out_shape=jax.ShapeDtypeStruct((g, 512, tq), BF16),
        scratch_shapes=[pltpu.VMEM((nk, tk, tq), I32), pltpu.VMEM((8, tq), F32), pltpu.VMEM((8, tq), F32),
                        pltpu.VMEM((512, tq), F32), pltpu.VMEM((8, tq), F32), pltpu.VMEM((HA, tk, tq), F32)],
        compiler_params=_cparams(("arbitrary", "arbitrary")),
        name="dsa_causal" if causal else "dsa_cached",
    )(qT, qihT, qilT, wiT, k4, vT4, ki34, bias, adm, ltri)


def _gdn_kernel(conv_ref, hist_ref, gb_ref, small_ref, s0_ref, cw_ref, alog_ref, dtb_ref, ng_ref,
                ob_ref, sfin_ref, ext_ref, s_ref, *, nvalid):
    c = CHUNK
    n = HB * c

    @pl.when(pl.program_id(1) == 0)
    def _():
        ext_ref[0:8, :] = hist_ref[0]
        s_ref[...] = s0_ref[0]

    ext_ref[8:8 + c, :] = conv_ref[0]
    cb = ext_ref[5:5 + c, :] * cw_ref[0:1, :]
    for j in range(1, CONV_B):
        cb = cb + ext_ref[5 + j:5 + j + c, :] * cw_ref[j:j + 1, :]
    cb = _silu(cb)
    ext_ref[0:8, :] = ext_ref[c:c + 8, :]

    small = small_ref[0]
    rowv = (lax.broadcasted_iota(I32, (c, 1), 0) < nvalid).astype(F32)
    beta_all = _sigmoid(small) * rowv
    sp = small + dtb_ref[...]
    g_all = -jnp.exp(alog_ref[...]) * (jnp.maximum(sp, 0.0) + jnp.log(1.0 + jnp.exp(-jnp.abs(sp)))) * rowv

    def stack(fn):
        return jnp.concatenate([fn(h) for h in range(HB)], axis=0)

    def l2n(x):
        return x * lax.rsqrt(jnp.sum(x * x, axis=-1, keepdims=True) + EPS)

    q = stack(lambda h: l2n(cb[:, h * DK:(h + 1) * DK]) * (DK ** -0.5))
    k = stack(lambda h: l2n(cb[:, WB + h * DK:WB + (h + 1) * DK]) * rowv)
    v = stack(lambda h: cb[:, 2 * WB + h * DV:2 * WB + (h + 1) * DV] * rowv)
    beta = stack(lambda h: beta_all[:, _L_BB + h:_L_BB + h + 1])
    g = stack(lambda h: g_all[:, _L_AB + h:_L_AB + h + 1])

    ri = lax.broadcasted_iota(I32, (n, n), 0)
    ci = lax.broadcasted_iota(I32, (n, n), 1)
    same = (ri // c) == (ci // c)
    tri = same & (ci <= ri)
    strict = same & (ci < ri)
    eye = (ri == ci).astype(F32)

    gcum = _dot3(tri.astype(F32), jnp.broadcast_to(g, (n, 128)))
    gcum_row = gcum.T[0:1, :]
    gcum_col = gcum[:, 0:1]
    decay = jnp.where(tri, jnp.exp(jnp.where(tri, gcum_col - gcum_row, 0.0)), 0.0)
    kb = k * beta
    kbf = k.astype(BF16)
    nt = (((1,), (1,)), ((), ()))
    kk = lax.dot_general(kb.astype(BF16), kbf, nt, preferred_element_type=F32)
    nmat = jnp.where(strict, kk * decay, 0.0)
    inv = eye - nmat
    pw = nmat
    for _ in range(int(np.log2(c)) - 1):
        pw = _dot3(pw, pw)
        inv = inv + _dot3(inv, pw)
    rhs = jnp.concatenate([v * beta, kb * jnp.exp(gcum_col)], axis=1)
    sol = _dot3(inv, rhs)
    u, w = sol[:, :DV], sol[:, DV:]
    attn = jnp.where(tri, lax.dot_general(q.astype(BF16), kbf, nt, preferred_element_type=F32) * decay, 0.0)

    qg = (q * jnp.exp(gcum_col)).astype(BF16)
    wb = w.astype(BF16)
    v_new = stack(lambda h: u[h * c:(h + 1) * c] - _dot(wb[h * c:(h + 1) * c], s_ref[h].astype(BF16)))
    vnb = v_new.astype(BF16)
    attnb = attn.astype(BF16)
    ng = ng_ref[...]
    tn = (((0,), (0,)), ((), ()))
    for h in range(HB):
        rows = slice(h * c, (h + 1) * c)
        s_old = s_ref[h]
        o = _dot(qg[rows], s_old.astype(BF16)) + _dot(attnb[rows], vnb)
        g_last = gcum_col[(h + 1) * c - 1:(h + 1) * c, :]
        kd = (k[rows] * jnp.exp(g_last - gcum_col[rows])).astype(BF16)
        s_ref[h] = s_old * jnp.exp(g_last) + lax.dot_general(kd, vnb[rows], tn, preferred_element_type=F32)
        gate = gb_ref[0, :, h * DV:(h + 1) * DV]
        ob_ref[0, :, h * DV:(h + 1) * DV] = (_rms(o, ng) * _silu(gate)).astype(BF16)
    sfin_ref[0] = s_ref[...]


def _gdn(conv_in, hist, gb, small, s0, conv_w, a_log, dt_bias, norm_gdn, nvalid):
    b, t, _ = conv_in.shape
    assert t % CHUNK == 0
    alog = jnp.zeros((1, 128), F32).at[0, _L_AB:_L_AB + HB].set(a_log)
    dtb = jnp.zeros((1, 128), F32).at[0, _L_AB:_L_AB + HB].set(dt_bias)
    row = lambda c_: pl.BlockSpec((1, CHUNK, c_), lambda bi, ti: (bi, ti, 0))
    per_b = lambda s: pl.BlockSpec((1,) + s, lambda bi, ti: (bi,) + (0,) * len(s))
    return pl.pallas_call(
        functools.partial(_gdn_kernel, nvalid=nvalid),
        grid=(b, t // CHUNK),
        in_specs=[row(C_CONV_B), per_b((8, C_CONV_B)), row(WB), row(128), per_b((HB, DK, DV)),
                  _const_spec((CONV_B, C_CONV_B)), _const_spec((1, 128)), _const_spec((1, 128)),
                  _const_spec((1, DV))],
        out_specs=[row(WB), per_b((HB, DK, DV))],
        out_shape=[jax.ShapeDtypeStruct((b, t, WB), BF16), jax.ShapeDtypeStruct((b, HB, DK, DV), F32)],
        scratch_shapes=[pltpu.VMEM((8 + CHUNK + 8, C_CONV_B), F32), pltpu.VMEM((HB, DK, DV), F32)],
        compiler_params=_cparams(("arbitrary", "arbitrary")),
        name="gdn",
    )(conv_in, hist, gb, small, s0, conv_w, alog, dtb, norm_gdn.reshape(1, DV))


def _merge_kernel(x_ref, oa_ref, ob_ref, ga_ref, gbr_ref, wa_ref, wb_ref, wo_ref, nf_ref, x1_ref, h2_ref):
    ya = _dot(oa_ref[...], wa_ref[...])
    yb = _dot(ob_ref[...], wb_ref[...])
    mix = _sigmoid(ga_ref[...]) * ya + _sigmoid(gbr_ref[...]) * yb
    x1 = x_ref[...] + _dot(mix.astype(BF16), wo_ref[...])
    x1_ref[...] = x1
    h2_ref[...] = _rms(x1, nf_ref[...]).astype(BF16)


def _merge(x2d, oa, ob, ga, gbr, w_proj_a, w_proj_b, w_out, norm_ffn, tm):
    n = x2d.shape[0]
    row = lambda c: pl.BlockSpec((tm, c), lambda i: (i, 0))
    return pl.pallas_call(
        _merge_kernel,
        grid=(n // tm,),
        in_specs=[row(D_MODEL), row(WA), row(WB), row(D_MODEL), row(D_MODEL),
                  _const_spec((WA, D_MODEL)), _const_spec((WB, D_MODEL)), _const_spec((D_MODEL, D_MODEL)),
                  _const_spec((1, D_MODEL))],
        out_specs=[row(D_MODEL), row(D_MODEL)],
        out_shape=[jax.ShapeDtypeStruct((n, D_MODEL), F32), jax.ShapeDtypeStruct((n, D_MODEL), BF16)],
        compiler_params=_cparams(("arbitrary",)),
        name="merge",
    )(x2d, oa, ob, ga, gbr, w_proj_a.astype(BF16), w_proj_b.astype(BF16), w_out.astype(BF16),
      norm_ffn.reshape(1, D_MODEL))


def _ffn_kernel(x1_ref, h2_ref, p_ref, hist_ref, wup_ref, cw_ref, wdn_ref, npl_ref, wpg_ref, wple_ref, nfin_ref,
                y_ref, tail_ref, ext_ref, *, tm):
    @pl.when(pl.program_id(1) == 0)
    def _():
        ext_ref[0:8, :] = hist_ref[0]

    h2 = h2_ref[0]
    ext_ref[8:8 + tm, :] = _dot(h2, wup_ref[:, 0:D_FF])
    u_val = _dot(h2, wup_ref[:, D_FF:2 * D_FF])
    cv = ext_ref[6:6 + tm, :] * cw_ref[0:1, :]
    for j in range(1, CONV_F):
        cv = cv + ext_ref[6 + j:6 + j + tm, :] * cw_ref[j:j + 1, :]
    tail = ext_ref[tm:tm + 8, :]
    ext_ref[0:8, :] = tail
    tail_ref[0] = tail
    act = 0.5 * cv * (1.0 + jnp.tanh(0.7978845608028654 * (cv + 0.044715 * (cv * cv * cv))))
    x2 = x1_ref[0] + _dot((act * u_val).astype(BF16), wdn_ref[...])
    gate = _sigmoid(_dot(_rms(x2, npl_ref[...]).astype(BF16), wpg_ref[...]))
    x3 = x2 + gate * _dot(p_ref[0].astype(BF16), wple_ref[...])
    y_ref[0] = _rms(x3, nfin_ref[...])


def _ffn_ple(x1, h2, p, hist, w_up, conv_ffn, w_down, norm_ple, w_ple_gate, w_ple, norm_final, tm):
    b, t, _ = x1.shape
    assert t % tm == 0 and tm >= 8
    row = lambda c: pl.BlockSpec((1, tm, c), lambda bi, ti: (bi, ti, 0))
    per_b = pl.BlockSpec((1, 8, D_FF), lambda bi, ti: (bi, 0, 0))
    return pl.pallas_call(
        functools.partial(_ffn_kernel, tm=tm),
        grid=(b, t // tm),
        in_specs=[row(D_MODEL), row(D_MODEL), row(D_PLE), per_b,
                  _const_spec((D_MODEL, 2 * D_FF)), _const_spec((CONV_F, D_FF)), _const_spec((D_FF, D_MODEL)),
                  _const_spec((1, D_MODEL)), _const_spec((D_MODEL, D_MODEL)), _const_spec((D_PLE, D_MODEL)),
                  _const_spec((1, D_MODEL))],
        out_specs=[row(D_MODEL), per_b],
        out_shape=[jax.ShapeDtypeStruct((b, t, D_MODEL), F32), jax.ShapeDtypeStruct((b, 8, D_FF), F32)],
        scratch_shapes=[pltpu.VMEM((8 + tm, D_FF), F32)],
        compiler_params=_cparams(("arbitrary", "arbitrary")),
        name="ffn_ple",
    )(x1, h2, p, hist, w_up.astype(BF16), conv_ffn, w_down.astype(BF16), norm_ple.reshape(1, D_MODEL),
      w_ple_gate.astype(BF16), w_ple.astype(BF16), norm_final.reshape(1, D_MODEL))


def _pad_hist(hist, rows=8):
    b, r, c = hist.shape
    return jnp.concatenate([jnp.zeros((b, rows - r, c), hist.dtype), hist], axis=1)


def _layer(x, p, past_k, past_v, past_kidx, s_gdn, conv_b_hist, ffn_hist, wts, *, tm, tq):
    (norm_mix, w_in, conv_b, a_log, dt_bias, norm_gdn, w_proj_a, w_proj_b, w_out, norm_ffn, w_up, conv_ffn,
     w_down, norm_ple, w_ple, w_ple_gate, rel_bias, norm_final) = wts
    b, t, _ = x.shape
    n = b * t
    past = past_k.shape[1]
    topk = min(TOPK_MAX, (past + t) // 4)
    x2d = x.reshape(n, D_MODEL)
    (qa, ka, va, kbf, vbf, conv_in, gb, ga, gbr, qih, qil, small, ki3) = _in_proj(x2d, norm_mix, w_in, min(tm, n))

    if past == 0:
        assert t % tq == 0 and tq % CHUNK == 0 and tq >= topk
        nq, tk, lanes = t // tq, tq, tq
        to_q = lambda a: jnp.swapaxes(a.reshape(b * nq, tq, a.shape[-1]), 1, 2)
        k4 = kbf.reshape(b, nq, tk, WA)
        vT4 = jnp.swapaxes(vbf.reshape(b, nq, tk, WA), 2, 3)
        ki34 = ki3.reshape(b, nq, tk, 256)
        kj = np.arange(tk)[:, None]
        adm = ((kj // CHUNK) <= (np.arange(tq)[None, :] // CHUNK)).astype(np.float32)
        nvalid, causal = tq, True
    else:
        tk, lanes, nq = 128, 128, 1
        assert past % tk == 0 and t <= tk and t <= lanes
        padq = lambda a: jnp.pad(a.reshape(b, t, a.shape[-1]), ((0, 0), (0, lanes - t), (0, 0)))
        to_q = lambda a: jnp.swapaxes(padq(a), 1, 2)
        padk = lambda new, old: jnp.concatenate(
            [old, new.reshape(b, t, new.shape[-1]), jnp.zeros((b, tk - t, new.shape[-1]), new.dtype)], axis=1)
        nk = past // tk + 1
        k4 = padk(kbf, past_k.reshape(b, past, WA).astype(BF16)).reshape(b, nk, tk, WA)
        vT4 = jnp.swapaxes(padk(vbf, past_v.reshape(b, past, WA).astype(BF16)).reshape(b, nk, tk, WA), 2, 3)
        pk = past_kidx.astype(F32)
        pkh = pk.astype(BF16)
        pkl = (pk - pkh.astype(F32)).astype(BF16)
        past3 = jnp.concatenate([pkh, pkh, pkl, jnp.zeros_like(pkh)], axis=-1)
        ki34 = padk(ki3, past3).reshape(b, nk, tk, 256)
        adm = np.broadcast_to(np.arange(tk)[:, None] < t, (tk, lanes)).astype(np.float32)
        nvalid, causal = t, False
    bias = _bias_tables(rel_bias, tk, lanes, nvalid)
    oT = _dsa(to_q(qa), to_q(qih), to_q(qil), to_q(small[:, _L_WI:_L_WI + H_IDX]), k4, vT4, ki34, bias,
              jnp.asarray(adm), nq=nq, causal=causal, topk=topk, nvalid=nvalid)
    oa = jnp.swapaxes(oT, 1, 2)[:, :min(lanes, t) if past else lanes].reshape(n, WA)

    tp = -(-t // CHUNK) * CHUNK
    padt = lambda a: jnp.pad(a.reshape(b, t, a.shape[-1]), ((0, 0), (0, tp - t), (0, 0)))
    ob, s_new = _gdn(padt(conv_in), _pad_hist(conv_b_hist), padt(gb), padt(small), s_gdn, conv_b, a_log, dt_bias,
                     norm_gdn, nvalid=min(t, CHUNK))
    ob = ob[:, :t].reshape(n, WB)
    new_conv_b = jnp.concatenate([conv_b_hist, conv_in.reshape(b, t, C_CONV_B)], axis=1)[:, t:]

    x1, h2 = _merge(x2d, oa, ob, ga, gbr, w_proj_a, w_proj_b, w_out, norm_ffn, min(tm, n))
    tmf = min(tm, t)
    y, tail = _ffn_ple(x1.reshape(b, t, D_MODEL), h2.reshape(b, t, D_MODEL), p, _pad_hist(ffn_hist), w_up,
                       conv_ffn, w_down, norm_ple, w_ple_gate, w_ple, norm_final, tmf)
    new_ffn = tail[:, 8 - (CONV_F - 1):]
    return (y, ka.reshape(b, t, HA, DH), va.reshape(b, t, HA, DH), small[:, :D_IDX].reshape(b, t, D_IDX),
            s_new, new_conv_b, new_ffn)


def kernel(x_prompt, x_sample, p_prompt, p_sample, cache_k, cache_v, cache_kidx, state_gdn, state_gdn_conv,
           state_ffn_conv, norm_mix, w_in, conv_b, a_log, dt_bias, norm_gdn, w_proj_a, w_proj_b, w_out, norm_ffn,
           w_up, conv_ffn, w_down, norm_ple, w_ple, w_ple_gate, rel_bias, norm_final):
    assert norm_mix.shape[0] == 1
    bp = x_prompt.shape[0]
    dt = x_prompt.dtype
    wts = (norm_mix[0], w_in[0], conv_b[0], a_log[0], dt_bias[0], norm_gdn[0], w_proj_a[0], w_proj_b[0], w_out[0],
           norm_ffn[0], w_up[0], conv_ffn[0], w_down[0], norm_ple[0], w_ple[0], w_ple_gate[0], rel_bias, norm_final)
    outs_p = _layer(x_prompt, p_prompt[0], jnp.zeros((bp, 0, HA, DH), dt), jnp.zeros((bp, 0, HA, DH), dt),
                    jnp.zeros((bp, 0, D_IDX), dt), jnp.zeros((bp, HB, DK, DV), dt),
                    jnp.zeros((bp, CONV_B - 1, C_CONV_B), dt), jnp.zeros((bp, CONV_F - 1, D_FF), dt),
                    wts, tm=256, tq=256)
    outs_s = _layer(x_sample, p_sample[0], cache_k[0], cache_v[0], cache_kidx[0], state_gdn[0],
                    state_gdn_conv[0], state_ffn_conv[0], wts, tm=256, tq=256)
    yp, ys = outs_p[0], outs_s[0]
    return (yp, ys) + tuple(a[None] for a in outs_p[1:]) + tuple(a[None] for a in outs_s[1:])
```

```python
import functools

import numpy as np
import jax
import jax.numpy as jnp
from jax import lax
from jax.experimental import pallas as pl
from jax.experimental.pallas import tpu as pltpu

F32 = jnp.float32
BF16 = jnp.bfloat16
I32 = jnp.int32

D_MODEL = 1024
CHUNK = 64
HA, DH = 8, 64
H_IDX, D_IDX = 8, 64
TOPK_MAX = 256
NUM_BUCKETS, MAX_DISTANCE = 32, 128
HB, DK, DV = 4, 128, 128
CONV_B = 4
D_FF = 2816
CONV_F = 3
D_PLE = 256
EPS = 1e-6
NEG = -1e30
WA = HA * DH
WB = HB * DK
C_CONV_B = 3 * WB
INT_MIN = -2 ** 31
LOG2E = 1.4426950408889634

_O_QA, _O_KA, _O_VA, _O_QI, _O_KI, _O_WI = 0, 512, 1024, 1536, 2048, 2112
_O_QB, _O_GB, _O_BB, _O_AB, _O_GA, _O_GBR = 2120, 3656, 4168, 4172, 4176, 5200
_L_WI, _L_BB, _L_AB = 64, 72, 76

VMEM_LIMIT = 56 * 1024 * 1024


def _cparams(sem):
    return pltpu.CompilerParams(dimension_semantics=sem, vmem_limit_bytes=VMEM_LIMIT)


def _const_spec(shape):
    nd = len(shape)
    return pl.BlockSpec(shape, lambda *_: (0,) * nd, pipeline_mode=pl.Buffered(1))


def _rms(x, g):
    return x * lax.rsqrt(jnp.mean(x * x, axis=-1, keepdims=True) + EPS) * g


def _split(x):
    hi = x.astype(BF16)
    lo = (x - hi.astype(F32)).astype(BF16)
    return hi, lo


def _dot(a, b):
    return jnp.dot(a, b, preferred_element_type=F32)


def _dot3(a, b):
    ah, al = _split(a)
    bh, bl = _split(b)
    return _dot(ah, bh) + (_dot(al, bh) + _dot(ah, bl))


def _sigmoid(x):
    return 1.0 / (1.0 + jnp.exp(-x))


def _silu(x):
    return x * _sigmoid(x)


def _in_proj_kernel(x_ref, g_ref, wm_ref, wih_ref, wil_ref, ph_ref, plo_ref, sc_ref,
                    qa_ref, ka_ref, va_ref, kbf_ref, vbf_ref, conv_ref, gb_ref, ga_ref, gbr_ref,
                    qih_ref, qil_ref, small_ref, ki3_ref):
    h = _rms(x_ref[...], g_ref[...])
    hh, hl = _split(h)

    def main(lo, hi):
        return _dot(hh, wm_ref[:, lo:hi])

    qa_ref[...] = main(0, 512).astype(BF16)
    ka = main(512, 1024)
    ka_ref[...] = ka
    kbf_ref[...] = ka.astype(BF16)
    va = main(1024, 1536)
    va_ref[...] = va
    vbf_ref[...] = va.astype(BF16)
    conv_ref[...] = main(1536, 3072)
    gb_ref[...] = main(3072, 3584)
    ga_ref[...] = main(3584, 4608)
    gbr_ref[...] = main(4608, 5632)

    def idx(lo, hi):
        wh = wih_ref[:, lo:hi]
        return _dot(hh, wh) + (_dot(hl, wh) + _dot(hh, wil_ref[:, lo:hi]))

    qi = idx(0, 512)
    qh, ql = _split(qi)
    qih_ref[...] = qh
    qil_ref[...] = ql
    small = idx(512, 640) * sc_ref[...]
    small_ref[...] = small
    sh, sl = _split(small)
    ki3_ref[...] = (_dot(sh, ph_ref[...]) + _dot(sl, plo_ref[...])).astype(BF16)


def _in_proj(x2d, norm_mix, w_in, tm):
    n = x2d.shape[0]
    assert n % tm == 0
    w = w_in
    wm = jnp.concatenate([w[:, _O_QA:_O_KA] * (DH ** -0.5 * LOG2E), w[:, _O_KA:_O_QI],
                          w[:, _O_QB:_O_BB], w[:, _O_GA:]], axis=1).astype(BF16)
    wi = jnp.concatenate([w[:, _O_QI:_O_KI] * (D_IDX ** -0.5), w[:, _O_KI:_O_QB],
                          w[:, _O_BB:_O_GA], jnp.zeros((D_MODEL, 48), F32)], axis=1)
    wih = wi.astype(BF16)
    wil = (wi - wih.astype(F32)).astype(BF16)
    ph = np.zeros((128, 256), np.float32)
    plo = np.zeros((128, 256), np.float32)
    for c in range(64):
        ph[c, c] = 1.0
        ph[c, 64 + c] = 1.0
        plo[c, 128 + c] = 1.0
    sc = np.ones((1, 128), np.float32)
    sc[0, _L_WI:_L_WI + H_IDX] = H_IDX ** -0.5
    row = lambda c: pl.BlockSpec((tm, c), lambda i: (i, 0))
    out_cols = [(512, BF16), (512, F32), (512, F32), (512, BF16), (512, BF16), (1536, F32), (512, F32),
                (1024, F32), (1024, F32), (512, BF16), (512, BF16), (128, F32), (256, BF16)]
    return pl.pallas_call(
        _in_proj_kernel,
        grid=(n // tm,),
        in_specs=[row(D_MODEL), _const_spec((1, D_MODEL)), _const_spec((D_MODEL, 5632)),
                  _const_spec((D_MODEL, 640)), _const_spec((D_MODEL, 640)),
                  _const_spec((128, 256)), _const_spec((128, 256)), _const_spec((1, 128))],
        out_specs=[row(c) for c, _ in out_cols],
        out_shape=[jax.ShapeDtypeStruct((n, c), d) for c, d in out_cols],
        compiler_params=_cparams(("arbitrary",)),
        name="in_proj",
    )(x2d, norm_mix.reshape(1, D_MODEL), wm, wih, wil, jnp.asarray(ph, BF16), jnp.asarray(plo, BF16),
      jnp.asarray(sc))


def _dsa_kernel(qT_ref, qihT_ref, qilT_ref, wiT_ref, k_ref, vT_ref, ki3_ref, bias_ref, adm_ref, ltri_ref,
                o_ref, key_ref, m_ref, l_ref, acc_ref, c_ref, lg_ref, *, tq, tk, nk_static, causal, topk, nvalid):
    nk = (pl.program_id(1) + 1) if causal else nk_static
    wiT = wiT_ref[0]
    adm = adm_ref[...] > 0.5

    zeros64 = jnp.zeros((64, tq), BF16)
    q3 = []
    for h in range(H_IDX):
        hi = qihT_ref[0, h * 64:(h + 1) * 64, :]
        lo = qilT_ref[0, h * 64:(h + 1) * 64, :]
        q3.append(jnp.concatenate([hi, lo, hi, zeros64], axis=0))

    def score_tile(j, is_last):
        kt = ki3_ref[0, j]
        acc = None
        for h in range(H_IDX):
            t = jnp.maximum(_dot(kt, q3[h]), 0.0) * wiT[h:h + 1, :]
            acc = t if acc is None else acc + t
        if is_last:
            acc = jnp.where(adm, acc, NEG)
        bits = pltpu.bitcast(acc, I32)
        bits = jnp.where(bits == INT_MIN, 0, bits)
        key_ref[j] = bits ^ ((bits >> 31) & 0x7FFFFFFF)

    def score_body(j, c):
        score_tile(j, False)
        return c

    lax.fori_loop(0, nk - 1, score_body, 0)
    score_tile(nk - 1, True)

    def count(cand, strict):
        def body(j, acc):
            kk = key_ref[j]
            hit = (kk > cand) if strict else (kk >= cand)
            return acc + jnp.where(hit, 1, 0).astype(I32).reshape(tk // 8, 8, tq).sum(axis=0)

        acc = lax.fori_loop(0, nk, body, jnp.zeros((8, tq), I32))
        return acc.sum(axis=0, keepdims=True)

    def bisect(it, tu):
        cand_u = tu | jnp.left_shift(jnp.int32(1), 31 - it)
        return jnp.where(count(cand_u ^ INT_MIN, False) >= topk, cand_u, tu)

    thr = lax.fori_loop(0, 32, bisect, jnp.zeros((1, tq), I32)) ^ INT_MIN
    lane_ok = lax.broadcasted_iota(I32, (1, tq), 1) < nvalid
    excess = jnp.max(jnp.where(lane_ok, count(thr, False) - topk, 0))

    m_ref[...] = jnp.full(m_ref.shape, NEG, F32)
    l_ref[...] = jnp.zeros(l_ref.shape, F32)
    acc_ref[...] = jnp.zeros(acc_ref.shape, F32)
    c_ref[...] = jnp.zeros(c_ref.shape, F32)
    qm = []
    for h in range(HA):
        qh = qT_ref[0, h * 64:(h + 1) * 64, :]
        qm.append(jnp.concatenate([qh, zeros64] if h % 2 == 0 else [zeros64, qh], axis=0))

    def attn_tile(j, kind, need):
        kk = key_ref[j]
        if need is not None:
            eq = kk == thr
            eqb = jnp.where(eq, 1.0, 0.0).astype(BF16)
            rank = _dot(ltri_ref[...], eqb) + c_ref[0:1, :]
            sel = (kk > thr) | (eq & (rank <= need))
            c_ref[0:1, :] = c_ref[0:1, :] + jnp.sum(eqb.astype(F32), axis=0, keepdims=True)
        else:
            sel = kk >= thr
        if kind == 2:
            sel = sel & adm
        mx = []
        for h in range(HA):
            pr = h // 2
            lg = _dot(k_ref[0, j, :, pr * 128:(pr + 1) * 128], qm[h])
            if kind >= 1:
                lg = lg + bias_ref[kind - 1, h]
            lg = jnp.where(sel, lg, NEG)
            lg_ref[h] = lg
            mx.append(jnp.max(lg, axis=0, keepdims=True))
        m_old = m_ref[...]
        m_new = jnp.maximum(m_old, jnp.concatenate(mx, axis=0))
        alpha = jnp.exp2(m_old - m_new)
        m_ref[...] = m_new
        ps = []
        for h in range(HA):
            p = jnp.exp2(lg_ref[h] - m_new[h:h + 1, :])
            ps.append(jnp.sum(p, axis=0, keepdims=True))
            pv = _dot(vT_ref[0, j, h * 64:(h + 1) * 64, :], p.astype(BF16))
            acc_ref[h * 64:(h + 1) * 64, :] = alpha[h:h + 1, :] * acc_ref[h * 64:(h + 1) * 64, :] + pv
        l_ref[...] = alpha * l_ref[...] + jnp.concatenate(ps, axis=0)

    def attend(need):
        def far(j, c):
            attn_tile(j, 0, need)
            return c

        lax.fori_loop(0, nk - 2, far, 0)
        if causal:
            @pl.when(nk >= 2)
            def _():
                attn_tile(nk - 2, 1, need)
        elif nk_static >= 2:
            attn_tile(nk - 2, 1, need)
        attn_tile(nk - 1, 2, need)

    @pl.when(excess > 0)
    def _():
        attend((topk - count(thr, True)).astype(F32))

    @pl.when(excess <= 0)
    def _():
        attend(None)

    for h in range(HA):
        o_ref[0, h * 64:(h + 1) * 64, :] = (acc_ref[h * 64:(h + 1) * 64, :] / l_ref[h:h + 1, :]).astype(BF16)


def _t5_bucket_np(rel):
    half = NUM_BUCKETS // 2
    max_exact = half // 2
    out = np.zeros(rel.shape, np.int64)
    flat_rel = rel.reshape(-1)
    flat = out.reshape(-1)
    for a in range(flat_rel.size):
        r = int(flat_rel[a])
        n = abs(r)
        b = n if n < max_exact else min(half - 1, (n * n).bit_length() + 1)
        flat[a] = b + (half if r > 0 else 0)
    return out


def _bias_tables(rel_bias, tk, tq, tq_valid):
    kj = np.arange(tk)[:, None]
    t = np.minimum(np.arange(tq), tq_valid - 1)[None, :]
    rel_to_bucket = _t5_bucket_np(np.arange(-2 * tk - tq, tk + 1))
    lut = lambda rel: rel_to_bucket[rel + 2 * tk + tq]
    idx = np.stack([lut(kj - tk - t), lut(kj - t)], axis=0)
    tab = rel_bias.astype(F32) * LOG2E
    far = tab[NUM_BUCKETS // 2 - 1]
    onehot = jnp.asarray(idx[..., None] == np.arange(NUM_BUCKETS), F32)
    return jnp.einsum("ktqb,bh->khtq", onehot, tab - far, precision=lax.Precision.HIGHEST)


def _dsa(qT, qihT, qilT, wiT, k4, vT4, ki34, bias, adm, *, nq, causal, topk, nvalid):
    g, _, tq = qT.shape
    b, nk, tk, _ = k4.shape
    assert g == b * nq and tk % 8 == 0
    ltri = jnp.asarray(np.tril(np.ones((tk, tk), np.float32)), BF16)
    qspec = lambda r: pl.BlockSpec((1, r, tq), lambda bi, i: (bi * nq + i, 0, 0))
    kspec = lambda s: pl.BlockSpec((1,) + s, lambda bi, i: (bi, 0, 0, 0), pipeline_mode=pl.Buffered(1))
    kern = functools.partial(_dsa_kernel, tq=tq, tk=tk, nk_static=nk, causal=causal, topk=topk, nvalid=nvalid)
    return pl.pallas_call(
        kern,
        grid=(b, nq),
        in_specs=[qspec(512), qspec(512), qspec(512), qspec(8),
                  kspec((nk, tk, 512)), kspec((nk, 512, tk)), kspec((nk, tk, 256)),
                  _const_spec((2, HA, tk, tq)), _const_spec((tk, tq)), _const_spec((tk, tk))],
        out_specs=qspec(512),
        out_shape=jax.ShapeDtypeStruct((g, 512, tq), BF16),
        scratch_shapes=[pltpu.VMEM((nk, tk, tq), I32), pltpu.VMEM((8, tq), F32), pltpu.VMEM((8, tq), F32),
                        pltpu.VMEM((512, tq), F32), pltpu.VMEM((8, tq), F32), pltpu.VMEM((HA, tk, tq), F32)],
        compiler_params=_cparams(("arbitrary", "arbitrary")),
        name="dsa_causal" if causal else "dsa_cached",
    )(qT, qihT, qilT, wiT, k4, vT4, ki34, bias, adm, ltri)


def _gdn_pre_kernel(conv_ref, prev_ref, hist_ref, small_ref, cw_ref, alog_ref, dtb_ref,
                    wm_ref, um_ref, qe_ref, oi_ref, e_ref, ext_ref, *, rows, nvalid):
    c = CHUNK
    n = HB * c
    ext_ref[0:8, :] = jnp.where(pl.program_id(1) == 0, hist_ref[0], prev_ref[0])
    ext_ref[8:8 + rows, :] = conv_ref[0]
    cb = ext_ref[5:5 + rows, :] * cw_ref[0:1, :]
    for j in range(1, CONV_B):
        cb = cb + ext_ref[5 + j:5 + j + rows, :] * cw_ref[j:j + 1, :]
    cb = _silu(cb)

    small = small_ref[0]
    pos = lax.broadcasted_iota(I32, (rows, 1), 0) % c
    rowv = (pos < nvalid).astype(F32)
    beta_all = _sigmoid(small) * rowv
    sp = small + dtb_ref[...]
    g_all = -jnp.exp(alog_ref[...]) * (jnp.maximum(sp, 0.0) + jnp.log(1.0 + jnp.exp(-jnp.abs(sp)))) * rowv

    ri = lax.broadcasted_iota(I32, (n, n), 0)
    ci = lax.broadcasted_iota(I32, (n, n), 1)
    same = (ri // c) == (ci // c)
    tri_b = (same & (ci <= ri)).astype(BF16)
    bd_f = same.astype(F32)
    bd_b = same.astype(BF16)
    wr = lax.broadcasted_iota(I32, (c, n), 0)
    wl = lax.broadcasted_iota(I32, (c, n), 1)
    grp = wl // c
    tri_w = (wl % c) <= wr
    strict_w = (wl % c) < wr
    eye_w = ((wl % c) == wr).astype(F32)
    nt = (((1,), (1,)), ((), ()))
    tn = (((0,), (0,)), ((), ()))

    def l2n(x):
        return x * lax.rsqrt(jnp.sum(x * x, axis=-1, keepdims=True) + EPS)

    def to_wide(full):
        out = jnp.where(grp == 0, full[0:c, :], 0.0)
        for h in range(1, HB):
            out = out + jnp.where(grp == h, full[h * c:(h + 1) * c, :], 0.0)
        return out

    def tile4(x):
        return jnp.concatenate([x] * HB, axis=0)

    def chunk(k0):
        def stack(fn):
            return jnp.concatenate([fn(h) for h in range(HB)], axis=0)

        rv = rowv[k0:k0 + c]
        q = stack(lambda h: l2n(cb[k0:k0 + c, h * DK:(h + 1) * DK]) * (DK ** -0.5))
        k = stack(lambda h: l2n(cb[k0:k0 + c, WB + h * DK:WB + (h + 1) * DK]) * rv)
        v = stack(lambda h: cb[k0:k0 + c, 2 * WB + h * DV:2 * WB + (h + 1) * DV] * rv)
        beta = stack(lambda h: beta_all[k0:k0 + c, _L_BB + h:_L_BB + h + 1])
        g = stack(lambda h: g_all[k0:k0 + c, _L_AB + h:_L_AB + h + 1])
        yield

        gh, gl = _split(jnp.broadcast_to(g, (n, 128)))
        gcum = _dot(tri_b, gh) + _dot(tri_b, gl)
        yield
        gcum_row = gcum.T[0:1, :]
        gcum_col = gcum[:, 0:1]
        col_w = to_wide(jnp.concatenate([gcum, gcum], axis=1))
        decay_w = jnp.where(tri_w, jnp.exp(jnp.where(tri_w, col_w - gcum_row, 0.0)), 0.0)
        kb = k * beta
        kbf = k.astype(BF16)
        kk_w = to_wide(lax.dot_general(kb.astype(BF16), kbf, nt, preferred_element_type=F32))
        qk_w = to_wide(lax.dot_general(q.astype(BF16), kbf, nt, preferred_element_type=F32))
        nmat_w = jnp.where(strict_w, kk_w * decay_w, 0.0)
        attn_w = jnp.where(tri_w, qk_w * decay_w, 0.0)
        yield

        inv_w = eye_w - nmat_w
        ph, pl_ = _split(nmat_w)
        bh, bl = tile4(ph) * bd_b, tile4(pl_) * bd_b
        for _ in range(int(np.log2(c)) - 1):
            pw_w = _dot(ph, bh) + (_dot(pl_, bh) + _dot(ph, bl))
            ph, pl_ = _split(pw_w)
            yield
            bh, bl = tile4(ph) * bd_b, tile4(pl_) * bd_b
            ih, il = _split(inv_w)
            inv_w = inv_w + (_dot(ih, bh) + (_dot(il, bh) + _dot(ih, bl)))
            yield
        rhs = jnp.concatenate([v * beta, kb * jnp.exp(gcum_col)], axis=1)
        sol = _dot3(tile4(inv_w) * bd_f, rhs)
        yield
        solb = sol.astype(BF16)
        aw = _dot((tile4(attn_w) * bd_f).astype(BF16), solb)
        yield
        ck = k0 // c
        oi_ref[0, ck] = aw[:, :DV]
        qe_ref[0, ck] = (q * jnp.exp(gcum_col) - aw[:, DV:]).astype(BF16)
        for h in range(HB):
            rs = slice(h * c, (h + 1) * c)
            g_last = gcum[(h + 1) * c - 1:(h + 1) * c, :]
            kd = (k[rs] * jnp.exp(g_last - gcum[rs])).astype(BF16)
            uw = lax.dot_general(kd, solb[rs], tn, preferred_element_type=F32)
            um_ref[0, ck, h] = uw[:, :DV]
            wm_ref[0, ck, h] = uw[:, DV:].astype(BF16)
            e_ref[0, ck, h:h + 1, :] = jnp.exp(g_last)
        e_ref[0, ck, HB:8, :] = jnp.zeros((8 - HB, 128), F32)

    gens = [chunk(k0) for k0 in range(0, rows, c)]
    while gens:
        alive = []
        for gen in gens:
            try:
                next(gen)
                alive.append(gen)
            except StopIteration:
                pass
        gens = alive


def _gdn_scan_kernel(wm_ref, um_ref, qe_ref, oi_ref, e_ref, gb_ref, s0_ref, ng_ref, ob_ref, sfin_ref, s_ref,
                     *, bb, g):
    c = CHUNK

    @pl.when(pl.program_id(1) == 0)
    def _():
        s_ref[...] = s0_ref[...]

    ng = ng_ref[...]
    for b in range(bb):
        for ck in range(g):
            for h in range(HB):
                s = s_ref[b, h]
                sb = s.astype(BF16)
                rs = slice(h * c, (h + 1) * c)
                o = _dot(qe_ref[b, ck, rs, :], sb) + oi_ref[b, ck, rs, :]
                s_ref[b, h] = e_ref[b, ck, h:h + 1, :] * s + (um_ref[b, ck, h] - _dot(wm_ref[b, ck, h], sb))
                gate = gb_ref[b, ck * c:(ck + 1) * c, h * DV:(h + 1) * DV]
                ob_ref[b, ck * c:(ck + 1) * c, h * DV:(h + 1) * DV] = (_rms(o, ng) * _silu(gate)).astype(BF16)
    sfin_ref[...] = s_ref[...]


def _gdn(conv_in, hist, gb, small, s0, conv_w, a_log, dt_bias, norm_gdn, nvalid):
    b, t, _ = conv_in.shape
    assert t % CHUNK == 0
    rows = 256 if t % 256 == 0 else CHUNK
    nc, cps = t // CHUNK, rows // CHUNK
    n = HB * CHUNK
    alog = jnp.zeros((1, 128), F32).at[0, _L_AB:_L_AB + HB].set(a_log)
    dtb = jnp.zeros((1, 128), F32).at[0, _L_AB:_L_AB + HB].set(dt_bias)
    row = lambda c_: pl.BlockSpec((1, rows, c_), lambda bi, ti: (bi, ti, 0))
    prev = pl.BlockSpec((1, 8, C_CONV_B), lambda bi, ti: (bi, jnp.maximum(ti * (rows // 8) - 1, 0), 0))
    per_b = lambda s: pl.BlockSpec((1,) + s, lambda bi, ti: (bi,) + (0,) * len(s))
    chunked = lambda s: pl.BlockSpec((1, cps) + s, lambda bi, ti: (bi, ti) + (0,) * len(s))
    wm, um, qe, oi, e = pl.pallas_call(
        functools.partial(_gdn_pre_kernel, rows=rows, nvalid=nvalid),
        grid=(b, t // rows),
        in_specs=[row(C_CONV_B), prev, per_b((8, C_CONV_B)), row(128),
                  _const_spec((CONV_B, C_CONV_B)), _const_spec((1, 128)), _const_spec((1, 128))],
        out_specs=[chunked((HB, DK, DV)), chunked((HB, DK, DV)), chunked((n, DK)), chunked((n, DV)),
                   chunked((8, 128))],
        out_shape=[jax.ShapeDtypeStruct((b, nc, HB, DK, DV), BF16), jax.ShapeDtypeStruct((b, nc, HB, DK, DV), F32),
                   jax.ShapeDtypeStruct((b, nc, n, DK), BF16), jax.ShapeDtypeStruct((b, nc, n, DV), F32),
                   jax.ShapeDtypeStruct((b, nc, 8, 128), F32)],
        scratch_shapes=[pltpu.VMEM((8 + rows, C_CONV_B), F32)],
        compiler_params=_cparams(("arbitrary", "arbitrary")),
        name="gdn_pre",
    )(conv_in, conv_in, hist, small, conv_w, alog, dtb)

    bb = 2 if b % 2 == 0 else 1
    g = 2 if nc % 2 == 0 else 1
    blk = lambda s: pl.BlockSpec((bb, g) + s, lambda bi, ci: (bi, ci) + (0,) * len(s))
    rowb = pl.BlockSpec((bb, g * CHUNK, WB), lambda bi, ci: (bi, ci, 0))
    state = pl.BlockSpec((bb, HB, DK, DV), lambda bi, ci: (bi, 0, 0, 0))
    return pl.pallas_call(
        functools.partial(_gdn_scan_kernel, bb=bb, g=g),
        grid=(b // bb, nc // g),
        in_specs=[blk((HB, DK, DV)), blk((HB, DK, DV)), blk((n, DK)), blk((n, DV)), blk((8, 128)), rowb, state,
                  _const_spec((1, DV))],
        out_specs=[rowb, state],
        out_shape=[jax.ShapeDtypeStruct((b, t, WB), BF16), jax.ShapeDtypeStruct((b, HB, DK, DV), F32)],
        scratch_shapes=[pltpu.VMEM((bb, HB, DK, DV), F32)],
        compiler_params=_cparams(("arbitrary", "arbitrary")),
        name="gdn_scan",
    )(wm, um, qe, oi, e, gb, s0, norm_gdn.reshape(1, DV))


def _merge_kernel(x_ref, oa_ref, ob_ref, ga_ref, gbr_ref, wa_ref, wb_ref, wo_ref, nf_ref, x1_ref, h2_ref):
    ya = _dot(oa_ref[...], wa_ref[...])
    yb = _dot(ob_ref[...], wb_ref[...])
    mix = _sigmoid(ga_ref[...]) * ya + _sigmoid(gbr_ref[...]) * yb
    x1 = x_ref[...] + _dot(mix.astype(BF16), wo_ref[...])
    x1_ref[...] = x1
    h2_ref[...] = _rms(x1, nf_ref[...]).astype(BF16)


def _merge(x2d, oa, ob, ga, gbr, w_proj_a, w_proj_b, w_out, norm_ffn, tm):
    n = x2d.shape[0]
    row = lambda c: pl.BlockSpec((tm, c), lambda i: (i, 0))
    return pl.pallas_call(
        _merge_kernel,
        grid=(n // tm,),
        in_specs=[row(D_MODEL), row(WA), row(WB), row(D_MODEL), row(D_MODEL),
                  _const_spec((WA, D_MODEL)), _const_spec((WB, D_MODEL)), _const_spec((D_MODEL, D_MODEL)),
                  _const_spec((1, D_MODEL))],
        out_specs=[row(D_MODEL), row(D_MODEL)],
        out_shape=[jax.ShapeDtypeStruct((n, D_MODEL), F32), jax.ShapeDtypeStruct((n, D_MODEL), BF16)],
        compiler_params=_cparams(("arbitrary",)),
        name="merge",
    )(x2d, oa, ob, ga, gbr, w_proj_a.astype(BF16), w_proj_b.astype(BF16), w_out.astype(BF16),
      norm_ffn.reshape(1, D_MODEL))


def _ffn_kernel(x1_ref, h2_ref, p_ref, hist_ref, wup_ref, cw_ref, wdn_ref, npl_ref, wpg_ref, wple_ref, nfin_ref,
                y_ref, tail_ref, ext_ref, *, tm):
    @pl.when(pl.program_id(1) == 0)
    def _():
        ext_ref[0:8, :] = hist_ref[0]

    h2 = h2_ref[0]
    ext_ref[8:8 + tm, :] = _dot(h2, wup_ref[:, 0:D_FF])
    u_val = _dot(h2, wup_ref[:, D_FF:2 * D_FF])
    cv = ext_ref[6:6 + tm, :] * cw_ref[0:1, :]
    for j in range(1, CONV_F):
        cv = cv + ext_ref[6 + j:6 + j + tm, :] * cw_ref[j:j + 1, :]
    tail = ext_ref[tm:tm + 8, :]
    ext_ref[0:8, :] = tail
    tail_ref[0] = tail
    act = 0.5 * cv * (1.0 + jnp.tanh(0.7978845608028654 * (cv + 0.044715 * (cv * cv * cv))))
    x2 = x1_ref[0] + _dot((act * u_val).astype(BF16), wdn_ref[...])
    gate = _sigmoid(_dot(_rms(x2, npl_ref[...]).astype(BF16), wpg_ref[...]))
    x3 = x2 + gate * _dot(p_ref[0].astype(BF16), wple_ref[...])
    y_ref[0] = _rms(x3, nfin_ref[...])


def _ffn_ple(x1, h2, p, hist, w_up, conv_ffn, w_down, norm_ple, w_ple_gate, w_ple, norm_final, tm):
    b, t, _ = x1.shape
    assert t % tm == 0 and tm >= 8
    row = lambda c: pl.BlockSpec((1, tm, c), lambda bi, ti: (bi, ti, 0))
    per_b = pl.BlockSpec((1, 8, D_FF), lambda bi, ti: (bi, 0, 0))
    return pl.pallas_call(
        functools.partial(_ffn_kernel, tm=tm),
        grid=(b, t // tm),
        in_specs=[row(D_MODEL), row(D_MODEL), row(D_PLE), per_b,
                  _const_spec((D_MODEL, 2 * D_FF)), _const_spec((CONV_F, D_FF)), _const_spec((D_FF, D_MODEL)),
                  _const_spec((1, D_MODEL)), _const_spec((D_MODEL, D_MODEL)), _const_spec((D_PLE, D_MODEL)),
                  _const_spec((1, D_MODEL))],
        out_specs=[row(D_MODEL), per_b],
        out_shape=[jax.ShapeDtypeStruct((b, t, D_MODEL), F32), jax.ShapeDtypeStruct((b, 8, D_FF), F32)],
        scratch_shapes=[pltpu.VMEM((8 + tm, D_FF), F32)],
        compiler_params=_cparams(("arbitrary", "arbitrary")),
        name="ffn_ple",
    )(x1, h2, p, hist, w_up.astype(BF16), conv_ffn, w_down.astype(BF16), norm_ple.reshape(1, D_MODEL),
      w_ple_gate.astype(BF16), w_ple.astype(BF16), norm_final.reshape(1, D_MODEL))


def _pad_hist(hist, rows=8):
    b, r, c = hist.shape
    return jnp.concatenate([jnp.zeros((b, rows - r, c), hist.dtype), hist], axis=1)


def _layer(x, p, past_k, past_v, past_kidx, s_gdn, conv_b_hist, ffn_hist, wts, *, tm, tq):
    (norm_mix, w_in, conv_b, a_log, dt_bias, norm_gdn, w_proj_a, w_proj_b, w_out, norm_ffn, w_up, conv_ffn,
     w_down, norm_ple, w_ple, w_ple_gate, rel_bias, norm_final) = wts
    b, t, _ = x.shape
    n = b * t
    past = past_k.shape[1]
    topk = min(TOPK_MAX, (past + t) // 4)
    x2d = x.reshape(n, D_MODEL)
    (qa, ka, va, kbf, vbf, conv_in, gb, ga, gbr, qih, qil, small, ki3) = _in_proj(x2d, norm_mix, w_in, min(tm, n))

    if past == 0:
        assert t % tq == 0 and tq % CHUNK == 0 and tq >= topk
        nq, tk, lanes = t // tq, tq, tq
        to_q = lambda a: jnp.swapaxes(a.reshape(b * nq, tq, a.shape[-1]), 1, 2)
        k4 = kbf.reshape(b, nq, tk, WA)
        vT4 = jnp.swapaxes(vbf.reshape(b, nq, tk, WA), 2, 3)
        ki34 = ki3.reshape(b, nq, tk, 256)
        kj = np.arange(tk)[:, None]
        adm = ((kj // CHUNK) <= (np.arange(tq)[None, :] // CHUNK)).astype(np.float32)
        nvalid, causal = tq, True
    else:
        tk, lanes, nq = 128, 128, 1
        assert past % tk == 0 and t <= tk and t <= lanes
        padq = lambda a: jnp.pad(a.reshape(b, t, a.shape[-1]), ((0, 0), (0, lanes - t), (0, 0)))
        to_q = lambda a: jnp.swapaxes(padq(a), 1, 2)
        padk = lambda new, old: jnp.concatenate(
            [old, new.reshape(b, t, new.shape[-1]), jnp.zeros((b, tk - t, new.shape[-1]), new.dtype)], axis=1)
        nk = past // tk + 1
        k4 = padk(kbf, past_k.reshape(b, past, WA).astype(BF16)).reshape(b, nk, tk, WA)
        vT4 = jnp.swapaxes(padk(vbf, past_v.reshape(b, past, WA).astype(BF16)).reshape(b, nk, tk, WA), 2, 3)
        pk = past_kidx.astype(F32)
        pkh = pk.astype(BF16)
        pkl = (pk - pkh.astype(F32)).astype(BF16)
        past3 = jnp.concatenate([pkh, pkh, pkl, jnp.zeros_like(pkh)], axis=-1)
        ki34 = padk(ki3, past3).reshape(b, nk, tk, 256)
        adm = np.broadcast_to(np.arange(tk)[:, None] < t, (tk, lanes)).astype(np.float32)
        nvalid, causal = t, False
    bias = _bias_tables(rel_bias, tk, lanes, nvalid)
    oT = _dsa(to_q(qa), to_q(qih), to_q(qil), to_q(small[:, _L_WI:_L_WI + H_IDX]), k4, vT4, ki34, bias,
              jnp.asarray(adm), nq=nq, causal=causal, topk=topk, nvalid=nvalid)
    oa = jnp.swapaxes(oT, 1, 2)[:, :min(lanes, t) if past else lanes].reshape(n, WA)

    tp = -(-t // CHUNK) * CHUNK
    padt = lambda a: jnp.pad(a.reshape(b, t, a.shape[-1]), ((0, 0), (0, tp - t), (0, 0)))
    ob, s_new = _gdn(padt(conv_in), _pad_hist(conv_b_hist), padt(gb), padt(small), s_gdn, conv_b, a_log, dt_bias,
                     norm_gdn, nvalid=min(t, CHUNK))
    ob = ob[:, :t].reshape(n, WB)
    new_conv_b = jnp.concatenate([conv_b_hist, conv_in.reshape(b, t, C_CONV_B)], axis=1)[:, t:]

    x1, h2 = _merge(x2d, oa, ob, ga, gbr, w_proj_a, w_proj_b, w_out, norm_ffn, min(tm, n))
    tmf = min(tm, t)
    y, tail = _ffn_ple(x1.reshape(b, t, D_MODEL), h2.reshape(b, t, D_MODEL), p, _pad_hist(ffn_hist), w_up,
                       conv_ffn, w_down, norm_ple, w_ple_gate, w_ple, norm_final, tmf)
    new_ffn = tail[:, 8 - (CONV_F - 1):]
    return (y, ka.reshape(b, t, HA, DH), va.reshape(b, t, HA, DH), small[:, :D_IDX].reshape(b, t, D_IDX),
            s_new, new_conv_b, new_ffn)


def kernel(x_prompt, x_sample, p_prompt, p_sample, cache_k, cache_v, cache_kidx, state_gdn, state_gdn_conv,
           state_ffn_conv, norm_mix, w_in, conv_b, a_log, dt_bias, norm_gdn, w_proj_a, w_proj_b, w_out, norm_ffn,
           w_up, conv_ffn, w_down, norm_ple, w_ple, w_ple_gate, rel_bias, norm_final):
    assert norm_mix.shape[0] == 1
    bp = x_prompt.shape[0]
    dt = x_prompt.dtype
    wts = (norm_mix[0], w_in[0], conv_b[0], a_log[0], dt_bias[0], norm_gdn[0], w_proj_a[0], w_proj_b[0], w_out[0],
           norm_ffn[0], w_up[0], conv_ffn[0], w_down[0], norm_ple[0], w_ple[0], w_ple_gate[0], rel_bias, norm_final)
    outs_p = _layer(x_prompt, p_prompt[0], jnp.zeros((bp, 0, HA, DH), dt), jnp.zeros((bp, 0, HA, DH), dt),
                    jnp.zeros((bp, 0, D_IDX), dt), jnp.zeros((bp, HB, DK, DV), dt),
                    jnp.zeros((bp, CONV_B - 1, C_CONV_B), dt), jnp.zeros((bp, CONV_F - 1, D_FF), dt),
                    wts, tm=256, tq=256)
    outs_s = _layer(x_sample, p_sample[0], cache_k[0], cache_v[0], cache_kidx[0], state_gdn[0],
                    state_gdn_conv[0], state_ffn_conv[0], wts, tm=256, tq=256)
    yp, ys = outs_p[0], outs_s[0]
    return (yp, ys) + tuple(a[None] for a in outs_p[1:]) + tuple(a[None] for a in outs_s[1:])
```

```python
import functools

import numpy as np
import jax
import jax.numpy as jnp
from jax import lax
from jax.experimental import pallas as pl
from jax.experimental.pallas import tpu as pltpu

F32 = jnp.float32
BF16 = jnp.bfloat16
I32 = jnp.int32

D_MODEL = 1024
CHUNK = 64
HA, DH = 8, 64
H_IDX, D_IDX = 8, 64
TOPK_MAX = 256
NUM_BUCKETS, MAX_DISTANCE = 32, 128
HB, DK, DV = 4, 128, 128
CONV_B = 4
D_FF = 2816
CONV_F = 3
D_PLE = 256
EPS = 1e-6
NEG = -1e30
WA = HA * DH
WB = HB * DK
C_CONV_B = 3 * WB
INT_MIN = -2 ** 31
LOG2E = 1.4426950408889634

_O_QA, _O_KA, _O_VA, _O_QI, _O_KI, _O_WI = 0, 512, 1024, 1536, 2048, 2112
_O_QB, _O_GB, _O_BB, _O_AB, _O_GA, _O_GBR = 2120, 3656, 4168, 4172, 4176, 5200
_L_WI, _L_BB, _L_AB = 64, 72, 76

VMEM_LIMIT = 56 * 1024 * 1024


def _cparams(sem):
    return pltpu.CompilerParams(dimension_semantics=sem, vmem_limit_bytes=VMEM_LIMIT)


def _const_spec(shape):
    nd = len(shape)
    return pl.BlockSpec(shape, lambda *_: (0,) * nd, pipeline_mode=pl.Buffered(1))


def _rms(x, g):
    return x * lax.rsqrt(jnp.mean(x * x, axis=-1, keepdims=True) + EPS) * g


def _split(x):
    hi = x.astype(BF16)
    lo = (x - hi.astype(F32)).astype(BF16)
    return hi, lo


def _dot(a, b):
    return jnp.dot(a, b, preferred_element_type=F32)


def _dot3(a, b):
    ah, al = _split(a)
    bh, bl = _split(b)
    return _dot(ah, bh) + (_dot(al, bh) + _dot(ah, bl))


def _sigmoid(x):
    return 1.0 / (1.0 + jnp.exp(-x))


def _silu(x):
    return x * _sigmoid(x)


def _transpose32(a):
    a = list(a)
    j, m = 16, 0x0000FFFF
    while j:
        k = 0
        while k < 32:
            t = (a[k] ^ lax.shift_right_logical(a[k + j], jnp.int32(j))) & jnp.int32(m - (1 << 32) if m >> 31 else m)
            a[k] = a[k] ^ t
            a[k + j] = a[k + j] ^ (t << j)
            k = (k + j + 1) & ~j
        j >>= 1
        m = (m ^ (m << j)) & 0xFFFFFFFF
    return a


def _in_proj_kernel(x_ref, g_ref, wm_ref, wih_ref, wil_ref, ph_ref, plo_ref, sc_ref,
                    qa_ref, ka_ref, va_ref, kbf_ref, vbf_ref, conv_ref, gb_ref, ga_ref, gbr_ref,
                    qih_ref, qil_ref, small_ref, ki3_ref):
    h = _rms(x_ref[...], g_ref[...])
    hh, hl = _split(h)

    def main(lo, hi):
        return _dot(hh, wm_ref[:, lo:hi])

    qa_ref[...] = main(0, 512).astype(BF16)
    ka = main(512, 1024)
    ka_ref[...] = ka
    kbf_ref[...] = ka.astype(BF16)
    va = main(1024, 1536)
    va_ref[...] = va
    vbf_ref[...] = va.astype(BF16)
    conv_ref[...] = main(1536, 3072)
    gb_ref[...] = main(3072, 3584)
    ga_ref[...] = main(3584, 4608)
    gbr_ref[...] = main(4608, 5632)

    def idx(lo, hi):
        wh = wih_ref[:, lo:hi]
        return _dot(hh, wh) + (_dot(hl, wh) + _dot(hh, wil_ref[:, lo:hi]))

    qi = idx(0, 512)
    qh, ql = _split(qi)
    qih_ref[...] = qh
    qil_ref[...] = ql
    small = idx(512, 640) * sc_ref[...]
    small_ref[...] = small
    sh, sl = _split(small)
    ki3_ref[...] = (_dot(sh, ph_ref[...]) + _dot(sl, plo_ref[...])).astype(BF16)


def _in_proj(x2d, norm_mix, w_in, tm):
    n = x2d.shape[0]
    assert n % tm == 0
    w = w_in
    wm = jnp.concatenate([w[:, _O_QA:_O_KA] * (DH ** -0.5 * LOG2E), w[:, _O_KA:_O_QI],
                          w[:, _O_QB:_O_BB], w[:, _O_GA:]], axis=1).astype(BF16)
    wi = jnp.concatenate([w[:, _O_QI:_O_KI] * (D_IDX ** -0.5), w[:, _O_KI:_O_QB],
                          w[:, _O_BB:_O_GA], jnp.zeros((D_MODEL, 48), F32)], axis=1)
    wih = wi.astype(BF16)
    wil = (wi - wih.astype(F32)).astype(BF16)
    ph = np.zeros((128, 256), np.float32)
    plo = np.zeros((128, 256), np.float32)
    for c in range(64):
        ph[c, c] = 1.0
        ph[c, 64 + c] = 1.0
        plo[c, 128 + c] = 1.0
    sc = np.ones((1, 128), np.float32)
    sc[0, _L_WI:_L_WI + H_IDX] = H_IDX ** -0.5
    row = lambda c: pl.BlockSpec((tm, c), lambda i: (i, 0))
    out_cols = [(512, BF16), (512, F32), (512, F32), (512, BF16), (512, BF16), (1536, F32), (512, F32),
                (1024, F32), (1024, F32), (512, BF16), (512, BF16), (128, F32), (256, BF16)]
    return pl.pallas_call(
        _in_proj_kernel,
        grid=(n // tm,),
        in_specs=[row(D_MODEL), _const_spec((1, D_MODEL)), _const_spec((D_MODEL, 5632)),
                  _const_spec((D_MODEL, 640)), _const_spec((D_MODEL, 640)),
                  _const_spec((128, 256)), _const_spec((128, 256)), _const_spec((1, 128))],
        out_specs=[row(c) for c, _ in out_cols],
        out_shape=[jax.ShapeDtypeStruct((n, c), d) for c, d in out_cols],
        compiler_params=_cparams(("arbitrary",)),
        name="in_proj",
    )(x2d, norm_mix.reshape(1, D_MODEL), wm, wih, wil, jnp.asarray(ph, BF16), jnp.asarray(plo, BF16),
      jnp.asarray(sc))


def _dsa_kernel(qT_ref, qihT_ref, qilT_ref, wiT_ref, k_ref, vT_ref, ki3_ref, bias_ref, adm_ref, ltri_ref,
                o_ref, key_ref, m_ref, l_ref, acc_ref, c_ref, lg_ref, pln_ref, eq_ref, *, tq, tk, nk_static, causal, topk, nvalid):
    nk = (pl.program_id(1) + 1) if causal else nk_static
    wiT = wiT_ref[0]
    adm = adm_ref[...] > 0.5

    zeros64 = jnp.zeros((64, tq), BF16)
    q3 = []
    for h in range(H_IDX):
        hi = qihT_ref[0, h * 64:(h + 1) * 64, :]
        lo = qilT_ref[0, h * 64:(h + 1) * 64, :]
        q3.append(jnp.concatenate([hi, lo, hi, zeros64], axis=0))

    def score_tile(j, is_last):
        kt = ki3_ref[0, j]
        acc = None
        for h in range(H_IDX):
            t = jnp.maximum(_dot(kt, q3[h]), 0.0) * wiT[h:h + 1, :]
            acc = t if acc is None else acc + t
        if is_last:
            acc = jnp.where(adm, acc, NEG)
        bits = pltpu.bitcast(acc, I32)
        bits = jnp.where(bits == INT_MIN, 0, bits)
        key = bits ^ ((bits >> 31) & 0x7FFFFFFF)
        key_ref[j] = key
        u3 = (key ^ INT_MIN).reshape(tk // 8, 8, tq)
        planes = _transpose32([u3[r] for r in range(32)])
        ones = jnp.full((8, tq), -1, I32)
        pln_ref[j, 0] = ones
        for b in range(32):
            pln_ref[j, b + 1] = planes[b]
        eq_ref[j] = ones

    def score_body(j, c):
        score_tile(j, False)
        return c

    lax.fori_loop(0, nk - 1, score_body, 0)
    score_tile(nk - 1, True)

    for d in range(3):
        pln_ref[nk + d] = jnp.zeros((33, 8, tq), I32)
        eq_ref[nk + d] = jnp.zeros((8, tq), I32)

    def sweep(it, carry):
        tu, n_gt, acc_i = carry
        acc_prev = acc_i != 0

        def body(jq, cnt):
            for d in range(4):
                j = jq * 4 + d
                e = eq_ref[j]
                tp = e & pln_ref[j, it]
                e = jnp.where(acc_prev, tp, e ^ tp)
                eq_ref[j] = e
                cnt = cnt + lax.population_count(e & pln_ref[j, it + 1])
            return cnt

        cnt = lax.fori_loop(0, (nk + 3) // 4, body, jnp.zeros((8, tq), I32))
        cnt = n_gt + cnt.sum(axis=0, keepdims=True)
        acc = cnt >= topk
        tu = jnp.where(acc, tu | jnp.left_shift(jnp.int32(1), 31 - it), tu)
        return tu, jnp.where(acc, n_gt, cnt), acc.astype(I32)

    zero_row = jnp.zeros((1, tq), I32)
    tu, n_gt, acc_i = lax.fori_loop(0, 32, sweep, (zero_row, zero_row, zero_row + 1))
    acc_last = acc_i != 0
    thr = tu ^ INT_MIN

    def last(j, cnt):
        e = eq_ref[j]
        tp = e & pln_ref[j, 32]
        return cnt + lax.population_count(jnp.where(acc_last, tp, e ^ tp))

    n_eq = lax.fori_loop(0, nk, last, jnp.zeros((8, tq), I32)).sum(axis=0, keepdims=True)
    lane_ok = lax.broadcasted_iota(I32, (1, tq), 1) < nvalid
    excess = jnp.max(jnp.where(lane_ok, n_gt + n_eq - topk, 0))
    need = (topk - n_gt).astype(F32)

    m_ref[...] = jnp.full(m_ref.shape, NEG, F32)
    l_ref[...] = jnp.zeros(l_ref.shape, F32)
    acc_ref[...] = jnp.zeros(acc_ref.shape, F32)
    c_ref[...] = jnp.zeros(c_ref.shape, F32)
    qm = []
    for h in range(HA):
        qh = qT_ref[0, h * 64:(h + 1) * 64, :]
        qm.append(jnp.concatenate([qh, zeros64] if h % 2 == 0 else [zeros64, qh], axis=0))

    def attn_tile(j, kind, need):
        kk = key_ref[j]
        if need is not None:
            eq = kk == thr
            eqb = jnp.where(eq, 1.0, 0.0).astype(BF16)
            rank = _dot(ltri_ref[...], eqb) + c_ref[0:1, :]
            sel = (kk > thr) | (eq & (rank <= need))
            c_ref[0:1, :] = c_ref[0:1, :] + jnp.sum(eqb.astype(F32), axis=0, keepdims=True)
        else:
            sel = kk >= thr
        if kind == 2:
            sel = sel & adm
        mx = []
        for h in range(HA):
            pr = h // 2
            lg = _dot(k_ref[0, j, :, pr * 128:(pr + 1) * 128], qm[h])
            if kind >= 1:
                lg = lg + bias_ref[kind - 1, h]
            lg = jnp.where(sel, lg, NEG)
            lg_ref[h] = lg
            mx.append(jnp.max(lg, axis=0, keepdims=True))
        m_old = m_ref[...]
        m_new = jnp.maximum(m_old, jnp.concatenate(mx, axis=0))
        alpha = jnp.exp2(m_old - m_new)
        m_ref[...] = m_new
        ps = []
        for h in range(HA):
            p = jnp.exp2(lg_ref[h] - m_new[h:h + 1, :])
            ps.append(jnp.sum(p, axis=0, keepdims=True))
            pv = _dot(vT_ref[0, j, h * 64:(h + 1) * 64, :], p.astype(BF16))
            acc_ref[h * 64:(h + 1) * 64, :] = alpha[h:h + 1, :] * acc_ref[h * 64:(h + 1) * 64, :] + pv
        l_ref[...] = alpha * l_ref[...] + jnp.concatenate(ps, axis=0)

    def attend(need):
        def far(j, c):
            attn_tile(j, 0, need)
            return c

        lax.fori_loop(0, nk - 2, far, 0)
        if causal:
            @pl.when(nk >= 2)
            def _():
                attn_tile(nk - 2, 1, need)
        elif nk_static >= 2:
            attn_tile(nk - 2, 1, need)
        attn_tile(nk - 1, 2, need)

    @pl.when(excess > 0)
    def _():
        attend(need)

    @pl.when(excess <= 0)
    def _():
        attend(None)

    for h in range(HA):
        o_ref[0, h * 64:(h + 1) * 64, :] = (acc_ref[h * 64:(h + 1) * 64, :] / l_ref[h:h + 1, :]).astype(BF16)


def _t5_bucket_np(rel):
    half = NUM_BUCKETS // 2
    max_exact = half // 2
    out = np.zeros(rel.shape, np.int64)
    flat_rel = rel.reshape(-1)
    flat = out.reshape(-1)
    for a in range(flat_rel.size):
        r = int(flat_rel[a])
        n = abs(r)
        b = n if n < max_exact else min(half - 1, (n * n).bit_length() + 1)
        flat[a] = b + (half if r > 0 else 0)
    return out


def _bias_tables(rel_bias, tk, tq, tq_valid):
    kj = np.arange(tk)[:, None]
    t = np.minimum(np.arange(tq), tq_valid - 1)[None, :]
    rel_to_bucket = _t5_bucket_np(np.arange(-2 * tk - tq, tk + 1))
    lut = lambda rel: rel_to_bucket[rel + 2 * tk + tq]
    idx = np.stack([lut(kj - tk - t), lut(kj - t)], axis=0)
    tab = rel_bias.astype(F32) * LOG2E
    far = tab[NUM_BUCKETS // 2 - 1]
    onehot = jnp.asarray(idx[..., None] == np.arange(NUM_BUCKETS), F32)
    return jnp.einsum("ktqb,bh->khtq", onehot, tab - far, precision=lax.Precision.HIGHEST)


def _dsa(qT, qihT, qilT, wiT, k4, vT4, ki34, bias, adm, *, nq, causal, topk, nvalid):
    g, _, tq = qT.shape
    b, nk, tk, _ = k4.shape
    assert g == b * nq and tk == 256
    ltri = jnp.asarray(np.tril(np.ones((tk, tk), np.float32)), BF16)
    qspec = lambda r: pl.BlockSpec((1, r, tq), lambda bi, i: (bi * nq + i, 0, 0))
    kspec = lambda s: pl.BlockSpec((1,) + s, lambda bi, i: (bi, 0, 0, 0), pipeline_mode=pl.Buffered(1))
    kern = functools.partial(_dsa_kernel, tq=tq, tk=tk, nk_static=nk, causal=causal, topk=topk, nvalid=nvalid)
    return pl.pallas_call(
        kern,
        grid=(b, nq),
        in_specs=[qspec(512), qspec(512), qspec(512), qspec(8),
                  kspec((nk, tk, 512)), kspec((nk, 512, tk)), kspec((nk, tk, 256)),
                  _const_spec((2, HA, tk, tq)), _const_spec((tk, tq)), _const_spec((tk, tk))],
        out_specs=qspec(512),
        out_shape=jax.ShapeDtypeStruct((g, 512, tq), BF16),
        scratch_shapes=[pltpu.VMEM((nk, tk, tq), I32), pltpu.VMEM((8, tq), F32), pltpu.VMEM((8, tq), F32),
                        pltpu.VMEM((512, tq), F32), pltpu.VMEM((8, tq), F32), pltpu.VMEM((HA, tk, tq), F32),
                        pltpu.VMEM((nk + 3, 33, 8, tq), I32), pltpu.VMEM((nk + 3, 8, tq), I32)],
        compiler_params=_cparams(("arbitrary", "arbitrary")),
        name="dsa_causal" if causal else "dsa_cached",
    )(qT, qihT, qilT, wiT, k4, vT4, ki34, bias, adm, ltri)


def _gdn_pre_kernel(conv_ref, prev_ref, hist_ref, small_ref, cw_ref, alog_ref, dtb_ref,
                    wm_ref, um_ref, qe_ref, oi_ref, e_ref, ext_ref, *, rows, nvalid):
    c = CHUNK
    n = HB * c
    ext_ref[0:8, :] = jnp.where(pl.program_id(1) == 0, hist_ref[0], prev_ref[0])
    ext_ref[8:8 + rows, :] = conv_ref[0]
    cb = ext_ref[5:5 + rows, :] * cw_ref[0:1, :]
    for j in range(1, CONV_B):
        cb = cb + ext_ref[5 + j:5 + j + rows, :] * cw_ref[j:j + 1, :]
    cb = _silu(cb)

    small = small_ref[0]
    pos = lax.broadcasted_iota(I32, (rows, 1), 0) % c
    rowv = (pos < nvalid).astype(F32)
    beta_all = _sigmoid(small) * rowv
    sp = small + dtb_ref[...]
    g_all = -jnp.exp(alog_ref[...]) * (jnp.maximum(sp, 0.0) + jnp.log(1.0 + jnp.exp(-jnp.abs(sp)))) * rowv

    ri = lax.broadcasted_iota(I32, (n, n), 0)
    ci = lax.broadcasted_iota(I32, (n, n), 1)
    same = (ri // c) == (ci // c)
    tri_b = (same & (ci <= ri)).astype(BF16)
    bd_f = same.astype(F32)
    bd_b = same.astype(BF16)
    wr = lax.broadcasted_iota(I32, (c, n), 0)
    wl = lax.broadcasted_iota(I32, (c, n), 1)
    grp = wl // c
    tri_w = (wl % c) <= wr
    strict_w = (wl % c) < wr
    eye_w = ((wl % c) == wr).astype(F32)
    nt = (((1,), (1,)), ((), ()))
    tn = (((0,), (0,)), ((), ()))

    def l2n(x):
        return x * lax.rsqrt(jnp.sum(x * x, axis=-1, keepdims=True) + EPS)

    def to_wide(full):
        out = jnp.where(grp == 0, full[0:c, :], 0.0)
        for h in range(1, HB):
            out = out + jnp.where(grp == h, full[h * c:(h + 1) * c, :], 0.0)
        return out

    def tile4(x):
        return jnp.concatenate([x] * HB, axis=0)

    def chunk(k0):
        def stack(fn):
            return jnp.concatenate([fn(h) for h in range(HB)], axis=0)

        rv = rowv[k0:k0 + c]
        q = stack(lambda h: l2n(cb[k0:k0 + c, h * DK:(h + 1) * DK]) * (DK ** -0.5))
        k = stack(lambda h: l2n(cb[k0:k0 + c, WB + h * DK:WB + (h + 1) * DK]) * rv)
        v = stack(lambda h: cb[k0:k0 + c, 2 * WB + h * DV:2 * WB + (h + 1) * DV] * rv)
        beta = stack(lambda h: beta_all[k0:k0 + c, _L_BB + h:_L_BB + h + 1])
        g = stack(lambda h: g_all[k0:k0 + c, _L_AB + h:_L_AB + h + 1])
        yield

        gh, gl = _split(jnp.broadcast_to(g, (n, 128)))
        gcum = _dot(tri_b, gh) + _dot(tri_b, gl)
        yield
        gcum_row = gcum.T[0:1, :]
        gcum_col = gcum[:, 0:1]
        col_w = to_wide(jnp.concatenate([gcum, gcum], axis=1))
        decay_w = jnp.where(tri_w, jnp.exp(jnp.where(tri_w, col_w - gcum_row, 0.0)), 0.0)
        kb = k * beta
        kbf = k.astype(BF16)
        kk_w = to_wide(lax.dot_general(kb.astype(BF16), kbf, nt, preferred_element_type=F32))
        qk_w = to_wide(lax.dot_general(q.astype(BF16), kbf, nt, preferred_element_type=F32))
        nmat_w = jnp.where(strict_w, kk_w * decay_w, 0.0)
        attn_w = jnp.where(tri_w, qk_w * decay_w, 0.0)
        yield

        inv_w = eye_w - nmat_w
        ph, pl_ = _split(nmat_w)
        bh, bl = tile4(ph) * bd_b, tile4(pl_) * bd_b
        for _ in range(int(np.log2(c)) - 1):
            pw_w = _dot(ph, bh) + (_dot(pl_, bh) + _dot(ph, bl))
            ph, pl_ = _split(pw_w)
            yield
            bh, bl = tile4(ph) * bd_b, tile4(pl_) * bd_b
            ih, il = _split(inv_w)
            inv_w = inv_w + (_dot(ih, bh) + (_dot(il, bh) + _dot(ih, bl)))
            yield
        rhs = jnp.concatenate([v * beta, kb * jnp.exp(gcum_col)], axis=1)
        sol = _dot3(tile4(inv_w) * bd_f, rhs)
        yield
        solb = sol.astype(BF16)
        aw = _dot((tile4(attn_w) * bd_f).astype(BF16), solb)
        yield
        ck = k0 // c
        oi_ref[0, ck] = aw[:, :DV]
        qe_ref[0, ck] = (q * jnp.exp(gcum_col) - aw[:, DV:]).astype(BF16)
        for h in range(HB):
            rs = slice(h * c, (h + 1) * c)
            g_last = gcum[(h + 1) * c - 1:(h + 1) * c, :]
            kd = (k[rs] * jnp.exp(g_last - gcum[rs])).astype(BF16)
            uw = lax.dot_general(kd, solb[rs], tn, preferred_element_type=F32)
            um_ref[0, ck, h] = uw[:, :DV]
            wm_ref[0, ck, h] = uw[:, DV:].astype(BF16)
            e_ref[0, ck, h:h + 1, :] = jnp.exp(g_last)
        e_ref[0, ck, HB:8, :] = jnp.zeros((8 - HB, 128), F32)

    gens = [chunk(k0) for k0 in range(0, rows, c)]
    while gens:
        alive = []
        for gen in gens:
            try:
                next(gen)
                alive.append(gen)
            except StopIteration:
                pass
        gens = alive


def _gdn_scan_kernel(wm_ref, um_ref, qe_ref, oi_ref, e_ref, gb_ref, s0_ref, ng_ref, ob_ref, sfin_ref, s_ref,
                     *, bb, g):
    c = CHUNK

    @pl.when(pl.program_id(1) == 0)
    def _():
        s_ref[...] = s0_ref[...]

    ng = ng_ref[...]
    for b in range(bb):
        for ck in range(g):
            for h in range(HB):
                s = s_ref[b, h]
                sb = s.astype(BF16)
                rs = slice(h * c, (h + 1) * c)
                o = _dot(qe_ref[b, ck, rs, :], sb) + oi_ref[b, ck, rs, :]
                s_ref[b, h] = e_ref[b, ck, h:h + 1, :] * s + (um_ref[b, ck, h] - _dot(wm_ref[b, ck, h], sb))
                gate = gb_ref[b, ck * c:(ck + 1) * c, h * DV:(h + 1) * DV]
                ob_ref[b, ck * c:(ck + 1) * c, h * DV:(h + 1) * DV] = (_rms(o, ng) * _silu(gate)).astype(BF16)
    sfin_ref[...] = s_ref[...]


def _gdn(conv_in, hist, gb, small, s0, conv_w, a_log, dt_bias, norm_gdn, nvalid):
    b, t, _ = conv_in.shape
    assert t % CHUNK == 0
    rows = 256 if t % 256 == 0 else CHUNK
    nc, cps = t // CHUNK, rows // CHUNK
    n = HB * CHUNK
    alog = jnp.zeros((1, 128), F32).at[0, _L_AB:_L_AB + HB].set(a_log)
    dtb = jnp.zeros((1, 128), F32).at[0, _L_AB:_L_AB + HB].set(dt_bias)
    row = lambda c_: pl.BlockSpec((1, rows, c_), lambda bi, ti: (bi, ti, 0))
    prev = pl.BlockSpec((1, 8, C_CONV_B), lambda bi, ti: (bi, jnp.maximum(ti * (rows // 8) - 1, 0), 0))
    per_b = lambda s: pl.BlockSpec((1,) + s, lambda bi, ti: (bi,) + (0,) * len(s))
    chunked = lambda s: pl.BlockSpec((1, cps) + s, lambda bi, ti: (bi, ti) + (0,) * len(s))
    wm, um, qe, oi, e = pl.pallas_call(
        functools.partial(_gdn_pre_kernel, rows=rows, nvalid=nvalid),
        grid=(b, t // rows),
        in_specs=[row(C_CONV_B), prev, per_b((8, C_CONV_B)), row(128),
                  _const_spec((CONV_B, C_CONV_B)), _const_spec((1, 128)), _const_spec((1, 128))],
        out_specs=[chunked((HB, DK, DV)), chunked((HB, DK, DV)), chunked((n, DK)), chunked((n, DV)),
                   chunked((8, 128))],
        out_shape=[jax.ShapeDtypeStruct((b, nc, HB, DK, DV), BF16), jax.ShapeDtypeStruct((b, nc, HB, DK, DV), F32),
                   jax.ShapeDtypeStruct((b, nc, n, DK), BF16), jax.ShapeDtypeStruct((b, nc, n, DV), F32),
                   jax.ShapeDtypeStruct((b, nc, 8, 128), F32)],
        scratch_shapes=[pltpu.VMEM((8 + rows, C_CONV_B), F32)],
        compiler_params=_cparams(("arbitrary", "arbitrary")),
        name="gdn_pre",
    )(conv_in, conv_in, hist, small, conv_w, alog, dtb)

    bb = 2 if b % 2 == 0 else 1
    g = 2 if nc % 2 == 0 else 1
    blk = lambda s: pl.BlockSpec((bb, g) + s, lambda bi, ci: (bi, ci) + (0,) * len(s))
    rowb = pl.BlockSpec((bb, g * CHUNK, WB), lambda bi, ci: (bi, ci, 0))
    state = pl.BlockSpec((bb, HB, DK, DV), lambda bi, ci: (bi, 0, 0, 0))
    return pl.pallas_call(
        functools.partial(_gdn_scan_kernel, bb=bb, g=g),
        grid=(b // bb, nc // g),
        in_specs=[blk((HB, DK, DV)), blk((HB, DK, DV)), blk((n, DK)), blk((n, DV)), blk((8, 128)), rowb, state,
                  _const_spec((1, DV))],
        out_specs=[rowb, state],
        out_shape=[jax.ShapeDtypeStruct((b, t, WB), BF16), jax.ShapeDtypeStruct((b, HB, DK, DV), F32)],
        scratch_shapes=[pltpu.VMEM((bb, HB, DK, DV), F32)],
        compiler_params=_cparams(("arbitrary", "arbitrary")),
        name="gdn_scan",
    )(wm, um, qe, oi, e, gb, s0, norm_gdn.reshape(1, DV))


def _merge_kernel(x_ref, oa_ref, ob_ref, ga_ref, gbr_ref, wa_ref, wb_ref, wo_ref, nf_ref, x1_ref, h2_ref):
    ya = _dot(oa_ref[...], wa_ref[...])
    yb = _dot(ob_ref[...], wb_ref[...])
    mix = _sigmoid(ga_ref[...]) * ya + _sigmoid(gbr_ref[...]) * yb
    x1 = x_ref[...] + _dot(mix.astype(BF16), wo_ref[...])
    x1_ref[...] = x1
    h2_ref[...] = _rms(x1, nf_ref[...]).astype(BF16)


def _merge(x2d, oa, ob, ga, gbr, w_proj_a, w_proj_b, w_out, norm_ffn, tm):
    n = x2d.shape[0]
    row = lambda c: pl.BlockSpec((tm, c), lambda i: (i, 0))
    return pl.pallas_call(
        _merge_kernel,
        grid=(n // tm,),
        in_specs=[row(D_MODEL), row(WA), row(WB), row(D_MODEL), row(D_MODEL),
                  _const_spec((WA, D_MODEL)), _const_spec((WB, D_MODEL)), _const_spec((D_MODEL, D_MODEL)),
                  _const_spec((1, D_MODEL))],
        out_specs=[row(D_MODEL), row(D_MODEL)],
        out_shape=[jax.ShapeDtypeStruct((n, D_MODEL), F32), jax.ShapeDtypeStruct((n, D_MODEL), BF16)],
        compiler_params=_cparams(("arbitrary",)),
        name="merge",
    )(x2d, oa, ob, ga, gbr, w_proj_a.astype(BF16), w_proj_b.astype(BF16), w_out.astype(BF16),
      norm_ffn.reshape(1, D_MODEL))


def _ffn_kernel(x1_ref, h2_ref, p_ref, hist_ref, wup_ref, cw_ref, wdn_ref, npl_ref, wpg_ref, wple_ref, nfin_ref,
                y_ref, tail_ref, ext_ref, *, tm):
    @pl.when(pl.program_id(1) == 0)
    def _():
        ext_ref[0:8, :] = hist_ref[0]

    h2 = h2_ref[0]
    ext_ref[8:8 + tm, :] = _dot(h2, wup_ref[:, 0:D_FF])
    u_val = _dot(h2, wup_ref[:, D_FF:2 * D_FF])
    cv = ext_ref[6:6 + tm, :] * cw_ref[0:1, :]
    for j in range(1, CONV_F):
        cv = cv + ext_ref[6 + j:6 + j + tm, :] * cw_ref[j:j + 1, :]
    tail = ext_ref[tm:tm + 8, :]
    ext_ref[0:8, :] = tail
    tail_ref[0] = tail
    act = 0.5 * cv * (1.0 + jnp.tanh(0.7978845608028654 * (cv + 0.044715 * (cv * cv * cv))))
    x2 = x1_ref[0] + _dot((act * u_val).astype(BF16), wdn_ref[...])
    gate = _sigmoid(_dot(_rms(x2, npl_ref[...]).astype(BF16), wpg_ref[...]))
    x3 = x2 + gate * _dot(p_ref[0].astype(BF16), wple_ref[...])
    y_ref[0] = _rms(x3, nfin_ref[...])


def _ffn_ple(x1, h2, p, hist, w_up, conv_ffn, w_down, norm_ple, w_ple_gate, w_ple, norm_final, tm):
    b, t, _ = x1.shape
    assert t % tm == 0 and tm >= 8
    row = lambda c: pl.BlockSpec((1, tm, c), lambda bi, ti: (bi, ti, 0))
    per_b = pl.BlockSpec((1, 8, D_FF), lambda bi, ti: (bi, 0, 0))
    return pl.pallas_call(
        functools.partial(_ffn_kernel, tm=tm),
        grid=(b, t // tm),
        in_specs=[row(D_MODEL), row(D_MODEL), row(D_PLE), per_b,
                  _const_spec((D_MODEL, 2 * D_FF)), _const_spec((CONV_F, D_FF)), _const_spec((D_FF, D_MODEL)),
                  _const_spec((1, D_MODEL)), _const_spec((D_MODEL, D_MODEL)), _const_spec((D_PLE, D_MODEL)),
                  _const_spec((1, D_MODEL))],
        out_specs=[row(D_MODEL), per_b],
        out_shape=[jax.ShapeDtypeStruct((b, t, D_MODEL), F32), jax.ShapeDtypeStruct((b, 8, D_FF), F32)],
        scratch_shapes=[pltpu.VMEM((8 + tm, D_FF), F32)],
        compiler_params=_cparams(("arbitrary", "arbitrary")),
        name="ffn_ple",
    )(x1, h2, p, hist, w_up.astype(BF16), conv_ffn, w_down.astype(BF16), norm_ple.reshape(1, D_MODEL),
      w_ple_gate.astype(BF16), w_ple.astype(BF16), norm_final.reshape(1, D_MODEL))


def _pad_hist(hist, rows=8):
    b, r, c = hist.shape
    return jnp.concatenate([jnp.zeros((b, rows - r, c), hist.dtype), hist], axis=1)


def _layer(x, p, past_k, past_v, past_kidx, s_gdn, conv_b_hist, ffn_hist, wts, *, tm, tq):
    (norm_mix, w_in, conv_b, a_log, dt_bias, norm_gdn, w_proj_a, w_proj_b, w_out, norm_ffn, w_up, conv_ffn,
     w_down, norm_ple, w_ple, w_ple_gate, rel_bias, norm_final) = wts
    b, t, _ = x.shape
    n = b * t
    past = past_k.shape[1]
    topk = min(TOPK_MAX, (past + t) // 4)
    x2d = x.reshape(n, D_MODEL)
    (qa, ka, va, kbf, vbf, conv_in, gb, ga, gbr, qih, qil, small, ki3) = _in_proj(x2d, norm_mix, w_in, min(tm, n))

    if past == 0:
        assert t % tq == 0 and tq % CHUNK == 0 and tq >= topk
        nq, tk, lanes = t // tq, tq, tq
        to_q = lambda a: jnp.swapaxes(a.reshape(b * nq, tq, a.shape[-1]), 1, 2)
        k4 = kbf.reshape(b, nq, tk, WA)
        vT4 = jnp.swapaxes(vbf.reshape(b, nq, tk, WA), 2, 3)
        ki34 = ki3.reshape(b, nq, tk, 256)
        kj = np.arange(tk)[:, None]
        adm = ((kj // CHUNK) <= (np.arange(tq)[None, :] // CHUNK)).astype(np.float32)
        nvalid, causal = tq, True
    else:
        tk, lanes, nq = 256, 128, 1
        assert past % tk == 0 and t <= tk and t <= lanes
        padq = lambda a: jnp.pad(a.reshape(b, t, a.shape[-1]), ((0, 0), (0, lanes - t), (0, 0)))
        to_q = lambda a: jnp.swapaxes(padq(a), 1, 2)
        padk = lambda new, old: jnp.concatenate(
            [old, new.reshape(b, t, new.shape[-1]), jnp.zeros((b, tk - t, new.shape[-1]), new.dtype)], axis=1)
        nk = past // tk + 1
        k4 = padk(kbf, past_k.reshape(b, past, WA).astype(BF16)).reshape(b, nk, tk, WA)
        vT4 = jnp.swapaxes(padk(vbf, past_v.reshape(b, past, WA).astype(BF16)).reshape(b, nk, tk, WA), 2, 3)
        pk = past_kidx.astype(F32)
        pkh = pk.astype(BF16)
        pkl = (pk - pkh.astype(F32)).astype(BF16)
        past3 = jnp.concatenate([pkh, pkh, pkl, jnp.zeros_like(pkh)], axis=-1)
        ki34 = padk(ki3, past3).reshape(b, nk, tk, 256)
        adm = np.broadcast_to(np.arange(tk)[:, None] < t, (tk, lanes)).astype(np.float32)
        nvalid, causal = t, False
    bias = _bias_tables(rel_bias, tk, lanes, nvalid)
    oT = _dsa(to_q(qa), to_q(qih), to_q(qil), to_q(small[:, _L_WI:_L_WI + H_IDX]), k4, vT4, ki34, bias,
              jnp.asarray(adm), nq=nq, causal=causal, topk=topk, nvalid=nvalid)
    oa = jnp.swapaxes(oT, 1, 2)[:, :min(lanes, t) if past else lanes].reshape(n, WA)

    tp = -(-t // CHUNK) * CHUNK
    padt = lambda a: jnp.pad(a.reshape(b, t, a.shape[-1]), ((0, 0), (0, tp - t), (0, 0)))
    ob, s_new = _gdn(padt(conv_in), _pad_hist(conv_b_hist), padt(gb), padt(small), s_gdn, conv_b, a_log, dt_bias,
                     norm_gdn, nvalid=min(t, CHUNK))
    ob = ob[:, :t].reshape(n, WB)
    new_conv_b = jnp.concatenate([conv_b_hist, conv_in.reshape(b, t, C_CONV_B)], axis=1)[:, t:]

    x1, h2 = _merge(x2d, oa, ob, ga, gbr, w_proj_a, w_proj_b, w_out, norm_ffn, min(tm, n))
    tmf = min(tm, t)
    y, tail = _ffn_ple(x1.reshape(b, t, D_MODEL), h2.reshape(b, t, D_MODEL), p, _pad_hist(ffn_hist), w_up,
                       conv_ffn, w_down, norm_ple, w_ple_gate, w_ple, norm_final, tmf)
    new_ffn = tail[:, 8 - (CONV_F - 1):]
    return (y, ka.reshape(b, t, HA, DH), va.reshape(b, t, HA, DH), small[:, :D_IDX].reshape(b, t, D_IDX),
            s_new, new_conv_b, new_ffn)


def kernel(x_prompt, x_sample, p_prompt, p_sample, cache_k, cache_v, cache_kidx, state_gdn, state_gdn_conv,
           state_ffn_conv, norm_mix, w_in, conv_b, a_log, dt_bias, norm_gdn, w_proj_a, w_proj_b, w_out, norm_ffn,
           w_up, conv_ffn, w_down, norm_ple, w_ple, w_ple_gate, rel_bias, norm_final):
    assert norm_mix.shape[0] == 1
    bp = x_prompt.shape[0]
    dt = x_prompt.dtype
    wts = (norm_mix[0], w_in[0], conv_b[0], a_log[0], dt_bias[0], norm_gdn[0], w_proj_a[0], w_proj_b[0], w_out[0],
           norm_ffn[0], w_up[0], conv_ffn[0], w_down[0], norm_ple[0], w_ple[0], w_ple_gate[0], rel_bias, norm_final)
    outs_p = _layer(x_prompt, p_prompt[0], jnp.zeros((bp, 0, HA, DH), dt), jnp.zeros((bp, 0, HA, DH), dt),
                    jnp.zeros((bp, 0, D_IDX), dt), jnp.zeros((bp, HB, DK, DV), dt),
                    jnp.zeros((bp, CONV_B - 1, C_CONV_B), dt), jnp.zeros((bp, CONV_F - 1, D_FF), dt),
                    wts, tm=256, tq=256)
    outs_s = _layer(x_sample, p_sample[0], cache_k[0], cache_v[0], cache_kidx[0], state_gdn[0],
                    state_gdn_conv[0], state_ffn_conv[0], wts, tm=256, tq=256)
    yp, ys = outs_p[0], outs_s[0]
    return (yp, ys) + tuple(a[None] for a in outs_p[1:]) + tuple(a[None] for a in outs_s[1:])
```

```python
import functools

import numpy as np
import jax
import jax.numpy as jnp
from jax import lax
from jax.experimental import pallas as pl
from jax.experimental.pallas import tpu as pltpu

F32 = jnp.float32
BF16 = jnp.bfloat16
I32 = jnp.int32

D_MODEL = 1024
CHUNK = 64
HA, DH = 8, 64
H_IDX, D_IDX = 8, 64
TOPK_MAX = 256
NUM_BUCKETS, MAX_DISTANCE = 32, 128
HB, DK, DV = 4, 128, 128
CONV_B = 4
D_FF = 2816
CONV_F = 3
D_PLE = 256
EPS = 1e-6
NEG = -1e30
WA = HA * DH
WB = HB * DK
C_CONV_B = 3 * WB
INT_MIN = -2 ** 31
LOG2E = 1.4426950408889634

_O_QA, _O_KA, _O_VA, _O_QI, _O_KI, _O_WI = 0, 512, 1024, 1536, 2048, 2112
_O_QB, _O_GB, _O_BB, _O_AB, _O_GA, _O_GBR = 2120, 3656, 4168, 4172, 4176, 5200
_L_WI, _L_BB, _L_AB = 64, 72, 76

VMEM_LIMIT = 56 * 1024 * 1024


def _cparams(sem):
    return pltpu.CompilerParams(dimension_semantics=sem, vmem_limit_bytes=VMEM_LIMIT)


def _const_spec(shape):
    nd = len(shape)
    return pl.BlockSpec(shape, lambda *_: (0,) * nd, pipeline_mode=pl.Buffered(1))


def _rms(x, g):
    return x * lax.rsqrt(jnp.mean(x * x, axis=-1, keepdims=True) + EPS) * g


def _split(x):
    hi = x.astype(BF16)
    lo = (x - hi.astype(F32)).astype(BF16)
    return hi, lo


def _dot(a, b):
    return jnp.dot(a, b, preferred_element_type=F32)


def _dot3(a, b):
    ah, al = _split(a)
    bh, bl = _split(b)
    return _dot(ah, bh) + (_dot(al, bh) + _dot(ah, bl))


def _sigmoid(x):
    return 1.0 / (1.0 + jnp.exp(-x))


def _silu(x):
    return x * _sigmoid(x)


def _transpose32(a):
    a = list(a)
    j, m = 16, 0x0000FFFF
    while j:
        k = 0
        while k < 32:
            t = (a[k] ^ lax.shift_right_logical(a[k + j], jnp.int32(j))) & jnp.int32(m - (1 << 32) if m >> 31 else m)
            a[k] = a[k] ^ t
            a[k + j] = a[k + j] ^ (t << j)
            k = (k + j + 1) & ~j
        j >>= 1
        m = (m ^ (m << j)) & 0xFFFFFFFF
    return a


V_ROWS = DH + 16
_NT = (((1,), (1,)), ((), ()))


def _in_proj_kernel(x_ref, g_ref, wm_ref, wt_ref, wqh_ref, wql_ref, wsh_ref, wsl_ref, wwh_ref, wwl_ref,
                    ph_ref, plo_ref, sc_ref,
                    ka_ref, va_ref, kbf_ref, conv_ref, gb_ref, ga_ref, gbr_ref, small_ref, ki3_ref,
                    qT_ref, qihT_ref, qilT_ref, wiT_ref, vTa_ref):
    tm = x_ref.shape[0]
    h = _rms(x_ref[...], g_ref[...])
    hh, hl = _split(h)

    def main(lo, hi):
        return _dot(hh, wm_ref[:, lo:hi])

    def nt(w, a):
        return lax.dot_general(w, a, _NT, preferred_element_type=F32)

    def nt3(wh_ref, wl_ref):
        return nt(wh_ref[...], hh) + (nt(wh_ref[...], hl) + nt(wl_ref[...], hh))

    ka = main(0, 512)
    ka_ref[...] = ka
    kbf_ref[...] = ka.astype(BF16)
    va_ref[...] = main(512, 1024)
    conv_ref[...] = main(1024, 2560)
    gb_ref[...] = main(2560, 3072)
    ga_ref[...] = main(3072, 4096)
    gbr_ref[...] = main(4096, 5120)

    qT_ref[0] = nt(wt_ref[0:WA, :], hh).astype(BF16)
    vT = nt(wt_ref[WA:2 * WA, :], hh).astype(BF16)
    ones = jnp.ones((V_ROWS - DH, tm), BF16)
    for hd in range(HA):
        vTa_ref[0, hd * V_ROWS:hd * V_ROWS + DH, :] = vT[hd * DH:(hd + 1) * DH]
        vTa_ref[0, hd * V_ROWS + DH:(hd + 1) * V_ROWS, :] = ones
    qh, ql = _split(nt3(wqh_ref, wql_ref))
    qihT_ref[0] = qh
    qilT_ref[0] = ql
    wiT_ref[0] = nt3(wwh_ref, wwl_ref)[0:H_IDX] * (H_IDX ** -0.5)

    small = (_dot(hh, wsh_ref[...]) + (_dot(hl, wsh_ref[...]) + _dot(hh, wsl_ref[...]))) * sc_ref[...]
    small_ref[...] = small
    sh, sl = _split(small)
    ki3_ref[...] = (_dot(sh, ph_ref[...]) + _dot(sl, plo_ref[...])).astype(BF16)


def _in_proj(x2d, norm_mix, w_in, tm):
    n = x2d.shape[0]
    assert n % tm == 0
    g = n // tm
    w = w_in
    wm = jnp.concatenate([w[:, _O_KA:_O_QI], w[:, _O_QB:_O_BB], w[:, _O_GA:]], axis=1).astype(BF16)
    wt = jnp.concatenate([w[:, _O_QA:_O_KA] * (DH ** -0.5 * LOG2E), w[:, _O_VA:_O_QI]], axis=1).T.astype(BF16)
    hilo = lambda a: (a.astype(BF16), (a - a.astype(BF16).astype(F32)).astype(BF16))
    wqh, wql = hilo((w[:, _O_QI:_O_KI] * (D_IDX ** -0.5)).T)
    wsh, wsl = hilo(jnp.concatenate([w[:, _O_KI:_O_QB], w[:, _O_BB:_O_GA], jnp.zeros((D_MODEL, 48), F32)], axis=1))
    wwh, wwl = hilo(jnp.concatenate([w[:, _O_WI:_O_QB], jnp.zeros((D_MODEL, 8), F32)], axis=1).T)
    ph = np.zeros((128, 256), np.float32)
    plo = np.zeros((128, 256), np.float32)
    for c in range(64):
        ph[c, c] = 1.0
        ph[c, 64 + c] = 1.0
        plo[c, 128 + c] = 1.0
    sc = np.ones((1, 128), np.float32)
    sc[0, _L_WI:_L_WI + H_IDX] = H_IDX ** -0.5
    row = lambda c: pl.BlockSpec((tm, c), lambda i: (i, 0))
    colT = lambda r: pl.BlockSpec((1, r, tm), lambda i: (i, 0, 0))
    out_cols = [(512, F32), (512, F32), (512, BF16), (1536, F32), (512, F32), (1024, F32), (1024, F32), (128, F32),
                (256, BF16)]
    out_rows = [(WA, BF16), (WA, BF16), (WA, BF16), (H_IDX, F32), (HA * V_ROWS, BF16)]
    return pl.pallas_call(
        _in_proj_kernel,
        grid=(g,),
        in_specs=[row(D_MODEL), _const_spec((1, D_MODEL)), _const_spec((D_MODEL, 5120)),
                  _const_spec((2 * WA, D_MODEL)), _const_spec((WA, D_MODEL)), _const_spec((WA, D_MODEL)),
                  _const_spec((D_MODEL, 128)), _const_spec((D_MODEL, 128)),
                  _const_spec((16, D_MODEL)), _const_spec((16, D_MODEL)),
                  _const_spec((128, 256)), _const_spec((128, 256)), _const_spec((1, 128))],
        out_specs=[row(c) for c, _ in out_cols] + [colT(r) for r, _ in out_rows],
        out_shape=[jax.ShapeDtypeStruct((n, c), d) for c, d in out_cols]
        + [jax.ShapeDtypeStruct((g, r, tm), d) for r, d in out_rows],
        compiler_params=_cparams(("arbitrary",)),
        name="in_proj",
    )(x2d, norm_mix.reshape(1, D_MODEL), wm, wt, wqh, wql, wsh, wsl, wwh, wwl, jnp.asarray(ph, BF16),
      jnp.asarray(plo, BF16), jnp.asarray(sc))


def _dsa_kernel(qT_ref, qihT_ref, qilT_ref, wiT_ref, k_ref, vT_ref, ki3_ref, bias_ref, adm_ref, ltri_ref,
                o_ref, key_ref, l_ref, acc_ref, c_ref, lg_ref, pln_ref, eq_ref,
                *, tq, tk, nk_static, causal, topk, nvalid):
    nk = (pl.program_id(1) + 1) if causal else nk_static
    wiT = wiT_ref[0]
    adm = adm_ref[...] > 0.5

    zeros64 = jnp.zeros((64, tq), BF16)
    q3 = []
    for h in range(H_IDX):
        hi = qihT_ref[0, h * 64:(h + 1) * 64, :]
        lo = qilT_ref[0, h * 64:(h + 1) * 64, :]
        q3.append(jnp.concatenate([hi, lo, hi, zeros64], axis=0))

    def score_tile(j, is_last):
        kt = ki3_ref[0, j]
        acc = None
        for h in range(H_IDX):
            t = jnp.maximum(_dot(kt, q3[h]), 0.0) * wiT[h:h + 1, :]
            acc = t if acc is None else acc + t
        if is_last:
            acc = jnp.where(adm, acc, NEG)
        bits = pltpu.bitcast(acc, I32)
        bits = jnp.where(bits == INT_MIN, 0, bits)
        key = bits ^ ((bits >> 31) & 0x7FFFFFFF)
        key_ref[j] = key
        u3 = (key ^ INT_MIN).reshape(tk // 8, 8, tq)
        planes = _transpose32([u3[r] for r in range(32)])
        ones = jnp.full((8, tq), -1, I32)
        pln_ref[j, 0] = ones
        for b in range(32):
            pln_ref[j, b + 1] = planes[b]
        eq_ref[j] = ones

    def score_body(j, c):
        score_tile(j, False)
        return c

    lax.fori_loop(0, nk - 1, score_body, 0)
    score_tile(nk - 1, True)

    for d in range(3):
        pln_ref[nk + d] = jnp.zeros((33, 8, tq), I32)
        eq_ref[nk + d] = jnp.zeros((8, tq), I32)

    def sweep(it, carry):
        tu, n_gt, acc_i = carry
        acc_prev = acc_i != 0

        def body(jq, cnt):
            for d in range(4):
                j = jq * 4 + d
                e = eq_ref[j]
                tp = e & pln_ref[j, it]
                e = jnp.where(acc_prev, tp, e ^ tp)
                eq_ref[j] = e
                cnt = cnt + lax.population_count(e & pln_ref[j, it + 1])
            return cnt

        cnt = lax.fori_loop(0, (nk + 3) // 4, body, jnp.zeros((8, tq), I32))
        cnt = n_gt + cnt.sum(axis=0, keepdims=True)
        acc = cnt >= topk
        tu = jnp.where(acc, tu | jnp.left_shift(jnp.int32(1), 31 - it), tu)
        return tu, jnp.where(acc, n_gt, cnt), acc.astype(I32)

    zero_row = jnp.zeros((1, tq), I32)
    tu, n_gt, acc_i = lax.fori_loop(0, 32, sweep, (zero_row, zero_row, zero_row + 1))
    acc_last = acc_i != 0
    thr = tu ^ INT_MIN

    def last(j, cnt):
        e = eq_ref[j]
        tp = e & pln_ref[j, 32]
        return cnt + lax.population_count(jnp.where(acc_last, tp, e ^ tp))

    n_eq = lax.fori_loop(0, nk, last, jnp.zeros((8, tq), I32)).sum(axis=0, keepdims=True)
    lane_ok = lax.broadcasted_iota(I32, (1, tq), 1) < nvalid
    excess = jnp.max(jnp.where(lane_ok, n_gt + n_eq - topk, 0))
    need = (topk - n_gt).astype(F32)

    l_ref[...] = jnp.zeros(l_ref.shape, F32)
    acc_ref[...] = jnp.zeros(acc_ref.shape, F32)
    c_ref[...] = jnp.zeros(c_ref.shape, F32)
    qm = []
    for h in range(HA):
        qh = qT_ref[0, h * 64:(h + 1) * 64, :]
        qm.append(jnp.concatenate([qh, zeros64] if h % 2 == 0 else [zeros64, qh], axis=0))

    def step(jl, jv, carry, need):
        if jl is not None:
            kk = key_ref[jl]
            if need is not None:
                eq = kk == thr
                eqb = jnp.where(eq, 1.0, 0.0).astype(BF16)
                rank = _dot(ltri_ref[...], eqb) + c_ref[0:1, :]
                sel = (kk > thr) | (eq & (rank <= need))
                c_ref[0:1, :] = c_ref[0:1, :] + jnp.sum(eqb.astype(F32), axis=0, keepdims=True)
            else:
                sel = kk >= thr
            kind = jnp.clip(jl - (nk - 3), 0, 2)
        mx, ls = [], []
        if jv is not None:
            m_new, alpha = carry
            for h in range(HA):
                p = jnp.exp2(lg_ref[jv % 2, h] - m_new[h:h + 1, :].astype(BF16))
                pv = _dot(vT_ref[0, jv, h * V_ROWS:(h + 1) * V_ROWS, :], p)
                acc_ref[h * 64:(h + 1) * 64, :] = alpha[h:h + 1, :] * acc_ref[h * 64:(h + 1) * 64, :] + pv[0:DH]
                ls.append(pv[DH:DH + 1])
            l_ref[...] = alpha * l_ref[...] + jnp.concatenate(ls, axis=0)
        if jl is None:
            return None
        for h in range(HA):
            pr = h // 2
            lg = _dot(k_ref[0, jl, :, pr * 128:(pr + 1) * 128], qm[h]) + bias_ref[kind, h]
            lg = jnp.where(sel, lg, NEG).astype(BF16)
            lg_ref[jl % 2, h] = lg
            part = jnp.max(lg.reshape(tk // 16, 16, tq), axis=0).astype(F32)
            mx.append(jnp.max(part, axis=0, keepdims=True))
        return jnp.concatenate(mx, axis=0)

    def rescale(m_old, mx):
        m_new = jnp.maximum(m_old, mx)
        return m_new, jnp.exp2(m_old - m_new)

    def attend(need):
        def body(j, carry):
            return rescale(carry[0], step(j + 1, j, carry, need))

        first = rescale(jnp.full((HA, tq), NEG, F32), step(0, None, None, need))
        last = lax.fori_loop(0, nk - 1, body, first)
        step(None, nk - 1, last, need)

    @pl.when(excess > 0)
    def _():
        attend(need)

    @pl.when(excess <= 0)
    def _():
        attend(None)

    for h in range(HA):
        o_ref[0, h * 64:(h + 1) * 64, :] = (acc_ref[h * 64:(h + 1) * 64, :] / l_ref[h:h + 1, :]).astype(BF16)


def _t5_bucket_np(rel):
    half = NUM_BUCKETS // 2
    max_exact = half // 2
    out = np.zeros(rel.shape, np.int64)
    flat_rel = rel.reshape(-1)
    flat = out.reshape(-1)
    for a in range(flat_rel.size):
        r = int(flat_rel[a])
        n = abs(r)
        b = n if n < max_exact else min(half - 1, (n * n).bit_length() + 1)
        flat[a] = b + (half if r > 0 else 0)
    return out


def _bias_tables(rel_bias, adm, tk, tq, tq_valid):
    kj = np.arange(tk)[:, None]
    t = np.minimum(np.arange(tq), tq_valid - 1)[None, :]
    rel_to_bucket = _t5_bucket_np(np.arange(-2 * tk - tq, tk + 1))
    lut = lambda rel: rel_to_bucket[rel + 2 * tk + tq]
    far_bucket = NUM_BUCKETS // 2 - 1
    idx = np.stack([np.full((tk, tq), far_bucket), lut(kj - tk - t), lut(kj - t)], axis=0)
    tab = rel_bias.astype(F32) * LOG2E
    onehot = jnp.asarray(idx[..., None] == np.arange(NUM_BUCKETS), F32)
    bias = jnp.einsum("ktqb,bh->khtq", onehot, tab - tab[far_bucket], precision=lax.Precision.HIGHEST)
    mask = np.zeros((3, 1, tk, tq), np.float32)
    mask[2, 0] = np.where(adm > 0.5, 0.0, NEG)
    return bias + jnp.asarray(mask)


def _dsa(qT, qihT, qilT, wiT, k4, vT4, ki34, bias, adm, *, nq, causal, topk, nvalid):
    g, _, tq = qT.shape
    b, nk, tk, _ = k4.shape
    assert g == b * nq and tk == 256
    ltri = jnp.asarray(np.tril(np.ones((tk, tk), np.float32)), BF16)
    qspec = lambda r: pl.BlockSpec((1, r, tq), lambda bi, i: (bi * nq + i, 0, 0))
    kspec = lambda s: pl.BlockSpec((1,) + s, lambda bi, i: (bi, 0, 0, 0), pipeline_mode=pl.Buffered(1))
    kern = functools.partial(_dsa_kernel, tq=tq, tk=tk, nk_static=nk, causal=causal, topk=topk, nvalid=nvalid)
    return pl.pallas_call(
        kern,
        grid=(b, nq),
        in_specs=[qspec(512), qspec(512), qspec(512), qspec(8),
                  kspec((nk, tk, 512)), kspec((nk, HA * V_ROWS, tk)), kspec((nk, tk, 256)),
                  _const_spec((3, HA, tk, tq)), _const_spec((tk, tq)), _const_spec((tk, tk))],
        out_specs=qspec(512),
        out_shape=jax.ShapeDtypeStruct((g, 512, tq), BF16),
        scratch_shapes=[pltpu.VMEM((nk, tk, tq), I32), pltpu.VMEM((8, tq), F32),
                        pltpu.VMEM((512, tq), F32), pltpu.VMEM((8, tq), F32), pltpu.VMEM((2, HA, tk, tq), BF16),
                        pltpu.VMEM((nk + 3, 33, 8, tq), I32), pltpu.VMEM((nk + 3, 8, tq), I32)],
        compiler_params=_cparams(("arbitrary", "arbitrary")),
        name="dsa_causal" if causal else "dsa_cached",
    )(qT, qihT, qilT, wiT, k4, vT4, ki34, bias, adm, ltri)


def _gdn_pre_kernel(conv_ref, prev_ref, hist_ref, small_ref, cw_ref, alog_ref, dtb_ref,
                    wm_ref, um_ref, qe_ref, oi_ref, e_ref, ext_ref, *, rows, nvalid):
    c = CHUNK
    n = HB * c
    ext_ref[0:8, :] = jnp.where(pl.program_id(1) == 0, hist_ref[0], prev_ref[0])
    ext_ref[8:8 + rows, :] = conv_ref[0]
    cb = ext_ref[5:5 + rows, :] * cw_ref[0:1, :]
    for j in range(1, CONV_B):
        cb = cb + ext_ref[5 + j:5 + j + rows, :] * cw_ref[j:j + 1, :]
    cb = _silu(cb)

    small = small_ref[0]
    pos = lax.broadcasted_iota(I32, (rows, 1), 0) % c
    rowv = (pos < nvalid).astype(F32)
    beta_all = _sigmoid(small) * rowv
    sp = small + dtb_ref[...]
    g_all = -jnp.exp(alog_ref[...]) * (jnp.maximum(sp, 0.0) + jnp.log(1.0 + jnp.exp(-jnp.abs(sp)))) * rowv

    ri = lax.broadcasted_iota(I32, (n, n), 0)
    ci = lax.broadcasted_iota(I32, (n, n), 1)
    same = (ri // c) == (ci // c)
    tri_b = (same & (ci <= ri)).astype(BF16)
    bd_f = same.astype(F32)
    bd_b = same.astype(BF16)
    wr = lax.broadcasted_iota(I32, (c, n), 0)
    wl = lax.broadcasted_iota(I32, (c, n), 1)
    grp = wl // c
    tri_w = (wl % c) <= wr
    strict_w = (wl % c) < wr
    eye_w = ((wl % c) == wr).astype(F32)
    nt = (((1,), (1,)), ((), ()))
    tn = (((0,), (0,)), ((), ()))

    def l2n(x):
        return x * lax.rsqrt(jnp.sum(x * x, axis=-1, keepdims=True) + EPS)

    def to_wide(full):
        out = jnp.where(grp == 0, full[0:c, :], 0.0)
        for h in range(1, HB):
            out = out + jnp.where(grp == h, full[h * c:(h + 1) * c, :], 0.0)
        return out

    def tile4(x):
        return jnp.concatenate([x] * HB, axis=0)

    def chunk(k0):
        def stack(fn):
            return jnp.concatenate([fn(h) for h in range(HB)], axis=0)

        rv = rowv[k0:k0 + c]
        q = stack(lambda h: l2n(cb[k0:k0 + c, h * DK:(h + 1) * DK]) * (DK ** -0.5))
        k = stack(lambda h: l2n(cb[k0:k0 + c, WB + h * DK:WB + (h + 1) * DK]) * rv)
        v = stack(lambda h: cb[k0:k0 + c, 2 * WB + h * DV:2 * WB + (h + 1) * DV] * rv)
        beta = stack(lambda h: beta_all[k0:k0 + c, _L_BB + h:_L_BB + h + 1])
        g = stack(lambda h: g_all[k0:k0 + c, _L_AB + h:_L_AB + h + 1])
        yield

        gh, gl = _split(jnp.broadcast_to(g, (n, 128)))
        gcum = _dot(tri_b, gh) + _dot(tri_b, gl)
        yield
        gcum_row = gcum.T[0:1, :]
        gcum_col = gcum[:, 0:1]
        col_w = to_wide(jnp.concatenate([gcum, gcum], axis=1))
        decay_w = jnp.where(tri_w, jnp.exp(jnp.where(tri_w, col_w - gcum_row, 0.0)), 0.0)
        kb = k * beta
        kbf = k.astype(BF16)
        kk_w = to_wide(lax.dot_general(kb.astype(BF16), kbf, nt, preferred_element_type=F32))
        qk_w = to_wide(lax.dot_general(q.astype(BF16), kbf, nt, preferred_element_type=F32))
        nmat_w = jnp.where(strict_w, kk_w * decay_w, 0.0)
        attn_w = jnp.where(tri_w, qk_w * decay_w, 0.0)
        yield

        inv_w = eye_w - nmat_w
        ph, pl_ = _split(nmat_w)
        bh, bl = tile4(ph) * bd_b, tile4(pl_) * bd_b
        for _ in range(int(np.log2(c)) - 1):
            pw_w = _dot(ph, bh) + (_dot(pl_, bh) + _dot(ph, bl))
            ph, pl_ = _split(pw_w)
            yield
            bh, bl = tile4(ph) * bd_b, tile4(pl_) * bd_b
            ih, il = _split(inv_w)
            inv_w = inv_w + (_dot(ih, bh) + (_dot(il, bh) + _dot(ih, bl)))
            yield
        rhs = jnp.concatenate([v * beta, kb * jnp.exp(gcum_col)], axis=1)
        sol = _dot3(tile4(inv_w) * bd_f, rhs)
        yield
        solb = sol.astype(BF16)
        aw = _dot((tile4(attn_w) * bd_f).astype(BF16), solb)
        yield
        ck = k0 // c
        oi_ref[0, ck] = aw[:, :DV]
        qe_ref[0, ck] = (q * jnp.exp(gcum_col) - aw[:, DV:]).astype(BF16)
        for h in range(HB):
            rs = slice(h * c, (h + 1) * c)
            g_last = gcum[(h + 1) * c - 1:(h + 1) * c, :]
            kd = (k[rs] * jnp.exp(g_last - gcum[rs])).astype(BF16)
            uw = lax.dot_general(kd, solb[rs], tn, preferred_element_type=F32)
            um_ref[0, ck, h] = uw[:, :DV]
            wm_ref[0, ck, h] = uw[:, DV:].astype(BF16)
            e_ref[0, ck, h:h + 1, :] = jnp.exp(g_last)
        e_ref[0, ck, HB:8, :] = jnp.zeros((8 - HB, 128), F32)

    gens = [chunk(k0) for k0 in range(0, rows, c)]
    while gens:
        alive = []
        for gen in gens:
            try:
                next(gen)
                alive.append(gen)
            except StopIteration:
                pass
        gens = alive


def _gdn_scan_kernel(wm_ref, um_ref, qe_ref, oi_ref, e_ref, gb_ref, s0_ref, ng_ref, ob_ref, sfin_ref, s_ref,
                     *, bb, g):
    c = CHUNK

    @pl.when(pl.program_id(1) == 0)
    def _():
        s_ref[...] = s0_ref[...]

    ng = ng_ref[...]
    for b in range(bb):
        for ck in range(g):
            for h in range(HB):
                s = s_ref[b, h]
                sb = s.astype(BF16)
                rs = slice(h * c, (h + 1) * c)
                o = _dot(qe_ref[b, ck, rs, :], sb) + oi_ref[b, ck, rs, :]
                s_ref[b, h] = e_ref[b, ck, h:h + 1, :] * s + (um_ref[b, ck, h] - _dot(wm_ref[b, ck, h], sb))
                gate = gb_ref[b, ck * c:(ck + 1) * c, h * DV:(h + 1) * DV]
                ob_ref[b, ck * c:(ck + 1) * c, h * DV:(h + 1) * DV] = (_rms(o, ng) * _silu(gate)).astype(BF16)
    sfin_ref[...] = s_ref[...]


def _gdn(conv_in, hist, gb, small, s0, conv_w, a_log, dt_bias, norm_gdn, nvalid):
    b, t, _ = conv_in.shape
    assert t % CHUNK == 0
    rows = 256 if t % 256 == 0 else CHUNK
    nc, cps = t // CHUNK, rows // CHUNK
    n = HB * CHUNK
    alog = jnp.zeros((1, 128), F32).at[0, _L_AB:_L_AB + HB].set(a_log)
    dtb = jnp.zeros((1, 128), F32).at[0, _L_AB:_L_AB + HB].set(dt_bias)
    row = lambda c_: pl.BlockSpec((1, rows, c_), lambda bi, ti: (bi, ti, 0))
    prev = pl.BlockSpec((1, 8, C_CONV_B), lambda bi, ti: (bi, jnp.maximum(ti * (rows // 8) - 1, 0), 0))
    per_b = lambda s: pl.BlockSpec((1,) + s, lambda bi, ti: (bi,) + (0,) * len(s))
    chunked = lambda s: pl.BlockSpec((1, cps) + s, lambda bi, ti: (bi, ti) + (0,) * len(s))
    wm, um, qe, oi, e = pl.pallas_call(
        functools.partial(_gdn_pre_kernel, rows=rows, nvalid=nvalid),
        grid=(b, t // rows),
        in_specs=[row(C_CONV_B), prev, per_b((8, C_CONV_B)), row(128),
                  _const_spec((CONV_B, C_CONV_B)), _const_spec((1, 128)), _const_spec((1, 128))],
        out_specs=[chunked((HB, DK, DV)), chunked((HB, DK, DV)), chunked((n, DK)), chunked((n, DV)),
                   chunked((8, 128))],
        out_shape=[jax.ShapeDtypeStruct((b, nc, HB, DK, DV), BF16), jax.ShapeDtypeStruct((b, nc, HB, DK, DV), F32),
                   jax.ShapeDtypeStruct((b, nc, n, DK), BF16), jax.ShapeDtypeStruct((b, nc, n, DV), F32),
                   jax.ShapeDtypeStruct((b, nc, 8, 128), F32)],
        scratch_shapes=[pltpu.VMEM((8 + rows, C_CONV_B), F32)],
        compiler_params=_cparams(("arbitrary", "arbitrary")),
        name="gdn_pre",
    )(conv_in, conv_in, hist, small, conv_w, alog, dtb)

    bb = 2 if b % 2 == 0 else 1
    g = 2 if nc % 2 == 0 else 1
    blk = lambda s: pl.BlockSpec((bb, g) + s, lambda bi, ci: (bi, ci) + (0,) * len(s))
    rowb = pl.BlockSpec((bb, g * CHUNK, WB), lambda bi, ci: (bi, ci, 0))
    state = pl.BlockSpec((bb, HB, DK, DV), lambda bi, ci: (bi, 0, 0, 0))
    return pl.pallas_call(
        functools.partial(_gdn_scan_kernel, bb=bb, g=g),
        grid=(b // bb, nc // g),
        in_specs=[blk((HB, DK, DV)), blk((HB, DK, DV)), blk((n, DK)), blk((n, DV)), blk((8, 128)), rowb, state,
                  _const_spec((1, DV))],
        out_specs=[rowb, state],
        out_shape=[jax.ShapeDtypeStruct((b, t, WB), BF16), jax.ShapeDtypeStruct((b, HB, DK, DV), F32)],
        scratch_shapes=[pltpu.VMEM((bb, HB, DK, DV), F32)],
        compiler_params=_cparams(("arbitrary", "arbitrary")),
        name="gdn_scan",
    )(wm, um, qe, oi, e, gb, s0, norm_gdn.reshape(1, DV))


def _merge_kernel(x_ref, oaT_ref, ob_ref, ga_ref, gbr_ref, wa_ref, wb_ref, wo_ref, nf_ref, x1_ref, h2_ref):
    ya = lax.dot_general(oaT_ref[0], wa_ref[...], (((0,), (0,)), ((), ())), preferred_element_type=F32)
    yb = _dot(ob_ref[...], wb_ref[...])
    mix = _sigmoid(ga_ref[...]) * ya + _sigmoid(gbr_ref[...]) * yb
    x1 = x_ref[...] + _dot(mix.astype(BF16), wo_ref[...])
    x1_ref[...] = x1
    h2_ref[...] = _rms(x1, nf_ref[...]).astype(BF16)


def _merge(x2d, oaT, ob, ga, gbr, w_proj_a, w_proj_b, w_out, norm_ffn, tm):
    n = x2d.shape[0]
    assert oaT.shape == (n // tm, WA, tm)
    row = lambda c: pl.BlockSpec((tm, c), lambda i: (i, 0))
    return pl.pallas_call(
        _merge_kernel,
        grid=(n // tm,),
        in_specs=[row(D_MODEL), pl.BlockSpec((1, WA, tm), lambda i: (i, 0, 0)), row(WB), row(D_MODEL), row(D_MODEL),
                  _const_spec((WA, D_MODEL)), _const_spec((WB, D_MODEL)), _const_spec((D_MODEL, D_MODEL)),
                  _const_spec((1, D_MODEL))],
        out_specs=[row(D_MODEL), row(D_MODEL)],
        out_shape=[jax.ShapeDtypeStruct((n, D_MODEL), F32), jax.ShapeDtypeStruct((n, D_MODEL), BF16)],
        compiler_params=_cparams(("arbitrary",)),
        name="merge",
    )(x2d, oaT, ob, ga, gbr, w_proj_a.astype(BF16), w_proj_b.astype(BF16), w_out.astype(BF16),
      norm_ffn.reshape(1, D_MODEL))


def _ffn_kernel(x1_ref, h2_ref, p_ref, hist_ref, wup_ref, cw_ref, wdn_ref, npl_ref, wpg_ref, wple_ref, nfin_ref,
                y_ref, tail_ref, ext_ref, *, tm):
    @pl.when(pl.program_id(1) == 0)
    def _():
        ext_ref[0:8, :] = hist_ref[0]

    h2 = h2_ref[0]
    ext_ref[8:8 + tm, :] = _dot(h2, wup_ref[:, 0:D_FF])
    u_val = _dot(h2, wup_ref[:, D_FF:2 * D_FF])
    cv = ext_ref[6:6 + tm, :] * cw_ref[0:1, :]
    for j in range(1, CONV_F):
        cv = cv + ext_ref[6 + j:6 + j + tm, :] * cw_ref[j:j + 1, :]
    tail = ext_ref[tm:tm + 8, :]
    ext_ref[0:8, :] = tail
    tail_ref[0] = tail
    act = 0.5 * cv * (1.0 + jnp.tanh(0.7978845608028654 * (cv + 0.044715 * (cv * cv * cv))))
    x2 = x1_ref[0] + _dot((act * u_val).astype(BF16), wdn_ref[...])
    gate = _sigmoid(_dot(_rms(x2, npl_ref[...]).astype(BF16), wpg_ref[...]))
    x3 = x2 + gate * _dot(p_ref[0].astype(BF16), wple_ref[...])
    y_ref[0] = _rms(x3, nfin_ref[...])


def _ffn_ple(x1, h2, p, hist, w_up, conv_ffn, w_down, norm_ple, w_ple_gate, w_ple, norm_final, tm):
    b, t, _ = x1.shape
    assert t % tm == 0 and tm >= 8
    row = lambda c: pl.BlockSpec((1, tm, c), lambda bi, ti: (bi, ti, 0))
    per_b = pl.BlockSpec((1, 8, D_FF), lambda bi, ti: (bi, 0, 0))
    return pl.pallas_call(
        functools.partial(_ffn_kernel, tm=tm),
        grid=(b, t // tm),
        in_specs=[row(D_MODEL), row(D_MODEL), row(D_PLE), per_b,
                  _const_spec((D_MODEL, 2 * D_FF)), _const_spec((CONV_F, D_FF)), _const_spec((D_FF, D_MODEL)),
                  _const_spec((1, D_MODEL)), _const_spec((D_MODEL, D_MODEL)), _const_spec((D_PLE, D_MODEL)),
                  _const_spec((1, D_MODEL))],
        out_specs=[row(D_MODEL), per_b],
        out_shape=[jax.ShapeDtypeStruct((b, t, D_MODEL), F32), jax.ShapeDtypeStruct((b, 8, D_FF), F32)],
        scratch_shapes=[pltpu.VMEM((8 + tm, D_FF), F32)],
        compiler_params=_cparams(("arbitrary", "arbitrary")),
        name="ffn_ple",
    )(x1, h2, p, hist, w_up.astype(BF16), conv_ffn, w_down.astype(BF16), norm_ple.reshape(1, D_MODEL),
      w_ple_gate.astype(BF16), w_ple.astype(BF16), norm_final.reshape(1, D_MODEL))


def _pad_hist(hist, rows=8):
    b, r, c = hist.shape
    return jnp.concatenate([jnp.zeros((b, rows - r, c), hist.dtype), hist], axis=1)


def _layer(x, p, past_k, past_v, past_kidx, s_gdn, conv_b_hist, ffn_hist, wts, *, tm, tq):
    (norm_mix, w_in, conv_b, a_log, dt_bias, norm_gdn, w_proj_a, w_proj_b, w_out, norm_ffn, w_up, conv_ffn,
     w_down, norm_ple, w_ple, w_ple_gate, rel_bias, norm_final) = wts
    b, t, _ = x.shape
    n = b * t
    past = past_k.shape[1]
    topk = min(TOPK_MAX, (past + t) // 4)
    x2d = x.reshape(n, D_MODEL)
    tmi = min(tm, n)
    (ka, va, kbf, conv_in, gb, ga, gbr, small, ki3, qT, qihT, qilT, wiT, vTa) = _in_proj(x2d, norm_mix, w_in, tmi)

    if past == 0:
        assert tq == tmi and t % tq == 0 and tq % CHUNK == 0 and tq >= topk
        nq, tk, lanes = t // tq, tq, tq
        q_ops = (qT, qihT, qilT, wiT)
        k4 = kbf.reshape(b, nq, tk, WA)
        vT4 = vTa.reshape(b, nq, HA * V_ROWS, tk)
        ki34 = ki3.reshape(b, nq, tk, 256)
        kj = np.arange(tk)[:, None]
        adm = ((kj // CHUNK) <= (np.arange(tq)[None, :] // CHUNK)).astype(np.float32)
        nvalid, causal = tq, True
    else:
        tk, lanes, nq = 256, 128, 1
        assert n == tmi and past % tk == 0 and t <= tk and t <= lanes
        nkc = past // tk
        per_b = lambda a, width: jnp.pad(jnp.swapaxes(a[0].reshape(a.shape[1], b, t), 0, 1),
                                         ((0, 0), (0, 0), (0, width - t)))
        q_ops = tuple(per_b(a, lanes) for a in (qT, qihT, qilT, wiT))
        padk = lambda new, old: jnp.concatenate(
            [old, new.reshape(b, t, new.shape[-1]), jnp.zeros((b, tk - t, new.shape[-1]), new.dtype)], axis=1)
        k4 = padk(kbf, past_k.reshape(b, past, WA).astype(BF16)).reshape(b, nkc + 1, tk, WA)
        vc = jnp.transpose(past_v.reshape(b, nkc, tk, HA, DH).astype(BF16), (0, 1, 3, 4, 2))
        vc = jnp.concatenate([vc, jnp.ones((b, nkc, HA, V_ROWS - DH, tk), BF16)], axis=3)
        vT4 = jnp.concatenate([vc.reshape(b, nkc, HA * V_ROWS, tk), per_b(vTa, tk)[:, None]], axis=1)
        pk = past_kidx.astype(F32)
        pkh = pk.astype(BF16)
        pkl = (pk - pkh.astype(F32)).astype(BF16)
        past3 = jnp.concatenate([pkh, pkh, pkl, jnp.zeros_like(pkh)], axis=-1)
        ki34 = padk(ki3, past3).reshape(b, nkc + 1, tk, 256)
        adm = np.broadcast_to(np.arange(tk)[:, None] < t, (tk, lanes)).astype(np.float32)
        nvalid, causal = t, False
    bias = _bias_tables(rel_bias, adm, tk, lanes, nvalid)
    oT = _dsa(*q_ops, k4, vT4, ki34, bias, jnp.asarray(adm), nq=nq, causal=causal, topk=topk, nvalid=nvalid)
    if past:
        oT = jnp.swapaxes(oT[:, :, :t], 0, 1).reshape(1, WA, n)

    tp = -(-t // CHUNK) * CHUNK
    padt = lambda a: jnp.pad(a.reshape(b, t, a.shape[-1]), ((0, 0), (0, tp - t), (0, 0)))
    ob, s_new = _gdn(padt(conv_in), _pad_hist(conv_b_hist), padt(gb), padt(small), s_gdn, conv_b, a_log, dt_bias,
                     norm_gdn, nvalid=min(t, CHUNK))
    ob = ob[:, :t].reshape(n, WB)
    new_conv_b = jnp.concatenate([conv_b_hist, conv_in.reshape(b, t, C_CONV_B)], axis=1)[:, t:]

    x1, h2 = _merge(x2d, oT, ob, ga, gbr, w_proj_a, w_proj_b, w_out, norm_ffn, tmi)
    tmf = min(tm, t)
    y, tail = _ffn_ple(x1.reshape(b, t, D_MODEL), h2.reshape(b, t, D_MODEL), p, _pad_hist(ffn_hist), w_up,
                       conv_ffn, w_down, norm_ple, w_ple_gate, w_ple, norm_final, tmf)
    new_ffn = tail[:, 8 - (CONV_F - 1):]
    return (y, ka.reshape(b, t, HA, DH), va.reshape(b, t, HA, DH), small[:, :D_IDX].reshape(b, t, D_IDX),
            s_new, new_conv_b, new_ffn)


def kernel(x_prompt, x_sample, p_prompt, p_sample, cache_k, cache_v, cache_kidx, state_gdn, state_gdn_conv,
           state_ffn_conv, norm_mix, w_in, conv_b, a_log, dt_bias, norm_gdn, w_proj_a, w_proj_b, w_out, norm_ffn,
           w_up, conv_ffn, w_down, norm_ple, w_ple, w_ple_gate, rel_bias, norm_final):
    assert norm_mix.shape[0] == 1
    bp = x_prompt.shape[0]
    dt = x_prompt.dtype
    wts = (norm_mix[0], w_in[0], conv_b[0], a_log[0], dt_bias[0], norm_gdn[0], w_proj_a[0], w_proj_b[0], w_out[0],
           norm_ffn[0], w_up[0], conv_ffn[0], w_down[0], norm_ple[0], w_ple[0], w_ple_gate[0], rel_bias, norm_final)
    outs_p = _layer(x_prompt, p_prompt[0], jnp.zeros((bp, 0, HA, DH), dt), jnp.zeros((bp, 0, HA, DH), dt),
                    jnp.zeros((bp, 0, D_IDX), dt), jnp.zeros((bp, HB, DK, DV), dt),
                    jnp.zeros((bp, CONV_B - 1, C_CONV_B), dt), jnp.zeros((bp, CONV_F - 1, D_FF), dt),
                    wts, tm=256, tq=256)
    outs_s = _layer(x_sample, p_sample[0], cache_k[0], cache_v[0], cache_kidx[0], state_gdn[0],
                    state_gdn_conv[0], state_ffn_conv[0], wts, tm=256, tq=256)
    yp, ys = outs_p[0], outs_s[0]
    return (yp, ys) + tuple(a[None] for a in outs_p[1:]) + tuple(a[None] for a in outs_s[1:])
```

```python
import functools

import numpy as np
import jax
import jax.numpy as jnp
from jax import lax
from jax.experimental import pallas as pl
from jax.experimental.pallas import tpu as pltpu

F32 = jnp.float32
BF16 = jnp.bfloat16
I32 = jnp.int32

D_MODEL = 1024
CHUNK = 64
HA, DH = 8, 64
H_IDX, D_IDX = 8, 64
TOPK_MAX = 256
NUM_BUCKETS, MAX_DISTANCE = 32, 128
HB, DK, DV = 4, 128, 128
CONV_B = 4
D_FF = 2816
CONV_F = 3
D_PLE = 256
EPS = 1e-6
NEG = -1e30
WA = HA * DH
WB = HB * DK
C_CONV_B = 3 * WB
INT_MIN = -2 ** 31
LOG2E = 1.4426950408889634

_O_QA, _O_KA, _O_VA, _O_QI, _O_KI, _O_WI = 0, 512, 1024, 1536, 2048, 2112
_O_QB, _O_GB, _O_BB, _O_AB, _O_GA, _O_GBR = 2120, 3656, 4168, 4172, 4176, 5200
_L_WI, _L_BB, _L_AB = 64, 72, 76

VMEM_LIMIT = 56 * 1024 * 1024


def _cparams(sem):
    return pltpu.CompilerParams(dimension_semantics=sem, vmem_limit_bytes=VMEM_LIMIT)


def _const_spec(shape):
    nd = len(shape)
    return pl.BlockSpec(shape, lambda *_: (0,) * nd, pipeline_mode=pl.Buffered(1))


def _rms(x, g):
    return x * lax.rsqrt(jnp.mean(x * x, axis=-1, keepdims=True) + EPS) * g


def _split(x):
    hi = x.astype(BF16)
    lo = (x - hi.astype(F32)).astype(BF16)
    return hi, lo


def _dot(a, b):
    return jnp.dot(a, b, preferred_element_type=F32)


def _dot3(a, b):
    ah, al = _split(a)
    bh, bl = _split(b)
    return _dot(ah, bh) + (_dot(al, bh) + _dot(ah, bl))


def _sigmoid(x):
    return 1.0 / (1.0 + jnp.exp(-x))


def _silu(x):
    return x * _sigmoid(x)


def _transpose32(a):
    a = list(a)
    j, m = 16, 0x0000FFFF
    while j:
        k = 0
        while k < 32:
            t = (a[k] ^ lax.shift_right_logical(a[k + j], jnp.int32(j))) & jnp.int32(m - (1 << 32) if m >> 31 else m)
            a[k] = a[k] ^ t
            a[k + j] = a[k + j] ^ (t << j)
            k = (k + j + 1) & ~j
        j >>= 1
        m = (m ^ (m << j)) & 0xFFFFFFFF
    return a


V_ROWS = DH + 16
_NT = (((1,), (1,)), ((), ()))


def _in_proj_kernel(x_ref, g_ref, wm_ref, wt_ref, wqh_ref, wql_ref, wsh_ref, wsl_ref, wwh_ref, wwl_ref,
                    ph_ref, plo_ref, sc_ref,
                    ka_ref, va_ref, kbf_ref, conv_ref, gb_ref, ga_ref, gbr_ref, small_ref, ki3_ref,
                    qT_ref, qihT_ref, qilT_ref, wiT_ref, vTa_ref):
    tm = x_ref.shape[0]
    h = _rms(x_ref[...], g_ref[...])
    hh, hl = _split(h)

    def main(lo, hi):
        return _dot(hh, wm_ref[:, lo:hi])

    def nt(w, a):
        return lax.dot_general(w, a, _NT, preferred_element_type=F32)

    def nt3(wh_ref, wl_ref):
        return nt(wh_ref[...], hh) + (nt(wh_ref[...], hl) + nt(wl_ref[...], hh))

    ka = main(0, 512)
    ka_ref[...] = ka
    kbf_ref[...] = ka.astype(BF16)
    va_ref[...] = main(512, 1024)
    conv_ref[...] = main(1024, 2560)
    gb_ref[...] = main(2560, 3072)
    ga_ref[...] = main(3072, 4096)
    gbr_ref[...] = main(4096, 5120)

    qT_ref[0] = nt(wt_ref[0:WA, :], hh).astype(BF16)
    vT = nt(wt_ref[WA:2 * WA, :], hh).astype(BF16)
    ones = jnp.ones((V_ROWS - DH, tm), BF16)
    for hd in range(HA):
        vTa_ref[0, hd * V_ROWS:hd * V_ROWS + DH, :] = vT[hd * DH:(hd + 1) * DH]
        vTa_ref[0, hd * V_ROWS + DH:(hd + 1) * V_ROWS, :] = ones
    qh, ql = _split(nt3(wqh_ref, wql_ref))
    qihT_ref[0] = qh
    qilT_ref[0] = ql
    wiT_ref[0] = nt3(wwh_ref, wwl_ref)[0:H_IDX] * (H_IDX ** -0.5)

    small = (_dot(hh, wsh_ref[...]) + (_dot(hl, wsh_ref[...]) + _dot(hh, wsl_ref[...]))) * sc_ref[...]
    small_ref[...] = small
    sh, sl = _split(small)
    ki3_ref[...] = (_dot(sh, ph_ref[...]) + _dot(sl, plo_ref[...])).astype(BF16)


def _in_proj(x2d, norm_mix, w_in, tm):
    n = x2d.shape[0]
    assert n % tm == 0
    g = n // tm
    w = w_in
    wm = jnp.concatenate([w[:, _O_KA:_O_QI], w[:, _O_QB:_O_BB], w[:, _O_GA:]], axis=1).astype(BF16)
    wt = jnp.concatenate([w[:, _O_QA:_O_KA] * (DH ** -0.5 * LOG2E), w[:, _O_VA:_O_QI]], axis=1).T.astype(BF16)
    hilo = lambda a: (a.astype(BF16), (a - a.astype(BF16).astype(F32)).astype(BF16))
    wqh, wql = hilo((w[:, _O_QI:_O_KI] * (D_IDX ** -0.5)).T)
    wsh, wsl = hilo(jnp.concatenate([w[:, _O_KI:_O_QB], w[:, _O_BB:_O_GA], jnp.zeros((D_MODEL, 48), F32)], axis=1))
    wwh, wwl = hilo(jnp.concatenate([w[:, _O_WI:_O_QB], jnp.zeros((D_MODEL, 8), F32)], axis=1).T)
    ph = np.zeros((128, 256), np.float32)
    plo = np.zeros((128, 256), np.float32)
    for c in range(64):
        ph[c, c] = 1.0
        ph[c, 64 + c] = 1.0
        plo[c, 128 + c] = 1.0
    sc = np.ones((1, 128), np.float32)
    sc[0, _L_WI:_L_WI + H_IDX] = H_IDX ** -0.5
    row = lambda c: pl.BlockSpec((tm, c), lambda i: (i, 0))
    colT = lambda r: pl.BlockSpec((1, r, tm), lambda i: (i, 0, 0))
    out_cols = [(512, F32), (512, F32), (512, BF16), (1536, F32), (512, F32), (1024, F32), (1024, F32), (128, F32),
                (256, BF16)]
    out_rows = [(WA, BF16), (WA, BF16), (WA, BF16), (H_IDX, F32), (HA * V_ROWS, BF16)]
    return pl.pallas_call(
        _in_proj_kernel,
        grid=(g,),
        in_specs=[row(D_MODEL), _const_spec((1, D_MODEL)), _const_spec((D_MODEL, 5120)),
                  _const_spec((2 * WA, D_MODEL)), _const_spec((WA, D_MODEL)), _const_spec((WA, D_MODEL)),
                  _const_spec((D_MODEL, 128)), _const_spec((D_MODEL, 128)),
                  _const_spec((16, D_MODEL)), _const_spec((16, D_MODEL)),
                  _const_spec((128, 256)), _const_spec((128, 256)), _const_spec((1, 128))],
        out_specs=[row(c) for c, _ in out_cols] + [colT(r) for r, _ in out_rows],
        out_shape=[jax.ShapeDtypeStruct((n, c), d) for c, d in out_cols]
        + [jax.ShapeDtypeStruct((g, r, tm), d) for r, d in out_rows],
        compiler_params=_cparams(("arbitrary",)),
        name="in_proj",
    )(x2d, norm_mix.reshape(1, D_MODEL), wm, wt, wqh, wql, wsh, wsl, wwh, wwl, jnp.asarray(ph, BF16),
      jnp.asarray(plo, BF16), jnp.asarray(sc))


def _dsa_kernel(qT_ref, qihT_ref, qilT_ref, wiT_ref, k_ref, vT_ref, ki3_ref, bias_ref, cap_ref, ltri_ref,
                o_ref, key_ref, l_ref, acc_ref, c_ref, lg_ref, pln_ref, eq_ref, sc_ref, ma_ref,
                *, tq, tk, nk_static, causal, topk, nvalid):
    nk = (pl.program_id(1) + 1) if causal else nk_static
    wiT = wiT_ref[0]

    def when(cond):
        if isinstance(cond, bool):
            return (lambda f: f()) if cond else (lambda f: None)
        return pl.when(cond)

    zeros64 = jnp.zeros((64, tq), BF16)
    q3 = []
    for h in range(H_IDX):
        hi = qihT_ref[0, h * 64:(h + 1) * 64, :]
        lo = qilT_ref[0, h * 64:(h + 1) * 64, :]
        q3.append(jnp.concatenate([hi, lo, hi, zeros64], axis=0))

    def score_matmuls(j):
        kt = ki3_ref[0, j]
        acc = None
        for h in range(H_IDX):
            t = jnp.maximum(_dot(kt, q3[h]), 0.0) * wiT[h:h + 1, :]
            acc = t if acc is None else acc + t
        sc_ref[...] = jnp.minimum(acc, cap_ref[jnp.where(j == nk - 1, 1, 0)])

    def score_finish(j):
        bits = pltpu.bitcast(sc_ref[...], I32)
        bits = jnp.where(bits == INT_MIN, 0, bits)
        key = bits ^ ((bits >> 31) & 0x7FFFFFFF)
        key_ref[j] = key
        u3 = (key ^ INT_MIN).reshape(tk // 8, 8, tq)
        ones = jnp.full((8, tq), -1, I32)
        pln_ref[j, 0] = ones
        for l0 in range(0, tq, 128):
            planes = _transpose32([u3[r][:, l0:l0 + 128] for r in range(32)])
            for b in range(32):
                pln_ref[j, b + 1, :, l0:l0 + 128] = planes[b]
        eq_ref[j] = ones

    def score_body(j, c):
        score_finish(j - 1)
        score_matmuls(j)
        return c

    score_matmuls(0)
    lax.fori_loop(1, nk, score_body, 0)
    score_finish(nk - 1)

    for d in range(3):
        pln_ref[nk + d] = jnp.zeros((33, 8, tq), I32)
        eq_ref[nk + d] = jnp.zeros((8, tq), I32)

    def sweep(it, carry):
        tu, n_gt, acc_i = carry
        acc_prev = acc_i != 0

        def body(jq, cnt):
            for d in range(4):
                j = jq * 4 + d
                e = eq_ref[j]
                tp = e & pln_ref[j, it]
                e = jnp.where(acc_prev, tp, e ^ tp)
                eq_ref[j] = e
                cnt = cnt + lax.population_count(e & pln_ref[j, it + 1])
            return cnt

        cnt = lax.fori_loop(0, (nk + 3) // 4, body, jnp.zeros((8, tq), I32))
        cnt = n_gt + cnt.sum(axis=0, keepdims=True)
        acc = cnt >= topk
        tu = jnp.where(acc, tu | jnp.left_shift(jnp.int32(1), 31 - it), tu)
        return tu, jnp.where(acc, n_gt, cnt), acc.astype(I32)

    zero_row = jnp.zeros((1, tq), I32)
    tu, n_gt, acc_i = lax.fori_loop(0, 32, sweep, (zero_row, zero_row, zero_row + 1))
    acc_last = acc_i != 0
    thr = tu ^ INT_MIN

    def last(j, cnt):
        e = eq_ref[j]
        tp = e & pln_ref[j, 32]
        return cnt + lax.population_count(jnp.where(acc_last, tp, e ^ tp))

    n_eq = lax.fori_loop(0, nk, last, jnp.zeros((8, tq), I32)).sum(axis=0, keepdims=True)
    lane_ok = lax.broadcasted_iota(I32, (1, tq), 1) < nvalid
    excess = jnp.max(jnp.where(lane_ok, n_gt + n_eq - topk, 0))
    need = (topk - n_gt).astype(F32)

    l_ref[...] = jnp.zeros(l_ref.shape, F32)
    acc_ref[...] = jnp.zeros(acc_ref.shape, F32)
    c_ref[...] = jnp.zeros(c_ref.shape, F32)
    qm = []
    for h in range(HA):
        qh = qT_ref[0, h * 64:(h + 1) * 64, :]
        qm.append(jnp.concatenate([qh, zeros64] if h % 2 == 0 else [zeros64, qh], axis=0))

    def tile_logits(j, slot, need):
        kk = key_ref[j]
        if need is not None:
            eq = kk == thr
            eqb = jnp.where(eq, 1.0, 0.0).astype(BF16)
            rank = _dot(ltri_ref[...], eqb) + c_ref[0:1, :]
            sel = (kk > thr) | (eq & (rank <= need))
            c_ref[0:1, :] = c_ref[0:1, :] + jnp.sum(eqb.astype(F32), axis=0, keepdims=True)
        else:
            sel = kk >= thr
        kind = jnp.clip(j - (nk - 3), 0, 2)
        mx = []
        for h in range(HA):
            pr = h // 2
            lg = _dot(k_ref[0, j, :, pr * 128:(pr + 1) * 128], qm[h]) + bias_ref[kind, h]
            lg = jnp.where(sel, lg, NEG).astype(BF16)
            lg_ref[slot, h] = lg
            part = jnp.max(lg.reshape(tk // 16, 16, tq), axis=0).astype(F32)
            mx.append(jnp.max(part, axis=0, keepdims=True))
        return jnp.concatenate(mx, axis=0)

    def tile_values(j, slot, mx):
        m_old = ma_ref[...]
        m_new = jnp.maximum(m_old, mx)
        alpha = jnp.exp2(m_old - m_new)
        ma_ref[...] = m_new
        ls = []
        for h in range(HA):
            p = jnp.exp2(lg_ref[slot, h] - m_new[h:h + 1, :].astype(BF16))
            pv = _dot(vT_ref[0, j, h * V_ROWS:(h + 1) * V_ROWS, :], p)
            acc_ref[h * 64:(h + 1) * 64, :] = alpha[h:h + 1, :] * acc_ref[h * 64:(h + 1) * 64, :] + pv[0:DH]
            ls.append(pv[DH:DH + 1])
        l_ref[...] = alpha * l_ref[...] + jnp.concatenate(ls, axis=0)

    def attend(need):
        ma_ref[...] = jnp.full((HA, tq), NEG, F32)

        def pair(i, c):
            mx0 = tile_logits(2 * i, 0, need)
            mx1 = tile_logits(2 * i + 1, 1, need)
            tile_values(2 * i, 0, mx0)
            tile_values(2 * i + 1, 1, mx1)
            return c

        lax.fori_loop(0, nk // 2, pair, 0)

        @when(nk % 2 == 1)
        def _():
            tile_values(nk - 1, 0, tile_logits(nk - 1, 0, need))

    @pl.when(excess > 0)
    def _():
        attend(need)

    @pl.when(excess <= 0)
    def _():
        attend(None)

    for h in range(HA):
        o_ref[0, h * 64:(h + 1) * 64, :] = (acc_ref[h * 64:(h + 1) * 64, :] / l_ref[h:h + 1, :]).astype(BF16)


def _t5_bucket_np(rel):
    half = NUM_BUCKETS // 2
    max_exact = half // 2
    out = np.zeros(rel.shape, np.int64)
    flat_rel = rel.reshape(-1)
    flat = out.reshape(-1)
    for a in range(flat_rel.size):
        r = int(flat_rel[a])
        n = abs(r)
        b = n if n < max_exact else min(half - 1, (n * n).bit_length() + 1)
        flat[a] = b + (half if r > 0 else 0)
    return out


def _bias_tables(rel_bias, adm, tk, tq, tq_valid):
    kj = np.arange(tk)[:, None]
    t = np.minimum(np.arange(tq), tq_valid - 1)[None, :]
    rel_to_bucket = _t5_bucket_np(np.arange(-2 * tk - tq, tk + 1))
    lut = lambda rel: rel_to_bucket[rel + 2 * tk + tq]
    far_bucket = NUM_BUCKETS // 2 - 1
    idx = np.stack([np.full((tk, tq), far_bucket), lut(kj - tk - t), lut(kj - t)], axis=0)
    tab = rel_bias.astype(F32) * LOG2E
    onehot = jnp.asarray(idx[..., None] == np.arange(NUM_BUCKETS), F32)
    bias = jnp.einsum("ktqb,bh->khtq", onehot, tab - tab[far_bucket], precision=lax.Precision.HIGHEST)
    mask = np.zeros((3, 1, tk, tq), np.float32)
    mask[2, 0] = np.where(adm > 0.5, 0.0, NEG)
    return bias + jnp.asarray(mask)


def _dsa(qT, qihT, qilT, wiT, k4, vT4, ki34, bias, adm, *, nq, causal, topk, nvalid):
    g, _, tq = qT.shape
    b, nk, tk, _ = k4.shape
    assert g == b * nq and tk == 256
    ltri = jnp.asarray(np.tril(np.ones((tk, tk), np.float32)), BF16)
    cap = jnp.asarray(np.stack([np.full((tk, tq), np.inf, np.float32), np.where(adm > 0.5, np.inf, NEG)]), F32)
    qspec = lambda r: pl.BlockSpec((1, r, tq), lambda bi, i: (bi * nq + i, 0, 0))
    kspec = lambda s: pl.BlockSpec((1,) + s, lambda bi, i: (bi, 0, 0, 0), pipeline_mode=pl.Buffered(1))
    kern = functools.partial(_dsa_kernel, tq=tq, tk=tk, nk_static=nk, causal=causal, topk=topk, nvalid=nvalid)
    return pl.pallas_call(
        kern,
        grid=(b, nq),
        in_specs=[qspec(512), qspec(512), qspec(512), qspec(8),
                  kspec((nk, tk, 512)), kspec((nk, HA * V_ROWS, tk)), kspec((nk, tk, 256)),
                  _const_spec((3, HA, tk, tq)), _const_spec((2, tk, tq)), _const_spec((tk, tk))],
        out_specs=qspec(512),
        out_shape=jax.ShapeDtypeStruct((g, 512, tq), BF16),
        scratch_shapes=[pltpu.VMEM((nk, tk, tq), I32), pltpu.VMEM((8, tq), F32),
                        pltpu.VMEM((512, tq), F32), pltpu.VMEM((8, tq), F32), pltpu.VMEM((2, HA, tk, tq), BF16),
                        pltpu.VMEM((nk + 3, 33, 8, tq), I32), pltpu.VMEM((nk + 3, 8, tq), I32),
                        pltpu.VMEM((tk, tq), F32), pltpu.VMEM((HA, tq), F32)],
        compiler_params=_cparams(("arbitrary", "arbitrary")),
        name="dsa_causal" if causal else "dsa_cached",
    )(qT, qihT, qilT, wiT, k4, vT4, ki34, bias, cap, ltri)


def _gdn_pre_kernel(conv_ref, prev_ref, hist_ref, small_ref, cw_ref, alog_ref, dtb_ref,
                    wm_ref, um_ref, qe_ref, oi_ref, e_ref, ext_ref, *, rows, nvalid):
    c = CHUNK
    n = HB * c
    ext_ref[0:8, :] = jnp.where(pl.program_id(1) == 0, hist_ref[0], prev_ref[0])
    ext_ref[8:8 + rows, :] = conv_ref[0]
    cb = ext_ref[5:5 + rows, :] * cw_ref[0:1, :]
    for j in range(1, CONV_B):
        cb = cb + ext_ref[5 + j:5 + j + rows, :] * cw_ref[j:j + 1, :]
    cb = _silu(cb)

    small = small_ref[0]
    pos = lax.broadcasted_iota(I32, (rows, 1), 0) % c
    rowv = (pos < nvalid).astype(F32)
    beta_all = _sigmoid(small) * rowv
    sp = small + dtb_ref[...]
    g_all = -jnp.exp(alog_ref[...]) * (jnp.maximum(sp, 0.0) + jnp.log(1.0 + jnp.exp(-jnp.abs(sp)))) * rowv

    ri = lax.broadcasted_iota(I32, (n, n), 0)
    ci = lax.broadcasted_iota(I32, (n, n), 1)
    same = (ri // c) == (ci // c)
    tri_b = (same & (ci <= ri)).astype(BF16)
    bd_f = same.astype(F32)
    bd_b = same.astype(BF16)
    wr = lax.broadcasted_iota(I32, (c, n), 0)
    wl = lax.broadcasted_iota(I32, (c, n), 1)
    grp = wl // c
    tri_w = (wl % c) <= wr
    strict_w = (wl % c) < wr
    eye_w = ((wl % c) == wr).astype(F32)
    nt = (((1,), (1,)), ((), ()))
    tn = (((0,), (0,)), ((), ()))

    def l2n(x):
        return x * lax.rsqrt(jnp.sum(x * x, axis=-1, keepdims=True) + EPS)

    def to_wide(full):
        out = jnp.where(grp == 0, full[0:c, :], 0.0)
        for h in range(1, HB):
            out = out + jnp.where(grp == h, full[h * c:(h + 1) * c, :], 0.0)
        return out

    def tile4(x):
        return jnp.concatenate([x] * HB, axis=0)

    def chunk(k0):
        def stack(fn):
            return jnp.concatenate([fn(h) for h in range(HB)], axis=0)

        rv = rowv[k0:k0 + c]
        q = stack(lambda h: l2n(cb[k0:k0 + c, h * DK:(h + 1) * DK]) * (DK ** -0.5))
        k = stack(lambda h: l2n(cb[k0:k0 + c, WB + h * DK:WB + (h + 1) * DK]) * rv)
        v = stack(lambda h: cb[k0:k0 + c, 2 * WB + h * DV:2 * WB + (h + 1) * DV] * rv)
        beta = stack(lambda h: beta_all[k0:k0 + c, _L_BB + h:_L_BB + h + 1])
        g = stack(lambda h: g_all[k0:k0 + c, _L_AB + h:_L_AB + h + 1])
        yield

        gh, gl = _split(jnp.broadcast_to(g, (n, 128)))
        gcum = _dot(tri_b, gh) + _dot(tri_b, gl)
        yield
        gcum_row = gcum.T[0:1, :]
        gcum_col = gcum[:, 0:1]
        col_w = to_wide(jnp.concatenate([gcum, gcum], axis=1))
        decay_w = jnp.where(tri_w, jnp.exp(jnp.where(tri_w, col_w - gcum_row, 0.0)), 0.0)
        kb = k * beta
        kbf = k.astype(BF16)
        kk_w = to_wide(lax.dot_general(kb.astype(BF16), kbf, nt, preferred_element_type=F32))
        qk_w = to_wide(lax.dot_general(q.astype(BF16), kbf, nt, preferred_element_type=F32))
        nmat_w = jnp.where(strict_w, kk_w * decay_w, 0.0)
        attn_w = jnp.where(tri_w, qk_w * decay_w, 0.0)
        yield

        inv_w = eye_w - nmat_w
        ph = nmat_w.astype(BF16)
        bh = tile4(ph) * bd_b
        for _ in range(int(np.log2(c)) - 1):
            ph = _dot(ph, bh).astype(BF16)
            yield
            bh = tile4(ph) * bd_b
            inv_w = inv_w + _dot(inv_w.astype(BF16), bh)
            yield
        rhs = jnp.concatenate([v * beta, kb * jnp.exp(gcum_col)], axis=1)
        sol = _dot3(tile4(inv_w) * bd_f, rhs)
        yield
        solb = sol.astype(BF16)
        aw = _dot((tile4(attn_w) * bd_f).astype(BF16), solb)
        yield
        ck = k0 // c
        oi_ref[0, ck] = aw[:, :DV]
        qe_ref[0, ck] = (q * jnp.exp(gcum_col) - aw[:, DV:]).astype(BF16)
        for h in range(HB):
            rs = slice(h * c, (h + 1) * c)
            g_last = gcum[(h + 1) * c - 1:(h + 1) * c, :]
            kd = (k[rs] * jnp.exp(g_last - gcum[rs])).astype(BF16)
            uw = lax.dot_general(kd, solb[rs], tn, preferred_element_type=F32)
            um_ref[0, ck, h] = uw[:, :DV]
            wm_ref[0, ck, h] = uw[:, DV:].astype(BF16)
            e_ref[0, ck, h:h + 1, :] = jnp.exp(g_last)
        e_ref[0, ck, HB:8, :] = jnp.zeros((8 - HB, 128), F32)

    gens = [chunk(k0) for k0 in range(0, rows, c)]
    while gens:
        alive = []
        for gen in gens:
            try:
                next(gen)
                alive.append(gen)
            except StopIteration:
                pass
        gens = alive


def _gdn_scan_kernel(wm_ref, um_ref, qe_ref, oi_ref, e_ref, gb_ref, s0_ref, ng_ref, ob_ref, sfin_ref, s_ref,
                     *, bb, g):
    c = CHUNK

    @pl.when(pl.program_id(1) == 0)
    def _():
        s_ref[...] = s0_ref[...]

    ng = ng_ref[...]
    for b in range(bb):
        for ck in range(g):
            for h in range(HB):
                s = s_ref[b, h]
                sb = s.astype(BF16)
                rs = slice(h * c, (h + 1) * c)
                o = _dot(qe_ref[b, ck, rs, :], sb) + oi_ref[b, ck, rs, :]
                s_ref[b, h] = e_ref[b, ck, h:h + 1, :] * s + (um_ref[b, ck, h] - _dot(wm_ref[b, ck, h], sb))
                gate = gb_ref[b, ck * c:(ck + 1) * c, h * DV:(h + 1) * DV]
                ob_ref[b, ck * c:(ck + 1) * c, h * DV:(h + 1) * DV] = (_rms(o, ng) * _silu(gate)).astype(BF16)
    sfin_ref[...] = s_ref[...]


def _gdn(conv_in, hist, gb, small, s0, conv_w, a_log, dt_bias, norm_gdn, nvalid):
    b, t, _ = conv_in.shape
    assert t % CHUNK == 0
    rows = 256 if t % 256 == 0 else CHUNK
    nc, cps = t // CHUNK, rows // CHUNK
    n = HB * CHUNK
    alog = jnp.zeros((1, 128), F32).at[0, _L_AB:_L_AB + HB].set(a_log)
    dtb = jnp.zeros((1, 128), F32).at[0, _L_AB:_L_AB + HB].set(dt_bias)
    row = lambda c_: pl.BlockSpec((1, rows, c_), lambda bi, ti: (bi, ti, 0))
    prev = pl.BlockSpec((1, 8, C_CONV_B), lambda bi, ti: (bi, jnp.maximum(ti * (rows // 8) - 1, 0), 0))
    per_b = lambda s: pl.BlockSpec((1,) + s, lambda bi, ti: (bi,) + (0,) * len(s))
    chunked = lambda s: pl.BlockSpec((1, cps) + s, lambda bi, ti: (bi, ti) + (0,) * len(s))
    wm, um, qe, oi, e = pl.pallas_call(
        functools.partial(_gdn_pre_kernel, rows=rows, nvalid=nvalid),
        grid=(b, t // rows),
        in_specs=[row(C_CONV_B), prev, per_b((8, C_CONV_B)), row(128),
                  _const_spec((CONV_B, C_CONV_B)), _const_spec((1, 128)), _const_spec((1, 128))],
        out_specs=[chunked((HB, DK, DV)), chunked((HB, DK, DV)), chunked((n, DK)), chunked((n, DV)),
                   chunked((8, 128))],
        out_shape=[jax.ShapeDtypeStruct((b, nc, HB, DK, DV), BF16), jax.ShapeDtypeStruct((b, nc, HB, DK, DV), F32),
                   jax.ShapeDtypeStruct((b, nc, n, DK), BF16), jax.ShapeDtypeStruct((b, nc, n, DV), F32),
                   jax.ShapeDtypeStruct((b, nc, 8, 128), F32)],
        scratch_shapes=[pltpu.VMEM((8 + rows, C_CONV_B), F32)],
        compiler_params=_cparams(("arbitrary", "arbitrary")),
        name="gdn_pre",
    )(conv_in, conv_in, hist, small, conv_w, alog, dtb)

    bb = 2 if b % 2 == 0 else 1
    g = 2 if nc % 2 == 0 else 1
    blk = lambda s: pl.BlockSpec((bb, g) + s, lambda bi, ci: (bi, ci) + (0,) * len(s))
    rowb = pl.BlockSpec((bb, g * CHUNK, WB), lambda bi, ci: (bi, ci, 0))
    state = pl.BlockSpec((bb, HB, DK, DV), lambda bi, ci: (bi, 0, 0, 0))
    return pl.pallas_call(
        functools.partial(_gdn_scan_kernel, bb=bb, g=g),
        grid=(b // bb, nc // g),
        in_specs=[blk((HB, DK, DV)), blk((HB, DK, DV)), blk((n, DK)), blk((n, DV)), blk((8, 128)), rowb, state,
                  _const_spec((1, DV))],
        out_specs=[rowb, state],
        out_shape=[jax.ShapeDtypeStruct((b, t, WB), BF16), jax.ShapeDtypeStruct((b, HB, DK, DV), F32)],
        scratch_shapes=[pltpu.VMEM((bb, HB, DK, DV), F32)],
        compiler_params=_cparams(("arbitrary", "arbitrary")),
        name="gdn_scan",
    )(wm, um, qe, oi, e, gb, s0, norm_gdn.reshape(1, DV))


def _merge_kernel(x_ref, oaT_ref, ob_ref, ga_ref, gbr_ref, wa_ref, wb_ref, wo_ref, nf_ref, x1_ref, h2_ref):
    ya = lax.dot_general(oaT_ref[0], wa_ref[...], (((0,), (0,)), ((), ())), preferred_element_type=F32)
    yb = _dot(ob_ref[...], wb_ref[...])
    mix = _sigmoid(ga_ref[...]) * ya + _sigmoid(gbr_ref[...]) * yb
    x1 = x_ref[...] + _dot(mix.astype(BF16), wo_ref[...])
    x1_ref[...] = x1
    h2_ref[...] = _rms(x1, nf_ref[...]).astype(BF16)


def _merge(x2d, oaT, ob, ga, gbr, w_proj_a, w_proj_b, w_out, norm_ffn, tm):
    n = x2d.shape[0]
    assert oaT.shape == (n // tm, WA, tm)
    row = lambda c: pl.BlockSpec((tm, c), lambda i: (i, 0))
    return pl.pallas_call(
        _merge_kernel,
        grid=(n // tm,),
        in_specs=[row(D_MODEL), pl.BlockSpec((1, WA, tm), lambda i: (i, 0, 0)), row(WB), row(D_MODEL), row(D_MODEL),
                  _const_spec((WA, D_MODEL)), _const_spec((WB, D_MODEL)), _const_spec((D_MODEL, D_MODEL)),
                  _const_spec((1, D_MODEL))],
        out_specs=[row(D_MODEL), row(D_MODEL)],
        out_shape=[jax.ShapeDtypeStruct((n, D_MODEL), F32), jax.ShapeDtypeStruct((n, D_MODEL), BF16)],
        compiler_params=_cparams(("arbitrary",)),
        name="merge",
    )(x2d, oaT, ob, ga, gbr, w_proj_a.astype(BF16), w_proj_b.astype(BF16), w_out.astype(BF16),
      norm_ffn.reshape(1, D_MODEL))


def _ffn_kernel(x1_ref, h2_ref, p_ref, hist_ref, wup_ref, cw_ref, wdn_ref, npl_ref, wpg_ref, wple_ref, nfin_ref,
                y_ref, tail_ref, ext_ref, *, tm):
    @pl.when(pl.program_id(1) == 0)
    def _():
        ext_ref[0:8, :] = hist_ref[0]

    h2 = h2_ref[0]
    ext_ref[8:8 + tm, :] = _dot(h2, wup_ref[:, 0:D_FF])
    u_val = _dot(h2, wup_ref[:, D_FF:2 * D_FF])
    cv = ext_ref[6:6 + tm, :] * cw_ref[0:1, :]
    for j in range(1, CONV_F):
        cv = cv + ext_ref[6 + j:6 + j + tm, :] * cw_ref[j:j + 1, :]
    tail = ext_ref[tm:tm + 8, :]
    ext_ref[0:8, :] = tail
    tail_ref[0] = tail
    act = 0.5 * cv * (1.0 + jnp.tanh(0.7978845608028654 * (cv + 0.044715 * (cv * cv * cv))))
    x2 = x1_ref[0] + _dot((act * u_val).astype(BF16), wdn_ref[...])
    gate = _sigmoid(_dot(_rms(x2, npl_ref[...]).astype(BF16), wpg_ref[...]))
    x3 = x2 + gate * _dot(p_ref[0].astype(BF16), wple_ref[...])
    y_ref[0] = _rms(x3, nfin_ref[...])


def _ffn_ple(x1, h2, p, hist, w_up, conv_ffn, w_down, norm_ple, w_ple_gate, w_ple, norm_final, tm):
    b, t, _ = x1.shape
    assert t % tm == 0 and tm >= 8
    row = lambda c: pl.BlockSpec((1, tm, c), lambda bi, ti: (bi, ti, 0))
    per_b = pl.BlockSpec((1, 8, D_FF), lambda bi, ti: (bi, 0, 0))
    return pl.pallas_call(
        functools.partial(_ffn_kernel, tm=tm),
        grid=(b, t // tm),
        in_specs=[row(D_MODEL), row(D_MODEL), row(D_PLE), per_b,
                  _const_spec((D_MODEL, 2 * D_FF)), _const_spec((CONV_F, D_FF)), _const_spec((D_FF, D_MODEL)),
                  _const_spec((1, D_MODEL)), _const_spec((D_MODEL, D_MODEL)), _const_spec((D_PLE, D_MODEL)),
                  _const_spec((1, D_MODEL))],
        out_specs=[row(D_MODEL), per_b],
        out_shape=[jax.ShapeDtypeStruct((b, t, D_MODEL), F32), jax.ShapeDtypeStruct((b, 8, D_FF), F32)],
        scratch_shapes=[pltpu.VMEM((8 + tm, D_FF), F32)],
        compiler_params=_cparams(("arbitrary", "arbitrary")),
        name="ffn_ple",
    )(x1, h2, p, hist, w_up.astype(BF16), conv_ffn, w_down.astype(BF16), norm_ple.reshape(1, D_MODEL),
      w_ple_gate.astype(BF16), w_ple.astype(BF16), norm_final.reshape(1, D_MODEL))


def _pad_hist(hist, rows=8):
    b, r, c = hist.shape
    return jnp.concatenate([jnp.zeros((b, rows - r, c), hist.dtype), hist], axis=1)


def _layer(x, p, past_k, past_v, past_kidx, s_gdn, conv_b_hist, ffn_hist, wts, *, tm, tq):
    (norm_mix, w_in, conv_b, a_log, dt_bias, norm_gdn, w_proj_a, w_proj_b, w_out, norm_ffn, w_up, conv_ffn,
     w_down, norm_ple, w_ple, w_ple_gate, rel_bias, norm_final) = wts
    b, t, _ = x.shape
    n = b * t
    past = past_k.shape[1]
    topk = min(TOPK_MAX, (past + t) // 4)
    x2d = x.reshape(n, D_MODEL)
    tmi = min(tm, n)
    (ka, va, kbf, conv_in, gb, ga, gbr, small, ki3, qT, qihT, qilT, wiT, vTa) = _in_proj(x2d, norm_mix, w_in, tmi)

    if past == 0:
        assert tq == tmi and t % tq == 0 and tq % CHUNK == 0 and tq >= topk
        nq, tk, lanes = t // tq, tq, tq
        q_ops = (qT, qihT, qilT, wiT)
        k4 = kbf.reshape(b, nq, tk, WA)
        vT4 = vTa.reshape(b, nq, HA * V_ROWS, tk)
        ki34 = ki3.reshape(b, nq, tk, 256)
        kj = np.arange(tk)[:, None]
        adm = ((kj // CHUNK) <= (np.arange(tq)[None, :] // CHUNK)).astype(np.float32)
        nvalid, causal = tq, True
    else:
        tk, lanes, nq = 256, 128, 1
        assert n == tmi and past % tk == 0 and t <= tk and t <= lanes
        nkc = past // tk
        per_b = lambda a, width: jnp.pad(jnp.swapaxes(a[0].reshape(a.shape[1], b, t), 0, 1),
                                         ((0, 0), (0, 0), (0, width - t)))
        q_ops = tuple(per_b(a, lanes) for a in (qT, qihT, qilT, wiT))
        padk = lambda new, old: jnp.concatenate(
            [old, new.reshape(b, t, new.shape[-1]), jnp.zeros((b, tk - t, new.shape[-1]), new.dtype)], axis=1)
        k4 = padk(kbf, past_k.reshape(b, past, WA).astype(BF16)).reshape(b, nkc + 1, tk, WA)
        vc = jnp.transpose(past_v.reshape(b, nkc, tk, HA, DH).astype(BF16), (0, 1, 3, 4, 2))
        vc = jnp.concatenate([vc, jnp.ones((b, nkc, HA, V_ROWS - DH, tk), BF16)], axis=3)
        vT4 = jnp.concatenate([vc.reshape(b, nkc, HA * V_ROWS, tk), per_b(vTa, tk)[:, None]], axis=1)
        pk = past_kidx.astype(F32)
        pkh = pk.astype(BF16)
        pkl = (pk - pkh.astype(F32)).astype(BF16)
        past3 = jnp.concatenate([pkh, pkh, pkl, jnp.zeros_like(pkh)], axis=-1)
        ki34 = padk(ki3, past3).reshape(b, nkc + 1, tk, 256)
        adm = np.broadcast_to(np.arange(tk)[:, None] < t, (tk, lanes)).astype(np.float32)
        nvalid, causal = t, False
    bias = _bias_tables(rel_bias, adm, tk, lanes, nvalid)
    oT = _dsa(*q_ops, k4, vT4, ki34, bias, adm, nq=nq, causal=causal, topk=topk, nvalid=nvalid)
    if past:
        oT = jnp.swapaxes(oT[:, :, :t], 0, 1).reshape(1, WA, n)

    tp = -(-t // CHUNK) * CHUNK
    padt = lambda a: jnp.pad(a.reshape(b, t, a.shape[-1]), ((0, 0), (0, tp - t), (0, 0)))
    ob, s_new = _gdn(padt(conv_in), _pad_hist(conv_b_hist), padt(gb), padt(small), s_gdn, conv_b, a_log, dt_bias,
                     norm_gdn, nvalid=min(t, CHUNK))
    ob = ob[:, :t].reshape(n, WB)
    new_conv_b = jnp.concatenate([conv_b_hist, conv_in.reshape(b, t, C_CONV_B)], axis=1)[:, t:]

    x1, h2 = _merge(x2d, oT, ob, ga, gbr, w_proj_a, w_proj_b, w_out, norm_ffn, tmi)
    tmf = min(tm, t)
    y, tail = _ffn_ple(x1.reshape(b, t, D_MODEL), h2.reshape(b, t, D_MODEL), p, _pad_hist(ffn_hist), w_up,
                       conv_ffn, w_down, norm_ple, w_ple_gate, w_ple, norm_final, tmf)
    new_ffn = tail[:, 8 - (CONV_F - 1):]
    return (y, ka.reshape(b, t, HA, DH), va.reshape(b, t, HA, DH), small[:, :D_IDX].reshape(b, t, D_IDX),
            s_new, new_conv_b, new_ffn)


def kernel(x_prompt, x_sample, p_prompt, p_sample, cache_k, cache_v, cache_kidx, state_gdn, state_gdn_conv,
           state_ffn_conv, norm_mix, w_in, conv_b, a_log, dt_bias, norm_gdn, w_proj_a, w_proj_b, w_out, norm_ffn,
           w_up, conv_ffn, w_down, norm_ple, w_ple, w_ple_gate, rel_bias, norm_final):
    assert norm_mix.shape[0] == 1
    bp = x_prompt.shape[0]
    dt = x_prompt.dtype
    wts = (norm_mix[0], w_in[0], conv_b[0], a_log[0], dt_bias[0], norm_gdn[0], w_proj_a[0], w_proj_b[0], w_out[0],
           norm_ffn[0], w_up[0], conv_ffn[0], w_down[0], norm_ple[0], w_ple[0], w_ple_gate[0], rel_bias, norm_final)
    outs_p = _layer(x_prompt, p_prompt[0], jnp.zeros((bp, 0, HA, DH), dt), jnp.zeros((bp, 0, HA, DH), dt),
                    jnp.zeros((bp, 0, D_IDX), dt), jnp.zeros((bp, HB, DK, DV), dt),
                    jnp.zeros((bp, CONV_B - 1, C_CONV_B), dt), jnp.zeros((bp, CONV_F - 1, D_FF), dt),
                    wts, tm=256, tq=256)
    outs_s = _layer(x_sample, p_sample[0], cache_k[0], cache_v[0], cache_kidx[0], state_gdn[0],
                    state_gdn_conv[0], state_ffn_conv[0], wts, tm=256, tq=256)
    yp, ys = outs_p[0], outs_s[0]
    return (yp, ys) + tuple(a[None] for a in outs_p[1:]) + tuple(a[None] for a in outs_s[1:])
```

```python
import functools

import numpy as np
import jax
import jax.numpy as jnp
from jax import lax
from jax.experimental import pallas as pl
from jax.experimental.pallas import tpu as pltpu

F32 = jnp.float32
BF16 = jnp.bfloat16
I32 = jnp.int32

D_MODEL = 1024
CHUNK = 64
HA, DH = 8, 64
H_IDX, D_IDX = 8, 64
TOPK_MAX = 256
NUM_BUCKETS, MAX_DISTANCE = 32, 128
HB, DK, DV = 4, 128, 128
CONV_B = 4
D_FF = 2816
CONV_F = 3
D_PLE = 256
EPS = 1e-6
NEG = -1e30
WA = HA * DH
WB = HB * DK
C_CONV_B = 3 * WB
INT_MIN = -2 ** 31
LOG2E = 1.4426950408889634

_O_QA, _O_KA, _O_VA, _O_QI, _O_KI, _O_WI = 0, 512, 1024, 1536, 2048, 2112
_O_QB, _O_GB, _O_BB, _O_AB, _O_GA, _O_GBR = 2120, 3656, 4168, 4172, 4176, 5200
_L_WI, _L_BB, _L_AB = 64, 72, 76

VMEM_LIMIT = 56 * 1024 * 1024


def _cparams(sem):
    return pltpu.CompilerParams(dimension_semantics=sem, vmem_limit_bytes=VMEM_LIMIT)


def _const_spec(shape):
    nd = len(shape)
    return pl.BlockSpec(shape, lambda *_: (0,) * nd, pipeline_mode=pl.Buffered(1))


def _rms(x, g):
    return x * lax.rsqrt(jnp.mean(x * x, axis=-1, keepdims=True) + EPS) * g


def _split(x):
    hi = x.astype(BF16)
    lo = (x - hi.astype(F32)).astype(BF16)
    return hi, lo


def _dot(a, b):
    return jnp.dot(a, b, preferred_element_type=F32)


def _dot3(a, b):
    ah, al = _split(a)
    bh, bl = _split(b)
    return _dot(ah, bh) + (_dot(al, bh) + _dot(ah, bl))


def _sigmoid(x):
    return 1.0 / (1.0 + jnp.exp(-x))


def _silu(x):
    return x * _sigmoid(x)


def _transpose32(a):
    a = list(a)
    j, m = 16, 0x0000FFFF
    while j:
        k = 0
        while k < 32:
            t = (a[k] ^ lax.shift_right_logical(a[k + j], jnp.int32(j))) & jnp.int32(m - (1 << 32) if m >> 31 else m)
            a[k] = a[k] ^ t
            a[k + j] = a[k + j] ^ (t << j)
            k = (k + j + 1) & ~j
        j >>= 1
        m = (m ^ (m << j)) & 0xFFFFFFFF
    return a


V_ROWS = DH + 16
RADIX_TILES = 8
_NT = (((1,), (1,)), ((), ()))


def _in_proj_kernel(x_ref, g_ref, wm_ref, wt_ref, wqh_ref, wql_ref, wsh_ref, wsl_ref, wwh_ref, wwl_ref,
                    ph_ref, plo_ref, sc_ref,
                    ka_ref, va_ref, kbf_ref, conv_ref, gb_ref, ga_ref, gbr_ref, small_ref, ki3_ref,
                    qT_ref, qihT_ref, qilT_ref, wiT_ref, vTa_ref):
    tm = x_ref.shape[0]
    h = _rms(x_ref[...], g_ref[...])
    hh, hl = _split(h)

    def main(lo, hi):
        return _dot(hh, wm_ref[:, lo:hi])

    def nt(w, a):
        return lax.dot_general(w, a, _NT, preferred_element_type=F32)

    def nt3(wh_ref, wl_ref):
        return nt(wh_ref[...], hh) + (nt(wh_ref[...], hl) + nt(wl_ref[...], hh))

    ka = main(0, 512)
    ka_ref[...] = ka
    kbf_ref[...] = ka.astype(BF16)
    va_ref[...] = main(512, 1024)
    conv_ref[...] = main(1024, 2560)
    gb_ref[...] = main(2560, 3072)
    ga_ref[...] = main(3072, 4096)
    gbr_ref[...] = main(4096, 5120)

    qT_ref[0] = nt(wt_ref[0:WA, :], hh).astype(BF16)
    vT = nt(wt_ref[WA:2 * WA, :], hh).astype(BF16)
    ones = jnp.ones((V_ROWS - DH, tm), BF16)
    for hd in range(HA):
        vTa_ref[0, hd * V_ROWS:hd * V_ROWS + DH, :] = vT[hd * DH:(hd + 1) * DH]
        vTa_ref[0, hd * V_ROWS + DH:(hd + 1) * V_ROWS, :] = ones
    qh, ql = _split(nt3(wqh_ref, wql_ref))
    qihT_ref[0] = qh
    qilT_ref[0] = ql
    wiT_ref[0] = nt3(wwh_ref, wwl_ref)[0:H_IDX] * (H_IDX ** -0.5)

    small = (_dot(hh, wsh_ref[...]) + (_dot(hl, wsh_ref[...]) + _dot(hh, wsl_ref[...]))) * sc_ref[...]
    small_ref[...] = small
    sh, sl = _split(small)
    ki3_ref[...] = (_dot(sh, ph_ref[...]) + _dot(sl, plo_ref[...])).astype(BF16)


def _in_proj(x2d, norm_mix, w_in, tm):
    n = x2d.shape[0]
    assert n % tm == 0
    g = n // tm
    w = w_in
    wm = jnp.concatenate([w[:, _O_KA:_O_QI], w[:, _O_QB:_O_BB], w[:, _O_GA:]], axis=1).astype(BF16)
    wt = jnp.concatenate([w[:, _O_QA:_O_KA] * (DH ** -0.5 * LOG2E), w[:, _O_VA:_O_QI]], axis=1).T.astype(BF16)
    hilo = lambda a: (a.astype(BF16), (a - a.astype(BF16).astype(F32)).astype(BF16))
    wqh, wql = hilo((w[:, _O_QI:_O_KI] * (D_IDX ** -0.5)).T)
    wsh, wsl = hilo(jnp.concatenate([w[:, _O_KI:_O_QB], w[:, _O_BB:_O_GA], jnp.zeros((D_MODEL, 48), F32)], axis=1))
    wwh, wwl = hilo(jnp.concatenate([w[:, _O_WI:_O_QB], jnp.zeros((D_MODEL, 8), F32)], axis=1).T)
    ph = np.zeros((128, 256), np.float32)
    plo = np.zeros((128, 256), np.float32)
    for c in range(64):
        ph[c, c] = 1.0
        ph[c, 64 + c] = 1.0
        plo[c, 128 + c] = 1.0
    sc = np.ones((1, 128), np.float32)
    sc[0, _L_WI:_L_WI + H_IDX] = H_IDX ** -0.5
    row = lambda c: pl.BlockSpec((tm, c), lambda i: (i, 0))
    colT = lambda r: pl.BlockSpec((1, r, tm), lambda i: (i, 0, 0))
    out_cols = [(512, F32), (512, F32), (512, BF16), (1536, F32), (512, F32), (1024, F32), (1024, F32), (128, F32),
                (256, BF16)]
    out_rows = [(WA, BF16), (WA, BF16), (WA, BF16), (H_IDX, F32), (HA * V_ROWS, BF16)]
    return pl.pallas_call(
        _in_proj_kernel,
        grid=(g,),
        in_specs=[row(D_MODEL), _const_spec((1, D_MODEL)), _const_spec((D_MODEL, 5120)),
                  _const_spec((2 * WA, D_MODEL)), _const_spec((WA, D_MODEL)), _const_spec((WA, D_MODEL)),
                  _const_spec((D_MODEL, 128)), _const_spec((D_MODEL, 128)),
                  _const_spec((16, D_MODEL)), _const_spec((16, D_MODEL)),
                  _const_spec((128, 256)), _const_spec((128, 256)), _const_spec((1, 128))],
        out_specs=[row(c) for c, _ in out_cols] + [colT(r) for r, _ in out_rows],
        out_shape=[jax.ShapeDtypeStruct((n, c), d) for c, d in out_cols]
        + [jax.ShapeDtypeStruct((g, r, tm), d) for r, d in out_rows],
        compiler_params=_cparams(("arbitrary",)),
        name="in_proj",
    )(x2d, norm_mix.reshape(1, D_MODEL), wm, wt, wqh, wql, wsh, wsl, wwh, wwl, jnp.asarray(ph, BF16),
      jnp.asarray(plo, BF16), jnp.asarray(sc))


def _dsa_kernel(qT_ref, qihT_ref, qilT_ref, wiT_ref, k_ref, vT_ref, ki3_ref, bias_ref, cap_ref,
                o_ref, key_ref, l_ref, acc_ref, lg_ref, pln_ref, eq_ref, sc_ref, ma_ref,
                *, tq, tk, nk_static, causal, topk):
    nk = (pl.program_id(1) + 1) if causal else nk_static
    wiT = wiT_ref[0]

    def when(cond):
        if isinstance(cond, bool):
            return (lambda f: f()) if cond else (lambda f: None)
        return pl.when(cond)

    zeros64 = jnp.zeros((64, tq), BF16)
    q3 = []
    for h in range(H_IDX):
        hi = qihT_ref[0, h * 64:(h + 1) * 64, :]
        lo = qilT_ref[0, h * 64:(h + 1) * 64, :]
        q3.append(jnp.concatenate([hi, lo, hi, zeros64], axis=0))

    def score_matmuls(j):
        kt = ki3_ref[0, j]
        acc = None
        for h in range(H_IDX):
            t = jnp.maximum(_dot(kt, q3[h]), 0.0) * wiT[h:h + 1, :]
            acc = t if acc is None else acc + t
        sc_ref[...] = jnp.minimum(acc, cap_ref[jnp.where(j == nk - 1, 1, 0)])

    def score_finish(j):
        bits = pltpu.bitcast(sc_ref[...], I32)
        bits = jnp.where(bits == INT_MIN, 0, bits)
        key = bits ^ ((bits >> 31) & 0x7FFFFFFF)
        key_ref[j] = key
        u3 = (key ^ INT_MIN).reshape(tk // 8, 8, tq)
        ones = jnp.full((8, tq), -1, I32)
        pln_ref[0, j] = ones
        for l0 in range(0, tq, 128):
            planes = _transpose32([u3[r][:, l0:l0 + 128] for r in range(32)])
            for b in range(32):
                pln_ref[b + 1, j, :, l0:l0 + 128] = planes[b]
        eq_ref[0, j] = ones

    def score_body(j, c):
        score_finish(j - 1)
        score_matmuls(j)
        return c

    score_matmuls(0)
    lax.fori_loop(1, nk, score_body, 0)
    score_finish(nk - 1)

    for d in range(RADIX_TILES - 1):
        pln_ref[:, nk + d] = jnp.zeros((33, 8, tq), I32)
        eq_ref[0, nk + d] = jnp.zeros((8, tq), I32)

    sub = lax.broadcasted_iota(I32, (8, tq), 0)
    group_masks = (-0x10000, -0xFF0100, -0xF0F0F10, -0x33333334, -0x55555556)

    def pos_plane(e, j):
        if e < 6:
            return jnp.broadcast_to(-((~j >> (5 - e)) & 1), (8, tq))
        if e < 11:
            return jnp.full((8, tq), group_masks[e - 6], I32)
        return -((~sub >> (13 - e)) & 1)

    def sweep(prev, cur, carry, src, dst):
        n_gt, flip = carry

        def body(jq, cnt):
            for d in range(RADIX_TILES):
                j = jq * RADIX_TILES + d
                e = eq_ref[src, j] & (prev(j) ^ flip)
                eq_ref[dst, j] = e
                cnt = cnt + lax.population_count(e & cur(j))
            return cnt

        cnt = lax.fori_loop(0, (nk + RADIX_TILES - 1) // RADIX_TILES, body, jnp.zeros((8, tq), I32))
        cnt = n_gt + cnt.sum(axis=0, keepdims=True)
        acc = cnt >= topk
        return acc, (jnp.where(acc, n_gt, cnt), jnp.where(acc, 0, -1))

    def key_sweep(it, carry, src, dst):
        tu, rest = carry
        acc, rest = sweep(lambda j: pln_ref[it, j], lambda j: pln_ref[it + 1, j], rest, src, dst)
        return jnp.where(acc, tu | jnp.left_shift(jnp.int32(1), 31 - it), tu), rest

    def two_bits(i, carry):
        return key_sweep(2 * i + 1, key_sweep(2 * i, carry, 0, 1), 1, 0)

    zero_row = jnp.zeros((1, tq), I32)
    tu, rest = lax.fori_loop(0, 16, two_bits, (zero_row, (zero_row, zero_row)))
    thr = tu ^ INT_MIN
    inv_pos = zero_row
    for e in range(14):
        prev = (lambda j: pln_ref[32, j]) if e == 0 else functools.partial(pos_plane, e - 1)
        acc, rest = sweep(prev, functools.partial(pos_plane, e), rest, e % 2, 1 - e % 2)
        inv_pos = jnp.where(acc, inv_pos | (1 << (13 - e)), inv_pos)
    pos_thr = ~inv_pos & 0x3FFF

    l_ref[...] = jnp.zeros(l_ref.shape, F32)
    acc_ref[...] = jnp.zeros(acc_ref.shape, F32)
    qm = []
    for h in range(HA):
        qh = qT_ref[0, h * 64:(h + 1) * 64, :]
        qm.append(jnp.concatenate([qh, zeros64] if h % 2 == 0 else [zeros64, qh], axis=0))
    row = lax.broadcasted_iota(I32, (tk, tq), 0)

    def tile_logits(j, slot):
        sel = key_ref[j] > jnp.where(row <= pos_thr - j * tk, thr - 1, thr)
        kind = jnp.clip(j - (nk - 3), 0, 2)
        mx = []
        for h in range(HA):
            pr = h // 2
            lg = _dot(k_ref[0, j, :, pr * 128:(pr + 1) * 128], qm[h]) + bias_ref[kind, h]
            lg = jnp.where(sel, lg, NEG).astype(BF16)
            lg_ref[slot, h] = lg
            part = jnp.max(lg.reshape(tk // 16, 16, tq), axis=0).astype(F32)
            mx.append(jnp.max(part, axis=0, keepdims=True))
        return jnp.concatenate(mx, axis=0)

    def tile_values(j, slot, mx):
        m_old = ma_ref[...]
        m_new = jnp.maximum(m_old, mx)
        alpha = jnp.exp2(m_old - m_new)
        ma_ref[...] = m_new
        ls = []
        for h in range(HA):
            p = jnp.exp2(lg_ref[slot, h] - m_new[h:h + 1, :].astype(BF16))
            pv = _dot(vT_ref[0, j, h * V_ROWS:(h + 1) * V_ROWS, :], p)
            acc_ref[h * 64:(h + 1) * 64, :] = alpha[h:h + 1, :] * acc_ref[h * 64:(h + 1) * 64, :] + pv[0:DH]
            ls.append(pv[DH:DH + 1])
        l_ref[...] = alpha * l_ref[...] + jnp.concatenate(ls, axis=0)

    ma_ref[...] = jnp.full((HA, tq), NEG, F32)

    def pair(i, c):
        mx0 = tile_logits(2 * i, 0)
        mx1 = tile_logits(2 * i + 1, 1)
        tile_values(2 * i, 0, mx0)
        tile_values(2 * i + 1, 1, mx1)
        return c

    lax.fori_loop(0, nk // 2, pair, 0)

    @when(nk % 2 == 1)
    def _():
        tile_values(nk - 1, 0, tile_logits(nk - 1, 0))

    for h in range(HA):
        o_ref[0, h * 64:(h + 1) * 64, :] = (acc_ref[h * 64:(h + 1) * 64, :] / l_ref[h:h + 1, :]).astype(BF16)


def _t5_bucket_np(rel):
    half = NUM_BUCKETS // 2
    max_exact = half // 2
    out = np.zeros(rel.shape, np.int64)
    flat_rel = rel.reshape(-1)
    flat = out.reshape(-1)
    for a in range(flat_rel.size):
        r = int(flat_rel[a])
        n = abs(r)
        b = n if n < max_exact else min(half - 1, (n * n).bit_length() + 1)
        flat[a] = b + (half if r > 0 else 0)
    return out


def _bias_tables(rel_bias, adm, tk, tq, tq_valid):
    kj = np.arange(tk)[:, None]
    t = np.minimum(np.arange(tq), tq_valid - 1)[None, :]
    rel_to_bucket = _t5_bucket_np(np.arange(-2 * tk - tq, tk + 1))
    lut = lambda rel: rel_to_bucket[rel + 2 * tk + tq]
    far_bucket = NUM_BUCKETS // 2 - 1
    idx = np.stack([np.full((tk, tq), far_bucket), lut(kj - tk - t), lut(kj - t)], axis=0)
    tab = rel_bias.astype(F32) * LOG2E
    onehot = jnp.asarray(idx[..., None] == np.arange(NUM_BUCKETS), F32)
    bias = jnp.einsum("ktqb,bh->khtq", onehot, tab - tab[far_bucket], precision=lax.Precision.HIGHEST)
    mask = np.zeros((3, 1, tk, tq), np.float32)
    mask[2, 0] = np.where(adm > 0.5, 0.0, NEG)
    return bias + jnp.asarray(mask)


def _dsa(qT, qihT, qilT, wiT, k4, vT4, ki34, bias, adm, *, nq, causal, topk):
    g, _, tq = qT.shape
    b, nk, tk, _ = k4.shape
    assert g == b * nq and tk == 256 and nk + RADIX_TILES - 1 <= 64
    cap = jnp.asarray(np.stack([np.full((tk, tq), np.inf, np.float32), np.where(adm > 0.5, np.inf, NEG)]), F32)
    qspec = lambda r: pl.BlockSpec((1, r, tq), lambda bi, i: (bi * nq + i, 0, 0))
    kspec = lambda s: pl.BlockSpec((1,) + s, lambda bi, i: (bi, 0, 0, 0), pipeline_mode=pl.Buffered(1))
    kern = functools.partial(_dsa_kernel, tq=tq, tk=tk, nk_static=nk, causal=causal, topk=topk)
    return pl.pallas_call(
        kern,
        grid=(b, nq),
        in_specs=[qspec(512), qspec(512), qspec(512), qspec(8),
                  kspec((nk, tk, 512)), kspec((nk, HA * V_ROWS, tk)), kspec((nk, tk, 256)),
                  _const_spec((3, HA, tk, tq)), _const_spec((2, tk, tq))],
        out_specs=qspec(512),
        out_shape=jax.ShapeDtypeStruct((g, 512, tq), BF16),
        scratch_shapes=[pltpu.VMEM((nk, tk, tq), I32), pltpu.VMEM((8, tq), F32),
                        pltpu.VMEM((512, tq), F32), pltpu.VMEM((2, HA, tk, tq), BF16),
                        pltpu.VMEM((33, nk + RADIX_TILES - 1, 8, tq), I32),
                        pltpu.VMEM((2, nk + RADIX_TILES - 1, 8, tq), I32),
                        pltpu.VMEM((tk, tq), F32), pltpu.VMEM((HA, tq), F32)],
        compiler_params=_cparams(("arbitrary", "arbitrary")),
        name="dsa_causal" if causal else "dsa_cached",
    )(qT, qihT, qilT, wiT, k4, vT4, ki34, bias, cap)


def _gdn_pre_kernel(conv_ref, prev_ref, hist_ref, small_ref, cw_ref, alog_ref, dtb_ref,
                    wm_ref, um_ref, qe_ref, oi_ref, e_ref, ext_ref, *, rows, nvalid):
    c = CHUNK
    n = HB * c
    ext_ref[0:8, :] = jnp.where(pl.program_id(1) == 0, hist_ref[0], prev_ref[0])
    ext_ref[8:8 + rows, :] = conv_ref[0]
    cb = ext_ref[5:5 + rows, :] * cw_ref[0:1, :]
    for j in range(1, CONV_B):
        cb = cb + ext_ref[5 + j:5 + j + rows, :] * cw_ref[j:j + 1, :]
    cb = _silu(cb)

    small = small_ref[0]
    pos = lax.broadcasted_iota(I32, (rows, 1), 0) % c
    rowv = (pos < nvalid).astype(F32)
    beta_all = _sigmoid(small) * rowv
    sp = small + dtb_ref[...]
    g_all = -jnp.exp(alog_ref[...]) * (jnp.maximum(sp, 0.0) + jnp.log(1.0 + jnp.exp(-jnp.abs(sp)))) * rowv

    ri = lax.broadcasted_iota(I32, (n, n), 0)
    ci = lax.broadcasted_iota(I32, (n, n), 1)
    same = (ri // c) == (ci // c)
    tri_b = (same & (ci <= ri)).astype(BF16)
    bd_f = same.astype(F32)
    bd_b = same.astype(BF16)
    wr = lax.broadcasted_iota(I32, (c, n), 0)
    wl = lax.broadcasted_iota(I32, (c, n), 1)
    grp = wl // c
    tri_w = (wl % c) <= wr
    strict_w = (wl % c) < wr
    eye_w = ((wl % c) == wr).astype(F32)
    nt = (((1,), (1,)), ((), ()))
    tn = (((0,), (0,)), ((), ()))

    def l2n(x):
        return x * lax.rsqrt(jnp.sum(x * x, axis=-1, keepdims=True) + EPS)

    def to_wide(full):
        out = jnp.where(grp == 0, full[0:c, :], 0.0)
        for h in range(1, HB):
            out = out + jnp.where(grp == h, full[h * c:(h + 1) * c, :], 0.0)
        return out

    def tile4(x):
        return jnp.concatenate([x] * HB, axis=0)

    def chunk(k0):
        def stack(fn):
            return jnp.concatenate([fn(h) for h in range(HB)], axis=0)

        rv = rowv[k0:k0 + c]
        q = stack(lambda h: l2n(cb[k0:k0 + c, h * DK:(h + 1) * DK]) * (DK ** -0.5))
        k = stack(lambda h: l2n(cb[k0:k0 + c, WB + h * DK:WB + (h + 1) * DK]) * rv)
        v = stack(lambda h: cb[k0:k0 + c, 2 * WB + h * DV:2 * WB + (h + 1) * DV] * rv)
        beta = stack(lambda h: beta_all[k0:k0 + c, _L_BB + h:_L_BB + h + 1])
        g = stack(lambda h: g_all[k0:k0 + c, _L_AB + h:_L_AB + h + 1])
        yield

        gh, gl = _split(jnp.broadcast_to(g, (n, 128)))
        gcum = _dot(tri_b, gh) + _dot(tri_b, gl)
        yield
        gcum_row = gcum.T[0:1, :]
        gcum_col = gcum[:, 0:1]
        col_w = to_wide(jnp.concatenate([gcum, gcum], axis=1))
        decay_w = jnp.where(tri_w, jnp.exp(jnp.where(tri_w, col_w - gcum_row, 0.0)), 0.0)
        kb = k * beta
        kbf = k.astype(BF16)
        kk_w = to_wide(lax.dot_general(kb.astype(BF16), kbf, nt, preferred_element_type=F32))
        qk_w = to_wide(lax.dot_general(q.astype(BF16), kbf, nt, preferred_element_type=F32))
        nmat_w = jnp.where(strict_w, kk_w * decay_w, 0.0)
        attn_w = jnp.where(tri_w, qk_w * decay_w, 0.0)
        yield

        inv_w = eye_w - nmat_w
        ph = nmat_w.astype(BF16)
        bh = tile4(ph) * bd_b
        for _ in range(int(np.log2(c)) - 1):
            ph = _dot(ph, bh).astype(BF16)
            yield
            bh = tile4(ph) * bd_b
            inv_w = inv_w + _dot(inv_w.astype(BF16), bh)
            yield
        rhs = jnp.concatenate([v * beta, kb * jnp.exp(gcum_col)], axis=1)
        sol = _dot3(tile4(inv_w) * bd_f, rhs)
        yield
        solb = sol.astype(BF16)
        aw = _dot((tile4(attn_w) * bd_f).astype(BF16), solb)
        yield
        ck = k0 // c
        oi_ref[0, ck] = aw[:, :DV]
        qe_ref[0, ck] = (q * jnp.exp(gcum_col) - aw[:, DV:]).astype(BF16)
        for h in range(HB):
            rs = slice(h * c, (h + 1) * c)
            g_last = gcum[(h + 1) * c - 1:(h + 1) * c, :]
            kd = (k[rs] * jnp.exp(g_last - gcum[rs])).astype(BF16)
            uw = lax.dot_general(kd, solb[rs], tn, preferred_element_type=F32)
            um_ref[0, ck, h] = uw[:, :DV]
            wm_ref[0, ck, h] = uw[:, DV:].astype(BF16)
            e_ref[0, ck, h:h + 1, :] = jnp.exp(g_last)
        e_ref[0, ck, HB:8, :] = jnp.zeros((8 - HB, 128), F32)

    gens = [chunk(k0) for k0 in range(0, rows, c)]
    while gens:
        alive = []
        for gen in gens:
            try:
                next(gen)
                alive.append(gen)
            except StopIteration:
                pass
        gens = alive


def _gdn_scan_kernel(wm_ref, um_ref, qe_ref, oi_ref, e_ref, gb_ref, s0_ref, ng_ref, ob_ref, sfin_ref, s_ref,
                     *, bb, g):
    c = CHUNK

    @pl.when(pl.program_id(1) == 0)
    def _():
        s_ref[...] = s0_ref[...]

    ng = ng_ref[...]
    for b in range(bb):
        for ck in range(g):
            for h in range(HB):
                s = s_ref[b, h]
                sb = s.astype(BF16)
                rs = slice(h * c, (h + 1) * c)
                o = _dot(qe_ref[b, ck, rs, :], sb) + oi_ref[b, ck, rs, :]
                s_ref[b, h] = e_ref[b, ck, h:h + 1, :] * s + (um_ref[b, ck, h] - _dot(wm_ref[b, ck, h], sb))
                gate = gb_ref[b, ck * c:(ck + 1) * c, h * DV:(h + 1) * DV]
                ob_ref[b, ck * c:(ck + 1) * c, h * DV:(h + 1) * DV] = (_rms(o, ng) * _silu(gate)).astype(BF16)
    sfin_ref[...] = s_ref[...]


def _gdn(conv_in, hist, gb, small, s0, conv_w, a_log, dt_bias, norm_gdn, nvalid):
    b, t, _ = conv_in.shape
    assert t % CHUNK == 0
    rows = 256 if t % 256 == 0 else CHUNK
    nc, cps = t // CHUNK, rows // CHUNK
    n = HB * CHUNK
    alog = jnp.zeros((1, 128), F32).at[0, _L_AB:_L_AB + HB].set(a_log)
    dtb = jnp.zeros((1, 128), F32).at[0, _L_AB:_L_AB + HB].set(dt_bias)
    row = lambda c_: pl.BlockSpec((1, rows, c_), lambda bi, ti: (bi, ti, 0))
    prev = pl.BlockSpec((1, 8, C_CONV_B), lambda bi, ti: (bi, jnp.maximum(ti * (rows // 8) - 1, 0), 0))
    per_b = lambda s: pl.BlockSpec((1,) + s, lambda bi, ti: (bi,) + (0,) * len(s))
    chunked = lambda s: pl.BlockSpec((1, cps) + s, lambda bi, ti: (bi, ti) + (0,) * len(s))
    wm, um, qe, oi, e = pl.pallas_call(
        functools.partial(_gdn_pre_kernel, rows=rows, nvalid=nvalid),
        grid=(b, t // rows),
        in_specs=[row(C_CONV_B), prev, per_b((8, C_CONV_B)), row(128),
                  _const_spec((CONV_B, C_CONV_B)), _const_spec((1, 128)), _const_spec((1, 128))],
        out_specs=[chunked((HB, DK, DV)), chunked((HB, DK, DV)), chunked((n, DK)), chunked((n, DV)),
                   chunked((8, 128))],
        out_shape=[jax.ShapeDtypeStruct((b, nc, HB, DK, DV), BF16), jax.ShapeDtypeStruct((b, nc, HB, DK, DV), F32),
                   jax.ShapeDtypeStruct((b, nc, n, DK), BF16), jax.ShapeDtypeStruct((b, nc, n, DV), F32),
                   jax.ShapeDtypeStruct((b, nc, 8, 128), F32)],
        scratch_shapes=[pltpu.VMEM((8 + rows, C_CONV_B), F32)],
        compiler_params=_cparams(("arbitrary", "arbitrary")),
        name="gdn_pre",
    )(conv_in, conv_in, hist, small, conv_w, alog, dtb)

    bb = 2 if b % 2 == 0 else 1
    g = 2 if nc % 2 == 0 else 1
    blk = lambda s: pl.BlockSpec((bb, g) + s, lambda bi, ci: (bi, ci) + (0,) * len(s))
    rowb = pl.BlockSpec((bb, g * CHUNK, WB), lambda bi, ci: (bi, ci, 0))
    state = pl.BlockSpec((bb, HB, DK, DV), lambda bi, ci: (bi, 0, 0, 0))
    return pl.pallas_call(
        functools.partial(_gdn_scan_kernel, bb=bb, g=g),
        grid=(b // bb, nc // g),
        in_specs=[blk((HB, DK, DV)), blk((HB, DK, DV)), blk((n, DK)), blk((n, DV)), blk((8, 128)), rowb, state,
                  _const_spec((1, DV))],
        out_specs=[rowb, state],
        out_shape=[jax.ShapeDtypeStruct((b, t, WB), BF16), jax.ShapeDtypeStruct((b, HB, DK, DV), F32)],
        scratch_shapes=[pltpu.VMEM((bb, HB, DK, DV), F32)],
        compiler_params=_cparams(("arbitrary", "arbitrary")),
        name="gdn_scan",
    )(wm, um, qe, oi, e, gb, s0, norm_gdn.reshape(1, DV))


def _merge_kernel(x_ref, oaT_ref, ob_ref, ga_ref, gbr_ref, wa_ref, wb_ref, wo_ref, nf_ref, x1_ref, h2_ref):
    ya = lax.dot_general(oaT_ref[0], wa_ref[...], (((0,), (0,)), ((), ())), preferred_element_type=F32)
    yb = _dot(ob_ref[...], wb_ref[...])
    mix = _sigmoid(ga_ref[...]) * ya + _sigmoid(gbr_ref[...]) * yb
    x1 = x_ref[...] + _dot(mix.astype(BF16), wo_ref[...])
    x1_ref[...] = x1
    h2_ref[...] = _rms(x1, nf_ref[...]).astype(BF16)


def _merge(x2d, oaT, ob, ga, gbr, w_proj_a, w_proj_b, w_out, norm_ffn, tm):
    n = x2d.shape[0]
    assert oaT.shape == (n // tm, WA, tm)
    row = lambda c: pl.BlockSpec((tm, c), lambda i: (i, 0))
    return pl.pallas_call(
        _merge_kernel,
        grid=(n // tm,),
        in_specs=[row(D_MODEL), pl.BlockSpec((1, WA, tm), lambda i: (i, 0, 0)), row(WB), row(D_MODEL), row(D_MODEL),
                  _const_spec((WA, D_MODEL)), _const_spec((WB, D_MODEL)), _const_spec((D_MODEL, D_MODEL)),
                  _const_spec((1, D_MODEL))],
        out_specs=[row(D_MODEL), row(D_MODEL)],
        out_shape=[jax.ShapeDtypeStruct((n, D_MODEL), F32), jax.ShapeDtypeStruct((n, D_MODEL), BF16)],
        compiler_params=_cparams(("arbitrary",)),
        name="merge",
    )(x2d, oaT, ob, ga, gbr, w_proj_a.astype(BF16), w_proj_b.astype(BF16), w_out.astype(BF16),
      norm_ffn.reshape(1, D_MODEL))


def _ffn_kernel(x1_ref, h2_ref, p_ref, hist_ref, wup_ref, cw_ref, wdn_ref, npl_ref, wpg_ref, wple_ref, nfin_ref,
                y_ref, tail_ref, ext_ref, *, tm):
    @pl.when(pl.program_id(1) == 0)
    def _():
        ext_ref[0:8, :] = hist_ref[0]

    h2 = h2_ref[0]
    ext_ref[8:8 + tm, :] = _dot(h2, wup_ref[:, 0:D_FF])
    u_val = _dot(h2, wup_ref[:, D_FF:2 * D_FF])
    cv = ext_ref[6:6 + tm, :] * cw_ref[0:1, :]
    for j in range(1, CONV_F):
        cv = cv + ext_ref[6 + j:6 + j + tm, :] * cw_ref[j:j + 1, :]
    tail = ext_ref[tm:tm + 8, :]
    ext_ref[0:8, :] = tail
    tail_ref[0] = tail
    act = 0.5 * cv * (1.0 + jnp.tanh(0.7978845608028654 * (cv + 0.044715 * (cv * cv * cv))))
    x2 = x1_ref[0] + _dot((act * u_val).astype(BF16), wdn_ref[...])
    gate = _sigmoid(_dot(_rms(x2, npl_ref[...]).astype(BF16), wpg_ref[...]))
    x3 = x2 + gate * _dot(p_ref[0].astype(BF16), wple_ref[...])
    y_ref[0] = _rms(x3, nfin_ref[...])


def _ffn_ple(x1, h2, p, hist, w_up, conv_ffn, w_down, norm_ple, w_ple_gate, w_ple, norm_final, tm):
    b, t, _ = x1.shape
    assert t % tm == 0 and tm >= 8
    row = lambda c: pl.BlockSpec((1, tm, c), lambda bi, ti: (bi, ti, 0))
    per_b = pl.BlockSpec((1, 8, D_FF), lambda bi, ti: (bi, 0, 0))
    return pl.pallas_call(
        functools.partial(_ffn_kernel, tm=tm),
        grid=(b, t // tm),
        in_specs=[row(D_MODEL), row(D_MODEL), row(D_PLE), per_b,
                  _const_spec((D_MODEL, 2 * D_FF)), _const_spec((CONV_F, D_FF)), _const_spec((D_FF, D_MODEL)),
                  _const_spec((1, D_MODEL)), _const_spec((D_MODEL, D_MODEL)), _const_spec((D_PLE, D_MODEL)),
                  _const_spec((1, D_MODEL))],
        out_specs=[row(D_MODEL), per_b],
        out_shape=[jax.ShapeDtypeStruct((b, t, D_MODEL), F32), jax.ShapeDtypeStruct((b, 8, D_FF), F32)],
        scratch_shapes=[pltpu.VMEM((8 + tm, D_FF), F32)],
        compiler_params=_cparams(("arbitrary", "arbitrary")),
        name="ffn_ple",
    )(x1, h2, p, hist, w_up.astype(BF16), conv_ffn, w_down.astype(BF16), norm_ple.reshape(1, D_MODEL),
      w_ple_gate.astype(BF16), w_ple.astype(BF16), norm_final.reshape(1, D_MODEL))


def _pad_hist(hist, rows=8):
    b, r, c = hist.shape
    return jnp.concatenate([jnp.zeros((b, rows - r, c), hist.dtype), hist], axis=1)


def _layer(x, p, past_k, past_v, past_kidx, s_gdn, conv_b_hist, ffn_hist, wts, *, tm, tq):
    (norm_mix, w_in, conv_b, a_log, dt_bias, norm_gdn, w_proj_a, w_proj_b, w_out, norm_ffn, w_up, conv_ffn,
     w_down, norm_ple, w_ple, w_ple_gate, rel_bias, norm_final) = wts
    b, t, _ = x.shape
    n = b * t
    past = past_k.shape[1]
    topk = min(TOPK_MAX, (past + t) // 4)
    x2d = x.reshape(n, D_MODEL)
    tmi = min(tm, n)
    (ka, va, kbf, conv_in, gb, ga, gbr, small, ki3, qT, qihT, qilT, wiT, vTa) = _in_proj(x2d, norm_mix, w_in, tmi)

    if past == 0:
        assert tq == tmi and t % tq == 0 and tq % CHUNK == 0 and tq >= topk
        nq, tk, lanes = t // tq, tq, tq
        q_ops = (qT, qihT, qilT, wiT)
        k4 = kbf.reshape(b, nq, tk, WA)
        vT4 = vTa.reshape(b, nq, HA * V_ROWS, tk)
        ki34 = ki3.reshape(b, nq, tk, 256)
        kj = np.arange(tk)[:, None]
        adm = ((kj // CHUNK) <= (np.arange(tq)[None, :] // CHUNK)).astype(np.float32)
        nvalid, causal = tq, True
    else:
        tk, lanes, nq = 256, 128, 1
        assert n == tmi and past % tk == 0 and t <= tk and t <= lanes
        nkc = past // tk
        per_b = lambda a, width: jnp.pad(jnp.swapaxes(a[0].reshape(a.shape[1], b, t), 0, 1),
                                         ((0, 0), (0, 0), (0, width - t)))
        q_ops = tuple(per_b(a, lanes) for a in (qT, qihT, qilT, wiT))
        padk = lambda new, old: jnp.concatenate(
            [old, new.reshape(b, t, new.shape[-1]), jnp.zeros((b, tk - t, new.shape[-1]), new.dtype)], axis=1)
        k4 = padk(kbf, past_k.reshape(b, past, WA).astype(BF16)).reshape(b, nkc + 1, tk, WA)
        vc = jnp.transpose(past_v.reshape(b, nkc, tk, HA, DH).astype(BF16), (0, 1, 3, 4, 2))
        vc = jnp.concatenate([vc, jnp.ones((b, nkc, HA, V_ROWS - DH, tk), BF16)], axis=3)
        vT4 = jnp.concatenate([vc.reshape(b, nkc, HA * V_ROWS, tk), per_b(vTa, tk)[:, None]], axis=1)
        pk = past_kidx.astype(F32)
        pkh = pk.astype(BF16)
        pkl = (pk - pkh.astype(F32)).astype(BF16)
        past3 = jnp.concatenate([pkh, pkh, pkl, jnp.zeros_like(pkh)], axis=-1)
        ki34 = padk(ki3, past3).reshape(b, nkc + 1, tk, 256)
        adm = np.broadcast_to(np.arange(tk)[:, None] < t, (tk, lanes)).astype(np.float32)
        nvalid, causal = t, False
    bias = _bias_tables(rel_bias, adm, tk, lanes, nvalid)
    oT = _dsa(*q_ops, k4, vT4, ki34, bias, adm, nq=nq, causal=causal, topk=topk)
    if past:
        oT = jnp.swapaxes(oT[:, :, :t], 0, 1).reshape(1, WA, n)

    tp = -(-t // CHUNK) * CHUNK
    padt = lambda a: jnp.pad(a.reshape(b, t, a.shape[-1]), ((0, 0), (0, tp - t), (0, 0)))
    ob, s_new = _gdn(padt(conv_in), _pad_hist(conv_b_hist), padt(gb), padt(small), s_gdn, conv_b, a_log, dt_bias,
                     norm_gdn, nvalid=min(t, CHUNK))
    ob = ob[:, :t].reshape(n, WB)
    new_conv_b = jnp.concatenate([conv_b_hist, conv_in.reshape(b, t, C_CONV_B)], axis=1)[:, t:]

    x1, h2 = _merge(x2d, oT, ob, ga, gbr, w_proj_a, w_proj_b, w_out, norm_ffn, tmi)
    tmf = min(tm, t)
    y, tail = _ffn_ple(x1.reshape(b, t, D_MODEL), h2.reshape(b, t, D_MODEL), p, _pad_hist(ffn_hist), w_up,
                       conv_ffn, w_down, norm_ple, w_ple_gate, w_ple, norm_final, tmf)
    new_ffn = tail[:, 8 - (CONV_F - 1):]
    return (y, ka.reshape(b, t, HA, DH), va.reshape(b, t, HA, DH), small[:, :D_IDX].reshape(b, t, D_IDX),
            s_new, new_conv_b, new_ffn)


def kernel(x_prompt, x_sample, p_prompt, p_sample, cache_k, cache_v, cache_kidx, state_gdn, state_gdn_conv,
           state_ffn_conv, norm_mix, w_in, conv_b, a_log, dt_bias, norm_gdn, w_proj_a, w_proj_b, w_out, norm_ffn,
           w_up, conv_ffn, w_down, norm_ple, w_ple, w_ple_gate, rel_bias, norm_final):
    assert norm_mix.shape[0] == 1
    bp = x_prompt.shape[0]
    dt = x_prompt.dtype
    wts = (norm_mix[0], w_in[0], conv_b[0], a_log[0], dt_bias[0], norm_gdn[0], w_proj_a[0], w_proj_b[0], w_out[0],
           norm_ffn[0], w_up[0], conv_ffn[0], w_down[0], norm_ple[0], w_ple[0], w_ple_gate[0], rel_bias, norm_final)
    outs_p = _layer(x_prompt, p_prompt[0], jnp.zeros((bp, 0, HA, DH), dt), jnp.zeros((bp, 0, HA, DH), dt),
                    jnp.zeros((bp, 0, D_IDX), dt), jnp.zeros((bp, HB, DK, DV), dt),
                    jnp.zeros((bp, CONV_B - 1, C_CONV_B), dt), jnp.zeros((bp, CONV_F - 1, D_FF), dt),
                    wts, tm=256, tq=256)
    outs_s = _layer(x_sample, p_sample[0], cache_k[0], cache_v[0], cache_kidx[0], state_gdn[0],
                    state_gdn_conv[0], state_ffn_conv[0], wts, tm=256, tq=256)
    yp, ys = outs_p[0], outs_s[0]
    return (yp, ys) + tuple(a[None] for a in outs_p[1:]) + tuple(a[None] for a in outs_s[1:])
```

```python
import functools

import numpy as np
import jax
import jax.numpy as jnp
from jax import lax
from jax.experimental import pallas as pl
from jax.experimental.pallas import tpu as pltpu

F32 = jnp.float32
BF16 = jnp.bfloat16
I32 = jnp.int32

D_MODEL = 1024
CHUNK = 64
HA, DH = 8, 64
H_IDX, D_IDX = 8, 64
TOPK_MAX = 256
NUM_BUCKETS, MAX_DISTANCE = 32, 128
HB, DK, DV = 4, 128, 128
CONV_B = 4
D_FF = 2816
CONV_F = 3
D_PLE = 256
EPS = 1e-6
NEG = -1e30
WA = HA * DH
WB = HB * DK
C_CONV_B = 3 * WB
INT_MIN = -2 ** 31
LOG2E = 1.4426950408889634

_O_QA, _O_KA, _O_VA, _O_QI, _O_KI, _O_WI = 0, 512, 1024, 1536, 2048, 2112
_O_QB, _O_GB, _O_BB, _O_AB, _O_GA, _O_GBR = 2120, 3656, 4168, 4172, 4176, 5200
_L_WI, _L_BB, _L_AB = 64, 72, 76

VMEM_LIMIT = 56 * 1024 * 1024


def _cparams(sem):
    return pltpu.CompilerParams(dimension_semantics=sem, vmem_limit_bytes=VMEM_LIMIT)


def _const_spec(shape):
    nd = len(shape)
    return pl.BlockSpec(shape, lambda *_: (0,) * nd, pipeline_mode=pl.Buffered(1))


def _rms(x, g):
    return x * lax.rsqrt(jnp.mean(x * x, axis=-1, keepdims=True) + EPS) * g


def _split(x):
    hi = x.astype(BF16)
    lo = (x - hi.astype(F32)).astype(BF16)
    return hi, lo


def _dot(a, b):
    return jnp.dot(a, b, preferred_element_type=F32)


def _dot3(a, b):
    ah, al = _split(a)
    bh, bl = _split(b)
    return _dot(ah, bh) + (_dot(al, bh) + _dot(ah, bl))


def _sigmoid(x):
    return 1.0 / (1.0 + jnp.exp(-x))


def _silu(x):
    return x * _sigmoid(x)


def _transpose32(a):
    a = list(a)
    j, m = 16, 0x0000FFFF
    while j:
        k = 0
        while k < 32:
            t = (a[k] ^ lax.shift_right_logical(a[k + j], jnp.int32(j))) & jnp.int32(m - (1 << 32) if m >> 31 else m)
            a[k] = a[k] ^ t
            a[k + j] = a[k + j] ^ (t << j)
            k = (k + j + 1) & ~j
        j >>= 1
        m = (m ^ (m << j)) & 0xFFFFFFFF
    return a


V_ROWS = DH + 16
RADIX_TILES = 8
_NT = (((1,), (1,)), ((), ()))


def _in_proj_kernel(x_ref, g_ref, wm_ref, wt_ref, wqh_ref, wql_ref, wsh_ref, wsl_ref, wwh_ref, wwl_ref,
                    ph_ref, plo_ref, sc_ref,
                    ka_ref, va_ref, kbf_ref, conv_ref, gb_ref, ga_ref, gbr_ref, small_ref, ki3_ref,
                    qT_ref, qihT_ref, qilT_ref, wiT_ref, vTa_ref):
    tm = x_ref.shape[0]
    h = _rms(x_ref[...], g_ref[...])
    hh, hl = _split(h)

    def main(lo, hi):
        return _dot(hh, wm_ref[:, lo:hi])

    def nt(w, a):
        return lax.dot_general(w, a, _NT, preferred_element_type=F32)

    def nt3(wh_ref, wl_ref):
        return nt(wh_ref[...], hh) + (nt(wh_ref[...], hl) + nt(wl_ref[...], hh))

    ka = main(0, 512)
    ka_ref[...] = ka
    kbf_ref[...] = ka.astype(BF16)
    va_ref[...] = main(512, 1024)
    conv_ref[...] = main(1024, 2560)
    gb_ref[...] = main(2560, 3072)
    ga_ref[...] = main(3072, 4096)
    gbr_ref[...] = main(4096, 5120)

    qT_ref[0] = nt(wt_ref[0:WA, :], hh).astype(BF16)
    vT = nt(wt_ref[WA:2 * WA, :], hh).astype(BF16)
    ones = jnp.ones((V_ROWS - DH, tm), BF16)
    for hd in range(HA):
        vTa_ref[0, hd * V_ROWS:hd * V_ROWS + DH, :] = vT[hd * DH:(hd + 1) * DH]
        vTa_ref[0, hd * V_ROWS + DH:(hd + 1) * V_ROWS, :] = ones
    qh, ql = _split(nt3(wqh_ref, wql_ref))
    qihT_ref[0] = qh
    qilT_ref[0] = ql
    wiT_ref[0] = nt3(wwh_ref, wwl_ref)[0:H_IDX] * (H_IDX ** -0.5)

    small = (_dot(hh, wsh_ref[...]) + (_dot(hl, wsh_ref[...]) + _dot(hh, wsl_ref[...]))) * sc_ref[...]
    small_ref[...] = small
    sh, sl = _split(small)
    ki3_ref[...] = (_dot(sh, ph_ref[...]) + _dot(sl, plo_ref[...])).astype(BF16)


def _in_proj(x2d, norm_mix, w_in, tm):
    n = x2d.shape[0]
    assert n % tm == 0
    g = n // tm
    w = w_in
    wm = jnp.concatenate([w[:, _O_KA:_O_QI], w[:, _O_QB:_O_BB], w[:, _O_GA:]], axis=1).astype(BF16)
    wt = jnp.concatenate([w[:, _O_QA:_O_KA] * (DH ** -0.5 * LOG2E), w[:, _O_VA:_O_QI]], axis=1).T.astype(BF16)
    hilo = lambda a: (a.astype(BF16), (a - a.astype(BF16).astype(F32)).astype(BF16))
    wqh, wql = hilo((w[:, _O_QI:_O_KI] * (D_IDX ** -0.5)).T)
    wsh, wsl = hilo(jnp.concatenate([w[:, _O_KI:_O_QB], w[:, _O_BB:_O_GA], jnp.zeros((D_MODEL, 48), F32)], axis=1))
    wwh, wwl = hilo(jnp.concatenate([w[:, _O_WI:_O_QB], jnp.zeros((D_MODEL, 8), F32)], axis=1).T)
    ph = np.zeros((128, 256), np.float32)
    plo = np.zeros((128, 256), np.float32)
    for c in range(64):
        ph[c, c] = 1.0
        ph[c, 64 + c] = 1.0
        plo[c, 128 + c] = 1.0
    sc = np.ones((1, 128), np.float32)
    sc[0, _L_WI:_L_WI + H_IDX] = H_IDX ** -0.5
    row = lambda c: pl.BlockSpec((tm, c), lambda i: (i, 0))
    colT = lambda r: pl.BlockSpec((1, r, tm), lambda i: (i, 0, 0))
    out_cols = [(512, F32), (512, F32), (512, BF16), (1536, F32), (512, F32), (1024, F32), (1024, F32), (128, F32),
                (256, BF16)]
    out_rows = [(WA, BF16), (WA, BF16), (WA, BF16), (H_IDX, F32), (HA * V_ROWS, BF16)]
    return pl.pallas_call(
        _in_proj_kernel,
        grid=(g,),
        in_specs=[row(D_MODEL), _const_spec((1, D_MODEL)), _const_spec((D_MODEL, 5120)),
                  _const_spec((2 * WA, D_MODEL)), _const_spec((WA, D_MODEL)), _const_spec((WA, D_MODEL)),
                  _const_spec((D_MODEL, 128)), _const_spec((D_MODEL, 128)),
                  _const_spec((16, D_MODEL)), _const_spec((16, D_MODEL)),
                  _const_spec((128, 256)), _const_spec((128, 256)), _const_spec((1, 128))],
        out_specs=[row(c) for c, _ in out_cols] + [colT(r) for r, _ in out_rows],
        out_shape=[jax.ShapeDtypeStruct((n, c), d) for c, d in out_cols]
        + [jax.ShapeDtypeStruct((g, r, tm), d) for r, d in out_rows],
        compiler_params=_cparams(("arbitrary",)),
        name="in_proj",
    )(x2d, norm_mix.reshape(1, D_MODEL), wm, wt, wqh, wql, wsh, wsl, wwh, wwl, jnp.asarray(ph, BF16),
      jnp.asarray(plo, BF16), jnp.asarray(sc))


def _store_keys(score, j, key_ref, pln_ref, eq_ref):
    tk, tq = score.shape
    bits = pltpu.bitcast(score, I32)
    bits = jnp.where(bits == INT_MIN, 0, bits)
    key = bits ^ ((bits >> 31) & 0x7FFFFFFF)
    key_ref[j] = key
    u3 = (key ^ INT_MIN).reshape(tk // 8, 8, tq)
    ones = jnp.full((8, tq), -1, I32)
    pln_ref[0, j] = ones
    for l0 in range(0, tq, 128):
        planes = _transpose32([u3[r][:, l0:l0 + 128] for r in range(32)])
        for b in range(32):
            pln_ref[b + 1, j, :, l0:l0 + 128] = planes[b]
    eq_ref[0, j] = ones


def _radix_threshold(pln_ref, eq_ref, nk, tq, topk):
    for d in range(RADIX_TILES - 1):
        pln_ref[:, nk + d] = jnp.zeros((33, 8, tq), I32)
        eq_ref[0, nk + d] = jnp.zeros((8, tq), I32)

    sub = lax.broadcasted_iota(I32, (8, tq), 0)
    group_masks = (-0x10000, -0xFF0100, -0xF0F0F10, -0x33333334, -0x55555556)

    def pos_plane(e, j):
        if e < 6:
            return jnp.broadcast_to(-((~j >> (5 - e)) & 1), (8, tq))
        if e < 11:
            return jnp.full((8, tq), group_masks[e - 6], I32)
        return -((~sub >> (13 - e)) & 1)

    def sweep(prev, cur, carry, src, dst):
        n_gt, flip = carry

        def body(jq, cnt):
            for d in range(RADIX_TILES):
                j = jq * RADIX_TILES + d
                e = eq_ref[src, j] & (prev(j) ^ flip)
                eq_ref[dst, j] = e
                cnt = cnt + lax.population_count(e & cur(j))
            return cnt

        cnt = lax.fori_loop(0, (nk + RADIX_TILES - 1) // RADIX_TILES, body, jnp.zeros((8, tq), I32))
        cnt = n_gt + cnt.sum(axis=0, keepdims=True)
        acc = cnt >= topk
        return acc, (jnp.where(acc, n_gt, cnt), jnp.where(acc, 0, -1))

    def key_sweep(it, carry, src, dst):
        tu, rest = carry
        acc, rest = sweep(lambda j: pln_ref[it, j], lambda j: pln_ref[it + 1, j], rest, src, dst)
        return jnp.where(acc, tu | jnp.left_shift(jnp.int32(1), 31 - it), tu), rest

    def two_bits(i, carry):
        return key_sweep(2 * i + 1, key_sweep(2 * i, carry, 0, 1), 1, 0)

    zero_row = jnp.zeros((1, tq), I32)
    tu, rest = lax.fori_loop(0, 16, two_bits, (zero_row, (zero_row, zero_row)))
    inv_pos = zero_row
    for e in range(14):
        prev = (lambda j: pln_ref[32, j]) if e == 0 else functools.partial(pos_plane, e - 1)
        acc, rest = sweep(prev, functools.partial(pos_plane, e), rest, e % 2, 1 - e % 2)
        inv_pos = jnp.where(acc, inv_pos | (1 << (13 - e)), inv_pos)
    return tu ^ INT_MIN, ~inv_pos & 0x3FFF


def _dsa_kernel(qT_ref, qihT_ref, qilT_ref, wiT_ref, k_ref, vT_ref, ki3_ref, bias_ref, cap_ref,
                o_ref, key_ref, l_ref, acc_ref, lg_ref, pln_ref, eq_ref, sc_ref, ma_ref,
                *, tq, tk, nk_static, causal, topk):
    nk = (pl.program_id(1) + 1) if causal else nk_static
    wiT = wiT_ref[0]

    def when(cond):
        if isinstance(cond, bool):
            return (lambda f: f()) if cond else (lambda f: None)
        return pl.when(cond)

    zeros64 = jnp.zeros((64, tq), BF16)
    q3 = []
    for h in range(H_IDX):
        hi = qihT_ref[0, h * 64:(h + 1) * 64, :]
        lo = qilT_ref[0, h * 64:(h + 1) * 64, :]
        q3.append(jnp.concatenate([hi, lo, hi, zeros64], axis=0))

    def score_matmuls(j):
        kt = ki3_ref[0, j]
        acc = None
        for h in range(H_IDX):
            t = jnp.maximum(_dot(kt, q3[h]), 0.0) * wiT[h:h + 1, :]
            acc = t if acc is None else acc + t
        sc_ref[...] = jnp.minimum(acc, cap_ref[jnp.where(j == nk - 1, 1, 0)])

    def score_finish(j):
        _store_keys(sc_ref[...], j, key_ref, pln_ref, eq_ref)

    def score_body(j, c):
        score_finish(j - 1)
        score_matmuls(j)
        return c

    score_matmuls(0)
    lax.fori_loop(1, nk, score_body, 0)
    score_finish(nk - 1)

    thr, pos_thr = _radix_threshold(pln_ref, eq_ref, nk, tq, topk)

    l_ref[...] = jnp.zeros(l_ref.shape, F32)
    acc_ref[...] = jnp.zeros(acc_ref.shape, F32)
    qm = []
    for h in range(HA):
        qh = qT_ref[0, h * 64:(h + 1) * 64, :]
        qm.append(jnp.concatenate([qh, zeros64] if h % 2 == 0 else [zeros64, qh], axis=0))
    row = lax.broadcasted_iota(I32, (tk, tq), 0)

    def tile_logits(j, slot):
        sel = key_ref[j] > jnp.where(row <= pos_thr - j * tk, thr - 1, thr)
        kind = jnp.clip(j - (nk - 3), 0, 2)
        mx = []
        for h in range(HA):
            pr = h // 2
            lg = _dot(k_ref[0, j, :, pr * 128:(pr + 1) * 128], qm[h]) + bias_ref[kind, h]
            lg = jnp.where(sel, lg, NEG).astype(BF16)
            lg_ref[slot, h] = lg
            part = jnp.max(lg.reshape(tk // 16, 16, tq), axis=0).astype(F32)
            mx.append(jnp.max(part, axis=0, keepdims=True))
        return jnp.concatenate(mx, axis=0)

    def tile_values(j, slot, mx):
        m_old = ma_ref[...]
        m_new = jnp.maximum(m_old, mx)
        alpha = jnp.exp2(m_old - m_new)
        ma_ref[...] = m_new
        ls = []
        for h in range(HA):
            p = jnp.exp2(lg_ref[slot, h] - m_new[h:h + 1, :].astype(BF16))
            pv = _dot(vT_ref[0, j, h * V_ROWS:(h + 1) * V_ROWS, :], p)
            acc_ref[h * 64:(h + 1) * 64, :] = alpha[h:h + 1, :] * acc_ref[h * 64:(h + 1) * 64, :] + pv[0:DH]
            ls.append(pv[DH:DH + 1])
        l_ref[...] = alpha * l_ref[...] + jnp.concatenate(ls, axis=0)

    ma_ref[...] = jnp.full((HA, tq), NEG, F32)

    def pair(i, c):
        mx0 = tile_logits(2 * i, 0)
        mx1 = tile_logits(2 * i + 1, 1)
        tile_values(2 * i, 0, mx0)
        tile_values(2 * i + 1, 1, mx1)
        return c

    lax.fori_loop(0, nk // 2, pair, 0)

    @when(nk % 2 == 1)
    def _():
        tile_values(nk - 1, 0, tile_logits(nk - 1, 0))

    for h in range(HA):
        o_ref[0, h * 64:(h + 1) * 64, :] = (acc_ref[h * 64:(h + 1) * 64, :] / l_ref[h:h + 1, :]).astype(BF16)


def _t5_bucket_np(rel):
    half = NUM_BUCKETS // 2
    max_exact = half // 2
    out = np.zeros(rel.shape, np.int64)
    flat_rel = rel.reshape(-1)
    flat = out.reshape(-1)
    for a in range(flat_rel.size):
        r = int(flat_rel[a])
        n = abs(r)
        b = n if n < max_exact else min(half - 1, (n * n).bit_length() + 1)
        flat[a] = b + (half if r > 0 else 0)
    return out


def _bias_tables(rel_bias, adm, tk, tq, tq_valid):
    kj = np.arange(tk)[:, None]
    t = np.minimum(np.arange(tq), tq_valid - 1)[None, :]
    rel_to_bucket = _t5_bucket_np(np.arange(-2 * tk - tq, tk + 1))
    lut = lambda rel: rel_to_bucket[rel + 2 * tk + tq]
    far_bucket = NUM_BUCKETS // 2 - 1
    idx = np.stack([np.full((tk, tq), far_bucket), lut(kj - tk - t), lut(kj - t)], axis=0)
    tab = rel_bias.astype(F32) * LOG2E
    onehot = jnp.asarray(idx[..., None] == np.arange(NUM_BUCKETS), F32)
    bias = jnp.einsum("ktqb,bh->khtq", onehot, tab - tab[far_bucket], precision=lax.Precision.HIGHEST)
    mask = np.zeros((3, 1, tk, tq), np.float32)
    mask[2, 0] = np.where(adm > 0.5, 0.0, NEG)
    return bias + jnp.asarray(mask)


def _dsa(qT, qihT, qilT, wiT, k4, vT4, ki34, bias, adm, *, nq, causal, topk):
    g, _, tq = qT.shape
    b, nk, tk, _ = k4.shape
    assert g == b * nq and tk == 256 and nk + RADIX_TILES - 1 <= 64
    cap = jnp.asarray(np.stack([np.full((tk, tq), np.inf, np.float32), np.where(adm > 0.5, np.inf, NEG)]), F32)
    qspec = lambda r: pl.BlockSpec((1, r, tq), lambda bi, i: (bi * nq + i, 0, 0))
    kspec = lambda s: pl.BlockSpec((1,) + s, lambda bi, i: (bi, 0, 0, 0), pipeline_mode=pl.Buffered(1))
    kern = functools.partial(_dsa_kernel, tq=tq, tk=tk, nk_static=nk, causal=causal, topk=topk)
    return pl.pallas_call(
        kern,
        grid=(b, nq),
        in_specs=[qspec(512), qspec(512), qspec(512), qspec(8),
                  kspec((nk, tk, 512)), kspec((nk, HA * V_ROWS, tk)), kspec((nk, tk, 256)),
                  _const_spec((3, HA, tk, tq)), _const_spec((2, tk, tq))],
        out_specs=qspec(512),
        out_shape=jax.ShapeDtypeStruct((g, 512, tq), BF16),
        scratch_shapes=[pltpu.VMEM((nk, tk, tq), I32), pltpu.VMEM((8, tq), F32),
                        pltpu.VMEM((512, tq), F32), pltpu.VMEM((2, HA, tk, tq), BF16),
                        pltpu.VMEM((33, nk + RADIX_TILES - 1, 8, tq), I32),
                        pltpu.VMEM((2, nk + RADIX_TILES - 1, 8, tq), I32),
                        pltpu.VMEM((tk, tq), F32), pltpu.VMEM((HA, tq), F32)],
        compiler_params=_cparams(("arbitrary", "arbitrary")),
        name="dsa_causal" if causal else "dsa_cached",
    )(qT, qihT, qilT, wiT, k4, vT4, ki34, bias, cap)


def _dsa_step_kernel(qa_ref, qih_ref, qil_ref, w_ref, ck_ref, cv_ref, cki_ref, nk_ref, nv_ref, nki_ref,
                     bias_ref, cap_ref, blk_ref, o_ref, key_ref, pln_ref, eq_ref, m_ref, l_ref, acc_ref,
                     *, nkc, t, topk):
    tk, tq = nk_ref.shape[1], qa_ref.shape[2]
    qih, qil = qih_ref[0], qil_ref[0]
    w = w_ref[0, 0:1, :]
    shifts = [t << s for s in range((tq // t).bit_length() - 2, -1, -1)]

    def scores(kx, j, cap):
        kh, kl = _split(kx)
        s = _dot(kh, qih) + (_dot(kl, qih) + _dot(kh, qil))
        sc = jnp.maximum(s, 0.0) * w
        for sh in shifts:
            sc = sc + pltpu.roll(sc, sh, 1)
        if cap is not None:
            sc = jnp.minimum(sc, cap)
        _store_keys(sc, j, key_ref, pln_ref, eq_ref)

    def score_body(j, c):
        scores(cki_ref[0, pl.ds(pl.multiple_of(j * tk, tk), tk), :], j, None)
        return c

    lax.fori_loop(0, nkc, score_body, 0)
    scores(nki_ref[0], nkc, cap_ref[...])
    thr, pos_thr = _radix_threshold(pln_ref, eq_ref, nkc + 1, tq, topk)

    m_ref[...] = jnp.full(m_ref.shape, NEG, F32)
    l_ref[...] = jnp.zeros(l_ref.shape, F32)
    acc_ref[...] = jnp.zeros(acc_ref.shape, F32)
    qa = qa_ref[0]
    row = lax.broadcasted_iota(I32, (tk, tq), 0)
    ones8 = jnp.ones((8, tk), BF16)
    tn = (((0,), (0,)), ((), ()))

    def attend(k_bf, v_bf, j, bias):
        sel = key_ref[j] > jnp.where(row <= pos_thr - j * tk, thr - 1, thr)
        lg = _dot(k_bf, qa)
        if bias is not None:
            lg = lg + bias
        lg = jnp.where(sel, lg, NEG).astype(BF16)
        part = jnp.max(lg.reshape(tk // 16, 16, tq), axis=0).astype(F32)
        m_old = m_ref[0:1, :]
        m_new = jnp.maximum(m_old, jnp.max(part, axis=0, keepdims=True))
        alpha = jnp.exp2(m_old - m_new)
        p = jnp.exp2(lg - m_new.astype(BF16))
        l_ref[0:1, :] = alpha * l_ref[0:1, :] + _dot(ones8, p)[0:1]
        acc_ref[...] = alpha * acc_ref[...] + lax.dot_general(v_bf, p, tn, preferred_element_type=F32)
        m_ref[0:1, :] = m_new

    def cached(j, bias):
        rows = pl.ds(pl.multiple_of(j * tk, tk), tk)
        attend(ck_ref[0, rows, :].astype(BF16), cv_ref[0, rows, :].astype(BF16), j, bias)

    def far(j, c):
        cached(j, None)
        return c

    lax.fori_loop(0, nkc - 1, far, 0)
    cached(nkc - 1, bias_ref[0])
    attend(nk_ref[0], nv_ref[0], nkc, bias_ref[1])
    o = acc_ref[...] / l_ref[0:1, :] * blk_ref[...]
    for sh in shifts:
        o = o + pltpu.roll(o, sh, 1)
    o_ref[0] = o.astype(BF16)


def _dsa_step(qT, qihT, qilT, wiT, kbf, va, ki_new, past_k, past_v, past_kidx, rel_bias, *, b, t, topk):
    past = past_k.shape[1]
    tk, tq = 256, HA * t
    assert tq == 128 and past % tk == 0 and t <= tk
    nkc = past // tk
    nk = nkc + 1
    per_b = lambda a: jnp.swapaxes(a[0].reshape(a.shape[1], b, t), 0, 1)
    blk = (np.arange(WA)[:, None] // DH == np.arange(tq)[None, :] // t).astype(np.float32)
    qa = (jnp.tile(per_b(qT), (1, 1, HA)) * jnp.asarray(blk, BF16))
    lanes = lambda a: jnp.swapaxes(per_b(a).reshape(b, HA, D_IDX, t), 1, 2).reshape(b, D_IDX, tq)
    w = jnp.broadcast_to(per_b(wiT).reshape(b, 1, tq), (b, 8, tq))
    pad_rows = lambda a: jnp.pad(a.reshape(b, t, a.shape[-1]), ((0, 0), (0, tk - t), (0, 0)))
    adm = np.broadcast_to(np.arange(tk)[:, None] < t, (tk, tq)).astype(np.float32)
    bias = _bias_tables(rel_bias, adm[:, :t], tk, t, t)[1:]
    bias = jnp.swapaxes(bias, 1, 2).reshape(2, tk, tq)
    cap = jnp.asarray(np.where(adm > 0.5, np.inf, NEG), F32)
    bspec = lambda s: pl.BlockSpec((1,) + s, lambda bi: (bi,) + (0,) * len(s))
    return pl.pallas_call(
        functools.partial(_dsa_step_kernel, nkc=nkc, t=t, topk=topk),
        grid=(b,),
        in_specs=[bspec((WA, tq)), bspec((D_IDX, tq)), bspec((D_IDX, tq)), bspec((8, tq)),
                  bspec((past, WA)), bspec((past, WA)), bspec((past, D_IDX)),
                  bspec((tk, WA)), bspec((tk, WA)), bspec((tk, D_IDX)),
                  _const_spec((2, tk, tq)), _const_spec((tk, tq)), _const_spec((WA, tq))],
        out_specs=bspec((WA, tq)),
        out_shape=jax.ShapeDtypeStruct((b, WA, tq), BF16),
        scratch_shapes=[pltpu.VMEM((nk, tk, tq), I32), pltpu.VMEM((33, nk + RADIX_TILES - 1, 8, tq), I32),
                        pltpu.VMEM((2, nk + RADIX_TILES - 1, 8, tq), I32), pltpu.VMEM((8, tq), F32),
                        pltpu.VMEM((8, tq), F32), pltpu.VMEM((WA, tq), F32)],
        compiler_params=_cparams(("arbitrary",)),
        name="dsa_step",
    )(qa, lanes(qihT), lanes(qilT), w, past_k.reshape(b, past, WA), past_v.reshape(b, past, WA), past_kidx,
      pad_rows(kbf), pad_rows(va.astype(BF16)), pad_rows(ki_new), bias, cap, jnp.asarray(blk))


def _gdn_pre_kernel(conv_ref, prev_ref, hist_ref, small_ref, cw_ref, alog_ref, dtb_ref,
                    wm_ref, um_ref, qe_ref, oi_ref, e_ref, ext_ref, *, rows, nvalid):
    c = CHUNK
    n = HB * c
    ext_ref[0:8, :] = jnp.where(pl.program_id(1) == 0, hist_ref[0], prev_ref[0])
    ext_ref[8:8 + rows, :] = conv_ref[0]
    cb = ext_ref[5:5 + rows, :] * cw_ref[0:1, :]
    for j in range(1, CONV_B):
        cb = cb + ext_ref[5 + j:5 + j + rows, :] * cw_ref[j:j + 1, :]
    cb = _silu(cb)

    small = small_ref[0]
    pos = lax.broadcasted_iota(I32, (rows, 1), 0) % c
    rowv = (pos < nvalid).astype(F32)
    beta_all = _sigmoid(small) * rowv
    sp = small + dtb_ref[...]
    g_all = -jnp.exp(alog_ref[...]) * (jnp.maximum(sp, 0.0) + jnp.log(1.0 + jnp.exp(-jnp.abs(sp)))) * rowv

    ri = lax.broadcasted_iota(I32, (n, n), 0)
    ci = lax.broadcasted_iota(I32, (n, n), 1)
    same = (ri // c) == (ci // c)
    tri_b = (same & (ci <= ri)).astype(BF16)
    bd_f = same.astype(F32)
    bd_b = same.astype(BF16)
    wr = lax.broadcasted_iota(I32, (c, n), 0)
    wl = lax.broadcasted_iota(I32, (c, n), 1)
    grp = wl // c
    tri_w = (wl % c) <= wr
    strict_w = (wl % c) < wr
    eye_w = ((wl % c) == wr).astype(F32)
    nt = (((1,), (1,)), ((), ()))
    tn = (((0,), (0,)), ((), ()))

    def l2n(x):
        return x * lax.rsqrt(jnp.sum(x * x, axis=-1, keepdims=True) + EPS)

    def to_wide(full):
        out = jnp.where(grp == 0, full[0:c, :], 0.0)
        for h in range(1, HB):
            out = out + jnp.where(grp == h, full[h * c:(h + 1) * c, :], 0.0)
        return out

    def tile4(x):
        return jnp.concatenate([x] * HB, axis=0)

    def chunk(k0):
        def stack(fn):
            return jnp.concatenate([fn(h) for h in range(HB)], axis=0)

        rv = rowv[k0:k0 + c]
        q = stack(lambda h: l2n(cb[k0:k0 + c, h * DK:(h + 1) * DK]) * (DK ** -0.5))
        k = stack(lambda h: l2n(cb[k0:k0 + c, WB + h * DK:WB + (h + 1) * DK]) * rv)
        v = stack(lambda h: cb[k0:k0 + c, 2 * WB + h * DV:2 * WB + (h + 1) * DV] * rv)
        beta = stack(lambda h: beta_all[k0:k0 + c, _L_BB + h:_L_BB + h + 1])
        g = stack(lambda h: g_all[k0:k0 + c, _L_AB + h:_L_AB + h + 1])
        yield

        gh, gl = _split(jnp.broadcast_to(g, (n, 128)))
        gcum = _dot(tri_b, gh) + _dot(tri_b, gl)
        yield
        gcum_row = gcum.T[0:1, :]
        gcum_col = gcum[:, 0:1]
        col_w = to_wide(jnp.concatenate([gcum, gcum], axis=1))
        decay_w = jnp.where(tri_w, jnp.exp(jnp.where(tri_w, col_w - gcum_row, 0.0)), 0.0)
        kb = k * beta
        kbf = k.astype(BF16)
        kk_w = to_wide(lax.dot_general(kb.astype(BF16), kbf, nt, preferred_element_type=F32))
        qk_w = to_wide(lax.dot_general(q.astype(BF16), kbf, nt, preferred_element_type=F32))
        nmat_w = jnp.where(strict_w, kk_w * decay_w, 0.0)
        attn_w = jnp.where(tri_w, qk_w * decay_w, 0.0)
        yield

        inv_w = eye_w - nmat_w
        ph = nmat_w.astype(BF16)
        bh = tile4(ph) * bd_b
        for _ in range(int(np.log2(c)) - 1):
            ph = _dot(ph, bh).astype(BF16)
            yield
            bh = tile4(ph) * bd_b
            inv_w = inv_w + _dot(inv_w.astype(BF16), bh)
            yield
        rhs = jnp.concatenate([v * beta, kb * jnp.exp(gcum_col)], axis=1)
        sol = _dot3(tile4(inv_w) * bd_f, rhs)
        yield
        solb = sol.astype(BF16)
        aw = _dot((tile4(attn_w) * bd_f).astype(BF16), solb)
        yield
        ck = k0 // c
        oi_ref[0, ck] = aw[:, :DV]
        qe_ref[0, ck] = (q * jnp.exp(gcum_col) - aw[:, DV:]).astype(BF16)
        for h in range(HB):
            rs = slice(h * c, (h + 1) * c)
            g_last = gcum[(h + 1) * c - 1:(h + 1) * c, :]
            kd = (k[rs] * jnp.exp(g_last - gcum[rs])).astype(BF16)
            uw = lax.dot_general(kd, solb[rs], tn, preferred_element_type=F32)
            um_ref[0, ck, h] = uw[:, :DV]
            wm_ref[0, ck, h] = uw[:, DV:].astype(BF16)
            e_ref[0, ck, h:h + 1, :] = jnp.exp(g_last)
        e_ref[0, ck, HB:8, :] = jnp.zeros((8 - HB, 128), F32)

    gens = [chunk(k0) for k0 in range(0, rows, c)]
    while gens:
        alive = []
        for gen in gens:
            try:
                next(gen)
                alive.append(gen)
            except StopIteration:
                pass
        gens = alive


def _gdn_scan_kernel(wm_ref, um_ref, qe_ref, oi_ref, e_ref, gb_ref, s0_ref, ng_ref, ob_ref, sfin_ref, s_ref,
                     *, bb, g):
    c = CHUNK

    @pl.when(pl.program_id(1) == 0)
    def _():
        s_ref[...] = s0_ref[...]

    ng = ng_ref[...]
    for b in range(bb):
        for ck in range(g):
            for h in range(HB):
                s = s_ref[b, h]
                sb = s.astype(BF16)
                rs = slice(h * c, (h + 1) * c)
                o = _dot(qe_ref[b, ck, rs, :], sb) + oi_ref[b, ck, rs, :]
                s_ref[b, h] = e_ref[b, ck, h:h + 1, :] * s + (um_ref[b, ck, h] - _dot(wm_ref[b, ck, h], sb))
                gate = gb_ref[b, ck * c:(ck + 1) * c, h * DV:(h + 1) * DV]
                ob_ref[b, ck * c:(ck + 1) * c, h * DV:(h + 1) * DV] = (_rms(o, ng) * _silu(gate)).astype(BF16)
    sfin_ref[...] = s_ref[...]


def _gdn(conv_in, hist, gb, small, s0, conv_w, a_log, dt_bias, norm_gdn, nvalid):
    b, t, _ = conv_in.shape
    assert t % CHUNK == 0
    rows = 256 if t % 256 == 0 else CHUNK
    nc, cps = t // CHUNK, rows // CHUNK
    n = HB * CHUNK
    alog = jnp.zeros((1, 128), F32).at[0, _L_AB:_L_AB + HB].set(a_log)
    dtb = jnp.zeros((1, 128), F32).at[0, _L_AB:_L_AB + HB].set(dt_bias)
    row = lambda c_: pl.BlockSpec((1, rows, c_), lambda bi, ti: (bi, ti, 0))
    prev = pl.BlockSpec((1, 8, C_CONV_B), lambda bi, ti: (bi, jnp.maximum(ti * (rows // 8) - 1, 0), 0))
    per_b = lambda s: pl.BlockSpec((1,) + s, lambda bi, ti: (bi,) + (0,) * len(s))
    chunked = lambda s: pl.BlockSpec((1, cps) + s, lambda bi, ti: (bi, ti) + (0,) * len(s))
    wm, um, qe, oi, e = pl.pallas_call(
        functools.partial(_gdn_pre_kernel, rows=rows, nvalid=nvalid),
        grid=(b, t // rows),
        in_specs=[row(C_CONV_B), prev, per_b((8, C_CONV_B)), row(128),
                  _const_spec((CONV_B, C_CONV_B)), _const_spec((1, 128)), _const_spec((1, 128))],
        out_specs=[chunked((HB, DK, DV)), chunked((HB, DK, DV)), chunked((n, DK)), chunked((n, DV)),
                   chunked((8, 128))],
        out_shape=[jax.ShapeDtypeStruct((b, nc, HB, DK, DV), BF16), jax.ShapeDtypeStruct((b, nc, HB, DK, DV), F32),
                   jax.ShapeDtypeStruct((b, nc, n, DK), BF16), jax.ShapeDtypeStruct((b, nc, n, DV), F32),
                   jax.ShapeDtypeStruct((b, nc, 8, 128), F32)],
        scratch_shapes=[pltpu.VMEM((8 + rows, C_CONV_B), F32)],
        compiler_params=_cparams(("arbitrary", "arbitrary")),
        name="gdn_pre",
    )(conv_in, conv_in, hist, small, conv_w, alog, dtb)

    bb = 2 if b % 2 == 0 else 1
    g = 2 if nc % 2 == 0 else 1
    blk = lambda s: pl.BlockSpec((bb, g) + s, lambda bi, ci: (bi, ci) + (0,) * len(s))
    rowb = pl.BlockSpec((bb, g * CHUNK, WB), lambda bi, ci: (bi, ci, 0))
    state = pl.BlockSpec((bb, HB, DK, DV), lambda bi, ci: (bi, 0, 0, 0))
    return pl.pallas_call(
        functools.partial(_gdn_scan_kernel, bb=bb, g=g),
        grid=(b // bb, nc // g),
        in_specs=[blk((HB, DK, DV)), blk((HB, DK, DV)), blk((n, DK)), blk((n, DV)), blk((8, 128)), rowb, state,
                  _const_spec((1, DV))],
        out_specs=[rowb, state],
        out_shape=[jax.ShapeDtypeStruct((b, t, WB), BF16), jax.ShapeDtypeStruct((b, HB, DK, DV), F32)],
        scratch_shapes=[pltpu.VMEM((bb, HB, DK, DV), F32)],
        compiler_params=_cparams(("arbitrary", "arbitrary")),
        name="gdn_scan",
    )(wm, um, qe, oi, e, gb, s0, norm_gdn.reshape(1, DV))


def _merge_kernel(x_ref, oaT_ref, ob_ref, ga_ref, gbr_ref, wa_ref, wb_ref, wo_ref, nf_ref, x1_ref, h2_ref):
    ya = lax.dot_general(oaT_ref[0], wa_ref[...], (((0,), (0,)), ((), ())), preferred_element_type=F32)
    yb = _dot(ob_ref[...], wb_ref[...])
    mix = _sigmoid(ga_ref[...]) * ya + _sigmoid(gbr_ref[...]) * yb
    x1 = x_ref[...] + _dot(mix.astype(BF16), wo_ref[...])
    x1_ref[...] = x1
    h2_ref[...] = _rms(x1, nf_ref[...]).astype(BF16)


def _merge(x2d, oaT, ob, ga, gbr, w_proj_a, w_proj_b, w_out, norm_ffn, tm):
    n = x2d.shape[0]
    assert oaT.shape == (n // tm, WA, tm)
    row = lambda c: pl.BlockSpec((tm, c), lambda i: (i, 0))
    return pl.pallas_call(
        _merge_kernel,
        grid=(n // tm,),
        in_specs=[row(D_MODEL), pl.BlockSpec((1, WA, tm), lambda i: (i, 0, 0)), row(WB), row(D_MODEL), row(D_MODEL),
                  _const_spec((WA, D_MODEL)), _const_spec((WB, D_MODEL)), _const_spec((D_MODEL, D_MODEL)),
                  _const_spec((1, D_MODEL))],
        out_specs=[row(D_MODEL), row(D_MODEL)],
        out_shape=[jax.ShapeDtypeStruct((n, D_MODEL), F32), jax.ShapeDtypeStruct((n, D_MODEL), BF16)],
        compiler_params=_cparams(("arbitrary",)),
        name="merge",
    )(x2d, oaT, ob, ga, gbr, w_proj_a.astype(BF16), w_proj_b.astype(BF16), w_out.astype(BF16),
      norm_ffn.reshape(1, D_MODEL))


def _ffn_kernel(x1_ref, h2_ref, p_ref, hist_ref, wup_ref, cw_ref, wdn_ref, npl_ref, wpg_ref, wple_ref, nfin_ref,
                y_ref, tail_ref, ext_ref, *, tm):
    @pl.when(pl.program_id(1) == 0)
    def _():
        ext_ref[0:8, :] = hist_ref[0]

    h2 = h2_ref[0]
    ext_ref[8:8 + tm, :] = _dot(h2, wup_ref[:, 0:D_FF])
    u_val = _dot(h2, wup_ref[:, D_FF:2 * D_FF])
    cv = ext_ref[6:6 + tm, :] * cw_ref[0:1, :]
    for j in range(1, CONV_F):
        cv = cv + ext_ref[6 + j:6 + j + tm, :] * cw_ref[j:j + 1, :]
    tail = ext_ref[tm:tm + 8, :]
    ext_ref[0:8, :] = tail
    tail_ref[0] = tail
    act = 0.5 * cv * (1.0 + jnp.tanh(0.7978845608028654 * (cv + 0.044715 * (cv * cv * cv))))
    x2 = x1_ref[0] + _dot((act * u_val).astype(BF16), wdn_ref[...])
    gate = _sigmoid(_dot(_rms(x2, npl_ref[...]).astype(BF16), wpg_ref[...]))
    x3 = x2 + gate * _dot(p_ref[0].astype(BF16), wple_ref[...])
    y_ref[0] = _rms(x3, nfin_ref[...])


def _ffn_ple(x1, h2, p, hist, w_up, conv_ffn, w_down, norm_ple, w_ple_gate, w_ple, norm_final, tm):
    b, t, _ = x1.shape
    assert t % tm == 0 and tm >= 8
    row = lambda c: pl.BlockSpec((1, tm, c), lambda bi, ti: (bi, ti, 0))
    per_b = pl.BlockSpec((1, 8, D_FF), lambda bi, ti: (bi, 0, 0))
    return pl.pallas_call(
        functools.partial(_ffn_kernel, tm=tm),
        grid=(b, t // tm),
        in_specs=[row(D_MODEL), row(D_MODEL), row(D_PLE), per_b,
                  _const_spec((D_MODEL, 2 * D_FF)), _const_spec((CONV_F, D_FF)), _const_spec((D_FF, D_MODEL)),
                  _const_spec((1, D_MODEL)), _const_spec((D_MODEL, D_MODEL)), _const_spec((D_PLE, D_MODEL)),
                  _const_spec((1, D_MODEL))],
        out_specs=[row(D_MODEL), per_b],
        out_shape=[jax.ShapeDtypeStruct((b, t, D_MODEL), F32), jax.ShapeDtypeStruct((b, 8, D_FF), F32)],
        scratch_shapes=[pltpu.VMEM((8 + tm, D_FF), F32)],
        compiler_params=_cparams(("arbitrary", "arbitrary")),
        name="ffn_ple",
    )(x1, h2, p, hist, w_up.astype(BF16), conv_ffn, w_down.astype(BF16), norm_ple.reshape(1, D_MODEL),
      w_ple_gate.astype(BF16), w_ple.astype(BF16), norm_final.reshape(1, D_MODEL))


def _pad_hist(hist, rows=8):
    b, r, c = hist.shape
    return jnp.concatenate([jnp.zeros((b, rows - r, c), hist.dtype), hist], axis=1)


def _layer(x, p, past_k, past_v, past_kidx, s_gdn, conv_b_hist, ffn_hist, wts, *, tm, tq):
    (norm_mix, w_in, conv_b, a_log, dt_bias, norm_gdn, w_proj_a, w_proj_b, w_out, norm_ffn, w_up, conv_ffn,
     w_down, norm_ple, w_ple, w_ple_gate, rel_bias, norm_final) = wts
    b, t, _ = x.shape
    n = b * t
    past = past_k.shape[1]
    topk = min(TOPK_MAX, (past + t) // 4)
    x2d = x.reshape(n, D_MODEL)
    tmi = min(tm, n)
    (ka, va, kbf, conv_in, gb, ga, gbr, small, ki3, qT, qihT, qilT, wiT, vTa) = _in_proj(x2d, norm_mix, w_in, tmi)

    if past == 0:
        assert tq == tmi and t % tq == 0 and tq % CHUNK == 0 and tq >= topk
        nq = t // tq
        kj = np.arange(tq)[:, None]
        adm = ((kj // CHUNK) <= (np.arange(tq)[None, :] // CHUNK)).astype(np.float32)
        oT = _dsa(qT, qihT, qilT, wiT, kbf.reshape(b, nq, tq, WA), vTa.reshape(b, nq, HA * V_ROWS, tq),
                  ki3.reshape(b, nq, tq, 256), _bias_tables(rel_bias, adm, tq, tq, tq), adm,
                  nq=nq, causal=True, topk=topk)
    else:
        assert n == tmi
        oT = _dsa_step(qT, qihT, qilT, wiT, kbf, va, small[:, :D_IDX], past_k, past_v, past_kidx, rel_bias,
                       b=b, t=t, topk=topk)
        oT = jnp.swapaxes(oT[:, :, :t], 0, 1).reshape(1, WA, n)

    tp = -(-t // CHUNK) * CHUNK
    padt = lambda a: jnp.pad(a.reshape(b, t, a.shape[-1]), ((0, 0), (0, tp - t), (0, 0)))
    ob, s_new = _gdn(padt(conv_in), _pad_hist(conv_b_hist), padt(gb), padt(small), s_gdn, conv_b, a_log, dt_bias,
                     norm_gdn, nvalid=min(t, CHUNK))
    ob = ob[:, :t].reshape(n, WB)
    new_conv_b = jnp.concatenate([conv_b_hist, conv_in.reshape(b, t, C_CONV_B)], axis=1)[:, t:]

    x1, h2 = _merge(x2d, oT, ob, ga, gbr, w_proj_a, w_proj_b, w_out, norm_ffn, tmi)
    tmf = min(tm, t)
    y, tail = _ffn_ple(x1.reshape(b, t, D_MODEL), h2.reshape(b, t, D_MODEL), p, _pad_hist(ffn_hist), w_up,
                       conv_ffn, w_down, norm_ple, w_ple_gate, w_ple, norm_final, tmf)
    new_ffn = tail[:, 8 - (CONV_F - 1):]
    return (y, ka.reshape(b, t, HA, DH), va.reshape(b, t, HA, DH), small[:, :D_IDX].reshape(b, t, D_IDX),
            s_new, new_conv_b, new_ffn)


def kernel(x_prompt, x_sample, p_prompt, p_sample, cache_k, cache_v, cache_kidx, state_gdn, state_gdn_conv,
           state_ffn_conv, norm_mix, w_in, conv_b, a_log, dt_bias, norm_gdn, w_proj_a, w_proj_b, w_out, norm_ffn,
           w_up, conv_ffn, w_down, norm_ple, w_ple, w_ple_gate, rel_bias, norm_final):
    assert norm_mix.shape[0] == 1
    bp = x_prompt.shape[0]
    dt = x_prompt.dtype
    wts = (norm_mix[0], w_in[0], conv_b[0], a_log[0], dt_bias[0], norm_gdn[0], w_proj_a[0], w_proj_b[0], w_out[0],
           norm_ffn[0], w_up[0], conv_ffn[0], w_down[0], norm_ple[0], w_ple[0], w_ple_gate[0], rel_bias, norm_final)
    outs_p = _layer(x_prompt, p_prompt[0], jnp.zeros((bp, 0, HA, DH), dt), jnp.zeros((bp, 0, HA, DH), dt),
                    jnp.zeros((bp, 0, D_IDX), dt), jnp.zeros((bp, HB, DK, DV), dt),
                    jnp.zeros((bp, CONV_B - 1, C_CONV_B), dt), jnp.zeros((bp, CONV_F - 1, D_FF), dt),
                    wts, tm=256, tq=256)
    outs_s = _layer(x_sample, p_sample[0], cache_k[0], cache_v[0], cache_kidx[0], state_gdn[0],
                    state_gdn_conv[0], state_ffn_conv[0], wts, tm=256, tq=256)
    yp, ys = outs_p[0], outs_s[0]
    return (yp, ys) + tuple(a[None] for a in outs_p[1:]) + tuple(a[None] for a in outs_s[1:])
```

```python
import functools

import numpy as np
import jax
import jax.numpy as jnp
from jax import lax
from jax.experimental import pallas as pl
from jax.experimental.pallas import tpu as pltpu

F32 = jnp.float32
BF16 = jnp.bfloat16
I32 = jnp.int32

D_MODEL = 1024
CHUNK = 64
HA, DH = 8, 64
H_IDX, D_IDX = 8, 64
TOPK_MAX = 256
NUM_BUCKETS, MAX_DISTANCE = 32, 128
HB, DK, DV = 4, 128, 128
CONV_B = 4
D_FF = 2816
CONV_F = 3
D_PLE = 256
EPS = 1e-6
NEG = -1e30
WA = HA * DH
WB = HB * DK
C_CONV_B = 3 * WB
INT_MIN = -2 ** 31
LOG2E = 1.4426950408889634

_O_QA, _O_KA, _O_VA, _O_QI, _O_KI, _O_WI = 0, 512, 1024, 1536, 2048, 2112
_O_QB, _O_GB, _O_BB, _O_AB, _O_GA, _O_GBR = 2120, 3656, 4168, 4172, 4176, 5200
_L_WI, _L_BB, _L_AB = 64, 72, 76

VMEM_LIMIT = 56 * 1024 * 1024


def _cparams(sem):
    return pltpu.CompilerParams(dimension_semantics=sem, vmem_limit_bytes=VMEM_LIMIT)


def _const_spec(shape):
    nd = len(shape)
    return pl.BlockSpec(shape, lambda *_: (0,) * nd, pipeline_mode=pl.Buffered(1))


def _rms(x, g):
    return x * lax.rsqrt(jnp.mean(x * x, axis=-1, keepdims=True) + EPS) * g


def _split(x):
    hi = x.astype(BF16)
    lo = (x - hi.astype(F32)).astype(BF16)
    return hi, lo


def _dot(a, b):
    return jnp.dot(a, b, preferred_element_type=F32)


def _dot3(a, b):
    ah, al = _split(a)
    bh, bl = _split(b)
    return _dot(ah, bh) + (_dot(al, bh) + _dot(ah, bl))


def _sigmoid(x):
    return 1.0 / (1.0 + jnp.exp(-x))


def _silu(x):
    return x * _sigmoid(x)


def _transpose32(a):
    a = list(a)
    j, m = 16, 0x0000FFFF
    while j:
        k = 0
        while k < 32:
            t = (a[k] ^ lax.shift_right_logical(a[k + j], jnp.int32(j))) & jnp.int32(m - (1 << 32) if m >> 31 else m)
            a[k] = a[k] ^ t
            a[k + j] = a[k + j] ^ (t << j)
            k = (k + j + 1) & ~j
        j >>= 1
        m = (m ^ (m << j)) & 0xFFFFFFFF
    return a


V_ROWS = DH + 16
RADIX_TILES = 8
_NT = (((1,), (1,)), ((), ()))


def _in_proj_kernel(x_ref, g_ref, wm_ref, wt_ref, wqh_ref, wql_ref, wsh_ref, wsl_ref, wwh_ref, wwl_ref,
                    ph_ref, plo_ref, sc_ref,
                    ka_ref, va_ref, kbf_ref, conv_ref, gb_ref, ga_ref, gbr_ref, small_ref, ki3_ref,
                    qT_ref, qihT_ref, qilT_ref, wiT_ref, vTa_ref):
    tm = x_ref.shape[0]
    h = _rms(x_ref[...], g_ref[...])
    hh, hl = _split(h)

    def main(lo, hi):
        return _dot(hh, wm_ref[:, lo:hi])

    def nt(w, a):
        return lax.dot_general(w, a, _NT, preferred_element_type=F32)

    def nt3(wh_ref, wl_ref):
        return nt(wh_ref[...], hh) + (nt(wh_ref[...], hl) + nt(wl_ref[...], hh))

    ka = main(0, 512)
    ka_ref[...] = ka
    kbf_ref[...] = ka.astype(BF16)
    va_ref[...] = main(512, 1024)
    conv_ref[...] = main(1024, 2560)
    gb_ref[...] = main(2560, 3072)
    ga_ref[...] = main(3072, 4096)
    gbr_ref[...] = main(4096, 5120)

    qT_ref[0] = nt(wt_ref[0:WA, :], hh).astype(BF16)
    vT = nt(wt_ref[WA:2 * WA, :], hh).astype(BF16)
    ones = jnp.ones((V_ROWS - DH, tm), BF16)
    for hd in range(HA):
        vTa_ref[0, hd * V_ROWS:hd * V_ROWS + DH, :] = vT[hd * DH:(hd + 1) * DH]
        vTa_ref[0, hd * V_ROWS + DH:(hd + 1) * V_ROWS, :] = ones
    qh, ql = _split(nt3(wqh_ref, wql_ref))
    qihT_ref[0] = qh
    qilT_ref[0] = ql
    wiT_ref[0] = nt3(wwh_ref, wwl_ref)[0:H_IDX] * (H_IDX ** -0.5)

    small = (_dot(hh, wsh_ref[...]) + (_dot(hl, wsh_ref[...]) + _dot(hh, wsl_ref[...]))) * sc_ref[...]
    small_ref[...] = small
    sh, sl = _split(small)
    ki3_ref[...] = (_dot(sh, ph_ref[...]) + _dot(sl, plo_ref[...])).astype(BF16)


def _in_proj(x2d, norm_mix, w_in, tm):
    n = x2d.shape[0]
    assert n % tm == 0
    g = n // tm
    w = w_in
    wm = jnp.concatenate([w[:, _O_KA:_O_QI], w[:, _O_QB:_O_BB], w[:, _O_GA:]], axis=1).astype(BF16)
    wt = jnp.concatenate([w[:, _O_QA:_O_KA] * (DH ** -0.5 * LOG2E), w[:, _O_VA:_O_QI]], axis=1).T.astype(BF16)
    hilo = lambda a: (a.astype(BF16), (a - a.astype(BF16).astype(F32)).astype(BF16))
    wqh, wql = hilo((w[:, _O_QI:_O_KI] * (D_IDX ** -0.5)).T)
    wsh, wsl = hilo(jnp.concatenate([w[:, _O_KI:_O_QB], w[:, _O_BB:_O_GA], jnp.zeros((D_MODEL, 48), F32)], axis=1))
    wwh, wwl = hilo(jnp.concatenate([w[:, _O_WI:_O_QB], jnp.zeros((D_MODEL, 8), F32)], axis=1).T)
    ph = np.zeros((128, 256), np.float32)
    plo = np.zeros((128, 256), np.float32)
    for c in range(64):
        ph[c, c] = 1.0
        ph[c, 64 + c] = 1.0
        plo[c, 128 + c] = 1.0
    sc = np.ones((1, 128), np.float32)
    sc[0, _L_WI:_L_WI + H_IDX] = H_IDX ** -0.5
    row = lambda c: pl.BlockSpec((tm, c), lambda i: (i, 0))
    colT = lambda r: pl.BlockSpec((1, r, tm), lambda i: (i, 0, 0))
    out_cols = [(512, F32), (512, F32), (512, BF16), (1536, F32), (512, F32), (1024, F32), (1024, F32), (128, F32),
                (256, BF16)]
    out_rows = [(WA, BF16), (WA, BF16), (WA, BF16), (H_IDX, F32), (HA * V_ROWS, BF16)]
    return pl.pallas_call(
        _in_proj_kernel,
        grid=(g,),
        in_specs=[row(D_MODEL), _const_spec((1, D_MODEL)), _const_spec((D_MODEL, 5120)),
                  _const_spec((2 * WA, D_MODEL)), _const_spec((WA, D_MODEL)), _const_spec((WA, D_MODEL)),
                  _const_spec((D_MODEL, 128)), _const_spec((D_MODEL, 128)),
                  _const_spec((16, D_MODEL)), _const_spec((16, D_MODEL)),
                  _const_spec((128, 256)), _const_spec((128, 256)), _const_spec((1, 128))],
        out_specs=[row(c) for c, _ in out_cols] + [colT(r) for r, _ in out_rows],
        out_shape=[jax.ShapeDtypeStruct((n, c), d) for c, d in out_cols]
        + [jax.ShapeDtypeStruct((g, r, tm), d) for r, d in out_rows],
        compiler_params=_cparams(("arbitrary",)),
        name="in_proj",
    )(x2d, norm_mix.reshape(1, D_MODEL), wm, wt, wqh, wql, wsh, wsl, wwh, wwl, jnp.asarray(ph, BF16),
      jnp.asarray(plo, BF16), jnp.asarray(sc))


def _store_keys(score, j, key_ref, pln_ref, eq_ref):
    tk, tq = score.shape
    bits = pltpu.bitcast(score, I32)
    bits = jnp.where(bits == INT_MIN, 0, bits)
    key = bits ^ ((bits >> 31) & 0x7FFFFFFF)
    key_ref[j] = key
    u3 = (key ^ INT_MIN).reshape(tk // 8, 8, tq)
    ones = jnp.full((8, tq), -1, I32)
    pln_ref[0, j] = ones
    for l0 in range(0, tq, 128):
        planes = _transpose32([u3[r][:, l0:l0 + 128] for r in range(32)])
        for b in range(32):
            pln_ref[b + 1, j, :, l0:l0 + 128] = planes[b]
    eq_ref[0, j] = ones


def _radix_threshold(pln_ref, eq_ref, nk, tq, topk):
    for d in range(RADIX_TILES - 1):
        pln_ref[:, nk + d] = jnp.zeros((33, 8, tq), I32)
        eq_ref[0, nk + d] = jnp.zeros((8, tq), I32)

    sub = lax.broadcasted_iota(I32, (8, tq), 0)
    group_masks = (-0x10000, -0xFF0100, -0xF0F0F10, -0x33333334, -0x55555556)

    def pos_plane(e, j):
        if e < 6:
            return jnp.broadcast_to(-((~j >> (5 - e)) & 1), (8, tq))
        if e < 11:
            return jnp.full((8, tq), group_masks[e - 6], I32)
        return -((~sub >> (13 - e)) & 1)

    def sweep(prev, cur, carry, src, dst):
        n_gt, flip = carry

        def body(jq, cnt):
            for d in range(RADIX_TILES):
                j = jq * RADIX_TILES + d
                e = eq_ref[src, j] & (prev(j) ^ flip)
                eq_ref[dst, j] = e
                cnt = cnt + lax.population_count(e & cur(j))
            return cnt

        cnt = lax.fori_loop(0, (nk + RADIX_TILES - 1) // RADIX_TILES, body, jnp.zeros((8, tq), I32))
        cnt = n_gt + cnt.sum(axis=0, keepdims=True)
        acc = cnt >= topk
        return acc, (jnp.where(acc, n_gt, cnt), jnp.where(acc, 0, -1))

    def key_sweep(it, carry, src, dst):
        tu, rest = carry
        acc, rest = sweep(lambda j: pln_ref[it, j], lambda j: pln_ref[it + 1, j], rest, src, dst)
        return jnp.where(acc, tu | jnp.left_shift(jnp.int32(1), 31 - it), tu), rest

    def two_bits(i, carry):
        return key_sweep(2 * i + 1, key_sweep(2 * i, carry, 0, 1), 1, 0)

    zero_row = jnp.zeros((1, tq), I32)
    tu, rest = lax.fori_loop(0, 16, two_bits, (zero_row, (zero_row, zero_row)))
    inv_pos = zero_row
    for e in range(14):
        prev = (lambda j: pln_ref[32, j]) if e == 0 else functools.partial(pos_plane, e - 1)
        acc, rest = sweep(prev, functools.partial(pos_plane, e), rest, e % 2, 1 - e % 2)
        inv_pos = jnp.where(acc, inv_pos | (1 << (13 - e)), inv_pos)
    return tu ^ INT_MIN, ~inv_pos & 0x3FFF


def _dsa_kernel(qT_ref, qihT_ref, qilT_ref, wiT_ref, k_ref, vT_ref, ki3_ref, bias_ref, cap_ref,
                o_ref, key_ref, l_ref, acc_ref, lg_ref, pln_ref, eq_ref, sc_ref, ma_ref,
                *, tq, tk, nk_static, causal, topk):
    nk = (pl.program_id(1) + 1) if causal else nk_static
    wiT = wiT_ref[0]

    def when(cond):
        if isinstance(cond, bool):
            return (lambda f: f()) if cond else (lambda f: None)
        return pl.when(cond)

    zeros64 = jnp.zeros((64, tq), BF16)
    q3 = []
    for h in range(H_IDX):
        hi = qihT_ref[0, h * 64:(h + 1) * 64, :]
        lo = qilT_ref[0, h * 64:(h + 1) * 64, :]
        q3.append(jnp.concatenate([hi, lo, hi, zeros64], axis=0))

    def score_matmuls(j):
        kt = ki3_ref[0, j]
        acc = None
        for h in range(H_IDX):
            t = jnp.maximum(_dot(kt, q3[h]), 0.0) * wiT[h:h + 1, :]
            acc = t if acc is None else acc + t
        sc_ref[...] = jnp.minimum(acc, cap_ref[jnp.where(j == nk - 1, 1, 0)])

    def score_finish(j):
        _store_keys(sc_ref[...], j, key_ref, pln_ref, eq_ref)

    def score_body(j, c):
        score_finish(j - 1)
        score_matmuls(j)
        return c

    score_matmuls(0)
    lax.fori_loop(1, nk, score_body, 0)
    score_finish(nk - 1)

    thr, pos_thr = _radix_threshold(pln_ref, eq_ref, nk, tq, topk)

    l_ref[...] = jnp.zeros(l_ref.shape, F32)
    acc_ref[...] = jnp.zeros(acc_ref.shape, F32)
    qm = []
    for h in range(HA):
        qh = qT_ref[0, h * 64:(h + 1) * 64, :]
        qm.append(jnp.concatenate([qh, zeros64] if h % 2 == 0 else [zeros64, qh], axis=0))
    row = lax.broadcasted_iota(I32, (tk, tq), 0)

    def tile_logits(j, slot):
        sel = key_ref[j] > jnp.where(row <= pos_thr - j * tk, thr - 1, thr)
        kind = jnp.clip(j - (nk - 3), 0, 2)
        mx = []
        for h in range(HA):
            pr = h // 2
            lg = _dot(k_ref[0, j, :, pr * 128:(pr + 1) * 128], qm[h]) + bias_ref[kind, h]
            lg = jnp.where(sel, lg, NEG).astype(BF16)
            lg_ref[slot, h] = lg
            part = jnp.max(lg.reshape(tk // 16, 16, tq), axis=0).astype(F32)
            mx.append(jnp.max(part, axis=0, keepdims=True))
        return jnp.concatenate(mx, axis=0)

    def tile_values(j, slot, mx):
        m_old = ma_ref[...]
        m_new = jnp.maximum(m_old, mx)
        alpha = jnp.exp2(m_old - m_new)
        ma_ref[...] = m_new
        ls = []
        for h in range(HA):
            p = jnp.exp2(lg_ref[slot, h] - m_new[h:h + 1, :].astype(BF16))
            pv = _dot(vT_ref[0, j, h * V_ROWS:(h + 1) * V_ROWS, :], p)
            acc_ref[h * 64:(h + 1) * 64, :] = alpha[h:h + 1, :] * acc_ref[h * 64:(h + 1) * 64, :] + pv[0:DH]
            ls.append(pv[DH:DH + 1])
        l_ref[...] = alpha * l_ref[...] + jnp.concatenate(ls, axis=0)

    ma_ref[...] = jnp.full((HA, tq), NEG, F32)

    def pair(i, c):
        mx0 = tile_logits(2 * i, 0)
        mx1 = tile_logits(2 * i + 1, 1)
        tile_values(2 * i, 0, mx0)
        tile_values(2 * i + 1, 1, mx1)
        return c

    lax.fori_loop(0, nk // 2, pair, 0)

    @when(nk % 2 == 1)
    def _():
        tile_values(nk - 1, 0, tile_logits(nk - 1, 0))

    for h in range(HA):
        o_ref[0, h * 64:(h + 1) * 64, :] = (acc_ref[h * 64:(h + 1) * 64, :] / l_ref[h:h + 1, :]).astype(BF16)


def _t5_bucket_np(rel):
    half = NUM_BUCKETS // 2
    max_exact = half // 2
    out = np.zeros(rel.shape, np.int64)
    flat_rel = rel.reshape(-1)
    flat = out.reshape(-1)
    for a in range(flat_rel.size):
        r = int(flat_rel[a])
        n = abs(r)
        b = n if n < max_exact else min(half - 1, (n * n).bit_length() + 1)
        flat[a] = b + (half if r > 0 else 0)
    return out


def _bias_tables(rel_bias, adm, tk, tq, tq_valid):
    kj = np.arange(tk)[:, None]
    t = np.minimum(np.arange(tq), tq_valid - 1)[None, :]
    rel_to_bucket = _t5_bucket_np(np.arange(-2 * tk - tq, tk + 1))
    lut = lambda rel: rel_to_bucket[rel + 2 * tk + tq]
    far_bucket = NUM_BUCKETS // 2 - 1
    idx = np.stack([np.full((tk, tq), far_bucket), lut(kj - tk - t), lut(kj - t)], axis=0)
    tab = rel_bias.astype(F32) * LOG2E
    onehot = jnp.asarray(idx[..., None] == np.arange(NUM_BUCKETS), F32)
    bias = jnp.einsum("ktqb,bh->khtq", onehot, tab - tab[far_bucket], precision=lax.Precision.HIGHEST)
    mask = np.zeros((3, 1, tk, tq), np.float32)
    mask[2, 0] = np.where(adm > 0.5, 0.0, NEG)
    return bias + jnp.asarray(mask)


def _dsa(qT, qihT, qilT, wiT, k4, vT4, ki34, bias, adm, *, nq, causal, topk):
    g, _, tq = qT.shape
    b, nk, tk, _ = k4.shape
    assert g == b * nq and tk == 256 and nk + RADIX_TILES - 1 <= 64
    cap = jnp.asarray(np.stack([np.full((tk, tq), np.inf, np.float32), np.where(adm > 0.5, np.inf, NEG)]), F32)
    qspec = lambda r: pl.BlockSpec((1, r, tq), lambda bi, i: (bi * nq + i, 0, 0))
    kspec = lambda s: pl.BlockSpec((1,) + s, lambda bi, i: (bi, 0, 0, 0), pipeline_mode=pl.Buffered(1))
    kern = functools.partial(_dsa_kernel, tq=tq, tk=tk, nk_static=nk, causal=causal, topk=topk)
    return pl.pallas_call(
        kern,
        grid=(b, nq),
        in_specs=[qspec(512), qspec(512), qspec(512), qspec(8),
                  kspec((nk, tk, 512)), kspec((nk, HA * V_ROWS, tk)), kspec((nk, tk, 256)),
                  _const_spec((3, HA, tk, tq)), _const_spec((2, tk, tq))],
        out_specs=qspec(512),
        out_shape=jax.ShapeDtypeStruct((g, 512, tq), BF16),
        scratch_shapes=[pltpu.VMEM((nk, tk, tq), I32), pltpu.VMEM((8, tq), F32),
                        pltpu.VMEM((512, tq), F32), pltpu.VMEM((2, HA, tk, tq), BF16),
                        pltpu.VMEM((33, nk + RADIX_TILES - 1, 8, tq), I32),
                        pltpu.VMEM((2, nk + RADIX_TILES - 1, 8, tq), I32),
                        pltpu.VMEM((tk, tq), F32), pltpu.VMEM((HA, tq), F32)],
        compiler_params=_cparams(("arbitrary", "arbitrary")),
        name="dsa_causal" if causal else "dsa_cached",
    )(qT, qihT, qilT, wiT, k4, vT4, ki34, bias, cap)


def _dsa_step_kernel(qa_ref, qih_ref, qil_ref, w_ref, ck_ref, cv_ref, cki_ref, nk_ref, nv_ref, nki_ref,
                     bias_ref, cap_ref, blk_ref, o_ref, key_ref, pln_ref, eq_ref, m_ref, l_ref, acc_ref,
                     *, nkc, t, topk):
    tk, tq = nk_ref.shape[1], qa_ref.shape[2]
    qih, qil = qih_ref[0], qil_ref[0]
    w = w_ref[0, 0:1, :]
    shifts = [t << s for s in range((tq // t).bit_length() - 2, -1, -1)]

    def scores(kx, j, cap):
        kh, kl = _split(kx)
        s = _dot(kh, qih) + (_dot(kl, qih) + _dot(kh, qil))
        sc = jnp.maximum(s, 0.0) * w
        for sh in shifts:
            sc = sc + pltpu.roll(sc, sh, 1)
        if cap is not None:
            sc = jnp.minimum(sc, cap)
        _store_keys(sc, j, key_ref, pln_ref, eq_ref)

    def score_body(j, c):
        scores(cki_ref[0, pl.ds(pl.multiple_of(j * tk, tk), tk), :], j, None)
        return c

    lax.fori_loop(0, nkc, score_body, 0)
    scores(nki_ref[0], nkc, cap_ref[...])
    thr, pos_thr = _radix_threshold(pln_ref, eq_ref, nkc + 1, tq, topk)

    m_ref[...] = jnp.full(m_ref.shape, NEG, F32)
    l_ref[...] = jnp.zeros(l_ref.shape, F32)
    acc_ref[...] = jnp.zeros(acc_ref.shape, F32)
    qa = qa_ref[0]
    row = lax.broadcasted_iota(I32, (tk, tq), 0)
    ones8 = jnp.ones((8, tk), BF16)
    tn = (((0,), (0,)), ((), ()))

    def attend(k_bf, v_bf, j, bias):
        sel = key_ref[j] > jnp.where(row <= pos_thr - j * tk, thr - 1, thr)
        lg = _dot(k_bf, qa)
        if bias is not None:
            lg = lg + bias
        lg = jnp.where(sel, lg, NEG).astype(BF16)
        part = jnp.max(lg.reshape(tk // 16, 16, tq), axis=0).astype(F32)
        m_old = m_ref[0:1, :]
        m_new = jnp.maximum(m_old, jnp.max(part, axis=0, keepdims=True))
        alpha = jnp.exp2(m_old - m_new)
        p = jnp.exp2(lg - m_new.astype(BF16))
        l_ref[0:1, :] = alpha * l_ref[0:1, :] + _dot(ones8, p)[0:1]
        acc_ref[...] = alpha * acc_ref[...] + lax.dot_general(v_bf, p, tn, preferred_element_type=F32)
        m_ref[0:1, :] = m_new

    def cached(j, bias):
        rows = pl.ds(pl.multiple_of(j * tk, tk), tk)
        attend(ck_ref[0, rows, :].astype(BF16), cv_ref[0, rows, :].astype(BF16), j, bias)

    def far(j, c):
        cached(j, None)
        return c

    lax.fori_loop(0, nkc - 1, far, 0)
    cached(nkc - 1, bias_ref[0])
    attend(nk_ref[0], nv_ref[0], nkc, bias_ref[1])
    o = acc_ref[...] / l_ref[0:1, :] * blk_ref[...]
    for sh in shifts:
        o = o + pltpu.roll(o, sh, 1)
    o_ref[0] = o.astype(BF16)


def _dsa_step(qT, qihT, qilT, wiT, kbf, va, ki_new, past_k, past_v, past_kidx, rel_bias, *, b, t, topk):
    past = past_k.shape[1]
    tk, tq = 256, HA * t
    assert tq == 128 and past % tk == 0 and t <= tk
    nkc = past // tk
    nk = nkc + 1
    per_b = lambda a: jnp.swapaxes(a[0].reshape(a.shape[1], b, t), 0, 1)
    blk = (np.arange(WA)[:, None] // DH == np.arange(tq)[None, :] // t).astype(np.float32)
    qa = (jnp.tile(per_b(qT), (1, 1, HA)) * jnp.asarray(blk, BF16))
    lanes = lambda a: jnp.swapaxes(per_b(a).reshape(b, HA, D_IDX, t), 1, 2).reshape(b, D_IDX, tq)
    w = jnp.broadcast_to(per_b(wiT).reshape(b, 1, tq), (b, 8, tq))
    pad_rows = lambda a: jnp.pad(a.reshape(b, t, a.shape[-1]), ((0, 0), (0, tk - t), (0, 0)))
    adm = np.broadcast_to(np.arange(tk)[:, None] < t, (tk, tq)).astype(np.float32)
    bias = _bias_tables(rel_bias, adm[:, :t], tk, t, t)[1:]
    bias = jnp.swapaxes(bias, 1, 2).reshape(2, tk, tq)
    cap = jnp.asarray(np.where(adm > 0.5, np.inf, NEG), F32)
    bspec = lambda s: pl.BlockSpec((1,) + s, lambda bi: (bi,) + (0,) * len(s))
    return pl.pallas_call(
        functools.partial(_dsa_step_kernel, nkc=nkc, t=t, topk=topk),
        grid=(b,),
        in_specs=[bspec((WA, tq)), bspec((D_IDX, tq)), bspec((D_IDX, tq)), bspec((8, tq)),
                  bspec((past, WA)), bspec((past, WA)), bspec((past, D_IDX)),
                  bspec((tk, WA)), bspec((tk, WA)), bspec((tk, D_IDX)),
                  _const_spec((2, tk, tq)), _const_spec((tk, tq)), _const_spec((WA, tq))],
        out_specs=bspec((WA, tq)),
        out_shape=jax.ShapeDtypeStruct((b, WA, tq), BF16),
        scratch_shapes=[pltpu.VMEM((nk, tk, tq), I32), pltpu.VMEM((33, nk + RADIX_TILES - 1, 8, tq), I32),
                        pltpu.VMEM((2, nk + RADIX_TILES - 1, 8, tq), I32), pltpu.VMEM((8, tq), F32),
                        pltpu.VMEM((8, tq), F32), pltpu.VMEM((WA, tq), F32)],
        compiler_params=_cparams(("arbitrary",)),
        name="dsa_step",
    )(qa, lanes(qihT), lanes(qilT), w, past_k.reshape(b, past, WA), past_v.reshape(b, past, WA), past_kidx,
      pad_rows(kbf), pad_rows(va.astype(BF16)), pad_rows(ki_new), bias, cap, jnp.asarray(blk))


def _gdn_pre_kernel(conv_ref, prev_ref, hist_ref, small_ref, cw_ref, alog_ref, dtb_ref,
                    wm_ref, um_ref, qe_ref, oi_ref, e_ref, ext_ref, *, rows, nvalid):
    c = CHUNK
    n = HB * c
    ext_ref[0:8, :] = jnp.where(pl.program_id(1) == 0, hist_ref[0], prev_ref[0])
    ext_ref[8:8 + rows, :] = conv_ref[0]
    cb = ext_ref[5:5 + rows, :] * cw_ref[0:1, :]
    for j in range(1, CONV_B):
        cb = cb + ext_ref[5 + j:5 + j + rows, :] * cw_ref[j:j + 1, :]
    cb = _silu(cb)

    small = small_ref[0]
    pos = lax.broadcasted_iota(I32, (rows, 1), 0) % c
    rowv = (pos < nvalid).astype(F32)
    beta_all = _sigmoid(small) * rowv
    sp = small + dtb_ref[...]
    g_all = -jnp.exp(alog_ref[...]) * (jnp.maximum(sp, 0.0) + jnp.log(1.0 + jnp.exp(-jnp.abs(sp)))) * rowv

    ri = lax.broadcasted_iota(I32, (n, n), 0)
    ci = lax.broadcasted_iota(I32, (n, n), 1)
    same = (ri // c) == (ci // c)
    tri_b = (same & (ci <= ri)).astype(BF16)
    bd_f = same.astype(F32)
    bd_b = same.astype(BF16)
    wr = lax.broadcasted_iota(I32, (c, n), 0)
    wl = lax.broadcasted_iota(I32, (c, n), 1)
    grp = wl // c
    tri_w = (wl % c) <= wr
    strict_w = (wl % c) < wr
    eye_w = ((wl % c) == wr).astype(F32)
    nt = (((1,), (1,)), ((), ()))
    tn = (((0,), (0,)), ((), ()))

    def l2n(x):
        return x * lax.rsqrt(jnp.sum(x * x, axis=-1, keepdims=True) + EPS)

    def to_wide(full):
        out = jnp.where(grp == 0, full[0:c, :], 0.0)
        for h in range(1, HB):
            out = out + jnp.where(grp == h, full[h * c:(h + 1) * c, :], 0.0)
        return out

    def tile4(x):
        return jnp.concatenate([x] * HB, axis=0)

    def chunk(k0):
        def stack(fn):
            return jnp.concatenate([fn(h) for h in range(HB)], axis=0)

        rv = rowv[k0:k0 + c]
        q = stack(lambda h: l2n(cb[k0:k0 + c, h * DK:(h + 1) * DK]) * (DK ** -0.5))
        k = stack(lambda h: l2n(cb[k0:k0 + c, WB + h * DK:WB + (h + 1) * DK]) * rv)
        v = stack(lambda h: cb[k0:k0 + c, 2 * WB + h * DV:2 * WB + (h + 1) * DV] * rv)
        beta = stack(lambda h: beta_all[k0:k0 + c, _L_BB + h:_L_BB + h + 1])
        g = stack(lambda h: g_all[k0:k0 + c, _L_AB + h:_L_AB + h + 1])
        yield

        gh, gl = _split(jnp.broadcast_to(g, (n, 128)))
        gcum = _dot(tri_b, gh) + _dot(tri_b, gl)
        yield
        gcum_row = gcum.T[0:1, :]
        gcum_col = gcum[:, 0:1]
        col_w = to_wide(jnp.concatenate([gcum, gcum], axis=1))
        decay_w = jnp.where(tri_w, jnp.exp(jnp.where(tri_w, col_w - gcum_row, 0.0)), 0.0)
        kb = k * beta
        kbf = k.astype(BF16)
        kk_w = to_wide(lax.dot_general(kb.astype(BF16), kbf, nt, preferred_element_type=F32))
        qk_w = to_wide(lax.dot_general(q.astype(BF16), kbf, nt, preferred_element_type=F32))
        nmat_w = jnp.where(strict_w, kk_w * decay_w, 0.0)
        attn_w = jnp.where(tri_w, qk_w * decay_w, 0.0)
        yield

        inv_w = eye_w - nmat_w
        ph = nmat_w.astype(BF16)
        bh = tile4(ph) * bd_b
        for _ in range(int(np.log2(c)) - 1):
            ph = _dot(ph, bh).astype(BF16)
            yield
            bh = tile4(ph) * bd_b
            inv_w = inv_w + _dot(inv_w.astype(BF16), bh)
            yield
        rhs = jnp.concatenate([v * beta, kb * jnp.exp(gcum_col)], axis=1)
        sol = _dot3(tile4(inv_w) * bd_f, rhs)
        yield
        solb = sol.astype(BF16)
        aw = _dot((tile4(attn_w) * bd_f).astype(BF16), solb)
        yield
        ck = k0 // c
        oi_ref[0, ck] = aw[:, :DV]
        qe_ref[0, ck] = (q * jnp.exp(gcum_col) - aw[:, DV:]).astype(BF16)
        for h in range(HB):
            rs = slice(h * c, (h + 1) * c)
            g_last = gcum[(h + 1) * c - 1:(h + 1) * c, :]
            kd = (k[rs] * jnp.exp(g_last - gcum[rs])).astype(BF16)
            uw = lax.dot_general(kd, solb[rs], tn, preferred_element_type=F32)
            um_ref[0, ck, h] = uw[:, :DV]
            wm_ref[0, ck, h] = uw[:, DV:].astype(BF16)
            e_ref[0, ck, h:h + 1, :] = jnp.exp(g_last)
        e_ref[0, ck, HB:8, :] = jnp.zeros((8 - HB, 128), F32)

    gens = [chunk(k0) for k0 in range(0, rows, c)]
    while gens:
        alive = []
        for gen in gens:
            try:
                next(gen)
                alive.append(gen)
            except StopIteration:
                pass
        gens = alive


def _gdn_scan_kernel(wm_ref, um_ref, qe_ref, oi_ref, e_ref, gb_ref, s0_ref, ng_ref, ob_ref, sfin_ref, s_ref,
                     *, bb, g):
    c = CHUNK

    @pl.when(pl.program_id(1) == 0)
    def _():
        s_ref[...] = s0_ref[...]

    ng = ng_ref[...]
    for b in range(bb):
        for ck in range(g):
            for h in range(HB):
                s = s_ref[b, h]
                sb = s.astype(BF16)
                rs = slice(h * c, (h + 1) * c)
                o = _dot(qe_ref[b, ck, rs, :], sb) + oi_ref[b, ck, rs, :]
                s_ref[b, h] = e_ref[b, ck, h:h + 1, :] * s + (um_ref[b, ck, h] - _dot(wm_ref[b, ck, h], sb))
                gate = gb_ref[b, ck * c:(ck + 1) * c, h * DV:(h + 1) * DV]
                ob_ref[b, ck * c:(ck + 1) * c, h * DV:(h + 1) * DV] = (_rms(o, ng) * _silu(gate)).astype(BF16)
    sfin_ref[...] = s_ref[...]


def _gdn(conv_in, hist, gb, small, s0, conv_w, a_log, dt_bias, norm_gdn, nvalid):
    b, t, _ = conv_in.shape
    assert t % CHUNK == 0
    rows = 256 if t % 256 == 0 else CHUNK
    nc, cps = t // CHUNK, rows // CHUNK
    n = HB * CHUNK
    alog = jnp.zeros((1, 128), F32).at[0, _L_AB:_L_AB + HB].set(a_log)
    dtb = jnp.zeros((1, 128), F32).at[0, _L_AB:_L_AB + HB].set(dt_bias)
    row = lambda c_: pl.BlockSpec((1, rows, c_), lambda bi, ti: (bi, ti, 0))
    prev = pl.BlockSpec((1, 8, C_CONV_B), lambda bi, ti: (bi, jnp.maximum(ti * (rows // 8) - 1, 0), 0))
    per_b = lambda s: pl.BlockSpec((1,) + s, lambda bi, ti: (bi,) + (0,) * len(s))
    chunked = lambda s: pl.BlockSpec((1, cps) + s, lambda bi, ti: (bi, ti) + (0,) * len(s))
    wm, um, qe, oi, e = pl.pallas_call(
        functools.partial(_gdn_pre_kernel, rows=rows, nvalid=nvalid),
        grid=(b, t // rows),
        in_specs=[row(C_CONV_B), prev, per_b((8, C_CONV_B)), row(128),
                  _const_spec((CONV_B, C_CONV_B)), _const_spec((1, 128)), _const_spec((1, 128))],
        out_specs=[chunked((HB, DK, DV)), chunked((HB, DK, DV)), chunked((n, DK)), chunked((n, DV)),
                   chunked((8, 128))],
        out_shape=[jax.ShapeDtypeStruct((b, nc, HB, DK, DV), BF16), jax.ShapeDtypeStruct((b, nc, HB, DK, DV), F32),
                   jax.ShapeDtypeStruct((b, nc, n, DK), BF16), jax.ShapeDtypeStruct((b, nc, n, DV), F32),
                   jax.ShapeDtypeStruct((b, nc, 8, 128), F32)],
        scratch_shapes=[pltpu.VMEM((8 + rows, C_CONV_B), F32)],
        compiler_params=_cparams(("arbitrary", "arbitrary")),
        name="gdn_pre",
    )(conv_in, conv_in, hist, small, conv_w, alog, dtb)

    bb = 2 if b % 2 == 0 else 1
    g = 2 if nc % 2 == 0 else 1
    blk = lambda s: pl.BlockSpec((bb, g) + s, lambda bi, ci: (bi, ci) + (0,) * len(s))
    rowb = pl.BlockSpec((bb, g * CHUNK, WB), lambda bi, ci: (bi, ci, 0))
    state = pl.BlockSpec((bb, HB, DK, DV), lambda bi, ci: (bi, 0, 0, 0))
    return pl.pallas_call(
        functools.partial(_gdn_scan_kernel, bb=bb, g=g),
        grid=(b // bb, nc // g),
        in_specs=[blk((HB, DK, DV)), blk((HB, DK, DV)), blk((n, DK)), blk((n, DV)), blk((8, 128)), rowb, state,
                  _const_spec((1, DV))],
        out_specs=[rowb, state],
        out_shape=[jax.ShapeDtypeStruct((b, t, WB), BF16), jax.ShapeDtypeStruct((b, HB, DK, DV), F32)],
        scratch_shapes=[pltpu.VMEM((bb, HB, DK, DV), F32)],
        compiler_params=_cparams(("arbitrary", "arbitrary")),
        name="gdn_scan",
    )(wm, um, qe, oi, e, gb, s0, norm_gdn.reshape(1, DV))


def _post_kernel(x_ref, oaT_ref, ob_ref, ga_ref, gbr_ref, p_ref, hist_ref, wa_ref, wb_ref, wo_ref, nf_ref,
                 wup_ref, cw_ref, wdn_ref, npl_ref, wpg_ref, wple_ref, nfin_ref, y_ref, tail_ref, ext_ref, *, tm):
    @pl.when(pl.program_id(1) == 0)
    def _():
        ext_ref[0:8, :] = hist_ref[0]

    ya = lax.dot_general(oaT_ref[0], wa_ref[...], (((0,), (0,)), ((), ())), preferred_element_type=F32)
    yb = _dot(ob_ref[0], wb_ref[...])
    mix = _sigmoid(ga_ref[0]) * ya + _sigmoid(gbr_ref[0]) * yb
    x1 = x_ref[0] + _dot(mix.astype(BF16), wo_ref[...])
    h2 = _rms(x1, nf_ref[...]).astype(BF16)
    ext_ref[8:8 + tm, :] = _dot(h2, wup_ref[:, 0:D_FF])
    u_val = _dot(h2, wup_ref[:, D_FF:2 * D_FF])
    cv = ext_ref[6:6 + tm, :] * cw_ref[0:1, :]
    for j in range(1, CONV_F):
        cv = cv + ext_ref[6 + j:6 + j + tm, :] * cw_ref[j:j + 1, :]
    tail = ext_ref[tm:tm + 8, :]
    ext_ref[0:8, :] = tail
    tail_ref[0] = tail
    act = 0.5 * cv * (1.0 + jnp.tanh(0.7978845608028654 * (cv + 0.044715 * (cv * cv * cv))))
    x2 = x1 + _dot((act * u_val).astype(BF16), wdn_ref[...])
    gate = _sigmoid(_dot(_rms(x2, npl_ref[...]).astype(BF16), wpg_ref[...]))
    x3 = x2 + gate * _dot(p_ref[0].astype(BF16), wple_ref[...])
    y_ref[0] = _rms(x3, nfin_ref[...])


def _post(x, oaT, ob, ga, gbr, p, hist, w_proj_a, w_proj_b, w_out, norm_ffn, w_up, conv_ffn, w_down, norm_ple,
          w_ple_gate, w_ple, norm_final, tm):
    b, t, _ = x.shape
    nt = t // tm
    assert t % tm == 0 and tm >= 8 and oaT.shape == (b * nt, WA, tm)
    row = lambda c: pl.BlockSpec((1, tm, c), lambda bi, ti: (bi, ti, 0))
    per_b = pl.BlockSpec((1, 8, D_FF), lambda bi, ti: (bi, 0, 0))
    vec = _const_spec((1, D_MODEL))
    return pl.pallas_call(
        functools.partial(_post_kernel, tm=tm),
        grid=(b, nt),
        in_specs=[row(D_MODEL), pl.BlockSpec((1, WA, tm), lambda bi, ti: (bi * nt + ti, 0, 0)), row(WB),
                  row(D_MODEL), row(D_MODEL), row(D_PLE), per_b,
                  _const_spec((WA, D_MODEL)), _const_spec((WB, D_MODEL)), _const_spec((D_MODEL, D_MODEL)), vec,
                  _const_spec((D_MODEL, 2 * D_FF)), _const_spec((CONV_F, D_FF)), _const_spec((D_FF, D_MODEL)),
                  vec, _const_spec((D_MODEL, D_MODEL)), _const_spec((D_PLE, D_MODEL)), vec],
        out_specs=[row(D_MODEL), per_b],
        out_shape=[jax.ShapeDtypeStruct((b, t, D_MODEL), F32), jax.ShapeDtypeStruct((b, 8, D_FF), F32)],
        scratch_shapes=[pltpu.VMEM((8 + tm, D_FF), F32)],
        compiler_params=_cparams(("arbitrary", "arbitrary")),
        name="post",
    )(x, oaT, ob, ga, gbr, p, hist, w_proj_a.astype(BF16), w_proj_b.astype(BF16), w_out.astype(BF16),
      norm_ffn.reshape(1, D_MODEL), w_up.astype(BF16), conv_ffn, w_down.astype(BF16),
      norm_ple.reshape(1, D_MODEL), w_ple_gate.astype(BF16), w_ple.astype(BF16), norm_final.reshape(1, D_MODEL))


def _pad_hist(hist, rows=8):
    b, r, c = hist.shape
    return jnp.concatenate([jnp.zeros((b, rows - r, c), hist.dtype), hist], axis=1)


def _layer(x, p, past_k, past_v, past_kidx, s_gdn, conv_b_hist, ffn_hist, wts, *, tm, tq):
    (norm_mix, w_in, conv_b, a_log, dt_bias, norm_gdn, w_proj_a, w_proj_b, w_out, norm_ffn, w_up, conv_ffn,
     w_down, norm_ple, w_ple, w_ple_gate, rel_bias, norm_final) = wts
    b, t, _ = x.shape
    n = b * t
    past = past_k.shape[1]
    topk = min(TOPK_MAX, (past + t) // 4)
    x2d = x.reshape(n, D_MODEL)
    tmi = min(tm, n)
    (ka, va, kbf, conv_in, gb, ga, gbr, small, ki3, qT, qihT, qilT, wiT, vTa) = _in_proj(x2d, norm_mix, w_in, tmi)

    if past == 0:
        assert tq == tmi and t % tq == 0 and tq % CHUNK == 0 and tq >= topk
        nq = t // tq
        kj = np.arange(tq)[:, None]
        adm = ((kj // CHUNK) <= (np.arange(tq)[None, :] // CHUNK)).astype(np.float32)
        oT = _dsa(qT, qihT, qilT, wiT, kbf.reshape(b, nq, tq, WA), vTa.reshape(b, nq, HA * V_ROWS, tq),
                  ki3.reshape(b, nq, tq, 256), _bias_tables(rel_bias, adm, tq, tq, tq), adm,
                  nq=nq, causal=True, topk=topk)
    else:
        assert n == tmi
        oT = _dsa_step(qT, qihT, qilT, wiT, kbf, va, small[:, :D_IDX], past_k, past_v, past_kidx, rel_bias,
                       b=b, t=t, topk=topk)
        oT = oT[:, :, :t]

    tp = -(-t // CHUNK) * CHUNK
    padt = lambda a: jnp.pad(a.reshape(b, t, a.shape[-1]), ((0, 0), (0, tp - t), (0, 0)))
    ob, s_new = _gdn(padt(conv_in), _pad_hist(conv_b_hist), padt(gb), padt(small), s_gdn, conv_b, a_log, dt_bias,
                     norm_gdn, nvalid=min(t, CHUNK))
    new_conv_b = jnp.concatenate([conv_b_hist, conv_in.reshape(b, t, C_CONV_B)], axis=1)[:, t:]

    per_bt = lambda a: a.reshape(b, t, a.shape[-1])
    y, tail = _post(x, oT, ob[:, :t], per_bt(ga), per_bt(gbr), p, _pad_hist(ffn_hist), w_proj_a, w_proj_b, w_out,
                    norm_ffn, w_up, conv_ffn, w_down, norm_ple, w_ple_gate, w_ple, norm_final, min(tm, t))
    new_ffn = tail[:, 8 - (CONV_F - 1):]
    return (y, ka.reshape(b, t, HA, DH), va.reshape(b, t, HA, DH), small[:, :D_IDX].reshape(b, t, D_IDX),
            s_new, new_conv_b, new_ffn)


def kernel(x_prompt, x_sample, p_prompt, p_sample, cache_k, cache_v, cache_kidx, state_gdn, state_gdn_conv,
           state_ffn_conv, norm_mix, w_in, conv_b, a_log, dt_bias, norm_gdn, w_proj_a, w_proj_b, w_out, norm_ffn,
           w_up, conv_ffn, w_down, norm_ple, w_ple, w_ple_gate, rel_bias, norm_final):
    assert norm_mix.shape[0] == 1
    bp = x_prompt.shape[0]
    dt = x_prompt.dtype
    wts = (norm_mix[0], w_in[0], conv_b[0], a_log[0], dt_bias[0], norm_gdn[0], w_proj_a[0], w_proj_b[0], w_out[0],
           norm_ffn[0], w_up[0], conv_ffn[0], w_down[0], norm_ple[0], w_ple[0], w_ple_gate[0], rel_bias, norm_final)
    outs_p = _layer(x_prompt, p_prompt[0], jnp.zeros((bp, 0, HA, DH), dt), jnp.zeros((bp, 0, HA, DH), dt),
                    jnp.zeros((bp, 0, D_IDX), dt), jnp.zeros((bp, HB, DK, DV), dt),
                    jnp.zeros((bp, CONV_B - 1, C_CONV_B), dt), jnp.zeros((bp, CONV_F - 1, D_FF), dt),
                    wts, tm=256, tq=256)
    outs_s = _layer(x_sample, p_sample[0], cache_k[0], cache_v[0], cache_kidx[0], state_gdn[0],
                    state_gdn_conv[0], state_ffn_conv[0], wts, tm=256, tq=256)
    yp, ys = outs_p[0], outs_s[0]
    return (yp, ys) + tuple(a[None] for a in outs_p[1:]) + tuple(a[None] for a in outs_s[1:])
```

```python
import functools

import numpy as np
import jax
import jax.numpy as jnp
from jax import lax
from jax.experimental import pallas as pl
from jax.experimental.pallas import tpu as pltpu

F32 = jnp.float32
BF16 = jnp.bfloat16
I32 = jnp.int32

D_MODEL = 1024
CHUNK = 64
HA, DH = 8, 64
H_IDX, D_IDX = 8, 64
TOPK_MAX = 256
NUM_BUCKETS, MAX_DISTANCE = 32, 128
HB, DK, DV = 4, 128, 128
CONV_B = 4
D_FF = 2816
CONV_F = 3
D_PLE = 256
EPS = 1e-6
NEG = -1e30
WA = HA * DH
WB = HB * DK
C_CONV_B = 3 * WB
INT_MIN = -2 ** 31
LOG2E = 1.4426950408889634

_O_QA, _O_KA, _O_VA, _O_QI, _O_KI, _O_WI = 0, 512, 1024, 1536, 2048, 2112
_O_QB, _O_GB, _O_BB, _O_AB, _O_GA, _O_GBR = 2120, 3656, 4168, 4172, 4176, 5200
_L_WI, _L_BB, _L_AB = 64, 72, 76

VMEM_LIMIT = 56 * 1024 * 1024


def _cparams(sem):
    return pltpu.CompilerParams(dimension_semantics=sem, vmem_limit_bytes=VMEM_LIMIT)


def _const_spec(shape):
    nd = len(shape)
    return pl.BlockSpec(shape, lambda *_: (0,) * nd, pipeline_mode=pl.Buffered(1))


def _rms(x, g):
    return x * lax.rsqrt(jnp.mean(x * x, axis=-1, keepdims=True) + EPS) * g


def _split(x):
    hi = x.astype(BF16)
    lo = (x - hi.astype(F32)).astype(BF16)
    return hi, lo


def _dot(a, b):
    return jnp.dot(a, b, preferred_element_type=F32)


def _dot3(a, b):
    ah, al = _split(a)
    bh, bl = _split(b)
    return _dot(ah, bh) + (_dot(al, bh) + _dot(ah, bl))


def _sigmoid(x):
    return 1.0 / (1.0 + jnp.exp(-x))


def _silu(x):
    return x * _sigmoid(x)


def _transpose32(a):
    a = list(a)
    j, m = 16, 0x0000FFFF
    while j:
        k = 0
        while k < 32:
            t = (a[k] ^ lax.shift_right_logical(a[k + j], jnp.int32(j))) & jnp.int32(m - (1 << 32) if m >> 31 else m)
            a[k] = a[k] ^ t
            a[k + j] = a[k + j] ^ (t << j)
            k = (k + j + 1) & ~j
        j >>= 1
        m = (m ^ (m << j)) & 0xFFFFFFFF
    return a


V_ROWS = DH + 16
RADIX_TILES = 8
_NT = (((1,), (1,)), ((), ()))


def _in_proj_kernel(x_ref, g_ref, wm_ref, wt_ref, wqh_ref, wql_ref, wsh_ref, wsl_ref, wwh_ref, wwl_ref,
                    ph_ref, plo_ref, sc_ref,
                    ka_ref, va_ref, kbf_ref, conv_ref, gb_ref, ga_ref, gbr_ref, small_ref, ki3_ref,
                    qT_ref, qihT_ref, qilT_ref, wiT_ref, vTa_ref):
    tm = x_ref.shape[0]
    h = _rms(x_ref[...], g_ref[...])
    hh, hl = _split(h)

    def main(lo, hi):
        return _dot(hh, wm_ref[:, lo:hi])

    def nt(w, a):
        return lax.dot_general(w, a, _NT, preferred_element_type=F32)

    def nt3(wh_ref, wl_ref):
        return nt(wh_ref[...], hh) + (nt(wh_ref[...], hl) + nt(wl_ref[...], hh))

    ka = main(0, 512)
    ka_ref[...] = ka
    kbf_ref[...] = ka.astype(BF16)
    va_ref[...] = main(512, 1024)
    conv_ref[...] = main(1024, 2560)
    gb_ref[...] = main(2560, 3072)
    ga_ref[...] = main(3072, 4096)
    gbr_ref[...] = main(4096, 5120)

    qT_ref[0] = nt(wt_ref[0:WA, :], hh).astype(BF16)
    vT = nt(wt_ref[WA:2 * WA, :], hh).astype(BF16)
    ones = jnp.ones((V_ROWS - DH, tm), BF16)
    for hd in range(HA):
        vTa_ref[0, hd * V_ROWS:hd * V_ROWS + DH, :] = vT[hd * DH:(hd + 1) * DH]
        vTa_ref[0, hd * V_ROWS + DH:(hd + 1) * V_ROWS, :] = ones
    qh, ql = _split(nt3(wqh_ref, wql_ref))
    qihT_ref[0] = qh
    qilT_ref[0] = ql
    wiT_ref[0] = nt3(wwh_ref, wwl_ref)[0:H_IDX] * (H_IDX ** -0.5)

    small = (_dot(hh, wsh_ref[...]) + (_dot(hl, wsh_ref[...]) + _dot(hh, wsl_ref[...]))) * sc_ref[...]
    small_ref[...] = small
    sh, sl = _split(small)
    ki3_ref[...] = (_dot(sh, ph_ref[...]) + _dot(sl, plo_ref[...])).astype(BF16)


def _in_proj(x2d, norm_mix, w_in, tm):
    n = x2d.shape[0]
    assert n % tm == 0
    g = n // tm
    w = w_in
    wm = jnp.concatenate([w[:, _O_KA:_O_QI], w[:, _O_QB:_O_BB], w[:, _O_GA:]], axis=1).astype(BF16)
    wt = jnp.concatenate([w[:, _O_QA:_O_KA] * (DH ** -0.5 * LOG2E), w[:, _O_VA:_O_QI]], axis=1).T.astype(BF16)
    hilo = lambda a: (a.astype(BF16), (a - a.astype(BF16).astype(F32)).astype(BF16))
    wqh, wql = hilo((w[:, _O_QI:_O_KI] * (D_IDX ** -0.5)).T)
    wsh, wsl = hilo(jnp.concatenate([w[:, _O_KI:_O_QB], w[:, _O_BB:_O_GA], jnp.zeros((D_MODEL, 48), F32)], axis=1))
    wwh, wwl = hilo(jnp.concatenate([w[:, _O_WI:_O_QB], jnp.zeros((D_MODEL, 8), F32)], axis=1).T)
    ph = np.zeros((128, 256), np.float32)
    plo = np.zeros((128, 256), np.float32)
    for c in range(64):
        ph[c, c] = 1.0
        ph[c, 64 + c] = 1.0
        plo[c, 128 + c] = 1.0
    sc = np.ones((1, 128), np.float32)
    sc[0, _L_WI:_L_WI + H_IDX] = H_IDX ** -0.5
    row = lambda c: pl.BlockSpec((tm, c), lambda i: (i, 0))
    colT = lambda r: pl.BlockSpec((1, r, tm), lambda i: (i, 0, 0))
    out_cols = [(512, F32), (512, F32), (512, BF16), (1536, F32), (512, F32), (1024, F32), (1024, F32), (128, F32),
                (256, BF16)]
    out_rows = [(WA, BF16), (WA, BF16), (WA, BF16), (H_IDX, F32), (HA * V_ROWS, BF16)]
    return pl.pallas_call(
        _in_proj_kernel,
        grid=(g,),
        in_specs=[row(D_MODEL), _const_spec((1, D_MODEL)), _const_spec((D_MODEL, 5120)),
                  _const_spec((2 * WA, D_MODEL)), _const_spec((WA, D_MODEL)), _const_spec((WA, D_MODEL)),
                  _const_spec((D_MODEL, 128)), _const_spec((D_MODEL, 128)),
                  _const_spec((16, D_MODEL)), _const_spec((16, D_MODEL)),
                  _const_spec((128, 256)), _const_spec((128, 256)), _const_spec((1, 128))],
        out_specs=[row(c) for c, _ in out_cols] + [colT(r) for r, _ in out_rows],
        out_shape=[jax.ShapeDtypeStruct((n, c), d) for c, d in out_cols]
        + [jax.ShapeDtypeStruct((g, r, tm), d) for r, d in out_rows],
        compiler_params=_cparams(("arbitrary",)),
        name="in_proj",
    )(x2d, norm_mix.reshape(1, D_MODEL), wm, wt, wqh, wql, wsh, wsl, wwh, wwl, jnp.asarray(ph, BF16),
      jnp.asarray(plo, BF16), jnp.asarray(sc))


def _store_keys(score, j, key_ref, pln_ref, eq_ref):
    tk, tq = score.shape
    bits = pltpu.bitcast(score, I32)
    bits = jnp.where(bits == INT_MIN, 0, bits)
    key = bits ^ ((bits >> 31) & 0x7FFFFFFF)
    key_ref[j] = key
    u3 = (key ^ INT_MIN).reshape(tk // 8, 8, tq)
    ones = jnp.full((8, tq), -1, I32)
    pln_ref[0, j] = ones
    for l0 in range(0, tq, 128):
        planes = _transpose32([u3[r][:, l0:l0 + 128] for r in range(32)])
        for b in range(32):
            pln_ref[b + 1, j, :, l0:l0 + 128] = planes[b]
    eq_ref[0, j] = ones


def _radix_threshold(pln_ref, eq_ref, nk, tq, topk):
    for d in range(RADIX_TILES - 1):
        pln_ref[:, nk + d] = jnp.zeros((33, 8, tq), I32)
        eq_ref[0, nk + d] = jnp.zeros((8, tq), I32)

    sub = lax.broadcasted_iota(I32, (8, tq), 0)
    group_masks = (-0x10000, -0xFF0100, -0xF0F0F10, -0x33333334, -0x55555556)

    def pos_plane(e, j):
        if e < 6:
            return jnp.broadcast_to(-((~j >> (5 - e)) & 1), (8, tq))
        if e < 11:
            return jnp.full((8, tq), group_masks[e - 6], I32)
        return -((~sub >> (13 - e)) & 1)

    def sweep(prev, cur, carry, src, dst):
        n_gt, flip = carry

        def body(jq, cnt):
            for d in range(RADIX_TILES):
                j = jq * RADIX_TILES + d
                e = eq_ref[src, j] & (prev(j) ^ flip)
                eq_ref[dst, j] = e
                cnt = cnt + lax.population_count(e & cur(j))
            return cnt

        cnt = lax.fori_loop(0, (nk + RADIX_TILES - 1) // RADIX_TILES, body, jnp.zeros((8, tq), I32))
        cnt = n_gt + cnt.sum(axis=0, keepdims=True)
        acc = cnt >= topk
        return acc, (jnp.where(acc, n_gt, cnt), jnp.where(acc, 0, -1))

    def key_sweep(it, carry, src, dst):
        tu, rest = carry
        acc, rest = sweep(lambda j: pln_ref[it, j], lambda j: pln_ref[it + 1, j], rest, src, dst)
        return jnp.where(acc, tu | jnp.left_shift(jnp.int32(1), 31 - it), tu), rest

    def two_bits(i, carry):
        return key_sweep(2 * i + 1, key_sweep(2 * i, carry, 0, 1), 1, 0)

    zero_row = jnp.zeros((1, tq), I32)
    tu, rest = lax.fori_loop(0, 16, two_bits, (zero_row, (zero_row, zero_row)))
    inv_pos = zero_row
    for e in range(14):
        prev = (lambda j: pln_ref[32, j]) if e == 0 else functools.partial(pos_plane, e - 1)
        acc, rest = sweep(prev, functools.partial(pos_plane, e), rest, e % 2, 1 - e % 2)
        inv_pos = jnp.where(acc, inv_pos | (1 << (13 - e)), inv_pos)
    return tu ^ INT_MIN, ~inv_pos & 0x3FFF


def _dsa_kernel(qT_ref, qihT_ref, qilT_ref, wiT_ref, k_ref, vT_ref, ki3_ref, bias_ref, cap_ref,
                o_ref, key_ref, l_ref, acc_ref, lg_ref, pln_ref, eq_ref, sc_ref, ma_ref,
                *, tq, tk, nk_static, causal, topk):
    nk = (pl.program_id(1) + 1) if causal else nk_static
    wiT = wiT_ref[0]

    def when(cond):
        if isinstance(cond, bool):
            return (lambda f: f()) if cond else (lambda f: None)
        return pl.when(cond)

    zeros64 = jnp.zeros((64, tq), BF16)
    q3 = []
    for h in range(H_IDX):
        hi = qihT_ref[0, h * 64:(h + 1) * 64, :]
        lo = qilT_ref[0, h * 64:(h + 1) * 64, :]
        q3.append(jnp.concatenate([hi, lo, hi, zeros64], axis=0))

    def score_matmuls(j):
        kt = ki3_ref[0, j]
        acc = None
        for h in range(H_IDX):
            t = jnp.maximum(_dot(kt, q3[h]), 0.0) * wiT[h:h + 1, :]
            acc = t if acc is None else acc + t
        sc_ref[...] = jnp.minimum(acc, cap_ref[jnp.where(j == nk - 1, 1, 0)])

    def score_finish(j):
        _store_keys(sc_ref[...], j, key_ref, pln_ref, eq_ref)

    def score_body(j, c):
        score_finish(j - 1)
        score_matmuls(j)
        return c

    score_matmuls(0)
    lax.fori_loop(1, nk, score_body, 0)
    score_finish(nk - 1)

    thr, pos_thr = _radix_threshold(pln_ref, eq_ref, nk, tq, topk)

    l_ref[...] = jnp.zeros(l_ref.shape, F32)
    acc_ref[...] = jnp.zeros(acc_ref.shape, F32)
    qm = []
    for h in range(HA):
        qh = qT_ref[0, h * 64:(h + 1) * 64, :]
        qm.append(jnp.concatenate([qh, zeros64] if h % 2 == 0 else [zeros64, qh], axis=0))
    row = lax.broadcasted_iota(I32, (tk, tq), 0)

    def tile_logits(j, slot):
        sel = key_ref[j] > jnp.where(row <= pos_thr - j * tk, thr - 1, thr)
        kind = jnp.clip(j - (nk - 3), 0, 2)
        mx = []
        for h in range(HA):
            pr = h // 2
            lg = _dot(k_ref[0, j, :, pr * 128:(pr + 1) * 128], qm[h]) + bias_ref[kind, h]
            lg = jnp.where(sel, lg, NEG).astype(BF16)
            lg_ref[slot, h] = lg
            part = jnp.max(lg.reshape(tk // 16, 16, tq), axis=0).astype(F32)
            mx.append(jnp.max(part, axis=0, keepdims=True))
        return jnp.concatenate(mx, axis=0)

    def tile_values(j, slot, mx):
        m_old = ma_ref[...]
        m_new = jnp.maximum(m_old, mx)
        alpha = jnp.exp2(m_old - m_new)
        ma_ref[...] = m_new
        ls = []
        for h in range(HA):
            p = jnp.exp2(lg_ref[slot, h] - m_new[h:h + 1, :].astype(BF16))
            pv = _dot(vT_ref[0, j, h * V_ROWS:(h + 1) * V_ROWS, :], p)
            acc_ref[h * 64:(h + 1) * 64, :] = alpha[h:h + 1, :] * acc_ref[h * 64:(h + 1) * 64, :] + pv[0:DH]
            ls.append(pv[DH:DH + 1])
        l_ref[...] = alpha * l_ref[...] + jnp.concatenate(ls, axis=0)

    ma_ref[...] = jnp.full((HA, tq), NEG, F32)

    def pair(i, c):
        mx0 = tile_logits(2 * i, 0)
        mx1 = tile_logits(2 * i + 1, 1)
        tile_values(2 * i, 0, mx0)
        tile_values(2 * i + 1, 1, mx1)
        return c

    lax.fori_loop(0, nk // 2, pair, 0)

    @when(nk % 2 == 1)
    def _():
        tile_values(nk - 1, 0, tile_logits(nk - 1, 0))

    for h in range(HA):
        o_ref[0, h * 64:(h + 1) * 64, :] = (acc_ref[h * 64:(h + 1) * 64, :] / l_ref[h:h + 1, :]).astype(BF16)


def _t5_bucket_np(rel):
    half = NUM_BUCKETS // 2
    max_exact = half // 2
    out = np.zeros(rel.shape, np.int64)
    flat_rel = rel.reshape(-1)
    flat = out.reshape(-1)
    for a in range(flat_rel.size):
        r = int(flat_rel[a])
        n = abs(r)
        b = n if n < max_exact else min(half - 1, (n * n).bit_length() + 1)
        flat[a] = b + (half if r > 0 else 0)
    return out


def _bias_tables(rel_bias, adm, tk, tq, tq_valid):
    kj = np.arange(tk)[:, None]
    t = np.minimum(np.arange(tq), tq_valid - 1)[None, :]
    rel_to_bucket = _t5_bucket_np(np.arange(-2 * tk - tq, tk + 1))
    lut = lambda rel: rel_to_bucket[rel + 2 * tk + tq]
    far_bucket = NUM_BUCKETS // 2 - 1
    idx = np.stack([np.full((tk, tq), far_bucket), lut(kj - tk - t), lut(kj - t)], axis=0)
    tab = rel_bias.astype(F32) * LOG2E
    onehot = jnp.asarray(idx[..., None] == np.arange(NUM_BUCKETS), F32)
    bias = jnp.einsum("ktqb,bh->khtq", onehot, tab - tab[far_bucket], precision=lax.Precision.HIGHEST)
    mask = np.zeros((3, 1, tk, tq), np.float32)
    mask[2, 0] = np.where(adm > 0.5, 0.0, NEG)
    return bias + jnp.asarray(mask)


def _dsa(qT, qihT, qilT, wiT, k4, vT4, ki34, bias, adm, *, nq, causal, topk):
    g, _, tq = qT.shape
    b, nk, tk, _ = k4.shape
    assert g == b * nq and tk == 256 and nk + RADIX_TILES - 1 <= 64
    cap = jnp.asarray(np.stack([np.full((tk, tq), np.inf, np.float32), np.where(adm > 0.5, np.inf, NEG)]), F32)
    qspec = lambda r: pl.BlockSpec((1, r, tq), lambda bi, i: (bi * nq + i, 0, 0))
    kspec = lambda s: pl.BlockSpec((1,) + s, lambda bi, i: (bi, 0, 0, 0), pipeline_mode=pl.Buffered(1))
    kern = functools.partial(_dsa_kernel, tq=tq, tk=tk, nk_static=nk, causal=causal, topk=topk)
    return pl.pallas_call(
        kern,
        grid=(b, nq),
        in_specs=[qspec(512), qspec(512), qspec(512), qspec(8),
                  kspec((nk, tk, 512)), kspec((nk, HA * V_ROWS, tk)), kspec((nk, tk, 256)),
                  _const_spec((3, HA, tk, tq)), _const_spec((2, tk, tq))],
        out_specs=qspec(512),
        out_shape=jax.ShapeDtypeStruct((g, 512, tq), BF16),
        scratch_shapes=[pltpu.VMEM((nk, tk, tq), I32), pltpu.VMEM((8, tq), F32),
                        pltpu.VMEM((512, tq), F32), pltpu.VMEM((2, HA, tk, tq), BF16),
                        pltpu.VMEM((33, nk + RADIX_TILES - 1, 8, tq), I32),
                        pltpu.VMEM((2, nk + RADIX_TILES - 1, 8, tq), I32),
                        pltpu.VMEM((tk, tq), F32), pltpu.VMEM((HA, tq), F32)],
        compiler_params=_cparams(("arbitrary", "arbitrary")),
        name="dsa_causal" if causal else "dsa_cached",
    )(qT, qihT, qilT, wiT, k4, vT4, ki34, bias, cap)


def _dsa_step_kernel(qa_ref, qih_ref, qil_ref, w_ref, ck_ref, cv_ref, cki_ref, nk_ref, nv_ref, nki_ref,
                     bias_ref, cap_ref, blk_ref, o_ref, key_ref, pln_ref, eq_ref, m_ref, l_ref, acc_ref,
                     *, nkc, t, topk):
    tk, tq = nk_ref.shape[1], qa_ref.shape[2]
    qih, qil = qih_ref[0], qil_ref[0]
    w = w_ref[0, 0:1, :]
    shifts = [t << s for s in range((tq // t).bit_length() - 2, -1, -1)]

    def scores(kx, j, cap):
        kh, kl = _split(kx)
        s = _dot(kh, qih) + (_dot(kl, qih) + _dot(kh, qil))
        sc = jnp.maximum(s, 0.0) * w
        for sh in shifts:
            sc = sc + pltpu.roll(sc, sh, 1)
        if cap is not None:
            sc = jnp.minimum(sc, cap)
        _store_keys(sc, j, key_ref, pln_ref, eq_ref)

    def score_body(j, c):
        scores(cki_ref[0, pl.ds(pl.multiple_of(j * tk, tk), tk), :], j, None)
        return c

    lax.fori_loop(0, nkc, score_body, 0)
    scores(nki_ref[0], nkc, cap_ref[...])
    thr, pos_thr = _radix_threshold(pln_ref, eq_ref, nkc + 1, tq, topk)

    m_ref[...] = jnp.full(m_ref.shape, NEG, F32)
    l_ref[...] = jnp.zeros(l_ref.shape, F32)
    acc_ref[...] = jnp.zeros(acc_ref.shape, F32)
    qa = qa_ref[0]
    row = lax.broadcasted_iota(I32, (tk, tq), 0)
    ones8 = jnp.ones((8, tk), BF16)
    tn = (((0,), (0,)), ((), ()))

    def softmax_step(lg, j, bias):
        sel = key_ref[j] > jnp.where(row <= pos_thr - j * tk, thr - 1, thr)
        if bias is not None:
            lg = lg + bias
        lg = jnp.where(sel, lg, NEG).astype(BF16)
        part = jnp.max(lg.reshape(tk // 16, 16, tq), axis=0).astype(F32)
        m_old = m_ref[0:1, :]
        m_new = jnp.maximum(m_old, jnp.max(part, axis=0, keepdims=True))
        alpha = jnp.exp2(m_old - m_new)
        p = jnp.exp2(lg - m_new.astype(BF16))
        l_ref[0:1, :] = alpha * l_ref[0:1, :] + _dot(ones8, p)[0:1]
        m_ref[0:1, :] = m_new
        return p, alpha

    def cached(j, bias):
        head = lambda ref, h: ref[0, pl.ds(j * (tk * HA) + h, tk, stride=HA), :].astype(BF16)
        lg = _dot(head(ck_ref, 0), qa[0:DH, :])
        for h in range(1, HA):
            lg = lg + _dot(head(ck_ref, h), qa[h * DH:(h + 1) * DH, :])
        p, alpha = softmax_step(lg, j, bias)
        for h in range(HA):
            pv = lax.dot_general(head(cv_ref, h), p, tn, preferred_element_type=F32)
            acc_ref[h * DH:(h + 1) * DH, :] = alpha * acc_ref[h * DH:(h + 1) * DH, :] + pv

    def far(j, c):
        cached(j, None)
        return c

    lax.fori_loop(0, nkc - 1, far, 0)
    cached(nkc - 1, bias_ref[0])
    p, alpha = softmax_step(_dot(nk_ref[0], qa), nkc, bias_ref[1])
    acc_ref[...] = alpha * acc_ref[...] + lax.dot_general(nv_ref[0], p, tn, preferred_element_type=F32)
    o = acc_ref[...] / l_ref[0:1, :] * blk_ref[...]
    for sh in shifts:
        o = o + pltpu.roll(o, sh, 1)
    o_ref[0] = o.astype(BF16)


def _dsa_step(qT, qihT, qilT, wiT, kbf, va, ki_new, past_k, past_v, past_kidx, rel_bias, *, b, t, topk):
    past = past_k.shape[1]
    tk, tq = 256, HA * t
    assert tq == 128 and past % tk == 0 and t <= tk
    nkc = past // tk
    nk = nkc + 1
    per_b = lambda a: jnp.swapaxes(a[0].reshape(a.shape[1], b, t), 0, 1)
    blk = (np.arange(WA)[:, None] // DH == np.arange(tq)[None, :] // t).astype(np.float32)
    qa = (jnp.tile(per_b(qT), (1, 1, HA)) * jnp.asarray(blk, BF16))
    lanes = lambda a: jnp.swapaxes(per_b(a).reshape(b, HA, D_IDX, t), 1, 2).reshape(b, D_IDX, tq)
    w = jnp.broadcast_to(per_b(wiT).reshape(b, 1, tq), (b, 8, tq))
    pad_rows = lambda a: jnp.pad(a.reshape(b, t, a.shape[-1]), ((0, 0), (0, tk - t), (0, 0)))
    adm = np.broadcast_to(np.arange(tk)[:, None] < t, (tk, tq)).astype(np.float32)
    bias = _bias_tables(rel_bias, adm[:, :t], tk, t, t)[1:]
    bias = jnp.swapaxes(bias, 1, 2).reshape(2, tk, tq)
    cap = jnp.asarray(np.where(adm > 0.5, np.inf, NEG), F32)
    bspec = lambda s: pl.BlockSpec((1,) + s, lambda bi: (bi,) + (0,) * len(s))
    return pl.pallas_call(
        functools.partial(_dsa_step_kernel, nkc=nkc, t=t, topk=topk),
        grid=(b,),
        in_specs=[bspec((WA, tq)), bspec((D_IDX, tq)), bspec((D_IDX, tq)), bspec((8, tq)),
                  bspec((past * HA, DH)), bspec((past * HA, DH)), bspec((past, D_IDX)),
                  bspec((tk, WA)), bspec((tk, WA)), bspec((tk, D_IDX)),
                  _const_spec((2, tk, tq)), _const_spec((tk, tq)), _const_spec((WA, tq))],
        out_specs=bspec((WA, tq)),
        out_shape=jax.ShapeDtypeStruct((b, WA, tq), BF16),
        scratch_shapes=[pltpu.VMEM((nk, tk, tq), I32), pltpu.VMEM((33, nk + RADIX_TILES - 1, 8, tq), I32),
                        pltpu.VMEM((2, nk + RADIX_TILES - 1, 8, tq), I32), pltpu.VMEM((8, tq), F32),
                        pltpu.VMEM((8, tq), F32), pltpu.VMEM((WA, tq), F32)],
        compiler_params=_cparams(("arbitrary",)),
        name="dsa_step",
    )(qa, lanes(qihT), lanes(qilT), w, past_k.reshape(b, past * HA, DH), past_v.reshape(b, past * HA, DH), past_kidx,
      pad_rows(kbf), pad_rows(va.astype(BF16)), pad_rows(ki_new), bias, cap, jnp.asarray(blk))


def _gdn_pre_kernel(conv_ref, prev_ref, hist_ref, small_ref, cw_ref, alog_ref, dtb_ref,
                    wm_ref, um_ref, qe_ref, oi_ref, e_ref, ext_ref, *, rows, nvalid):
    c = CHUNK
    n = HB * c
    ext_ref[0:8, :] = jnp.where(pl.program_id(1) == 0, hist_ref[0], prev_ref[0])
    ext_ref[8:8 + rows, :] = conv_ref[0]
    cb = ext_ref[5:5 + rows, :] * cw_ref[0:1, :]
    for j in range(1, CONV_B):
        cb = cb + ext_ref[5 + j:5 + j + rows, :] * cw_ref[j:j + 1, :]
    cb = _silu(cb)

    small = small_ref[0]
    pos = lax.broadcasted_iota(I32, (rows, 1), 0) % c
    rowv = (pos < nvalid).astype(F32)
    beta_all = _sigmoid(small) * rowv
    sp = small + dtb_ref[...]
    g_all = -jnp.exp(alog_ref[...]) * (jnp.maximum(sp, 0.0) + jnp.log(1.0 + jnp.exp(-jnp.abs(sp)))) * rowv

    ri = lax.broadcasted_iota(I32, (n, n), 0)
    ci = lax.broadcasted_iota(I32, (n, n), 1)
    same = (ri // c) == (ci // c)
    tri_b = (same & (ci <= ri)).astype(BF16)
    bd_f = same.astype(F32)
    bd_b = same.astype(BF16)
    wr = lax.broadcasted_iota(I32, (c, n), 0)
    wl = lax.broadcasted_iota(I32, (c, n), 1)
    grp = wl // c
    tri_w = (wl % c) <= wr
    strict_w = (wl % c) < wr
    eye_w = ((wl % c) == wr).astype(F32)
    nt = (((1,), (1,)), ((), ()))
    tn = (((0,), (0,)), ((), ()))

    def l2n(x):
        return x * lax.rsqrt(jnp.sum(x * x, axis=-1, keepdims=True) + EPS)

    def to_wide(full):
        out = jnp.where(grp == 0, full[0:c, :], 0.0)
        for h in range(1, HB):
            out = out + jnp.where(grp == h, full[h * c:(h + 1) * c, :], 0.0)
        return out

    def tile4(x):
        return jnp.concatenate([x] * HB, axis=0)

    def chunk(k0):
        def stack(fn):
            return jnp.concatenate([fn(h) for h in range(HB)], axis=0)

        rv = rowv[k0:k0 + c]
        q = stack(lambda h: l2n(cb[k0:k0 + c, h * DK:(h + 1) * DK]) * (DK ** -0.5))
        k = stack(lambda h: l2n(cb[k0:k0 + c, WB + h * DK:WB + (h + 1) * DK]) * rv)
        v = stack(lambda h: cb[k0:k0 + c, 2 * WB + h * DV:2 * WB + (h + 1) * DV] * rv)
        beta = stack(lambda h: beta_all[k0:k0 + c, _L_BB + h:_L_BB + h + 1])
        g = stack(lambda h: g_all[k0:k0 + c, _L_AB + h:_L_AB + h + 1])
        yield

        gh, gl = _split(jnp.broadcast_to(g, (n, 128)))
        gcum = _dot(tri_b, gh) + _dot(tri_b, gl)
        yield
        gcum_row = gcum.T[0:1, :]
        gcum_col = gcum[:, 0:1]
        col_w = to_wide(jnp.concatenate([gcum, gcum], axis=1))
        decay_w = jnp.where(tri_w, jnp.exp(jnp.where(tri_w, col_w - gcum_row, 0.0)), 0.0)
        kb = k * beta
        kbf = k.astype(BF16)
        kk_w = to_wide(lax.dot_general(kb.astype(BF16), kbf, nt, preferred_element_type=F32))
        qk_w = to_wide(lax.dot_general(q.astype(BF16), kbf, nt, preferred_element_type=F32))
        nmat_w = jnp.where(strict_w, kk_w * decay_w, 0.0)
        attn_w = jnp.where(tri_w, qk_w * decay_w, 0.0)
        yield

        inv_w = eye_w - nmat_w
        ph = nmat_w.astype(BF16)
        bh = tile4(ph) * bd_b
        for _ in range(int(np.log2(c)) - 1):
            ph = _dot(ph, bh).astype(BF16)
            yield
            bh = tile4(ph) * bd_b
            inv_w = inv_w + _dot(inv_w.astype(BF16), bh)
            yield
        rhs = jnp.concatenate([v * beta, kb * jnp.exp(gcum_col)], axis=1)
        sol = _dot3(tile4(inv_w) * bd_f, rhs)
        yield
        solb = sol.astype(BF16)
        aw = _dot((tile4(attn_w) * bd_f).astype(BF16), solb)
        yield
        ck = k0 // c
        oi_ref[0, ck] = aw[:, :DV]
        qe_ref[0, ck] = (q * jnp.exp(gcum_col) - aw[:, DV:]).astype(BF16)
        for h in range(HB):
            rs = slice(h * c, (h + 1) * c)
            g_last = gcum[(h + 1) * c - 1:(h + 1) * c, :]
            kd = (k[rs] * jnp.exp(g_last - gcum[rs])).astype(BF16)
            uw = lax.dot_general(kd, solb[rs], tn, preferred_element_type=F32)
            um_ref[0, ck, h] = uw[:, :DV]
            wm_ref[0, ck, h] = uw[:, DV:].astype(BF16)
            e_ref[0, ck, h:h + 1, :] = jnp.exp(g_last)
        e_ref[0, ck, HB:8, :] = jnp.zeros((8 - HB, 128), F32)

    gens = [chunk(k0) for k0 in range(0, rows, c)]
    while gens:
        alive = []
        for gen in gens:
            try:
                next(gen)
                alive.append(gen)
            except StopIteration:
                pass
        gens = alive


def _gdn_scan_kernel(wm_ref, um_ref, qe_ref, oi_ref, e_ref, gb_ref, s0_ref, ng_ref, ob_ref, sfin_ref, s_ref,
                     *, bb, g):
    c = CHUNK

    @pl.when(pl.program_id(1) == 0)
    def _():
        s_ref[...] = s0_ref[...]

    ng = ng_ref[...]
    for b in range(bb):
        for ck in range(g):
            for h in range(HB):
                s = s_ref[b, h]
                sb = s.astype(BF16)
                rs = slice(h * c, (h + 1) * c)
                o = _dot(qe_ref[b, ck, rs, :], sb) + oi_ref[b, ck, rs, :]
                s_ref[b, h] = e_ref[b, ck, h:h + 1, :] * s + (um_ref[b, ck, h] - _dot(wm_ref[b, ck, h], sb))
                gate = gb_ref[b, ck * c:(ck + 1) * c, h * DV:(h + 1) * DV]
                ob_ref[b, ck * c:(ck + 1) * c, h * DV:(h + 1) * DV] = (_rms(o, ng) * _silu(gate)).astype(BF16)
    sfin_ref[...] = s_ref[...]


def _gdn(conv_in, hist, gb, small, s0, conv_w, a_log, dt_bias, norm_gdn, nvalid):
    b, t, _ = conv_in.shape
    assert t % CHUNK == 0
    rows = 256 if t % 256 == 0 else CHUNK
    nc, cps = t // CHUNK, rows // CHUNK
    n = HB * CHUNK
    alog = jnp.zeros((1, 128), F32).at[0, _L_AB:_L_AB + HB].set(a_log)
    dtb = jnp.zeros((1, 128), F32).at[0, _L_AB:_L_AB + HB].set(dt_bias)
    row = lambda c_: pl.BlockSpec((1, rows, c_), lambda bi, ti: (bi, ti, 0))
    prev = pl.BlockSpec((1, 8, C_CONV_B), lambda bi, ti: (bi, jnp.maximum(ti * (rows // 8) - 1, 0), 0))
    per_b = lambda s: pl.BlockSpec((1,) + s, lambda bi, ti: (bi,) + (0,) * len(s))
    chunked = lambda s: pl.BlockSpec((1, cps) + s, lambda bi, ti: (bi, ti) + (0,) * len(s))
    wm, um, qe, oi, e = pl.pallas_call(
        functools.partial(_gdn_pre_kernel, rows=rows, nvalid=nvalid),
        grid=(b, t // rows),
        in_specs=[row(C_CONV_B), prev, per_b((8, C_CONV_B)), row(128),
                  _const_spec((CONV_B, C_CONV_B)), _const_spec((1, 128)), _const_spec((1, 128))],
        out_specs=[chunked((HB, DK, DV)), chunked((HB, DK, DV)), chunked((n, DK)), chunked((n, DV)),
                   chunked((8, 128))],
        out_shape=[jax.ShapeDtypeStruct((b, nc, HB, DK, DV), BF16), jax.ShapeDtypeStruct((b, nc, HB, DK, DV), F32),
                   jax.ShapeDtypeStruct((b, nc, n, DK), BF16), jax.ShapeDtypeStruct((b, nc, n, DV), F32),
                   jax.ShapeDtypeStruct((b, nc, 8, 128), F32)],
        scratch_shapes=[pltpu.VMEM((8 + rows, C_CONV_B), F32)],
        compiler_params=_cparams(("arbitrary", "arbitrary")),
        name="gdn_pre",
    )(conv_in, conv_in, hist, small, conv_w, alog, dtb)

    bb = 2 if b % 2 == 0 else 1
    g = 2 if nc % 2 == 0 else 1
    blk = lambda s: pl.BlockSpec((bb, g) + s, lambda bi, ci: (bi, ci) + (0,) * len(s))
    rowb = pl.BlockSpec((bb, g * CHUNK, WB), lambda bi, ci: (bi, ci, 0))
    state = pl.BlockSpec((bb, HB, DK, DV), lambda bi, ci: (bi, 0, 0, 0))
    return pl.pallas_call(
        functools.partial(_gdn_scan_kernel, bb=bb, g=g),
        grid=(b // bb, nc // g),
        in_specs=[blk((HB, DK, DV)), blk((HB, DK, DV)), blk((n, DK)), blk((n, DV)), blk((8, 128)), rowb, state,
                  _const_spec((1, DV))],
        out_specs=[rowb, state],
        out_shape=[jax.ShapeDtypeStruct((b, t, WB), BF16), jax.ShapeDtypeStruct((b, HB, DK, DV), F32)],
        scratch_shapes=[pltpu.VMEM((bb, HB, DK, DV), F32)],
        compiler_params=_cparams(("arbitrary", "arbitrary")),
        name="gdn_scan",
    )(wm, um, qe, oi, e, gb, s0, norm_gdn.reshape(1, DV))


def _post_kernel(x_ref, oaT_ref, ob_ref, ga_ref, gbr_ref, p_ref, hist_ref, wa_ref, wb_ref, wo_ref, nf_ref,
                 wup_ref, cw_ref, wdn_ref, npl_ref, wpg_ref, wple_ref, nfin_ref, y_ref, tail_ref, ext_ref, *, tm):
    @pl.when(pl.program_id(1) == 0)
    def _():
        ext_ref[0:8, :] = hist_ref[0]

    ya = lax.dot_general(oaT_ref[0], wa_ref[...], (((0,), (0,)), ((), ())), preferred_element_type=F32)
    yb = _dot(ob_ref[0], wb_ref[...])
    mix = _sigmoid(ga_ref[0]) * ya + _sigmoid(gbr_ref[0]) * yb
    x1 = x_ref[0] + _dot(mix.astype(BF16), wo_ref[...])
    h2 = _rms(x1, nf_ref[...]).astype(BF16)
    ext_ref[8:8 + tm, :] = _dot(h2, wup_ref[:, 0:D_FF])
    u_val = _dot(h2, wup_ref[:, D_FF:2 * D_FF])
    cv = ext_ref[6:6 + tm, :] * cw_ref[0:1, :]
    for j in range(1, CONV_F):
        cv = cv + ext_ref[6 + j:6 + j + tm, :] * cw_ref[j:j + 1, :]
    tail = ext_ref[tm:tm + 8, :]
    ext_ref[0:8, :] = tail
    tail_ref[0] = tail
    act = 0.5 * cv * (1.0 + jnp.tanh(0.7978845608028654 * (cv + 0.044715 * (cv * cv * cv))))
    x2 = x1 + _dot((act * u_val).astype(BF16), wdn_ref[...])
    gate = _sigmoid(_dot(_rms(x2, npl_ref[...]).astype(BF16), wpg_ref[...]))
    x3 = x2 + gate * _dot(p_ref[0].astype(BF16), wple_ref[...])
    y_ref[0] = _rms(x3, nfin_ref[...])


def _post(x, oaT, ob, ga, gbr, p, hist, w_proj_a, w_proj_b, w_out, norm_ffn, w_up, conv_ffn, w_down, norm_ple,
          w_ple_gate, w_ple, norm_final, tm):
    b, t, _ = x.shape
    nt = t // tm
    assert t % tm == 0 and tm >= 8 and oaT.shape == (b * nt, WA, tm)
    row = lambda c: pl.BlockSpec((1, tm, c), lambda bi, ti: (bi, ti, 0))
    per_b = pl.BlockSpec((1, 8, D_FF), lambda bi, ti: (bi, 0, 0))
    vec = _const_spec((1, D_MODEL))
    return pl.pallas_call(
        functools.partial(_post_kernel, tm=tm),
        grid=(b, nt),
        in_specs=[row(D_MODEL), pl.BlockSpec((1, WA, tm), lambda bi, ti: (bi * nt + ti, 0, 0)), row(WB),
                  row(D_MODEL), row(D_MODEL), row(D_PLE), per_b,
                  _const_spec((WA, D_MODEL)), _const_spec((WB, D_MODEL)), _const_spec((D_MODEL, D_MODEL)), vec,
                  _const_spec((D_MODEL, 2 * D_FF)), _const_spec((CONV_F, D_FF)), _const_spec((D_FF, D_MODEL)),
                  vec, _const_spec((D_MODEL, D_MODEL)), _const_spec((D_PLE, D_MODEL)), vec],
        out_specs=[row(D_MODEL), per_b],
        out_shape=[jax.ShapeDtypeStruct((b, t, D_MODEL), F32), jax.ShapeDtypeStruct((b, 8, D_FF), F32)],
        scratch_shapes=[pltpu.VMEM((8 + tm, D_FF), F32)],
        compiler_params=_cparams(("arbitrary", "arbitrary")),
        name="post",
    )(x, oaT, ob, ga, gbr, p, hist, w_proj_a.astype(BF16), w_proj_b.astype(BF16), w_out.astype(BF16),
      norm_ffn.reshape(1, D_MODEL), w_up.astype(BF16), conv_ffn, w_down.astype(BF16),
      norm_ple.reshape(1, D_MODEL), w_ple_gate.astype(BF16), w_ple.astype(BF16), norm_final.reshape(1, D_MODEL))


def _pad_hist(hist, rows=8):
    b, r, c = hist.shape
    return jnp.concatenate([jnp.zeros((b, rows - r, c), hist.dtype), hist], axis=1)


def _layer(x, p, past_k, past_v, past_kidx, s_gdn, conv_b_hist, ffn_hist, wts, *, tm, tq):
    (norm_mix, w_in, conv_b, a_log, dt_bias, norm_gdn, w_proj_a, w_proj_b, w_out, norm_ffn, w_up, conv_ffn,
     w_down, norm_ple, w_ple, w_ple_gate, rel_bias, norm_final) = wts
    b, t, _ = x.shape
    n = b * t
    past = past_k.shape[1]
    topk = min(TOPK_MAX, (past + t) // 4)
    x2d = x.reshape(n, D_MODEL)
    tmi = min(tm, n)
    (ka, va, kbf, conv_in, gb, ga, gbr, small, ki3, qT, qihT, qilT, wiT, vTa) = _in_proj(x2d, norm_mix, w_in, tmi)

    if past == 0:
        assert tq == tmi and t % tq == 0 and tq % CHUNK == 0 and tq >= topk
        nq = t // tq
        kj = np.arange(tq)[:, None]
        adm = ((kj // CHUNK) <= (np.arange(tq)[None, :] // CHUNK)).astype(np.float32)
        oT = _dsa(qT, qihT, qilT, wiT, kbf.reshape(b, nq, tq, WA), vTa.reshape(b, nq, HA * V_ROWS, tq),
                  ki3.reshape(b, nq, tq, 256), _bias_tables(rel_bias, adm, tq, tq, tq), adm,
                  nq=nq, causal=True, topk=topk)
    else:
        assert n == tmi
        oT = _dsa_step(qT, qihT, qilT, wiT, kbf, va, small[:, :D_IDX], past_k, past_v, past_kidx, rel_bias,
                       b=b, t=t, topk=topk)
        oT = oT[:, :, :t]

    tp = -(-t // CHUNK) * CHUNK
    padt = lambda a: jnp.pad(a.reshape(b, t, a.shape[-1]), ((0, 0), (0, tp - t), (0, 0)))
    ob, s_new = _gdn(padt(conv_in), _pad_hist(conv_b_hist), padt(gb), padt(small), s_gdn, conv_b, a_log, dt_bias,
                     norm_gdn, nvalid=min(t, CHUNK))
    new_conv_b = jnp.concatenate([conv_b_hist, conv_in.reshape(b, t, C_CONV_B)], axis=1)[:, t:]

    per_bt = lambda a: a.reshape(b, t, a.shape[-1])
    y, tail = _post(x, oT, ob[:, :t], per_bt(ga), per_bt(gbr), p, _pad_hist(ffn_hist), w_proj_a, w_proj_b, w_out,
                    norm_ffn, w_up, conv_ffn, w_down, norm_ple, w_ple_gate, w_ple, norm_final, min(tm, t))
    new_ffn = tail[:, 8 - (CONV_F - 1):]
    return (y, ka.reshape(b, t, HA, DH), va.reshape(b, t, HA, DH), small[:, :D_IDX].reshape(b, t, D_IDX),
            s_new, new_conv_b, new_ffn)


def kernel(x_prompt, x_sample, p_prompt, p_sample, cache_k, cache_v, cache_kidx, state_gdn, state_gdn_conv,
           state_ffn_conv, norm_mix, w_in, conv_b, a_log, dt_bias, norm_gdn, w_proj_a, w_proj_b, w_out, norm_ffn,
           w_up, conv_ffn, w_down, norm_ple, w_ple, w_ple_gate, rel_bias, norm_final):
    assert norm_mix.shape[0] == 1
    bp = x_prompt.shape[0]
    dt = x_prompt.dtype
    wts = (norm_mix[0], w_in[0], conv_b[0], a_log[0], dt_bias[0], norm_gdn[0], w_proj_a[0], w_proj_b[0], w_out[0],
           norm_ffn[0], w_up[0], conv_ffn[0], w_down[0], norm_ple[0], w_ple[0], w_ple_gate[0], rel_bias, norm_final)
    outs_p = _layer(x_prompt, p_prompt[0], jnp.zeros((bp, 0, HA, DH), dt), jnp.zeros((bp, 0, HA, DH), dt),
                    jnp.zeros((bp, 0, D_IDX), dt), jnp.zeros((bp, HB, DK, DV), dt),
                    jnp.zeros((bp, CONV_B - 1, C_CONV_B), dt), jnp.zeros((bp, CONV_F - 1, D_FF), dt),
                    wts, tm=256, tq=256)
    outs_s = _layer(x_sample, p_sample[0], cache_k[0], cache_v[0], cache_kidx[0], state_gdn[0],
                    state_gdn_conv[0], state_ffn_conv[0], wts, tm=256, tq=256)
    yp, ys = outs_p[0], outs_s[0]
    return (yp, ys) + tuple(a[None] for a in outs_p[1:]) + tuple(a[None] for a in outs_s[1:])
```

```python
import functools

import numpy as np
import jax
import jax.numpy as jnp
from jax import lax
from jax.experimental import pallas as pl
from jax.experimental.pallas import tpu as pltpu

F32 = jnp.float32
BF16 = jnp.bfloat16
I32 = jnp.int32

D_MODEL = 1024
CHUNK = 64
HA, DH = 8, 64
H_IDX, D_IDX = 8, 64
TOPK_MAX = 256
NUM_BUCKETS, MAX_DISTANCE = 32, 128
HB, DK, DV = 4, 128, 128
CONV_B = 4
D_FF = 2816
CONV_F = 3
D_PLE = 256
EPS = 1e-6
NEG = -1e30
WA = HA * DH
WB = HB * DK
C_CONV_B = 3 * WB
INT_MIN = -2 ** 31
LOG2E = 1.4426950408889634

_O_QA, _O_KA, _O_VA, _O_QI, _O_KI, _O_WI = 0, 512, 1024, 1536, 2048, 2112
_O_QB, _O_GB, _O_BB, _O_AB, _O_GA, _O_GBR = 2120, 3656, 4168, 4172, 4176, 5200
_L_WI, _L_BB, _L_AB = 64, 72, 76

VMEM_LIMIT = 56 * 1024 * 1024


def _cparams(sem):
    return pltpu.CompilerParams(dimension_semantics=sem, vmem_limit_bytes=VMEM_LIMIT)


def _const_spec(shape):
    nd = len(shape)
    return pl.BlockSpec(shape, lambda *_: (0,) * nd, pipeline_mode=pl.Buffered(1))


def _rms(x, g):
    return x * lax.rsqrt(jnp.mean(x * x, axis=-1, keepdims=True) + EPS) * g


def _split(x):
    hi = x.astype(BF16)
    lo = (x - hi.astype(F32)).astype(BF16)
    return hi, lo


def _dot(a, b):
    return jnp.dot(a, b, preferred_element_type=F32)


def _dot3(a, b):
    ah, al = _split(a)
    bh, bl = _split(b)
    return _dot(ah, bh) + (_dot(al, bh) + _dot(ah, bl))


def _sigmoid(x):
    return 1.0 / (1.0 + jnp.exp(-x))


def _silu(x):
    return x * _sigmoid(x)


def _transpose32(a):
    a = list(a)
    j, m = 16, 0x0000FFFF
    while j:
        k = 0
        while k < 32:
            t = (a[k] ^ lax.shift_right_logical(a[k + j], jnp.int32(j))) & jnp.int32(m - (1 << 32) if m >> 31 else m)
            a[k] = a[k] ^ t
            a[k + j] = a[k + j] ^ (t << j)
            k = (k + j + 1) & ~j
        j >>= 1
        m = (m ^ (m << j)) & 0xFFFFFFFF
    return a


V_ROWS = DH + 16
RADIX_TILES = 8
_NT = (((1,), (1,)), ((), ()))


def _in_proj_kernel(x_ref, g_ref, wm_ref, wt_ref, wqh_ref, wql_ref, wsh_ref, wsl_ref, wwh_ref, wwl_ref,
                    ph_ref, plo_ref, sc_ref,
                    ka_ref, va_ref, kbf_ref, conv_ref, gb_ref, ga_ref, gbr_ref, small_ref, ki3_ref,
                    qT_ref, qihT_ref, qilT_ref, wiT_ref, vTa_ref):
    tm = x_ref.shape[0]
    h = _rms(x_ref[...], g_ref[...])
    hh, hl = _split(h)

    def main(lo, hi):
        return _dot(hh, wm_ref[:, lo:hi])

    def nt(w, a):
        return lax.dot_general(w, a, _NT, preferred_element_type=F32)

    def nt3(wh_ref, wl_ref):
        return nt(wh_ref[...], hh) + (nt(wh_ref[...], hl) + nt(wl_ref[...], hh))

    ka = main(0, 512)
    va = main(512, 1024)
    for hd in range(HA):
        ka_ref[:, hd, :] = ka[:, hd * DH:(hd + 1) * DH]
        va_ref[:, hd, :] = va[:, hd * DH:(hd + 1) * DH]
    kbf_ref[...] = ka.astype(BF16)
    conv_ref[...] = main(1024, 2560)
    gb_ref[...] = main(2560, 3072)
    ga_ref[...] = main(3072, 4096)
    gbr_ref[...] = main(4096, 5120)

    qT_ref[0] = nt(wt_ref[0:WA, :], hh).astype(BF16)
    vT = nt(wt_ref[WA:2 * WA, :], hh).astype(BF16)
    ones = jnp.ones((V_ROWS - DH, tm), BF16)
    for hd in range(HA):
        vTa_ref[0, hd * V_ROWS:hd * V_ROWS + DH, :] = vT[hd * DH:(hd + 1) * DH]
        vTa_ref[0, hd * V_ROWS + DH:(hd + 1) * V_ROWS, :] = ones
    qh, ql = _split(nt3(wqh_ref, wql_ref))
    qihT_ref[0] = qh
    qilT_ref[0] = ql
    wiT_ref[0] = nt3(wwh_ref, wwl_ref)[0:H_IDX] * (H_IDX ** -0.5)

    small = (_dot(hh, wsh_ref[...]) + (_dot(hl, wsh_ref[...]) + _dot(hh, wsl_ref[...]))) * sc_ref[...]
    small_ref[...] = small
    sh, sl = _split(small)
    ki3_ref[...] = (_dot(sh, ph_ref[...]) + _dot(sl, plo_ref[...])).astype(BF16)


def _in_proj(x2d, norm_mix, w_in, tm):
    n = x2d.shape[0]
    assert n % tm == 0
    g = n // tm
    w = w_in
    wm = jnp.concatenate([w[:, _O_KA:_O_QI], w[:, _O_QB:_O_BB], w[:, _O_GA:]], axis=1).astype(BF16)
    wt = jnp.concatenate([w[:, _O_QA:_O_KA] * (DH ** -0.5 * LOG2E), w[:, _O_VA:_O_QI]], axis=1).T.astype(BF16)
    hilo = lambda a: (a.astype(BF16), (a - a.astype(BF16).astype(F32)).astype(BF16))
    wqh, wql = hilo((w[:, _O_QI:_O_KI] * (D_IDX ** -0.5)).T)
    wsh, wsl = hilo(jnp.concatenate([w[:, _O_KI:_O_QB], w[:, _O_BB:_O_GA], jnp.zeros((D_MODEL, 48), F32)], axis=1))
    wwh, wwl = hilo(jnp.concatenate([w[:, _O_WI:_O_QB], jnp.zeros((D_MODEL, 8), F32)], axis=1).T)
    ph = np.zeros((128, 256), np.float32)
    plo = np.zeros((128, 256), np.float32)
    for c in range(64):
        ph[c, c] = 1.0
        ph[c, 64 + c] = 1.0
        plo[c, 128 + c] = 1.0
    sc = np.ones((1, 128), np.float32)
    sc[0, _L_WI:_L_WI + H_IDX] = H_IDX ** -0.5
    row = lambda c: pl.BlockSpec((tm, c), lambda i: (i, 0))
    colT = lambda r: pl.BlockSpec((1, r, tm), lambda i: (i, 0, 0))
    heads = pl.BlockSpec((tm, HA, DH), lambda i: (i, 0, 0))
    out_cols = [(512, BF16), (1536, F32), (512, F32), (1024, F32), (1024, F32), (128, F32), (256, BF16)]
    out_rows = [(WA, BF16), (WA, BF16), (WA, BF16), (H_IDX, F32), (HA * V_ROWS, BF16)]
    return pl.pallas_call(
        _in_proj_kernel,
        grid=(g,),
        in_specs=[row(D_MODEL), _const_spec((1, D_MODEL)), _const_spec((D_MODEL, 5120)),
                  _const_spec((2 * WA, D_MODEL)), _const_spec((WA, D_MODEL)), _const_spec((WA, D_MODEL)),
                  _const_spec((D_MODEL, 128)), _const_spec((D_MODEL, 128)),
                  _const_spec((16, D_MODEL)), _const_spec((16, D_MODEL)),
                  _const_spec((128, 256)), _const_spec((128, 256)), _const_spec((1, 128))],
        out_specs=[heads, heads] + [row(c) for c, _ in out_cols] + [colT(r) for r, _ in out_rows],
        out_shape=[jax.ShapeDtypeStruct((n, HA, DH), F32)] * 2
        + [jax.ShapeDtypeStruct((n, c), d) for c, d in out_cols]
        + [jax.ShapeDtypeStruct((g, r, tm), d) for r, d in out_rows],
        compiler_params=_cparams(("arbitrary",)),
        name="in_proj",
    )(x2d, norm_mix.reshape(1, D_MODEL), wm, wt, wqh, wql, wsh, wsl, wwh, wwl, jnp.asarray(ph, BF16),
      jnp.asarray(plo, BF16), jnp.asarray(sc))


def _store_keys(score, j, key_ref, pln_ref, eq_ref):
    tk, tq = score.shape
    bits = pltpu.bitcast(score, I32)
    bits = jnp.where(bits == INT_MIN, 0, bits)
    key = bits ^ ((bits >> 31) & 0x7FFFFFFF)
    key_ref[j] = key
    u3 = (key ^ INT_MIN).reshape(tk // 8, 8, tq)
    ones = jnp.full((8, tq), -1, I32)
    pln_ref[0, j] = ones
    for l0 in range(0, tq, 128):
        planes = _transpose32([u3[r][:, l0:l0 + 128] for r in range(32)])
        for b in range(32):
            pln_ref[b + 1, j, :, l0:l0 + 128] = planes[b]
    eq_ref[0, j] = ones


def _radix_threshold(pln_ref, eq_ref, nk, tq, topk):
    for d in range(RADIX_TILES - 1):
        pln_ref[:, nk + d] = jnp.zeros((33, 8, tq), I32)
        eq_ref[0, nk + d] = jnp.zeros((8, tq), I32)

    sub = lax.broadcasted_iota(I32, (8, tq), 0)
    group_masks = (-0x10000, -0xFF0100, -0xF0F0F10, -0x33333334, -0x55555556)

    def pos_plane(e, j):
        if e < 6:
            return jnp.broadcast_to(-((~j >> (5 - e)) & 1), (8, tq))
        if e < 11:
            return jnp.full((8, tq), group_masks[e - 6], I32)
        return -((~sub >> (13 - e)) & 1)

    def sweep(prev, cur, carry, src, dst):
        n_gt, flip = carry

        def body(jq, cnt):
            for d in range(RADIX_TILES):
                j = jq * RADIX_TILES + d
                e = eq_ref[src, j] & (prev(j) ^ flip)
                eq_ref[dst, j] = e
                cnt = cnt + lax.population_count(e & cur(j))
            return cnt

        cnt = lax.fori_loop(0, (nk + RADIX_TILES - 1) // RADIX_TILES, body, jnp.zeros((8, tq), I32))
        cnt = n_gt + cnt.sum(axis=0, keepdims=True)
        acc = cnt >= topk
        return acc, (jnp.where(acc, n_gt, cnt), jnp.where(acc, 0, -1))

    def key_sweep(it, carry, src, dst):
        tu, rest = carry
        acc, rest = sweep(lambda j: pln_ref[it, j], lambda j: pln_ref[it + 1, j], rest, src, dst)
        return jnp.where(acc, tu | jnp.left_shift(jnp.int32(1), 31 - it), tu), rest

    def two_bits(i, carry):
        return key_sweep(2 * i + 1, key_sweep(2 * i, carry, 0, 1), 1, 0)

    zero_row = jnp.zeros((1, tq), I32)
    tu, rest = lax.fori_loop(0, 16, two_bits, (zero_row, (zero_row, zero_row)))
    inv_pos = zero_row
    for e in range(14):
        prev = (lambda j: pln_ref[32, j]) if e == 0 else functools.partial(pos_plane, e - 1)
        acc, rest = sweep(prev, functools.partial(pos_plane, e), rest, e % 2, 1 - e % 2)
        inv_pos = jnp.where(acc, inv_pos | (1 << (13 - e)), inv_pos)
    return tu ^ INT_MIN, ~inv_pos & 0x3FFF


def _dsa_kernel(qT_ref, qihT_ref, qilT_ref, wiT_ref, k_ref, vT_ref, ki3_ref, bias_ref, cap_ref,
                o_ref, key_ref, l_ref, acc_ref, lg_ref, pln_ref, eq_ref, sc_ref, ma_ref,
                *, tq, tk, nk_static, causal, topk):
    nk = (pl.program_id(1) + 1) if causal else nk_static
    wiT = wiT_ref[0]

    def when(cond):
        if isinstance(cond, bool):
            return (lambda f: f()) if cond else (lambda f: None)
        return pl.when(cond)

    zeros64 = jnp.zeros((64, tq), BF16)
    q3 = []
    for h in range(H_IDX):
        hi = qihT_ref[0, h * 64:(h + 1) * 64, :]
        lo = qilT_ref[0, h * 64:(h + 1) * 64, :]
        q3.append(jnp.concatenate([hi, lo, hi, zeros64], axis=0))

    def score_matmuls(j):
        kt = ki3_ref[0, j]
        acc = None
        for h in range(H_IDX):
            t = jnp.maximum(_dot(kt, q3[h]), 0.0) * wiT[h:h + 1, :]
            acc = t if acc is None else acc + t
        sc_ref[...] = jnp.minimum(acc, cap_ref[jnp.where(j == nk - 1, 1, 0)])

    def score_finish(j):
        _store_keys(sc_ref[...], j, key_ref, pln_ref, eq_ref)

    def score_body(j, c):
        score_finish(j - 1)
        score_matmuls(j)
        return c

    score_matmuls(0)
    lax.fori_loop(1, nk, score_body, 0)
    score_finish(nk - 1)

    thr, pos_thr = _radix_threshold(pln_ref, eq_ref, nk, tq, topk)

    l_ref[...] = jnp.zeros(l_ref.shape, F32)
    acc_ref[...] = jnp.zeros(acc_ref.shape, F32)
    qm = []
    for h in range(HA):
        qh = qT_ref[0, h * 64:(h + 1) * 64, :]
        qm.append(jnp.concatenate([qh, zeros64] if h % 2 == 0 else [zeros64, qh], axis=0))
    row = lax.broadcasted_iota(I32, (tk, tq), 0)

    def tile_logits(j, slot):
        sel = key_ref[j] > jnp.where(row <= pos_thr - j * tk, thr - 1, thr)
        kind = jnp.clip(j - (nk - 3), 0, 2)
        mx = []
        for h in range(HA):
            pr = h // 2
            lg = _dot(k_ref[0, j, :, pr * 128:(pr + 1) * 128], qm[h]) + bias_ref[kind, h]
            lg = jnp.where(sel, lg, NEG).astype(BF16)
            lg_ref[slot, h] = lg
            part = jnp.max(lg.reshape(tk // 16, 16, tq), axis=0).astype(F32)
            mx.append(jnp.max(part, axis=0, keepdims=True))
        return jnp.concatenate(mx, axis=0)

    def tile_values(j, slot, mx):
        m_old = ma_ref[...]
        m_new = jnp.maximum(m_old, mx)
        alpha = jnp.exp2(m_old - m_new)
        ma_ref[...] = m_new
        ls = []
        for h in range(HA):
            p = jnp.exp2(lg_ref[slot, h] - m_new[h:h + 1, :].astype(BF16))
            pv = _dot(vT_ref[0, j, h * V_ROWS:(h + 1) * V_ROWS, :], p)
            acc_ref[h * 64:(h + 1) * 64, :] = alpha[h:h + 1, :] * acc_ref[h * 64:(h + 1) * 64, :] + pv[0:DH]
            ls.append(pv[DH:DH + 1])
        l_ref[...] = alpha * l_ref[...] + jnp.concatenate(ls, axis=0)

    ma_ref[...] = jnp.full((HA, tq), NEG, F32)

    def pair(i, c):
        mx0 = tile_logits(2 * i, 0)
        mx1 = tile_logits(2 * i + 1, 1)
        tile_values(2 * i, 0, mx0)
        tile_values(2 * i + 1, 1, mx1)
        return c

    lax.fori_loop(0, nk // 2, pair, 0)

    @when(nk % 2 == 1)
    def _():
        tile_values(nk - 1, 0, tile_logits(nk - 1, 0))

    for h in range(HA):
        o_ref[0, h * 64:(h + 1) * 64, :] = (acc_ref[h * 64:(h + 1) * 64, :] / l_ref[h:h + 1, :]).astype(BF16)


def _t5_bucket_np(rel):
    half = NUM_BUCKETS // 2
    max_exact = half // 2
    out = np.zeros(rel.shape, np.int64)
    flat_rel = rel.reshape(-1)
    flat = out.reshape(-1)
    for a in range(flat_rel.size):
        r = int(flat_rel[a])
        n = abs(r)
        b = n if n < max_exact else min(half - 1, (n * n).bit_length() + 1)
        flat[a] = b + (half if r > 0 else 0)
    return out


def _bias_tables(rel_bias, adm, tk, tq, tq_valid):
    kj = np.arange(tk)[:, None]
    t = np.minimum(np.arange(tq), tq_valid - 1)[None, :]
    rel_to_bucket = _t5_bucket_np(np.arange(-2 * tk - tq, tk + 1))
    lut = lambda rel: rel_to_bucket[rel + 2 * tk + tq]
    far_bucket = NUM_BUCKETS // 2 - 1
    idx = np.stack([np.full((tk, tq), far_bucket), lut(kj - tk - t), lut(kj - t)], axis=0)
    tab = rel_bias.astype(F32) * LOG2E
    onehot = jnp.asarray(idx[..., None] == np.arange(NUM_BUCKETS), F32)
    bias = jnp.einsum("ktqb,bh->khtq", onehot, tab - tab[far_bucket], precision=lax.Precision.HIGHEST)
    mask = np.zeros((3, 1, tk, tq), np.float32)
    mask[2, 0] = np.where(adm > 0.5, 0.0, NEG)
    return bias + jnp.asarray(mask)


def _dsa(qT, qihT, qilT, wiT, k4, vT4, ki34, bias, adm, *, nq, causal, topk):
    g, _, tq = qT.shape
    b, nk, tk, _ = k4.shape
    assert g == b * nq and tk == 256 and nk + RADIX_TILES - 1 <= 64
    cap = jnp.asarray(np.stack([np.full((tk, tq), np.inf, np.float32), np.where(adm > 0.5, np.inf, NEG)]), F32)
    qspec = lambda r: pl.BlockSpec((1, r, tq), lambda bi, i: (bi * nq + i, 0, 0))
    kspec = lambda s: pl.BlockSpec((1,) + s, lambda bi, i: (bi, 0, 0, 0), pipeline_mode=pl.Buffered(1))
    kern = functools.partial(_dsa_kernel, tq=tq, tk=tk, nk_static=nk, causal=causal, topk=topk)
    return pl.pallas_call(
        kern,
        grid=(b, nq),
        in_specs=[qspec(512), qspec(512), qspec(512), qspec(8),
                  kspec((nk, tk, 512)), kspec((nk, HA * V_ROWS, tk)), kspec((nk, tk, 256)),
                  _const_spec((3, HA, tk, tq)), _const_spec((2, tk, tq))],
        out_specs=qspec(512),
        out_shape=jax.ShapeDtypeStruct((g, 512, tq), BF16),
        scratch_shapes=[pltpu.VMEM((nk, tk, tq), I32), pltpu.VMEM((8, tq), F32),
                        pltpu.VMEM((512, tq), F32), pltpu.VMEM((2, HA, tk, tq), BF16),
                        pltpu.VMEM((33, nk + RADIX_TILES - 1, 8, tq), I32),
                        pltpu.VMEM((2, nk + RADIX_TILES - 1, 8, tq), I32),
                        pltpu.VMEM((tk, tq), F32), pltpu.VMEM((HA, tq), F32)],
        compiler_params=_cparams(("arbitrary", "arbitrary")),
        name="dsa_causal" if causal else "dsa_cached",
    )(qT, qihT, qilT, wiT, k4, vT4, ki34, bias, cap)


def _dsa_step_kernel(qa_ref, qih_ref, qil_ref, w_ref, ck_ref, cv_ref, cki_ref, nk_ref, nv_ref, nki_ref,
                     bias_ref, cap_ref, blk_ref, o_ref, key_ref, pln_ref, eq_ref, m_ref, l_ref, acc_ref,
                     *, nkc, t, topk):
    tk, tq = nk_ref.shape[1], qa_ref.shape[2]
    qih, qil = qih_ref[0], qil_ref[0]
    w = w_ref[0, 0:1, :]
    shifts = [t << s for s in range((tq // t).bit_length() - 2, -1, -1)]

    def scores(kx, j, cap):
        kh, kl = _split(kx)
        s = _dot(kh, qih) + (_dot(kl, qih) + _dot(kh, qil))
        sc = jnp.maximum(s, 0.0) * w
        for sh in shifts:
            sc = sc + pltpu.roll(sc, sh, 1)
        if cap is not None:
            sc = jnp.minimum(sc, cap)
        _store_keys(sc, j, key_ref, pln_ref, eq_ref)

    def score_body(j, c):
        scores(cki_ref[0, pl.ds(pl.multiple_of(j * tk, tk), tk), :], j, None)
        return c

    lax.fori_loop(0, nkc, score_body, 0)
    scores(nki_ref[0], nkc, cap_ref[...])
    thr, pos_thr = _radix_threshold(pln_ref, eq_ref, nkc + 1, tq, topk)

    m_ref[...] = jnp.full(m_ref.shape, NEG, F32)
    l_ref[...] = jnp.zeros(l_ref.shape, F32)
    acc_ref[...] = jnp.zeros(acc_ref.shape, F32)
    qa = qa_ref[0]
    row = lax.broadcasted_iota(I32, (tk, tq), 0)
    ones8 = jnp.ones((8, tk), BF16)
    tn = (((0,), (0,)), ((), ()))

    def softmax_step(lg, j, bias):
        sel = key_ref[j] > jnp.where(row <= pos_thr - j * tk, thr - 1, thr)
        if bias is not None:
            lg = lg + bias
        lg = jnp.where(sel, lg, NEG).astype(BF16)
        part = jnp.max(lg.reshape(tk // 16, 16, tq), axis=0).astype(F32)
        m_old = m_ref[0:1, :]
        m_new = jnp.maximum(m_old, jnp.max(part, axis=0, keepdims=True))
        alpha = jnp.exp2(m_old - m_new)
        p = jnp.exp2(lg - m_new.astype(BF16))
        l_ref[0:1, :] = alpha * l_ref[0:1, :] + _dot(ones8, p)[0:1]
        m_ref[0:1, :] = m_new
        return p, alpha

    def cached(j, bias):
        head = lambda ref, h: ref[0, pl.ds(pl.multiple_of(j * tk, tk), tk), h, :].astype(BF16)
        lg = _dot(head(ck_ref, 0), qa[0:DH, :])
        for h in range(1, HA):
            lg = lg + _dot(head(ck_ref, h), qa[h * DH:(h + 1) * DH, :])
        p, alpha = softmax_step(lg, j, bias)
        for h in range(HA):
            pv = lax.dot_general(head(cv_ref, h), p, tn, preferred_element_type=F32)
            acc_ref[h * DH:(h + 1) * DH, :] = alpha * acc_ref[h * DH:(h + 1) * DH, :] + pv

    def far(j, c):
        cached(j, None)
        return c

    lax.fori_loop(0, nkc - 1, far, 0)
    cached(nkc - 1, bias_ref[0])
    p, alpha = softmax_step(_dot(nk_ref[0], qa), nkc, bias_ref[1])
    acc_ref[...] = alpha * acc_ref[...] + lax.dot_general(nv_ref[0], p, tn, preferred_element_type=F32)
    o = acc_ref[...] / l_ref[0:1, :] * blk_ref[...]
    for sh in shifts:
        o = o + pltpu.roll(o, sh, 1)
    o_ref[0] = o.astype(BF16)


def _dsa_step(qT, qihT, qilT, wiT, kbf, va, ki_new, past_k, past_v, past_kidx, rel_bias, *, b, t, topk):
    past = past_k.shape[1]
    tk, tq = 256, HA * t
    assert tq == 128 and past % tk == 0 and t <= tk
    nkc = past // tk
    nk = nkc + 1
    per_b = lambda a: jnp.swapaxes(a[0].reshape(a.shape[1], b, t), 0, 1)
    blk = (np.arange(WA)[:, None] // DH == np.arange(tq)[None, :] // t).astype(np.float32)
    qa = (jnp.tile(per_b(qT), (1, 1, HA)) * jnp.asarray(blk, BF16))
    lanes = lambda a: jnp.swapaxes(per_b(a).reshape(b, HA, D_IDX, t), 1, 2).reshape(b, D_IDX, tq)
    w = jnp.broadcast_to(per_b(wiT).reshape(b, 1, tq), (b, 8, tq))
    pad_rows = lambda a: jnp.pad(a.reshape(b, t, a.shape[-1]), ((0, 0), (0, tk - t), (0, 0)))
    adm = np.broadcast_to(np.arange(tk)[:, None] < t, (tk, tq)).astype(np.float32)
    bias = _bias_tables(rel_bias, adm[:, :t], tk, t, t)[1:]
    bias = jnp.swapaxes(bias, 1, 2).reshape(2, tk, tq)
    cap = jnp.asarray(np.where(adm > 0.5, np.inf, NEG), F32)
    bspec = lambda s: pl.BlockSpec((1,) + s, lambda bi: (bi,) + (0,) * len(s))
    return pl.pallas_call(
        functools.partial(_dsa_step_kernel, nkc=nkc, t=t, topk=topk),
        grid=(b,),
        in_specs=[bspec((WA, tq)), bspec((D_IDX, tq)), bspec((D_IDX, tq)), bspec((8, tq)),
                  bspec((past, HA, DH)), bspec((past, HA, DH)), bspec((past, D_IDX)),
                  bspec((tk, WA)), bspec((tk, WA)), bspec((tk, D_IDX)),
                  _const_spec((2, tk, tq)), _const_spec((tk, tq)), _const_spec((WA, tq))],
        out_specs=bspec((WA, tq)),
        out_shape=jax.ShapeDtypeStruct((b, WA, tq), BF16),
        scratch_shapes=[pltpu.VMEM((nk, tk, tq), I32), pltpu.VMEM((33, nk + RADIX_TILES - 1, 8, tq), I32),
                        pltpu.VMEM((2, nk + RADIX_TILES - 1, 8, tq), I32), pltpu.VMEM((8, tq), F32),
                        pltpu.VMEM((8, tq), F32), pltpu.VMEM((WA, tq), F32)],
        compiler_params=_cparams(("arbitrary",)),
        name="dsa_step",
    )(qa, lanes(qihT), lanes(qilT), w, past_k, past_v, past_kidx,
      pad_rows(kbf), pad_rows(va.astype(BF16)), pad_rows(ki_new), bias, cap, jnp.asarray(blk))


def _gdn_pre_kernel(conv_ref, prev_ref, hist_ref, small_ref, cw_ref, alog_ref, dtb_ref,
                    wm_ref, um_ref, qe_ref, oi_ref, e_ref, ext_ref, *, rows, nvalid):
    c = CHUNK
    n = HB * c
    ext_ref[0:8, :] = jnp.where(pl.program_id(1) == 0, hist_ref[0], prev_ref[0])
    ext_ref[8:8 + rows, :] = conv_ref[0]
    cb = ext_ref[5:5 + rows, :] * cw_ref[0:1, :]
    for j in range(1, CONV_B):
        cb = cb + ext_ref[5 + j:5 + j + rows, :] * cw_ref[j:j + 1, :]
    cb = _silu(cb)

    small = small_ref[0]
    pos = lax.broadcasted_iota(I32, (rows, 1), 0) % c
    rowv = (pos < nvalid).astype(F32)
    beta_all = _sigmoid(small) * rowv
    sp = small + dtb_ref[...]
    g_all = -jnp.exp(alog_ref[...]) * (jnp.maximum(sp, 0.0) + jnp.log(1.0 + jnp.exp(-jnp.abs(sp)))) * rowv

    ri = lax.broadcasted_iota(I32, (n, n), 0)
    ci = lax.broadcasted_iota(I32, (n, n), 1)
    same = (ri // c) == (ci // c)
    tri_b = (same & (ci <= ri)).astype(BF16)
    bd_f = same.astype(F32)
    bd_b = same.astype(BF16)
    wr = lax.broadcasted_iota(I32, (c, n), 0)
    wl = lax.broadcasted_iota(I32, (c, n), 1)
    grp = wl // c
    tri_w = (wl % c) <= wr
    strict_w = (wl % c) < wr
    eye_w = ((wl % c) == wr).astype(F32)
    nt = (((1,), (1,)), ((), ()))
    tn = (((0,), (0,)), ((), ()))

    def l2n(x):
        return x * lax.rsqrt(jnp.sum(x * x, axis=-1, keepdims=True) + EPS)

    def to_wide(full):
        out = jnp.where(grp == 0, full[0:c, :], 0.0)
        for h in range(1, HB):
            out = out + jnp.where(grp == h, full[h * c:(h + 1) * c, :], 0.0)
        return out

    def tile4(x):
        return jnp.concatenate([x] * HB, axis=0)

    def chunk(k0):
        def stack(fn):
            return jnp.concatenate([fn(h) for h in range(HB)], axis=0)

        rv = rowv[k0:k0 + c]
        q = stack(lambda h: l2n(cb[k0:k0 + c, h * DK:(h + 1) * DK]) * (DK ** -0.5))
        k = stack(lambda h: l2n(cb[k0:k0 + c, WB + h * DK:WB + (h + 1) * DK]) * rv)
        v = stack(lambda h: cb[k0:k0 + c, 2 * WB + h * DV:2 * WB + (h + 1) * DV] * rv)
        beta = stack(lambda h: beta_all[k0:k0 + c, _L_BB + h:_L_BB + h + 1])
        g = stack(lambda h: g_all[k0:k0 + c, _L_AB + h:_L_AB + h + 1])
        yield

        gh, gl = _split(jnp.broadcast_to(g, (n, 128)))
        gcum = _dot(tri_b, gh) + _dot(tri_b, gl)
        yield
        gcum_row = gcum.T[0:1, :]
        gcum_col = gcum[:, 0:1]
        col_w = to_wide(jnp.concatenate([gcum, gcum], axis=1))
        decay_w = jnp.where(tri_w, jnp.exp(jnp.where(tri_w, col_w - gcum_row, 0.0)), 0.0)
        kb = k * beta
        kbf = k.astype(BF16)
        kk_w = to_wide(lax.dot_general(kb.astype(BF16), kbf, nt, preferred_element_type=F32))
        qk_w = to_wide(lax.dot_general(q.astype(BF16), kbf, nt, preferred_element_type=F32))
        nmat_w = jnp.where(strict_w, kk_w * decay_w, 0.0)
        attn_w = jnp.where(tri_w, qk_w * decay_w, 0.0)
        yield

        inv_w = eye_w - nmat_w
        ph = nmat_w.astype(BF16)
        bh = tile4(ph) * bd_b
        for _ in range(int(np.log2(c)) - 1):
            ph = _dot(ph, bh).astype(BF16)
            yield
            bh = tile4(ph) * bd_b
            inv_w = inv_w + _dot(inv_w.astype(BF16), bh)
            yield
        rhs = jnp.concatenate([v * beta, kb * jnp.exp(gcum_col)], axis=1)
        sol = _dot3(tile4(inv_w) * bd_f, rhs)
        yield
        solb = sol.astype(BF16)
        aw = _dot((tile4(attn_w) * bd_f).astype(BF16), solb)
        yield
        ck = k0 // c
        oi_ref[0, ck] = aw[:, :DV]
        qe_ref[0, ck] = (q * jnp.exp(gcum_col) - aw[:, DV:]).astype(BF16)
        for h in range(HB):
            rs = slice(h * c, (h + 1) * c)
            g_last = gcum[(h + 1) * c - 1:(h + 1) * c, :]
            kd = (k[rs] * jnp.exp(g_last - gcum[rs])).astype(BF16)
            uw = lax.dot_general(kd, solb[rs], tn, preferred_element_type=F32)
            um_ref[0, ck, h] = uw[:, :DV]
            wm_ref[0, ck, h] = uw[:, DV:].astype(BF16)
            e_ref[0, ck, h:h + 1, :] = jnp.exp(g_last)
        e_ref[0, ck, HB:8, :] = jnp.zeros((8 - HB, 128), F32)

    gens = [chunk(k0) for k0 in range(0, rows, c)]
    while gens:
        alive = []
        for gen in gens:
            try:
                next(gen)
                alive.append(gen)
            except StopIteration:
                pass
        gens = alive


def _gdn_scan_kernel(wm_ref, um_ref, qe_ref, oi_ref, e_ref, gb_ref, s0_ref, ng_ref, ob_ref, sfin_ref, s_ref,
                     *, bb, g):
    c = CHUNK

    @pl.when(pl.program_id(1) == 0)
    def _():
        s_ref[...] = s0_ref[...]

    ng = ng_ref[...]
    for b in range(bb):
        for ck in range(g):
            for h in range(HB):
                s = s_ref[b, h]
                sb = s.astype(BF16)
                rs = slice(h * c, (h + 1) * c)
                o = _dot(qe_ref[b, ck, rs, :], sb) + oi_ref[b, ck, rs, :]
                s_ref[b, h] = e_ref[b, ck, h:h + 1, :] * s + (um_ref[b, ck, h] - _dot(wm_ref[b, ck, h], sb))
                gate = gb_ref[b, ck * c:(ck + 1) * c, h * DV:(h + 1) * DV]
                ob_ref[b, ck * c:(ck + 1) * c, h * DV:(h + 1) * DV] = (_rms(o, ng) * _silu(gate)).astype(BF16)
    sfin_ref[...] = s_ref[...]


def _gdn(conv_in, hist, gb, small, s0, conv_w, a_log, dt_bias, norm_gdn, nvalid):
    b, t, _ = conv_in.shape
    assert t % CHUNK == 0
    rows = 256 if t % 256 == 0 else CHUNK
    nc, cps = t // CHUNK, rows // CHUNK
    n = HB * CHUNK
    alog = jnp.zeros((1, 128), F32).at[0, _L_AB:_L_AB + HB].set(a_log)
    dtb = jnp.zeros((1, 128), F32).at[0, _L_AB:_L_AB + HB].set(dt_bias)
    row = lambda c_: pl.BlockSpec((1, rows, c_), lambda bi, ti: (bi, ti, 0))
    prev = pl.BlockSpec((1, 8, C_CONV_B), lambda bi, ti: (bi, jnp.maximum(ti * (rows // 8) - 1, 0), 0))
    per_b = lambda s: pl.BlockSpec((1,) + s, lambda bi, ti: (bi,) + (0,) * len(s))
    chunked = lambda s: pl.BlockSpec((1, cps) + s, lambda bi, ti: (bi, ti) + (0,) * len(s))
    wm, um, qe, oi, e = pl.pallas_call(
        functools.partial(_gdn_pre_kernel, rows=rows, nvalid=nvalid),
        grid=(b, t // rows),
        in_specs=[row(C_CONV_B), prev, per_b((8, C_CONV_B)), row(128),
                  _const_spec((CONV_B, C_CONV_B)), _const_spec((1, 128)), _const_spec((1, 128))],
        out_specs=[chunked((HB, DK, DV)), chunked((HB, DK, DV)), chunked((n, DK)), chunked((n, DV)),
                   chunked((8, 128))],
        out_shape=[jax.ShapeDtypeStruct((b, nc, HB, DK, DV), BF16), jax.ShapeDtypeStruct((b, nc, HB, DK, DV), F32),
                   jax.ShapeDtypeStruct((b, nc, n, DK), BF16), jax.ShapeDtypeStruct((b, nc, n, DV), F32),
                   jax.ShapeDtypeStruct((b, nc, 8, 128), F32)],
        scratch_shapes=[pltpu.VMEM((8 + rows, C_CONV_B), F32)],
        compiler_params=_cparams(("arbitrary", "arbitrary")),
        name="gdn_pre",
    )(conv_in, conv_in, hist, small, conv_w, alog, dtb)

    bb = 2 if b % 2 == 0 else 1
    g = 2 if nc % 2 == 0 else 1
    blk = lambda s: pl.BlockSpec((bb, g) + s, lambda bi, ci: (bi, ci) + (0,) * len(s))
    rowb = pl.BlockSpec((bb, g * CHUNK, WB), lambda bi, ci: (bi, ci, 0))
    state = pl.BlockSpec((bb, HB, DK, DV), lambda bi, ci: (bi, 0, 0, 0))
    return pl.pallas_call(
        functools.partial(_gdn_scan_kernel, bb=bb, g=g),
        grid=(b // bb, nc // g),
        in_specs=[blk((HB, DK, DV)), blk((HB, DK, DV)), blk((n, DK)), blk((n, DV)), blk((8, 128)), rowb, state,
                  _const_spec((1, DV))],
        out_specs=[rowb, state],
        out_shape=[jax.ShapeDtypeStruct((b, t, WB), BF16), jax.ShapeDtypeStruct((b, HB, DK, DV), F32)],
        scratch_shapes=[pltpu.VMEM((bb, HB, DK, DV), F32)],
        compiler_params=_cparams(("arbitrary", "arbitrary")),
        name="gdn_scan",
    )(wm, um, qe, oi, e, gb, s0, norm_gdn.reshape(1, DV))


def _post_kernel(x_ref, oaT_ref, ob_ref, ga_ref, gbr_ref, p_ref, hist_ref, wa_ref, wb_ref, wo_ref, nf_ref,
                 wup_ref, cw_ref, wdn_ref, npl_ref, wpg_ref, wple_ref, nfin_ref, y_ref, tail_ref, ext_ref, *, tm):
    @pl.when(pl.program_id(1) == 0)
    def _():
        ext_ref[0:8, :] = hist_ref[0]

    ya = lax.dot_general(oaT_ref[0], wa_ref[...], (((0,), (0,)), ((), ())), preferred_element_type=F32)
    yb = _dot(ob_ref[0], wb_ref[...])
    mix = _sigmoid(ga_ref[0]) * ya + _sigmoid(gbr_ref[0]) * yb
    x1 = x_ref[0] + _dot(mix.astype(BF16), wo_ref[...])
    h2 = _rms(x1, nf_ref[...]).astype(BF16)
    ext_ref[8:8 + tm, :] = _dot(h2, wup_ref[:, 0:D_FF])
    u_val = _dot(h2, wup_ref[:, D_FF:2 * D_FF])
    cv = ext_ref[6:6 + tm, :] * cw_ref[0:1, :]
    for j in range(1, CONV_F):
        cv = cv + ext_ref[6 + j:6 + j + tm, :] * cw_ref[j:j + 1, :]
    tail = ext_ref[tm:tm + 8, :]
    ext_ref[0:8, :] = tail
    tail_ref[0] = tail
    act = 0.5 * cv * (1.0 + jnp.tanh(0.7978845608028654 * (cv + 0.044715 * (cv * cv * cv))))
    x2 = x1 + _dot((act * u_val).astype(BF16), wdn_ref[...])
    gate = _sigmoid(_dot(_rms(x2, npl_ref[...]).astype(BF16), wpg_ref[...]))
    x3 = x2 + gate * _dot(p_ref[0].astype(BF16), wple_ref[...])
    y_ref[0] = _rms(x3, nfin_ref[...])


def _post(x, oaT, ob, ga, gbr, p, hist, w_proj_a, w_proj_b, w_out, norm_ffn, w_up, conv_ffn, w_down, norm_ple,
          w_ple_gate, w_ple, norm_final, tm):
    b, t, _ = x.shape
    nt = t // tm
    assert t % tm == 0 and tm >= 8 and oaT.shape == (b * nt, WA, tm)
    row = lambda c: pl.BlockSpec((1, tm, c), lambda bi, ti: (bi, ti, 0))
    per_b = pl.BlockSpec((1, 8, D_FF), lambda bi, ti: (bi, 0, 0))
    vec = _const_spec((1, D_MODEL))
    return pl.pallas_call(
        functools.partial(_post_kernel, tm=tm),
        grid=(b, nt),
        in_specs=[row(D_MODEL), pl.BlockSpec((1, WA, tm), lambda bi, ti: (bi * nt + ti, 0, 0)), row(WB),
                  row(D_MODEL), row(D_MODEL), row(D_PLE), per_b,
                  _const_spec((WA, D_MODEL)), _const_spec((WB, D_MODEL)), _const_spec((D_MODEL, D_MODEL)), vec,
                  _const_spec((D_MODEL, 2 * D_FF)), _const_spec((CONV_F, D_FF)), _const_spec((D_FF, D_MODEL)),
                  vec, _const_spec((D_MODEL, D_MODEL)), _const_spec((D_PLE, D_MODEL)), vec],
        out_specs=[row(D_MODEL), per_b],
        out_shape=[jax.ShapeDtypeStruct((b, t, D_MODEL), F32), jax.ShapeDtypeStruct((b, 8, D_FF), F32)],
        scratch_shapes=[pltpu.VMEM((8 + tm, D_FF), F32)],
        compiler_params=_cparams(("arbitrary", "arbitrary")),
        name="post",
    )(x, oaT, ob, ga, gbr, p, hist, w_proj_a.astype(BF16), w_proj_b.astype(BF16), w_out.astype(BF16),
      norm_ffn.reshape(1, D_MODEL), w_up.astype(BF16), conv_ffn, w_down.astype(BF16),
      norm_ple.reshape(1, D_MODEL), w_ple_gate.astype(BF16), w_ple.astype(BF16), norm_final.reshape(1, D_MODEL))


def _pad_hist(hist, rows=8):
    b, r, c = hist.shape
    return jnp.concatenate([jnp.zeros((b, rows - r, c), hist.dtype), hist], axis=1)


def _layer(x, p, past_k, past_v, past_kidx, s_gdn, conv_b_hist, ffn_hist, wts, *, tm, tq):
    (norm_mix, w_in, conv_b, a_log, dt_bias, norm_gdn, w_proj_a, w_proj_b, w_out, norm_ffn, w_up, conv_ffn,
     w_down, norm_ple, w_ple, w_ple_gate, rel_bias, norm_final) = wts
    b, t, _ = x.shape
    n = b * t
    past = past_k.shape[1]
    topk = min(TOPK_MAX, (past + t) // 4)
    x2d = x.reshape(n, D_MODEL)
    tmi = min(tm, n)
    (ka, va, kbf, conv_in, gb, ga, gbr, small, ki3, qT, qihT, qilT, wiT, vTa) = _in_proj(x2d, norm_mix, w_in, tmi)

    if past == 0:
        assert tq == tmi and t % tq == 0 and tq % CHUNK == 0 and tq >= topk
        nq = t // tq
        kj = np.arange(tq)[:, None]
        adm = ((kj // CHUNK) <= (np.arange(tq)[None, :] // CHUNK)).astype(np.float32)
        oT = _dsa(qT, qihT, qilT, wiT, kbf.reshape(b, nq, tq, WA), vTa.reshape(b, nq, HA * V_ROWS, tq),
                  ki3.reshape(b, nq, tq, 256), _bias_tables(rel_bias, adm, tq, tq, tq), adm,
                  nq=nq, causal=True, topk=topk)
    else:
        assert n == tmi
        oT = _dsa_step(qT, qihT, qilT, wiT, kbf, va.reshape(n, WA), small[:, :D_IDX], past_k, past_v, past_kidx, rel_bias,
                       b=b, t=t, topk=topk)
        oT = oT[:, :, :t]

    tp = -(-t // CHUNK) * CHUNK
    padt = lambda a: jnp.pad(a.reshape(b, t, a.shape[-1]), ((0, 0), (0, tp - t), (0, 0)))
    ob, s_new = _gdn(padt(conv_in), _pad_hist(conv_b_hist), padt(gb), padt(small), s_gdn, conv_b, a_log, dt_bias,
                     norm_gdn, nvalid=min(t, CHUNK))
    new_conv_b = jnp.concatenate([conv_b_hist, conv_in.reshape(b, t, C_CONV_B)], axis=1)[:, t:]

    per_bt = lambda a: a.reshape(b, t, a.shape[-1])
    y, tail = _post(x, oT, ob[:, :t], per_bt(ga), per_bt(gbr), p, _pad_hist(ffn_hist), w_proj_a, w_proj_b, w_out,
                    norm_ffn, w_up, conv_ffn, w_down, norm_ple, w_ple_gate, w_ple, norm_final, min(tm, t))
    new_ffn = tail[:, 8 - (CONV_F - 1):]
    return (y, ka.reshape(b, t, HA, DH), va.reshape(b, t, HA, DH), small[:, :D_IDX].reshape(b, t, D_IDX),
            s_new, new_conv_b, new_ffn)


def kernel(x_prompt, x_sample, p_prompt, p_sample, cache_k, cache_v, cache_kidx, state_gdn, state_gdn_conv,
           state_ffn_conv, norm_mix, w_in, conv_b, a_log, dt_bias, norm_gdn, w_proj_a, w_proj_b, w_out, norm_ffn,
           w_up, conv_ffn, w_down, norm_ple, w_ple, w_ple_gate, rel_bias, norm_final):
    assert norm_mix.shape[0] == 1
    bp = x_prompt.shape[0]
    dt = x_prompt.dtype
    wts = (norm_mix[0], w_in[0], conv_b[0], a_log[0], dt_bias[0], norm_gdn[0], w_proj_a[0], w_proj_b[0], w_out[0],
           norm_ffn[0], w_up[0], conv_ffn[0], w_down[0], norm_ple[0], w_ple[0], w_ple_gate[0], rel_bias, norm_final)
    outs_p = _layer(x_prompt, p_prompt[0], jnp.zeros((bp, 0, HA, DH), dt), jnp.zeros((bp, 0, HA, DH), dt),
                    jnp.zeros((bp, 0, D_IDX), dt), jnp.zeros((bp, HB, DK, DV), dt),
                    jnp.zeros((bp, CONV_B - 1, C_CONV_B), dt), jnp.zeros((bp, CONV_F - 1, D_FF), dt),
                    wts, tm=256, tq=256)
    outs_s = _layer(x_sample, p_sample[0], cache_k[0], cache_v[0], cache_kidx[0], state_gdn[0],
                    state_gdn_conv[0], state_ffn_conv[0], wts, tm=256, tq=256)
    yp, ys = outs_p[0], outs_s[0]
    return (yp, ys) + tuple(a[None] for a in outs_p[1:]) + tuple(a[None] for a in outs_s[1:])
```

```python
import functools

import numpy as np
import jax
import jax.numpy as jnp
from jax import lax
from jax.experimental import pallas as pl
from jax.experimental.pallas import tpu as pltpu

F32 = jnp.float32
BF16 = jnp.bfloat16
I32 = jnp.int32

D_MODEL = 1024
CHUNK = 64
HA, DH = 8, 64
H_IDX, D_IDX = 8, 64
TOPK_MAX = 256
NUM_BUCKETS, MAX_DISTANCE = 32, 128
HB, DK, DV = 4, 128, 128
CONV_B = 4
D_FF = 2816
CONV_F = 3
D_PLE = 256
EPS = 1e-6
NEG = -1e30
WA = HA * DH
WB = HB * DK
C_CONV_B = 3 * WB
INT_MIN = -2 ** 31
LOG2E = 1.4426950408889634

_O_QA, _O_KA, _O_VA, _O_QI, _O_KI, _O_WI = 0, 512, 1024, 1536, 2048, 2112
_O_QB, _O_GB, _O_BB, _O_AB, _O_GA, _O_GBR = 2120, 3656, 4168, 4172, 4176, 5200
_L_WI, _L_BB, _L_AB = 64, 72, 76

VMEM_LIMIT = 56 * 1024 * 1024


def _cparams(sem):
    return pltpu.CompilerParams(dimension_semantics=sem, vmem_limit_bytes=VMEM_LIMIT)


def _const_spec(shape):
    nd = len(shape)
    return pl.BlockSpec(shape, lambda *_: (0,) * nd, pipeline_mode=pl.Buffered(1))


def _rms(x, g):
    return x * lax.rsqrt(jnp.mean(x * x, axis=-1, keepdims=True) + EPS) * g


def _split(x):
    hi = x.astype(BF16)
    lo = (x - hi.astype(F32)).astype(BF16)
    return hi, lo


def _dot(a, b):
    return jnp.dot(a, b, preferred_element_type=F32)


def _dot3(a, b):
    ah, al = _split(a)
    bh, bl = _split(b)
    return _dot(ah, bh) + (_dot(al, bh) + _dot(ah, bl))


def _sigmoid(x):
    return 1.0 / (1.0 + jnp.exp(-x))


def _silu(x):
    return x * _sigmoid(x)


def _transpose32(a):
    a = list(a)
    j, m = 16, 0x0000FFFF
    while j:
        k = 0
        while k < 32:
            t = (a[k] ^ lax.shift_right_logical(a[k + j], jnp.int32(j))) & jnp.int32(m - (1 << 32) if m >> 31 else m)
            a[k] = a[k] ^ t
            a[k + j] = a[k + j] ^ (t << j)
            k = (k + j + 1) & ~j
        j >>= 1
        m = (m ^ (m << j)) & 0xFFFFFFFF
    return a


V_ROWS = DH + 16
RADIX_TILES = 8
_NT = (((1,), (1,)), ((), ()))


def _in_proj_kernel(x_ref, g_ref, wm_ref, wt_ref, wqh_ref, wql_ref, wsh_ref, wsl_ref, wwh_ref, wwl_ref,
                    ph_ref, plo_ref, sc_ref,
                    ka_ref, va_ref, kbf_ref, conv_ref, gb_ref, ga_ref, gbr_ref, small_ref, ki3_ref,
                    qT_ref, qihT_ref, qilT_ref, wiT_ref, vTa_ref):
    tm = x_ref.shape[0]
    h = _rms(x_ref[...], g_ref[...])
    hh, hl = _split(h)

    def main(lo, hi):
        return _dot(hh, wm_ref[:, lo:hi])

    def nt(w, a):
        return lax.dot_general(w, a, _NT, preferred_element_type=F32)

    def nt3(wh_ref, wl_ref):
        return nt(wh_ref[...], hh) + (nt(wh_ref[...], hl) + nt(wl_ref[...], hh))

    ka = main(0, 512)
    va = main(512, 1024)
    for hd in range(HA):
        ka_ref[:, hd, :] = ka[:, hd * DH:(hd + 1) * DH]
        va_ref[:, hd, :] = va[:, hd * DH:(hd + 1) * DH]
    kbf_ref[...] = ka.astype(BF16)
    conv_ref[...] = main(1024, 2560)
    gb_ref[...] = main(2560, 3072)
    ga_ref[...] = main(3072, 4096)
    gbr_ref[...] = main(4096, 5120)

    qT_ref[0] = nt(wt_ref[0:WA, :], hh).astype(BF16)
    vT = nt(wt_ref[WA:2 * WA, :], hh).astype(BF16)
    ones = jnp.ones((V_ROWS - DH, tm), BF16)
    for hd in range(HA):
        vTa_ref[0, hd * V_ROWS:hd * V_ROWS + DH, :] = vT[hd * DH:(hd + 1) * DH]
        vTa_ref[0, hd * V_ROWS + DH:(hd + 1) * V_ROWS, :] = ones
    qh, ql = _split(nt3(wqh_ref, wql_ref))
    qihT_ref[0] = qh
    qilT_ref[0] = ql
    wiT_ref[0] = nt3(wwh_ref, wwl_ref)[0:H_IDX] * (H_IDX ** -0.5)

    small = (_dot(hh, wsh_ref[...]) + (_dot(hl, wsh_ref[...]) + _dot(hh, wsl_ref[...]))) * sc_ref[...]
    small_ref[...] = small
    sh, sl = _split(small)
    ki3_ref[...] = (_dot(sh, ph_ref[...]) + _dot(sl, plo_ref[...])).astype(BF16)


def _in_proj(x2d, norm_mix, w_in, tm):
    n = x2d.shape[0]
    assert n % tm == 0
    g = n // tm
    w = w_in
    wm = jnp.concatenate([w[:, _O_KA:_O_QI], w[:, _O_QB:_O_BB], w[:, _O_GA:]], axis=1).astype(BF16)
    wt = jnp.concatenate([w[:, _O_QA:_O_KA] * (DH ** -0.5 * LOG2E), w[:, _O_VA:_O_QI]], axis=1).T.astype(BF16)
    hilo = lambda a: (a.astype(BF16), (a - a.astype(BF16).astype(F32)).astype(BF16))
    wqh, wql = hilo((w[:, _O_QI:_O_KI] * (D_IDX ** -0.5)).T)
    wsh, wsl = hilo(jnp.concatenate([w[:, _O_KI:_O_QB], w[:, _O_BB:_O_GA], jnp.zeros((D_MODEL, 48), F32)], axis=1))
    wwh, wwl = hilo(jnp.concatenate([w[:, _O_WI:_O_QB], jnp.zeros((D_MODEL, 8), F32)], axis=1).T)
    ph = np.zeros((128, 256), np.float32)
    plo = np.zeros((128, 256), np.float32)
    for c in range(64):
        ph[c, c] = 1.0
        ph[c, 64 + c] = 1.0
        plo[c, 128 + c] = 1.0
    sc = np.ones((1, 128), np.float32)
    sc[0, _L_WI:_L_WI + H_IDX] = H_IDX ** -0.5
    row = lambda c: pl.BlockSpec((tm, c), lambda i: (i, 0))
    colT = lambda r: pl.BlockSpec((1, r, tm), lambda i: (i, 0, 0))
    heads = pl.BlockSpec((tm, HA, DH), lambda i: (i, 0, 0))
    out_cols = [(512, BF16), (1536, F32), (512, F32), (1024, F32), (1024, F32), (128, F32), (256, BF16)]
    out_rows = [(WA, BF16), (WA, BF16), (WA, BF16), (H_IDX, F32), (HA * V_ROWS, BF16)]
    return pl.pallas_call(
        _in_proj_kernel,
        grid=(g,),
        in_specs=[row(D_MODEL), _const_spec((1, D_MODEL)), _const_spec((D_MODEL, 5120)),
                  _const_spec((2 * WA, D_MODEL)), _const_spec((WA, D_MODEL)), _const_spec((WA, D_MODEL)),
                  _const_spec((D_MODEL, 128)), _const_spec((D_MODEL, 128)),
                  _const_spec((16, D_MODEL)), _const_spec((16, D_MODEL)),
                  _const_spec((128, 256)), _const_spec((128, 256)), _const_spec((1, 128))],
        out_specs=[heads, heads] + [row(c) for c, _ in out_cols] + [colT(r) for r, _ in out_rows],
        out_shape=[jax.ShapeDtypeStruct((n, HA, DH), F32)] * 2
        + [jax.ShapeDtypeStruct((n, c), d) for c, d in out_cols]
        + [jax.ShapeDtypeStruct((g, r, tm), d) for r, d in out_rows],
        compiler_params=_cparams(("arbitrary",)),
        name="in_proj",
    )(x2d, norm_mix.reshape(1, D_MODEL), wm, wt, wqh, wql, wsh, wsl, wwh, wwl, jnp.asarray(ph, BF16),
      jnp.asarray(plo, BF16), jnp.asarray(sc))


def _store_keys(score, j, key_ref, pln_ref, eq_ref):
    tk, tq = score.shape
    bits = pltpu.bitcast(score, I32)
    bits = jnp.where(bits == INT_MIN, 0, bits)
    key = bits ^ ((bits >> 31) & 0x7FFFFFFF)
    key_ref[j] = key
    u3 = (key ^ INT_MIN).reshape(tk // 8, 8, tq)
    ones = jnp.full((8, tq), -1, I32)
    pln_ref[0, j] = ones
    for l0 in range(0, tq, 128):
        planes = _transpose32([u3[r][:, l0:l0 + 128] for r in range(32)])
        for b in range(32):
            pln_ref[b + 1, j, :, l0:l0 + 128] = planes[b]
    eq_ref[0, j] = ones


def _radix_threshold(pln_ref, eq_ref, nk, tq, topk):
    for d in range(RADIX_TILES - 1):
        pln_ref[:, nk + d] = jnp.zeros((33, 8, tq), I32)
        eq_ref[0, nk + d] = jnp.zeros((8, tq), I32)

    sub = lax.broadcasted_iota(I32, (8, tq), 0)
    group_masks = (-0x10000, -0xFF0100, -0xF0F0F10, -0x33333334, -0x55555556)

    def pos_plane(e, j):
        if e < 6:
            return jnp.broadcast_to(-((~j >> (5 - e)) & 1), (8, tq))
        if e < 11:
            return jnp.full((8, tq), group_masks[e - 6], I32)
        return -((~sub >> (13 - e)) & 1)

    def sweep(prev, cur, carry, src, dst):
        n_gt, flip = carry

        def body(jq, cnt):
            for d in range(RADIX_TILES):
                j = jq * RADIX_TILES + d
                e = eq_ref[src, j] & (prev(j) ^ flip)
                eq_ref[dst, j] = e
                cnt = cnt + lax.population_count(e & cur(j))
            return cnt

        cnt = lax.fori_loop(0, (nk + RADIX_TILES - 1) // RADIX_TILES, body, jnp.zeros((8, tq), I32))
        cnt = n_gt + cnt.sum(axis=0, keepdims=True)
        acc = cnt >= topk
        return acc, (jnp.where(acc, n_gt, cnt), jnp.where(acc, 0, -1))

    def key_sweep(it, carry, src, dst):
        tu, rest = carry
        acc, rest = sweep(lambda j: pln_ref[it, j], lambda j: pln_ref[it + 1, j], rest, src, dst)
        return jnp.where(acc, tu | jnp.left_shift(jnp.int32(1), 31 - it), tu), rest

    def two_bits(i, carry):
        return key_sweep(2 * i + 1, key_sweep(2 * i, carry, 0, 1), 1, 0)

    zero_row = jnp.zeros((1, tq), I32)
    tu, rest = lax.fori_loop(0, 16, two_bits, (zero_row, (zero_row, zero_row)))
    inv_pos = zero_row
    for e in range(14):
        prev = (lambda j: pln_ref[32, j]) if e == 0 else functools.partial(pos_plane, e - 1)
        acc, rest = sweep(prev, functools.partial(pos_plane, e), rest, e % 2, 1 - e % 2)
        inv_pos = jnp.where(acc, inv_pos | (1 << (13 - e)), inv_pos)
    return tu ^ INT_MIN, ~inv_pos & 0x3FFF


def _dsa_kernel(qT_ref, qihT_ref, qilT_ref, wiT_ref, k_ref, vT_ref, ki3_ref, bias_ref, cap_ref,
                o_ref, key_ref, l_ref, acc_ref, lg_ref, pln_ref, eq_ref, sc_ref, ma_ref,
                *, tq, tk, nk_static, causal, topk):
    nk = (pl.program_id(1) + 1) if causal else nk_static
    wiT = wiT_ref[0]

    def when(cond):
        if isinstance(cond, bool):
            return (lambda f: f()) if cond else (lambda f: None)
        return pl.when(cond)

    zeros64 = jnp.zeros((64, tq), BF16)
    q3 = []
    for h in range(H_IDX):
        hi = qihT_ref[0, h * 64:(h + 1) * 64, :]
        lo = qilT_ref[0, h * 64:(h + 1) * 64, :]
        q3.append(jnp.concatenate([hi, lo, hi, zeros64], axis=0))

    def score_matmuls(j):
        kt = ki3_ref[0, j]
        acc = None
        for h in range(H_IDX):
            t = jnp.maximum(_dot(kt, q3[h]), 0.0) * wiT[h:h + 1, :]
            acc = t if acc is None else acc + t
        sc_ref[...] = jnp.minimum(acc, cap_ref[jnp.where(j == nk - 1, 1, 0)])

    def score_finish(j):
        _store_keys(sc_ref[...], j, key_ref, pln_ref, eq_ref)

    def score_body(j, c):
        score_finish(j - 1)
        score_matmuls(j)
        return c

    score_matmuls(0)
    lax.fori_loop(1, nk, score_body, 0)
    score_finish(nk - 1)

    thr, pos_thr = _radix_threshold(pln_ref, eq_ref, nk, tq, topk)

    l_ref[...] = jnp.zeros(l_ref.shape, F32)
    acc_ref[...] = jnp.zeros(acc_ref.shape, F32)
    qm = []
    for h in range(HA):
        qh = qT_ref[0, h * 64:(h + 1) * 64, :]
        qm.append(jnp.concatenate([qh, zeros64] if h % 2 == 0 else [zeros64, qh], axis=0))
    row = lax.broadcasted_iota(I32, (tk, tq), 0)

    def tile_logits(j, slot, far):
        sel = key_ref[j] > jnp.where(row <= pos_thr - j * tk, thr - 1, thr)
        kind = jnp.clip(j - (nk - 3), 0, 2)
        mx = []
        for h in range(HA):
            pr = h // 2
            lg = _dot(k_ref[0, j, :, pr * 128:(pr + 1) * 128], qm[h])
            if not far:
                lg = lg + bias_ref[kind, h]
            lg = jnp.where(sel, lg, NEG).astype(BF16)
            lg_ref[slot, h] = lg
            part = jnp.max(lg.reshape(tk // 16, 16, tq), axis=0).astype(F32)
            mx.append(jnp.max(part, axis=0, keepdims=True))
        return jnp.concatenate(mx, axis=0)

    def tile_values(j, slot, mx):
        m_old = ma_ref[...]
        m_new = jnp.maximum(m_old, mx)
        alpha = jnp.exp2(m_old - m_new)
        ma_ref[...] = m_new
        ls = []
        for h in range(HA):
            p = jnp.exp2(lg_ref[slot, h] - m_new[h:h + 1, :].astype(BF16))
            pv = _dot(vT_ref[0, j, h * V_ROWS:(h + 1) * V_ROWS, :], p)
            acc_ref[h * 64:(h + 1) * 64, :] = alpha[h:h + 1, :] * acc_ref[h * 64:(h + 1) * 64, :] + pv[0:DH]
            ls.append(pv[DH:DH + 1])
        l_ref[...] = alpha * l_ref[...] + jnp.concatenate(ls, axis=0)

    ma_ref[...] = jnp.full((HA, tq), NEG, F32)

    def pair(j0, far):
        mx0 = tile_logits(j0, 0, far)
        mx1 = tile_logits(j0 + 1, 1, far)
        tile_values(j0, 0, mx0)
        tile_values(j0 + 1, 1, mx1)

    def far_pair(i, c):
        pair(2 * i, True)
        return c

    n_far = jnp.maximum(nk - 2, 0) // 2
    lax.fori_loop(0, n_far, far_pair, 0)

    @when(nk >= 2)
    def _():
        pair(2 * n_far, False)

    @when(nk % 2 == 1)
    def _():
        tile_values(nk - 1, 0, tile_logits(nk - 1, 0, False))

    for h in range(HA):
        o_ref[0, h * 64:(h + 1) * 64, :] = (acc_ref[h * 64:(h + 1) * 64, :] / l_ref[h:h + 1, :]).astype(BF16)


def _t5_bucket_np(rel):
    half = NUM_BUCKETS // 2
    max_exact = half // 2
    out = np.zeros(rel.shape, np.int64)
    flat_rel = rel.reshape(-1)
    flat = out.reshape(-1)
    for a in range(flat_rel.size):
        r = int(flat_rel[a])
        n = abs(r)
        b = n if n < max_exact else min(half - 1, (n * n).bit_length() + 1)
        flat[a] = b + (half if r > 0 else 0)
    return out


def _bias_tables(rel_bias, adm, tk, tq, tq_valid):
    kj = np.arange(tk)[:, None]
    t = np.minimum(np.arange(tq), tq_valid - 1)[None, :]
    rel_to_bucket = _t5_bucket_np(np.arange(-2 * tk - tq, tk + 1))
    lut = lambda rel: rel_to_bucket[rel + 2 * tk + tq]
    far_bucket = NUM_BUCKETS // 2 - 1
    idx = np.stack([np.full((tk, tq), far_bucket), lut(kj - tk - t), lut(kj - t)], axis=0)
    tab = rel_bias.astype(F32) * LOG2E
    onehot = jnp.asarray(idx[..., None] == np.arange(NUM_BUCKETS), F32)
    bias = jnp.einsum("ktqb,bh->khtq", onehot, tab - tab[far_bucket], precision=lax.Precision.HIGHEST)
    mask = np.zeros((3, 1, tk, tq), np.float32)
    mask[2, 0] = np.where(adm > 0.5, 0.0, NEG)
    return bias + jnp.asarray(mask)


def _dsa(qT, qihT, qilT, wiT, k4, vT4, ki34, bias, adm, *, nq, causal, topk):
    g, _, tq = qT.shape
    b, nk, tk, _ = k4.shape
    assert g == b * nq and tk == 256 and nk + RADIX_TILES - 1 <= 64
    cap = jnp.asarray(np.stack([np.full((tk, tq), np.inf, np.float32), np.where(adm > 0.5, np.inf, NEG)]), F32)
    qspec = lambda r: pl.BlockSpec((1, r, tq), lambda bi, i: (bi * nq + i, 0, 0))
    kspec = lambda s: pl.BlockSpec((1,) + s, lambda bi, i: (bi, 0, 0, 0), pipeline_mode=pl.Buffered(1))
    kern = functools.partial(_dsa_kernel, tq=tq, tk=tk, nk_static=nk, causal=causal, topk=topk)
    return pl.pallas_call(
        kern,
        grid=(b, nq),
        in_specs=[qspec(512), qspec(512), qspec(512), qspec(8),
                  kspec((nk, tk, 512)), kspec((nk, HA * V_ROWS, tk)), kspec((nk, tk, 256)),
                  _const_spec((3, HA, tk, tq)), _const_spec((2, tk, tq))],
        out_specs=qspec(512),
        out_shape=jax.ShapeDtypeStruct((g, 512, tq), BF16),
        scratch_shapes=[pltpu.VMEM((nk, tk, tq), I32), pltpu.VMEM((8, tq), F32),
                        pltpu.VMEM((512, tq), F32), pltpu.VMEM((2, HA, tk, tq), BF16),
                        pltpu.VMEM((33, nk + RADIX_TILES - 1, 8, tq), I32),
                        pltpu.VMEM((2, nk + RADIX_TILES - 1, 8, tq), I32),
                        pltpu.VMEM((tk, tq), F32), pltpu.VMEM((HA, tq), F32)],
        compiler_params=_cparams(("arbitrary", "arbitrary")),
        name="dsa_causal" if causal else "dsa_cached",
    )(qT, qihT, qilT, wiT, k4, vT4, ki34, bias, cap)


def _dsa_step_kernel(qa_ref, qih_ref, qil_ref, w_ref, ck_ref, cv_ref, cki_ref, nk_ref, nv_ref, nki_ref,
                     bias_ref, cap_ref, blk_ref, o_ref, key_ref, pln_ref, eq_ref, m_ref, l_ref, acc_ref,
                     *, nkc, t, topk):
    tk, tq = nk_ref.shape[1], qa_ref.shape[2]
    qih, qil = qih_ref[0], qil_ref[0]
    w = w_ref[0, 0:1, :]
    shifts = [t << s for s in range((tq // t).bit_length() - 2, -1, -1)]

    def scores(kx, j, cap):
        kh, kl = _split(kx)
        s = _dot(kh, qih) + (_dot(kl, qih) + _dot(kh, qil))
        sc = jnp.maximum(s, 0.0) * w
        for sh in shifts:
            sc = sc + pltpu.roll(sc, sh, 1)
        if cap is not None:
            sc = jnp.minimum(sc, cap)
        _store_keys(sc, j, key_ref, pln_ref, eq_ref)

    def score_body(j, c):
        scores(cki_ref[0, pl.ds(pl.multiple_of(j * tk, tk), tk), :], j, None)
        return c

    lax.fori_loop(0, nkc, score_body, 0)
    scores(nki_ref[0], nkc, cap_ref[...])
    thr, pos_thr = _radix_threshold(pln_ref, eq_ref, nkc + 1, tq, topk)

    m_ref[...] = jnp.full(m_ref.shape, NEG, F32)
    l_ref[...] = jnp.zeros(l_ref.shape, F32)
    acc_ref[...] = jnp.zeros(acc_ref.shape, F32)
    qa = qa_ref[0]
    row = lax.broadcasted_iota(I32, (tk, tq), 0)
    ones8 = jnp.ones((8, tk), BF16)
    tn = (((0,), (0,)), ((), ()))

    def attend(k_bf, v_bf, j, bias):
        sel = key_ref[j] > jnp.where(row <= pos_thr - j * tk, thr - 1, thr)
        lg = _dot(k_bf, qa)
        if bias is not None:
            lg = lg + bias
        lg = jnp.where(sel, lg, NEG).astype(BF16)
        part = jnp.max(lg.reshape(tk // 16, 16, tq), axis=0).astype(F32)
        m_old = m_ref[0:1, :]
        m_new = jnp.maximum(m_old, jnp.max(part, axis=0, keepdims=True))
        alpha = jnp.exp2(m_old - m_new)
        p = jnp.exp2(lg - m_new.astype(BF16))
        l_ref[0:1, :] = alpha * l_ref[0:1, :] + _dot(ones8, p)[0:1]
        acc_ref[...] = alpha * acc_ref[...] + lax.dot_general(v_bf, p, tn, preferred_element_type=F32)
        m_ref[0:1, :] = m_new

    def cached(j, bias):
        rows = pl.ds(pl.multiple_of(j * tk, tk), tk)
        attend(ck_ref[0, rows, :].astype(BF16), cv_ref[0, rows, :].astype(BF16), j, bias)

    def far(j, c):
        cached(j, None)
        return c

    lax.fori_loop(0, nkc - 1, far, 0)
    cached(nkc - 1, bias_ref[0])
    attend(nk_ref[0], nv_ref[0], nkc, bias_ref[1])
    o = acc_ref[...] / l_ref[0:1, :] * blk_ref[...]
    for sh in shifts:
        o = o + pltpu.roll(o, sh, 1)
    o_ref[0] = o.astype(BF16)


def _dsa_step(qT, qihT, qilT, wiT, kbf, va, ki_new, past_k, past_v, past_kidx, rel_bias, *, b, t, topk):
    past = past_k.shape[1]
    tk, tq = 256, HA * t
    assert tq == 128 and past % tk == 0 and t <= tk
    nkc = past // tk
    nk = nkc + 1
    per_b = lambda a: jnp.swapaxes(a[0].reshape(a.shape[1], b, t), 0, 1)
    blk = (np.arange(WA)[:, None] // DH == np.arange(tq)[None, :] // t).astype(np.float32)
    qa = (jnp.tile(per_b(qT), (1, 1, HA)) * jnp.asarray(blk, BF16))
    lanes = lambda a: jnp.swapaxes(per_b(a).reshape(b, HA, D_IDX, t), 1, 2).reshape(b, D_IDX, tq)
    w = jnp.broadcast_to(per_b(wiT).reshape(b, 1, tq), (b, 8, tq))
    pad_rows = lambda a: jnp.pad(a.reshape(b, t, a.shape[-1]), ((0, 0), (0, tk - t), (0, 0)))
    adm = np.broadcast_to(np.arange(tk)[:, None] < t, (tk, tq)).astype(np.float32)
    bias = _bias_tables(rel_bias, adm[:, :t], tk, t, t)[1:]
    bias = jnp.swapaxes(bias, 1, 2).reshape(2, tk, tq)
    cap = jnp.asarray(np.where(adm > 0.5, np.inf, NEG), F32)
    bspec = lambda s: pl.BlockSpec((1,) + s, lambda bi: (bi,) + (0,) * len(s))
    return pl.pallas_call(
        functools.partial(_dsa_step_kernel, nkc=nkc, t=t, topk=topk),
        grid=(b,),
        in_specs=[bspec((WA, tq)), bspec((D_IDX, tq)), bspec((D_IDX, tq)), bspec((8, tq)),
                  bspec((past, WA)), bspec((past, WA)), bspec((past, D_IDX)),
                  bspec((tk, WA)), bspec((tk, WA)), bspec((tk, D_IDX)),
                  _const_spec((2, tk, tq)), _const_spec((tk, tq)), _const_spec((WA, tq))],
        out_specs=bspec((WA, tq)),
        out_shape=jax.ShapeDtypeStruct((b, WA, tq), BF16),
        scratch_shapes=[pltpu.VMEM((nk, tk, tq), I32), pltpu.VMEM((33, nk + RADIX_TILES - 1, 8, tq), I32),
                        pltpu.VMEM((2, nk + RADIX_TILES - 1, 8, tq), I32), pltpu.VMEM((8, tq), F32),
                        pltpu.VMEM((8, tq), F32), pltpu.VMEM((WA, tq), F32)],
        compiler_params=_cparams(("arbitrary",)),
        name="dsa_step",
    )(qa, lanes(qihT), lanes(qilT), w, past_k.reshape(b, past, WA), past_v.reshape(b, past, WA), past_kidx,
      pad_rows(kbf), pad_rows(va.astype(BF16)), pad_rows(ki_new), bias, cap, jnp.asarray(blk))


def _gdn_pre_kernel(conv_ref, prev_ref, hist_ref, small_ref, cw_ref, alog_ref, dtb_ref,
                    wm_ref, um_ref, qe_ref, oi_ref, e_ref, ext_ref, *, rows, nvalid):
    c = CHUNK
    n = HB * c
    ext_ref[0:8, :] = jnp.where(pl.program_id(1) == 0, hist_ref[0], prev_ref[0])
    ext_ref[8:8 + rows, :] = conv_ref[0]
    cb = ext_ref[5:5 + rows, :] * cw_ref[0:1, :]
    for j in range(1, CONV_B):
        cb = cb + ext_ref[5 + j:5 + j + rows, :] * cw_ref[j:j + 1, :]
    cb = _silu(cb)

    small = small_ref[0]
    pos = lax.broadcasted_iota(I32, (rows, 1), 0) % c
    rowv = (pos < nvalid).astype(F32)
    beta_all = _sigmoid(small) * rowv
    sp = small + dtb_ref[...]
    g_all = -jnp.exp(alog_ref[...]) * (jnp.maximum(sp, 0.0) + jnp.log(1.0 + jnp.exp(-jnp.abs(sp)))) * rowv

    ri = lax.broadcasted_iota(I32, (n, n), 0)
    ci = lax.broadcasted_iota(I32, (n, n), 1)
    same = (ri // c) == (ci // c)
    tri_b = (same & (ci <= ri)).astype(BF16)
    bd_f = same.astype(F32)
    bd_b = same.astype(BF16)
    wr = lax.broadcasted_iota(I32, (c, n), 0)
    wl = lax.broadcasted_iota(I32, (c, n), 1)
    grp = wl // c
    tri_w = (wl % c) <= wr
    strict_w = (wl % c) < wr
    eye_w = ((wl % c) == wr).astype(F32)
    nt = (((1,), (1,)), ((), ()))
    tn = (((0,), (0,)), ((), ()))

    def l2n(x):
        return x * lax.rsqrt(jnp.sum(x * x, axis=-1, keepdims=True) + EPS)

    def to_wide(full):
        out = jnp.where(grp == 0, full[0:c, :], 0.0)
        for h in range(1, HB):
            out = out + jnp.where(grp == h, full[h * c:(h + 1) * c, :], 0.0)
        return out

    def tile4(x):
        return jnp.concatenate([x] * HB, axis=0)

    def chunk(k0):
        def stack(fn):
            return jnp.concatenate([fn(h) for h in range(HB)], axis=0)

        rv = rowv[k0:k0 + c]
        q = stack(lambda h: l2n(cb[k0:k0 + c, h * DK:(h + 1) * DK]) * (DK ** -0.5))
        k = stack(lambda h: l2n(cb[k0:k0 + c, WB + h * DK:WB + (h + 1) * DK]) * rv)
        v = stack(lambda h: cb[k0:k0 + c, 2 * WB + h * DV:2 * WB + (h + 1) * DV] * rv)
        beta = stack(lambda h: beta_all[k0:k0 + c, _L_BB + h:_L_BB + h + 1])
        g = stack(lambda h: g_all[k0:k0 + c, _L_AB + h:_L_AB + h + 1])
        yield

        gh, gl = _split(jnp.broadcast_to(g, (n, 128)))
        gcum = _dot(tri_b, gh) + _dot(tri_b, gl)
        yield
        gcum_row = gcum.T[0:1, :]
        gcum_col = gcum[:, 0:1]
        col_w = to_wide(jnp.concatenate([gcum, gcum], axis=1))
        decay_w = jnp.where(tri_w, jnp.exp(jnp.where(tri_w, col_w - gcum_row, 0.0)), 0.0)
        kb = k * beta
        kbf = k.astype(BF16)
        kk_w = to_wide(lax.dot_general(kb.astype(BF16), kbf, nt, preferred_element_type=F32))
        qk_w = to_wide(lax.dot_general(q.astype(BF16), kbf, nt, preferred_element_type=F32))
        nmat_w = jnp.where(strict_w, kk_w * decay_w, 0.0)
        attn_w = jnp.where(tri_w, qk_w * decay_w, 0.0)
        yield

        inv_w = eye_w - nmat_w
        ph = nmat_w.astype(BF16)
        bh = tile4(ph) * bd_b
        for _ in range(int(np.log2(c)) - 1):
            ph = _dot(ph, bh).astype(BF16)
            yield
            bh = tile4(ph) * bd_b
            inv_w = inv_w + _dot(inv_w.astype(BF16), bh)
            yield
        rhs = jnp.concatenate([v * beta, kb * jnp.exp(gcum_col)], axis=1)
        sol = _dot3(tile4(inv_w) * bd_f, rhs)
        yield
        solb = sol.astype(BF16)
        aw = _dot((tile4(attn_w) * bd_f).astype(BF16), solb)
        yield
        ck = k0 // c
        oi_ref[0, ck] = aw[:, :DV]
        qe_ref[0, ck] = (q * jnp.exp(gcum_col) - aw[:, DV:]).astype(BF16)
        for h in range(HB):
            rs = slice(h * c, (h + 1) * c)
            g_last = gcum[(h + 1) * c - 1:(h + 1) * c, :]
            kd = (k[rs] * jnp.exp(g_last - gcum[rs])).astype(BF16)
            uw = lax.dot_general(kd, solb[rs], tn, preferred_element_type=F32)
            um_ref[0, ck, h] = uw[:, :DV]
            wm_ref[0, ck, h] = uw[:, DV:].astype(BF16)
            e_ref[0, ck, h:h + 1, :] = jnp.exp(g_last)
        e_ref[0, ck, HB:8, :] = jnp.zeros((8 - HB, 128), F32)

    gens = [chunk(k0) for k0 in range(0, rows, c)]
    while gens:
        alive = []
        for gen in gens:
            try:
                next(gen)
                alive.append(gen)
            except StopIteration:
                pass
        gens = alive


def _gdn_scan_kernel(wm_ref, um_ref, qe_ref, oi_ref, e_ref, gb_ref, s0_ref, ng_ref, ob_ref, sfin_ref, s_ref,
                     *, bb, g):
    c = CHUNK

    @pl.when(pl.program_id(1) == 0)
    def _():
        s_ref[...] = s0_ref[...]

    ng = ng_ref[...]
    for b in range(bb):
        for ck in range(g):
            for h in range(HB):
                s = s_ref[b, h]
                sb = s.astype(BF16)
                rs = slice(h * c, (h + 1) * c)
                o = _dot(qe_ref[b, ck, rs, :], sb) + oi_ref[b, ck, rs, :]
                s_ref[b, h] = e_ref[b, ck, h:h + 1, :] * s + (um_ref[b, ck, h] - _dot(wm_ref[b, ck, h], sb))
                gate = gb_ref[b, ck * c:(ck + 1) * c, h * DV:(h + 1) * DV]
                ob_ref[b, ck * c:(ck + 1) * c, h * DV:(h + 1) * DV] = (_rms(o, ng) * _silu(gate)).astype(BF16)
    sfin_ref[...] = s_ref[...]


def _gdn(conv_in, hist, gb, small, s0, conv_w, a_log, dt_bias, norm_gdn, nvalid):
    b, t, _ = conv_in.shape
    assert t % CHUNK == 0
    rows = 256 if t % 256 == 0 else CHUNK
    nc, cps = t // CHUNK, rows // CHUNK
    n = HB * CHUNK
    alog = jnp.zeros((1, 128), F32).at[0, _L_AB:_L_AB + HB].set(a_log)
    dtb = jnp.zeros((1, 128), F32).at[0, _L_AB:_L_AB + HB].set(dt_bias)
    row = lambda c_: pl.BlockSpec((1, rows, c_), lambda bi, ti: (bi, ti, 0))
    prev = pl.BlockSpec((1, 8, C_CONV_B), lambda bi, ti: (bi, jnp.maximum(ti * (rows // 8) - 1, 0), 0))
    per_b = lambda s: pl.BlockSpec((1,) + s, lambda bi, ti: (bi,) + (0,) * len(s))
    chunked = lambda s: pl.BlockSpec((1, cps) + s, lambda bi, ti: (bi, ti) + (0,) * len(s))
    wm, um, qe, oi, e = pl.pallas_call(
        functools.partial(_gdn_pre_kernel, rows=rows, nvalid=nvalid),
        grid=(b, t // rows),
        in_specs=[row(C_CONV_B), prev, per_b((8, C_CONV_B)), row(128),
                  _const_spec((CONV_B, C_CONV_B)), _const_spec((1, 128)), _const_spec((1, 128))],
        out_specs=[chunked((HB, DK, DV)), chunked((HB, DK, DV)), chunked((n, DK)), chunked((n, DV)),
                   chunked((8, 128))],
        out_shape=[jax.ShapeDtypeStruct((b, nc, HB, DK, DV), BF16), jax.ShapeDtypeStruct((b, nc, HB, DK, DV), F32),
                   jax.ShapeDtypeStruct((b, nc, n, DK), BF16), jax.ShapeDtypeStruct((b, nc, n, DV), F32),
                   jax.ShapeDtypeStruct((b, nc, 8, 128), F32)],
        scratch_shapes=[pltpu.VMEM((8 + rows, C_CONV_B), F32)],
        compiler_params=_cparams(("arbitrary", "arbitrary")),
        name="gdn_pre",
    )(conv_in, conv_in, hist, small, conv_w, alog, dtb)

    bb = 2 if b % 2 == 0 else 1
    g = 2 if nc % 2 == 0 else 1
    blk = lambda s: pl.BlockSpec((bb, g) + s, lambda bi, ci: (bi, ci) + (0,) * len(s))
    rowb = pl.BlockSpec((bb, g * CHUNK, WB), lambda bi, ci: (bi, ci, 0))
    state = pl.BlockSpec((bb, HB, DK, DV), lambda bi, ci: (bi, 0, 0, 0))
    return pl.pallas_call(
        functools.partial(_gdn_scan_kernel, bb=bb, g=g),
        grid=(b // bb, nc // g),
        in_specs=[blk((HB, DK, DV)), blk((HB, DK, DV)), blk((n, DK)), blk((n, DV)), blk((8, 128)), rowb, state,
                  _const_spec((1, DV))],
        out_specs=[rowb, state],
        out_shape=[jax.ShapeDtypeStruct((b, t, WB), BF16), jax.ShapeDtypeStruct((b, HB, DK, DV), F32)],
        scratch_shapes=[pltpu.VMEM((bb, HB, DK, DV), F32)],
        compiler_params=_cparams(("arbitrary", "arbitrary")),
        name="gdn_scan",
    )(wm, um, qe, oi, e, gb, s0, norm_gdn.reshape(1, DV))


def _post_kernel(x_ref, oaT_ref, ob_ref, ga_ref, gbr_ref, p_ref, hist_ref, wa_ref, wb_ref, wo_ref, nf_ref,
                 wup_ref, cw_ref, wdn_ref, npl_ref, wpg_ref, wple_ref, nfin_ref, y_ref, tail_ref, ext_ref, *, tm):
    @pl.when(pl.program_id(1) == 0)
    def _():
        ext_ref[0:8, :] = hist_ref[0]

    ya = lax.dot_general(oaT_ref[0], wa_ref[...], (((0,), (0,)), ((), ())), preferred_element_type=F32)
    yb = _dot(ob_ref[0], wb_ref[...])
    mix = _sigmoid(ga_ref[0]) * ya + _sigmoid(gbr_ref[0]) * yb
    x1 = x_ref[0] + _dot(mix.astype(BF16), wo_ref[...])
    h2 = _rms(x1, nf_ref[...]).astype(BF16)
    ext_ref[8:8 + tm, :] = _dot(h2, wup_ref[:, 0:D_FF])
    u_val = _dot(h2, wup_ref[:, D_FF:2 * D_FF])
    cv = ext_ref[6:6 + tm, :] * cw_ref[0:1, :]
    for j in range(1, CONV_F):
        cv = cv + ext_ref[6 + j:6 + j + tm, :] * cw_ref[j:j + 1, :]
    tail = ext_ref[tm:tm + 8, :]
    ext_ref[0:8, :] = tail
    tail_ref[0] = tail
    act = 0.5 * cv * (1.0 + jnp.tanh(0.7978845608028654 * (cv + 0.044715 * (cv * cv * cv))))
    x2 = x1 + _dot((act * u_val).astype(BF16), wdn_ref[...])
    gate = _sigmoid(_dot(_rms(x2, npl_ref[...]).astype(BF16), wpg_ref[...]))
    x3 = x2 + gate * _dot(p_ref[0].astype(BF16), wple_ref[...])
    y_ref[0] = _rms(x3, nfin_ref[...])


def _post(x, oaT, ob, ga, gbr, p, hist, w_proj_a, w_proj_b, w_out, norm_ffn, w_up, conv_ffn, w_down, norm_ple,
          w_ple_gate, w_ple, norm_final, tm):
    b, t, _ = x.shape
    nt = t // tm
    assert t % tm == 0 and tm >= 8 and oaT.shape == (b * nt, WA, tm)
    row = lambda c: pl.BlockSpec((1, tm, c), lambda bi, ti: (bi, ti, 0))
    per_b = pl.BlockSpec((1, 8, D_FF), lambda bi, ti: (bi, 0, 0))
    vec = _const_spec((1, D_MODEL))
    return pl.pallas_call(
        functools.partial(_post_kernel, tm=tm),
        grid=(b, nt),
        in_specs=[row(D_MODEL), pl.BlockSpec((1, WA, tm), lambda bi, ti: (bi * nt + ti, 0, 0)), row(WB),
                  row(D_MODEL), row(D_MODEL), row(D_PLE), per_b,
                  _const_spec((WA, D_MODEL)), _const_spec((WB, D_MODEL)), _const_spec((D_MODEL, D_MODEL)), vec,
                  _const_spec((D_MODEL, 2 * D_FF)), _const_spec((CONV_F, D_FF)), _const_spec((D_FF, D_MODEL)),
                  vec, _const_spec((D_MODEL, D_MODEL)), _const_spec((D_PLE, D_MODEL)), vec],
        out_specs=[row(D_MODEL), per_b],
        out_shape=[jax.ShapeDtypeStruct((b, t, D_MODEL), F32), jax.ShapeDtypeStruct((b, 8, D_FF), F32)],
        scratch_shapes=[pltpu.VMEM((8 + tm, D_FF), F32)],
        compiler_params=_cparams(("arbitrary", "arbitrary")),
        name="post",
    )(x, oaT, ob, ga, gbr, p, hist, w_proj_a.astype(BF16), w_proj_b.astype(BF16), w_out.astype(BF16),
      norm_ffn.reshape(1, D_MODEL), w_up.astype(BF16), conv_ffn, w_down.astype(BF16),
      norm_ple.reshape(1, D_MODEL), w_ple_gate.astype(BF16), w_ple.astype(BF16), norm_final.reshape(1, D_MODEL))


def _pad_hist(hist, rows=8):
    b, r, c = hist.shape
    return jnp.concatenate([jnp.zeros((b, rows - r, c), hist.dtype), hist], axis=1)


def _layer(x, p, past_k, past_v, past_kidx, s_gdn, conv_b_hist, ffn_hist, wts, *, tm, tq):
    (norm_mix, w_in, conv_b, a_log, dt_bias, norm_gdn, w_proj_a, w_proj_b, w_out, norm_ffn, w_up, conv_ffn,
     w_down, norm_ple, w_ple, w_ple_gate, rel_bias, norm_final) = wts
    b, t, _ = x.shape
    n = b * t
    past = past_k.shape[1]
    topk = min(TOPK_MAX, (past + t) // 4)
    x2d = x.reshape(n, D_MODEL)
    tmi = min(tm, n)
    (ka, va, kbf, conv_in, gb, ga, gbr, small, ki3, qT, qihT, qilT, wiT, vTa) = _in_proj(x2d, norm_mix, w_in, tmi)

    if past == 0:
        assert tq == tmi and t % tq == 0 and tq % CHUNK == 0 and tq >= topk
        nq = t // tq
        kj = np.arange(tq)[:, None]
        adm = ((kj // CHUNK) <= (np.arange(tq)[None, :] // CHUNK)).astype(np.float32)
        oT = _dsa(qT, qihT, qilT, wiT, kbf.reshape(b, nq, tq, WA), vTa.reshape(b, nq, HA * V_ROWS, tq),
                  ki3.reshape(b, nq, tq, 256), _bias_tables(rel_bias, adm, tq, tq, tq), adm,
                  nq=nq, causal=True, topk=topk)
    else:
        assert n == tmi
        oT = _dsa_step(qT, qihT, qilT, wiT, kbf, va.reshape(n, WA), small[:, :D_IDX], past_k, past_v, past_kidx, rel_bias,
                       b=b, t=t, topk=topk)
        oT = oT[:, :, :t]

    tp = -(-t // CHUNK) * CHUNK
    padt = lambda a: jnp.pad(a.reshape(b, t, a.shape[-1]), ((0, 0), (0, tp - t), (0, 0)))
    ob, s_new = _gdn(padt(conv_in), _pad_hist(conv_b_hist), padt(gb), padt(small), s_gdn, conv_b, a_log, dt_bias,
                     norm_gdn, nvalid=min(t, CHUNK))
    new_conv_b = jnp.concatenate([conv_b_hist, conv_in.reshape(b, t, C_CONV_B)], axis=1)[:, t:]

    per_bt = lambda a: a.reshape(b, t, a.shape[-1])
    y, tail = _post(x, oT, ob[:, :t], per_bt(ga), per_bt(gbr), p, _pad_hist(ffn_hist), w_proj_a, w_proj_b, w_out,
                    norm_ffn, w_up, conv_ffn, w_down, norm_ple, w_ple_gate, w_ple, norm_final, min(tm, t))
    new_ffn = tail[:, 8 - (CONV_F - 1):]
    return (y, ka.reshape(b, t, HA, DH), va.reshape(b, t, HA, DH), small[:, :D_IDX].reshape(b, t, D_IDX),
            s_new, new_conv_b, new_ffn)


def kernel(x_prompt, x_sample, p_prompt, p_sample, cache_k, cache_v, cache_kidx, state_gdn, state_gdn_conv,
           state_ffn_conv, norm_mix, w_in, conv_b, a_log, dt_bias, norm_gdn, w_proj_a, w_proj_b, w_out, norm_ffn,
           w_up, conv_ffn, w_down, norm_ple, w_ple, w_ple_gate, rel_bias, norm_final):
    assert norm_mix.shape[0] == 1
    bp = x_prompt.shape[0]
    dt = x_prompt.dtype
    wts = (norm_mix[0], w_in[0], conv_b[0], a_log[0], dt_bias[0], norm_gdn[0], w_proj_a[0], w_proj_b[0], w_out[0],
           norm_ffn[0], w_up[0], conv_ffn[0], w_down[0], norm_ple[0], w_ple[0], w_ple_gate[0], rel_bias, norm_final)
    outs_p = _layer(x_prompt, p_prompt[0], jnp.zeros((bp, 0, HA, DH), dt), jnp.zeros((bp, 0, HA, DH), dt),
                    jnp.zeros((bp, 0, D_IDX), dt), jnp.zeros((bp, HB, DK, DV), dt),
                    jnp.zeros((bp, CONV_B - 1, C_CONV_B), dt), jnp.zeros((bp, CONV_F - 1, D_FF), dt),
                    wts, tm=256, tq=256)
    outs_s = _layer(x_sample, p_sample[0], cache_k[0], cache_v[0], cache_kidx[0], state_gdn[0],
                    state_gdn_conv[0], state_ffn_conv[0], wts, tm=256, tq=256)
    yp, ys = outs_p[0], outs_s[0]
    return (yp, ys) + tuple(a[None] for a in outs_p[1:]) + tuple(a[None] for a in outs_s[1:])
```

```python
import functools

import numpy as np
import jax
import jax.numpy as jnp
from jax import lax
from jax.experimental import pallas as pl
from jax.experimental.pallas import tpu as pltpu

F32 = jnp.float32
BF16 = jnp.bfloat16
I32 = jnp.int32

D_MODEL = 1024
CHUNK = 64
HA, DH = 8, 64
H_IDX, D_IDX = 8, 64
TOPK_MAX = 256
NUM_BUCKETS, MAX_DISTANCE = 32, 128
HB, DK, DV = 4, 128, 128
CONV_B = 4
D_FF = 2816
CONV_F = 3
D_PLE = 256
EPS = 1e-6
NEG = -1e30
WA = HA * DH
WB = HB * DK
C_CONV_B = 3 * WB
INT_MIN = -2 ** 31
LOG2E = 1.4426950408889634

_O_QA, _O_KA, _O_VA, _O_QI, _O_KI, _O_WI = 0, 512, 1024, 1536, 2048, 2112
_O_QB, _O_GB, _O_BB, _O_AB, _O_GA, _O_GBR = 2120, 3656, 4168, 4172, 4176, 5200
_L_WI, _L_BB, _L_AB = 64, 72, 76

VMEM_LIMIT = 56 * 1024 * 1024


def _cparams(sem):
    return pltpu.CompilerParams(dimension_semantics=sem, vmem_limit_bytes=VMEM_LIMIT)


def _const_spec(shape):
    nd = len(shape)
    return pl.BlockSpec(shape, lambda *_: (0,) * nd, pipeline_mode=pl.Buffered(1))


def _rms(x, g):
    return x * lax.rsqrt(jnp.mean(x * x, axis=-1, keepdims=True) + EPS) * g


def _split(x):
    hi = x.astype(BF16)
    lo = (x - hi.astype(F32)).astype(BF16)
    return hi, lo


def _dot(a, b):
    return jnp.dot(a, b, preferred_element_type=F32)


def _dot3(a, b):
    ah, al = _split(a)
    bh, bl = _split(b)
    return _dot(ah, bh) + (_dot(al, bh) + _dot(ah, bl))


def _sigmoid(x):
    return 1.0 / (1.0 + jnp.exp(-x))


def _silu(x):
    return x * _sigmoid(x)


def _transpose32(a):
    a = list(a)
    j, m = 16, 0x0000FFFF
    while j:
        k = 0
        while k < 32:
            t = (a[k] ^ lax.shift_right_logical(a[k + j], jnp.int32(j))) & jnp.int32(m - (1 << 32) if m >> 31 else m)
            a[k] = a[k] ^ t
            a[k + j] = a[k + j] ^ (t << j)
            k = (k + j + 1) & ~j
        j >>= 1
        m = (m ^ (m << j)) & 0xFFFFFFFF
    return a


V_ROWS = DH + 16
RADIX_TILES = 8
_NT = (((1,), (1,)), ((), ()))


def _in_proj_kernel(x_ref, g_ref, wm_ref, wt_ref, wqh_ref, wql_ref, wsh_ref, wsl_ref, wwh_ref, wwl_ref,
                    ph_ref, plo_ref, sc_ref,
                    ka_ref, va_ref, kbf_ref, conv_ref, gb_ref, ga_ref, gbr_ref, small_ref, ki3_ref,
                    qT_ref, qihT_ref, qilT_ref, wiT_ref, vTa_ref):
    tm = x_ref.shape[0]
    h = _rms(x_ref[...], g_ref[...])
    hh, hl = _split(h)

    def main(lo, hi):
        return _dot(hh, wm_ref[:, lo:hi])

    def nt(w, a):
        return lax.dot_general(w, a, _NT, preferred_element_type=F32)

    def nt3(wh_ref, wl_ref):
        return nt(wh_ref[...], hh) + (nt(wh_ref[...], hl) + nt(wl_ref[...], hh))

    ka = main(0, 512)
    va = main(512, 1024)
    for hd in range(HA):
        ka_ref[:, hd, :] = ka[:, hd * DH:(hd + 1) * DH]
        va_ref[:, hd, :] = va[:, hd * DH:(hd + 1) * DH]
    kbf_ref[...] = ka.astype(BF16)
    conv_ref[...] = main(1024, 2560)
    gb_ref[...] = main(2560, 3072)
    ga_ref[...] = main(3072, 4096)
    gbr_ref[...] = main(4096, 5120)

    qT_ref[0] = nt(wt_ref[0:WA, :], hh).astype(BF16)
    vT = nt(wt_ref[WA:2 * WA, :], hh).astype(BF16)
    ones = jnp.ones((V_ROWS - DH, tm), BF16)
    for hd in range(HA):
        vTa_ref[0, hd * V_ROWS:hd * V_ROWS + DH, :] = vT[hd * DH:(hd + 1) * DH]
        vTa_ref[0, hd * V_ROWS + DH:(hd + 1) * V_ROWS, :] = ones
    qh, ql = _split(nt3(wqh_ref, wql_ref))
    qihT_ref[0] = qh
    qilT_ref[0] = ql
    wiT_ref[0] = nt3(wwh_ref, wwl_ref)[0:H_IDX] * (H_IDX ** -0.5)

    small = (_dot(hh, wsh_ref[...]) + (_dot(hl, wsh_ref[...]) + _dot(hh, wsl_ref[...]))) * sc_ref[...]
    small_ref[...] = small
    sh, sl = _split(small)
    ki3_ref[...] = (_dot(sh, ph_ref[...]) + _dot(sl, plo_ref[...])).astype(BF16)


def _in_proj(x2d, norm_mix, w_in, tm):
    n = x2d.shape[0]
    assert n % tm == 0
    g = n // tm
    w = w_in
    wm = jnp.concatenate([w[:, _O_KA:_O_QI], w[:, _O_QB:_O_BB], w[:, _O_GA:]], axis=1).astype(BF16)
    wt = jnp.concatenate([w[:, _O_QA:_O_KA] * (DH ** -0.5 * LOG2E), w[:, _O_VA:_O_QI]], axis=1).T.astype(BF16)
    hilo = lambda a: (a.astype(BF16), (a - a.astype(BF16).astype(F32)).astype(BF16))
    wqh, wql = hilo((w[:, _O_QI:_O_KI] * (D_IDX ** -0.5)).T)
    wsh, wsl = hilo(jnp.concatenate([w[:, _O_KI:_O_QB], w[:, _O_BB:_O_GA], jnp.zeros((D_MODEL, 48), F32)], axis=1))
    wwh, wwl = hilo(jnp.concatenate([w[:, _O_WI:_O_QB], jnp.zeros((D_MODEL, 8), F32)], axis=1).T)
    ph = np.zeros((128, 256), np.float32)
    plo = np.zeros((128, 256), np.float32)
    for c in range(64):
        ph[c, c] = 1.0
        ph[c, 64 + c] = 1.0
        plo[c, 128 + c] = 1.0
    sc = np.ones((1, 128), np.float32)
    sc[0, _L_WI:_L_WI + H_IDX] = H_IDX ** -0.5
    row = lambda c: pl.BlockSpec((tm, c), lambda i: (i, 0))
    colT = lambda r: pl.BlockSpec((1, r, tm), lambda i: (i, 0, 0))
    heads = pl.BlockSpec((tm, HA, DH), lambda i: (i, 0, 0))
    out_cols = [(512, BF16), (1536, F32), (512, F32), (1024, F32), (1024, F32), (128, F32), (256, BF16)]
    out_rows = [(WA, BF16), (WA, BF16), (WA, BF16), (H_IDX, F32), (HA * V_ROWS, BF16)]
    return pl.pallas_call(
        _in_proj_kernel,
        grid=(g,),
        in_specs=[row(D_MODEL), _const_spec((1, D_MODEL)), _const_spec((D_MODEL, 5120)),
                  _const_spec((2 * WA, D_MODEL)), _const_spec((WA, D_MODEL)), _const_spec((WA, D_MODEL)),
                  _const_spec((D_MODEL, 128)), _const_spec((D_MODEL, 128)),
                  _const_spec((16, D_MODEL)), _const_spec((16, D_MODEL)),
                  _const_spec((128, 256)), _const_spec((128, 256)), _const_spec((1, 128))],
        out_specs=[heads, heads] + [row(c) for c, _ in out_cols] + [colT(r) for r, _ in out_rows],
        out_shape=[jax.ShapeDtypeStruct((n, HA, DH), F32)] * 2
        + [jax.ShapeDtypeStruct((n, c), d) for c, d in out_cols]
        + [jax.ShapeDtypeStruct((g, r, tm), d) for r, d in out_rows],
        compiler_params=_cparams(("arbitrary",)),
        name="in_proj",
    )(x2d, norm_mix.reshape(1, D_MODEL), wm, wt, wqh, wql, wsh, wsl, wwh, wwl, jnp.asarray(ph, BF16),
      jnp.asarray(plo, BF16), jnp.asarray(sc))


def _store_keys(score, j, key_ref, pln_ref, eq_ref):
    tk, tq = score.shape
    bits = pltpu.bitcast(score, I32)
    bits = jnp.where(bits == INT_MIN, 0, bits)
    key = bits ^ ((bits >> 31) & 0x7FFFFFFF)
    key_ref[j] = key
    u3 = (key ^ INT_MIN).reshape(tk // 8, 8, tq)
    ones = jnp.full((8, tq), -1, I32)
    pln_ref[0, j] = ones
    for l0 in range(0, tq, 128):
        planes = _transpose32([u3[r][:, l0:l0 + 128] for r in range(32)])
        for b in range(32):
            pln_ref[b + 1, j, :, l0:l0 + 128] = planes[b]
    eq_ref[0, j] = ones


def _radix_threshold(pln_ref, eq_ref, nk, tq, topk):
    for d in range(RADIX_TILES - 1):
        pln_ref[:, nk + d] = jnp.zeros((33, 8, tq), I32)
        eq_ref[0, nk + d] = jnp.zeros((8, tq), I32)

    sub = lax.broadcasted_iota(I32, (8, tq), 0)
    group_masks = (-0x10000, -0xFF0100, -0xF0F0F10, -0x33333334, -0x55555556)

    def pos_plane(e, j):
        if e < 6:
            return jnp.broadcast_to(-((~j >> (5 - e)) & 1), (8, tq))
        if e < 11:
            return jnp.full((8, tq), group_masks[e - 6], I32)
        return -((~sub >> (13 - e)) & 1)

    def sweep(prev, cur, carry, src, dst):
        n_gt, flip = carry

        def body(jq, cnt):
            for d in range(RADIX_TILES):
                j = jq * RADIX_TILES + d
                e = eq_ref[src, j] & (prev(j) ^ flip)
                eq_ref[dst, j] = e
                cnt = cnt + lax.population_count(e & cur(j))
            return cnt

        cnt = lax.fori_loop(0, (nk + RADIX_TILES - 1) // RADIX_TILES, body, jnp.zeros((8, tq), I32))
        cnt = n_gt + cnt.sum(axis=0, keepdims=True)
        acc = cnt >= topk
        return acc, (jnp.where(acc, n_gt, cnt), jnp.where(acc, 0, -1))

    def key_sweep(it, carry, src, dst):
        tu, rest = carry
        acc, rest = sweep(lambda j: pln_ref[it, j], lambda j: pln_ref[it + 1, j], rest, src, dst)
        return jnp.where(acc, tu | jnp.left_shift(jnp.int32(1), 31 - it), tu), rest

    def two_bits(i, carry):
        return key_sweep(2 * i + 1, key_sweep(2 * i, carry, 0, 1), 1, 0)

    zero_row = jnp.zeros((1, tq), I32)
    tu, rest = lax.fori_loop(0, 16, two_bits, (zero_row, (zero_row, zero_row)))
    inv_pos = zero_row
    for e in range(14):
        prev = (lambda j: pln_ref[32, j]) if e == 0 else functools.partial(pos_plane, e - 1)
        acc, rest = sweep(prev, functools.partial(pos_plane, e), rest, e % 2, 1 - e % 2)
        inv_pos = jnp.where(acc, inv_pos | (1 << (13 - e)), inv_pos)
    return tu ^ INT_MIN, ~inv_pos & 0x3FFF


def _dsa_kernel(qT_ref, qihT_ref, qilT_ref, wiT_ref, k_ref, vT_ref, ki3_ref, bias_ref, cap_ref,
                o_ref, key_ref, l_ref, acc_ref, lg_ref, pln_ref, eq_ref, sc_ref, ma_ref,
                *, tq, tk, nk_static, causal, topk):
    nk = (pl.program_id(1) + 1) if causal else nk_static
    wiT = wiT_ref[0]

    def when(cond):
        if isinstance(cond, bool):
            return (lambda f: f()) if cond else (lambda f: None)
        return pl.when(cond)

    zeros64 = jnp.zeros((64, tq), BF16)
    q3 = []
    for h in range(H_IDX):
        hi = qihT_ref[0, h * 64:(h + 1) * 64, :]
        lo = qilT_ref[0, h * 64:(h + 1) * 64, :]
        q3.append(jnp.concatenate([hi, lo, hi, zeros64], axis=0))

    def score_matmuls(j):
        kt = ki3_ref[0, j]
        acc = None
        for h in range(H_IDX):
            t = jnp.maximum(_dot(kt, q3[h]), 0.0) * wiT[h:h + 1, :]
            acc = t if acc is None else acc + t
        sc_ref[...] = jnp.minimum(acc, cap_ref[jnp.where(j == nk - 1, 1, 0)])

    def score_finish(j):
        _store_keys(sc_ref[...], j, key_ref, pln_ref, eq_ref)

    def score_body(j, c):
        score_finish(j - 1)
        score_matmuls(j)
        return c

    score_matmuls(0)
    lax.fori_loop(1, nk, score_body, 0)
    score_finish(nk - 1)

    thr, pos_thr = _radix_threshold(pln_ref, eq_ref, nk, tq, topk)

    l_ref[...] = jnp.zeros(l_ref.shape, F32)
    acc_ref[...] = jnp.zeros(acc_ref.shape, F32)
    qm = []
    for h in range(HA):
        qh = qT_ref[0, h * 64:(h + 1) * 64, :]
        qm.append(jnp.concatenate([qh, zeros64] if h % 2 == 0 else [zeros64, qh], axis=0))
    row = lax.broadcasted_iota(I32, (tk, tq), 0)

    def tile_logits(j, slot, far):
        sel = key_ref[j] > jnp.where(row <= pos_thr - j * tk, thr - 1, thr)
        kind = jnp.clip(j - (nk - 3), 0, 2)
        mx = []
        for h in range(HA):
            pr = h // 2
            lg = _dot(k_ref[0, j, :, pr * 128:(pr + 1) * 128], qm[h])
            if not far:
                lg = lg + bias_ref[kind, h]
            lg = jnp.where(sel, lg, NEG).astype(BF16)
            lg_ref[slot, h] = lg
            part = jnp.max(lg.reshape(tk // 16, 16, tq), axis=0).astype(F32)
            mx.append(jnp.max(part, axis=0, keepdims=True))
        return jnp.concatenate(mx, axis=0)

    def tile_values(j, slot, mx):
        m_old = ma_ref[...]
        m_new = jnp.maximum(m_old, mx)
        alpha = jnp.exp2(m_old - m_new)
        ma_ref[...] = m_new
        ls = []
        for h in range(HA):
            p = jnp.exp2(lg_ref[slot, h] - m_new[h:h + 1, :].astype(BF16))
            pv = _dot(vT_ref[0, j, h * V_ROWS:(h + 1) * V_ROWS, :], p)
            acc_ref[h * 64:(h + 1) * 64, :] = alpha[h:h + 1, :] * acc_ref[h * 64:(h + 1) * 64, :] + pv[0:DH]
            ls.append(pv[DH:DH + 1])
        l_ref[...] = alpha * l_ref[...] + jnp.concatenate(ls, axis=0)

    ma_ref[...] = jnp.full((HA, tq), NEG, F32)

    def pair(j0, far):
        mx0 = tile_logits(j0, 0, far)
        mx1 = tile_logits(j0 + 1, 1, far)
        tile_values(j0, 0, mx0)
        tile_values(j0 + 1, 1, mx1)

    def far_pair(i, c):
        pair(2 * i, True)
        return c

    n_far = jnp.maximum(nk - 2, 0) // 2
    lax.fori_loop(0, n_far, far_pair, 0)

    @when(nk >= 2)
    def _():
        pair(2 * n_far, False)

    @when(nk % 2 == 1)
    def _():
        tile_values(nk - 1, 0, tile_logits(nk - 1, 0, False))

    for h in range(HA):
        o_ref[0, h * 64:(h + 1) * 64, :] = (acc_ref[h * 64:(h + 1) * 64, :] / l_ref[h:h + 1, :]).astype(BF16)


def _t5_bucket_np(rel):
    half = NUM_BUCKETS // 2
    max_exact = half // 2
    out = np.zeros(rel.shape, np.int64)
    flat_rel = rel.reshape(-1)
    flat = out.reshape(-1)
    for a in range(flat_rel.size):
        r = int(flat_rel[a])
        n = abs(r)
        b = n if n < max_exact else min(half - 1, (n * n).bit_length() + 1)
        flat[a] = b + (half if r > 0 else 0)
    return out


def _bias_tables(rel_bias, adm, tk, tq, tq_valid):
    kj = np.arange(tk)[:, None]
    t = np.minimum(np.arange(tq), tq_valid - 1)[None, :]
    rel_to_bucket = _t5_bucket_np(np.arange(-2 * tk - tq, tk + 1))
    lut = lambda rel: rel_to_bucket[rel + 2 * tk + tq]
    far_bucket = NUM_BUCKETS // 2 - 1
    idx = np.stack([np.full((tk, tq), far_bucket), lut(kj - tk - t), lut(kj - t)], axis=0)
    tab = rel_bias.astype(F32) * LOG2E
    onehot = jnp.asarray(idx[..., None] == np.arange(NUM_BUCKETS), F32)
    bias = jnp.einsum("ktqb,bh->khtq", onehot, tab - tab[far_bucket], precision=lax.Precision.HIGHEST)
    mask = np.zeros((3, 1, tk, tq), np.float32)
    mask[2, 0] = np.where(adm > 0.5, 0.0, NEG)
    return bias + jnp.asarray(mask)


def _dsa(qT, qihT, qilT, wiT, k4, vT4, ki34, bias, adm, *, nq, causal, topk):
    g, _, tq = qT.shape
    b, nk, tk, _ = k4.shape
    assert g == b * nq and tk == 256 and nk + RADIX_TILES - 1 <= 64
    cap = jnp.asarray(np.stack([np.full((tk, tq), np.inf, np.float32), np.where(adm > 0.5, np.inf, NEG)]), F32)
    qspec = lambda r: pl.BlockSpec((1, r, tq), lambda bi, i: (bi * nq + i, 0, 0))
    kspec = lambda s: pl.BlockSpec((1,) + s, lambda bi, i: (bi, 0, 0, 0), pipeline_mode=pl.Buffered(1))
    kern = functools.partial(_dsa_kernel, tq=tq, tk=tk, nk_static=nk, causal=causal, topk=topk)
    return pl.pallas_call(
        kern,
        grid=(b, nq),
        in_specs=[qspec(512), qspec(512), qspec(512), qspec(8),
                  kspec((nk, tk, 512)), kspec((nk, HA * V_ROWS, tk)), kspec((nk, tk, 256)),
                  _const_spec((3, HA, tk, tq)), _const_spec((2, tk, tq))],
        out_specs=qspec(512),
        out_shape=jax.ShapeDtypeStruct((g, 512, tq), BF16),
        scratch_shapes=[pltpu.VMEM((nk, tk, tq), I32), pltpu.VMEM((8, tq), F32),
                        pltpu.VMEM((512, tq), F32), pltpu.VMEM((2, HA, tk, tq), BF16),
                        pltpu.VMEM((33, nk + RADIX_TILES - 1, 8, tq), I32),
                        pltpu.VMEM((2, nk + RADIX_TILES - 1, 8, tq), I32),
                        pltpu.VMEM((tk, tq), F32), pltpu.VMEM((HA, tq), F32)],
        compiler_params=_cparams(("arbitrary", "arbitrary")),
        name="dsa_causal" if causal else "dsa_cached",
    )(qT, qihT, qilT, wiT, k4, vT4, ki34, bias, cap)


def _dsa_step_kernel(qa_ref, qih_ref, qil_ref, w_ref, ck_ref, cv_ref, cki_ref, nk_ref, nv_ref, nki_ref,
                     bias_ref, cap_ref, blk_ref, o_ref, key_ref, pln_ref, eq_ref, m_ref, l_ref, acc_ref,
                     *, nkc, t, topk):
    tk, tq = nk_ref.shape[1], qa_ref.shape[2]
    qih, qil = qih_ref[0], qil_ref[0]
    w = w_ref[0, 0:1, :]
    shifts = [t << s for s in range((tq // t).bit_length() - 2, -1, -1)]

    def scores(kx, j, cap):
        kh, kl = _split(kx)
        s = _dot(kh, qih) + (_dot(kl, qih) + _dot(kh, qil))
        yield
        sc = jnp.maximum(s, 0.0) * w
        for sh in shifts:
            sc = sc + pltpu.roll(sc, sh, 1)
            yield
        if cap is not None:
            sc = jnp.minimum(sc, cap)
        _store_keys(sc, j, key_ref, pln_ref, eq_ref)

    def together(*gens):
        gens = list(gens)
        while gens:
            gens = [g for g in gens if next(g, gens) is not gens]

    def score_body(i, c):
        cached_idx = lambda j: cki_ref[0, pl.ds(pl.multiple_of(j * tk, tk), tk), :]
        together(scores(cached_idx(2 * i), 2 * i, None), scores(cached_idx(2 * i + 1), 2 * i + 1, None))
        return c

    lax.fori_loop(0, nkc // 2, score_body, 0)
    together(scores(nki_ref[0], nkc, cap_ref[...]))
    thr, pos_thr = _radix_threshold(pln_ref, eq_ref, nkc + 1, tq, topk)

    m_ref[...] = jnp.full(m_ref.shape, NEG, F32)
    l_ref[...] = jnp.zeros(l_ref.shape, F32)
    acc_ref[...] = jnp.zeros(acc_ref.shape, F32)
    qa = qa_ref[0]
    row = lax.broadcasted_iota(I32, (tk, tq), 0)
    ones8 = jnp.ones((8, tk), BF16)
    tn = (((0,), (0,)), ((), ()))

    def logits(k_bf, j, bias):
        sel = key_ref[j] > jnp.where(row <= pos_thr - j * tk, thr - 1, thr)
        lg = _dot(k_bf, qa)
        if bias is not None:
            lg = lg + bias
        return jnp.where(sel, lg, NEG).astype(BF16)

    def consume(lg, v_bf):
        part = jnp.max(lg.reshape(tk // 16, 16, tq), axis=0).astype(F32)
        m_old = m_ref[0:1, :]
        m_new = jnp.maximum(m_old, jnp.max(part, axis=0, keepdims=True))
        alpha = jnp.exp2(m_old - m_new)
        p = jnp.exp2(lg - m_new.astype(BF16))
        l_ref[0:1, :] = alpha * l_ref[0:1, :] + _dot(ones8, p)[0:1]
        acc_ref[...] = alpha * acc_ref[...] + lax.dot_general(v_bf, p, tn, preferred_element_type=F32)
        m_ref[0:1, :] = m_new

    def cached_pair(j0, bias1):
        r0 = pl.ds(pl.multiple_of(j0 * tk, tk), tk)
        r1 = pl.ds(pl.multiple_of(j0 * tk + tk, tk), tk)
        lg0 = logits(ck_ref[0, r0, :].astype(BF16), j0, None)
        lg1 = logits(ck_ref[0, r1, :].astype(BF16), j0 + 1, bias1)
        consume(lg0, cv_ref[0, r0, :].astype(BF16))
        consume(lg1, cv_ref[0, r1, :].astype(BF16))

    def far(i, c):
        cached_pair(2 * i, None)
        return c

    lax.fori_loop(0, nkc // 2 - 1, far, 0)
    cached_pair(nkc - 2, bias_ref[0])
    consume(logits(nk_ref[0], nkc, bias_ref[1]), nv_ref[0])
    o = acc_ref[...] / l_ref[0:1, :] * blk_ref[...]
    for sh in shifts:
        o = o + pltpu.roll(o, sh, 1)
    o_ref[0] = o.astype(BF16)


def _dsa_step(qT, qihT, qilT, wiT, kbf, va, ki_new, past_k, past_v, past_kidx, rel_bias, *, b, t, topk):
    past = past_k.shape[1]
    tk, tq = 256, HA * t
    assert tq == 128 and past % (2 * tk) == 0 and t <= tk
    nkc = past // tk
    nk = nkc + 1
    per_b = lambda a: jnp.swapaxes(a[0].reshape(a.shape[1], b, t), 0, 1)
    blk = (np.arange(WA)[:, None] // DH == np.arange(tq)[None, :] // t).astype(np.float32)
    qa = (jnp.tile(per_b(qT), (1, 1, HA)) * jnp.asarray(blk, BF16))
    lanes = lambda a: jnp.swapaxes(per_b(a).reshape(b, HA, D_IDX, t), 1, 2).reshape(b, D_IDX, tq)
    w = jnp.broadcast_to(per_b(wiT).reshape(b, 1, tq), (b, 8, tq))
    pad_rows = lambda a: jnp.pad(a.reshape(b, t, a.shape[-1]), ((0, 0), (0, tk - t), (0, 0)))
    adm = np.broadcast_to(np.arange(tk)[:, None] < t, (tk, tq)).astype(np.float32)
    bias = _bias_tables(rel_bias, adm[:, :t], tk, t, t)[1:]
    bias = jnp.swapaxes(bias, 1, 2).reshape(2, tk, tq)
    cap = jnp.asarray(np.where(adm > 0.5, np.inf, NEG), F32)
    bspec = lambda s: pl.BlockSpec((1,) + s, lambda bi: (bi,) + (0,) * len(s))
    return pl.pallas_call(
        functools.partial(_dsa_step_kernel, nkc=nkc, t=t, topk=topk),
        grid=(b,),
        in_specs=[bspec((WA, tq)), bspec((D_IDX, tq)), bspec((D_IDX, tq)), bspec((8, tq)),
                  bspec((past, WA)), bspec((past, WA)), bspec((past, D_IDX)),
                  bspec((tk, WA)), bspec((tk, WA)), bspec((tk, D_IDX)),
                  _const_spec((2, tk, tq)), _const_spec((tk, tq)), _const_spec((WA, tq))],
        out_specs=bspec((WA, tq)),
        out_shape=jax.ShapeDtypeStruct((b, WA, tq), BF16),
        scratch_shapes=[pltpu.VMEM((nk, tk, tq), I32), pltpu.VMEM((33, nk + RADIX_TILES - 1, 8, tq), I32),
                        pltpu.VMEM((2, nk + RADIX_TILES - 1, 8, tq), I32), pltpu.VMEM((8, tq), F32),
                        pltpu.VMEM((8, tq), F32), pltpu.VMEM((WA, tq), F32)],
        compiler_params=_cparams(("arbitrary",)),
        name="dsa_step",
    )(qa, lanes(qihT), lanes(qilT), w, past_k.reshape(b, past, WA), past_v.reshape(b, past, WA), past_kidx,
      pad_rows(kbf), pad_rows(va.astype(BF16)), pad_rows(ki_new), bias, cap, jnp.asarray(blk))


def _gdn_pre_kernel(conv_ref, prev_ref, hist_ref, small_ref, cw_ref, alog_ref, dtb_ref,
                    wm_ref, um_ref, qe_ref, oi_ref, e_ref, ext_ref, *, rows, nvalid):
    c = CHUNK
    n = HB * c
    ext_ref[0:8, :] = jnp.where(pl.program_id(1) == 0, hist_ref[0], prev_ref[0])
    ext_ref[8:8 + rows, :] = conv_ref[0]
    cb = ext_ref[5:5 + rows, :] * cw_ref[0:1, :]
    for j in range(1, CONV_B):
        cb = cb + ext_ref[5 + j:5 + j + rows, :] * cw_ref[j:j + 1, :]
    cb = _silu(cb)

    small = small_ref[0]
    pos = lax.broadcasted_iota(I32, (rows, 1), 0) % c
    rowv = (pos < nvalid).astype(F32)
    beta_all = _sigmoid(small) * rowv
    sp = small + dtb_ref[...]
    g_all = -jnp.exp(alog_ref[...]) * (jnp.maximum(sp, 0.0) + jnp.log(1.0 + jnp.exp(-jnp.abs(sp)))) * rowv

    ri = lax.broadcasted_iota(I32, (n, n), 0)
    ci = lax.broadcasted_iota(I32, (n, n), 1)
    same = (ri // c) == (ci // c)
    tri_b = (same & (ci <= ri)).astype(BF16)
    bd_f = same.astype(F32)
    bd_b = same.astype(BF16)
    wr = lax.broadcasted_iota(I32, (c, n), 0)
    wl = lax.broadcasted_iota(I32, (c, n), 1)
    grp = wl // c
    tri_w = (wl % c) <= wr
    strict_w = (wl % c) < wr
    eye_w = ((wl % c) == wr).astype(F32)
    nt = (((1,), (1,)), ((), ()))
    tn = (((0,), (0,)), ((), ()))

    def l2n(x):
        return x * lax.rsqrt(jnp.sum(x * x, axis=-1, keepdims=True) + EPS)

    def to_wide(full):
        out = jnp.where(grp == 0, full[0:c, :], 0.0)
        for h in range(1, HB):
            out = out + jnp.where(grp == h, full[h * c:(h + 1) * c, :], 0.0)
        return out

    def tile4(x):
        return jnp.concatenate([x] * HB, axis=0)

    def chunk(k0):
        def stack(fn):
            return jnp.concatenate([fn(h) for h in range(HB)], axis=0)

        rv = rowv[k0:k0 + c]
        q = stack(lambda h: l2n(cb[k0:k0 + c, h * DK:(h + 1) * DK]) * (DK ** -0.5))
        k = stack(lambda h: l2n(cb[k0:k0 + c, WB + h * DK:WB + (h + 1) * DK]) * rv)
        v = stack(lambda h: cb[k0:k0 + c, 2 * WB + h * DV:2 * WB + (h + 1) * DV] * rv)
        beta = stack(lambda h: beta_all[k0:k0 + c, _L_BB + h:_L_BB + h + 1])
        g = stack(lambda h: g_all[k0:k0 + c, _L_AB + h:_L_AB + h + 1])
        yield

        gh, gl = _split(jnp.broadcast_to(g, (n, 128)))
        gcum = _dot(tri_b, gh) + _dot(tri_b, gl)
        yield
        gcum_row = gcum.T[0:1, :]
        gcum_col = gcum[:, 0:1]
        col_w = to_wide(jnp.concatenate([gcum, gcum], axis=1))
        decay_w = jnp.where(tri_w, jnp.exp(jnp.where(tri_w, col_w - gcum_row, 0.0)), 0.0)
        kb = k * beta
        kbf = k.astype(BF16)
        kk_w = to_wide(lax.dot_general(kb.astype(BF16), kbf, nt, preferred_element_type=F32))
        qk_w = to_wide(lax.dot_general(q.astype(BF16), kbf, nt, preferred_element_type=F32))
        nmat_w = jnp.where(strict_w, kk_w * decay_w, 0.0)
        attn_w = jnp.where(tri_w, qk_w * decay_w, 0.0)
        yield

        inv_w = eye_w - nmat_w
        ph = nmat_w.astype(BF16)
        bh = tile4(ph) * bd_b
        for _ in range(int(np.log2(c)) - 1):
            ph = _dot(ph, bh).astype(BF16)
            yield
            bh = tile4(ph) * bd_b
            inv_w = inv_w + _dot(inv_w.astype(BF16), bh)
            yield
        rhs = jnp.concatenate([v * beta, kb * jnp.exp(gcum_col)], axis=1)
        sol = _dot3(tile4(inv_w) * bd_f, rhs)
        yield
        solb = sol.astype(BF16)
        aw = _dot((tile4(attn_w) * bd_f).astype(BF16), solb)
        yield
        ck = k0 // c
        oi_ref[0, ck] = aw[:, :DV]
        qe_ref[0, ck] = (q * jnp.exp(gcum_col) - aw[:, DV:]).astype(BF16)
        for h in range(HB):
            rs = slice(h * c, (h + 1) * c)
            g_last = gcum[(h + 1) * c - 1:(h + 1) * c, :]
            kd = (k[rs] * jnp.exp(g_last - gcum[rs])).astype(BF16)
            uw = lax.dot_general(kd, solb[rs], tn, preferred_element_type=F32)
            um_ref[0, ck, h] = uw[:, :DV]
            wm_ref[0, ck, h] = uw[:, DV:].astype(BF16)
            e_ref[0, ck, h:h + 1, :] = jnp.exp(g_last)
        e_ref[0, ck, HB:8, :] = jnp.zeros((8 - HB, 128), F32)

    gens = [chunk(k0) for k0 in range(0, rows, c)]
    while gens:
        alive = []
        for gen in gens:
            try:
                next(gen)
                alive.append(gen)
            except StopIteration:
                pass
        gens = alive


def _gdn_scan_kernel(wm_ref, um_ref, qe_ref, oi_ref, e_ref, gb_ref, s0_ref, ng_ref, ob_ref, sfin_ref, s_ref,
                     *, bb, g):
    c = CHUNK

    @pl.when(pl.program_id(1) == 0)
    def _():
        s_ref[...] = s0_ref[...]

    ng = ng_ref[...]
    for b in range(bb):
        for ck in range(g):
            for h in range(HB):
                s = s_ref[b, h]
                sb = s.astype(BF16)
                rs = slice(h * c, (h + 1) * c)
                o = _dot(qe_ref[b, ck, rs, :], sb) + oi_ref[b, ck, rs, :]
                s_ref[b, h] = e_ref[b, ck, h:h + 1, :] * s + (um_ref[b, ck, h] - _dot(wm_ref[b, ck, h], sb))
                gate = gb_ref[b, ck * c:(ck + 1) * c, h * DV:(h + 1) * DV]
                ob_ref[b, ck * c:(ck + 1) * c, h * DV:(h + 1) * DV] = (_rms(o, ng) * _silu(gate)).astype(BF16)
    sfin_ref[...] = s_ref[...]


def _gdn(conv_in, hist, gb, small, s0, conv_w, a_log, dt_bias, norm_gdn, nvalid):
    b, t, _ = conv_in.shape
    assert t % CHUNK == 0
    rows = 256 if t % 256 == 0 else CHUNK
    nc, cps = t // CHUNK, rows // CHUNK
    n = HB * CHUNK
    alog = jnp.zeros((1, 128), F32).at[0, _L_AB:_L_AB + HB].set(a_log)
    dtb = jnp.zeros((1, 128), F32).at[0, _L_AB:_L_AB + HB].set(dt_bias)
    row = lambda c_: pl.BlockSpec((1, rows, c_), lambda bi, ti: (bi, ti, 0))
    prev = pl.BlockSpec((1, 8, C_CONV_B), lambda bi, ti: (bi, jnp.maximum(ti * (rows // 8) - 1, 0), 0))
    per_b = lambda s: pl.BlockSpec((1,) + s, lambda bi, ti: (bi,) + (0,) * len(s))
    chunked = lambda s: pl.BlockSpec((1, cps) + s, lambda bi, ti: (bi, ti) + (0,) * len(s))
    wm, um, qe, oi, e = pl.pallas_call(
        functools.partial(_gdn_pre_kernel, rows=rows, nvalid=nvalid),
        grid=(b, t // rows),
        in_specs=[row(C_CONV_B), prev, per_b((8, C_CONV_B)), row(128),
                  _const_spec((CONV_B, C_CONV_B)), _const_spec((1, 128)), _const_spec((1, 128))],
        out_specs=[chunked((HB, DK, DV)), chunked((HB, DK, DV)), chunked((n, DK)), chunked((n, DV)),
                   chunked((8, 128))],
        out_shape=[jax.ShapeDtypeStruct((b, nc, HB, DK, DV), BF16), jax.ShapeDtypeStruct((b, nc, HB, DK, DV), F32),
                   jax.ShapeDtypeStruct((b, nc, n, DK), BF16), jax.ShapeDtypeStruct((b, nc, n, DV), F32),
                   jax.ShapeDtypeStruct((b, nc, 8, 128), F32)],
        scratch_shapes=[pltpu.VMEM((8 + rows, C_CONV_B), F32)],
        compiler_params=_cparams(("arbitrary", "arbitrary")),
        name="gdn_pre",
    )(conv_in, conv_in, hist, small, conv_w, alog, dtb)

    bb = 2 if b % 2 == 0 else 1
    g = 2 if nc % 2 == 0 else 1
    blk = lambda s: pl.BlockSpec((bb, g) + s, lambda bi, ci: (bi, ci) + (0,) * len(s))
    rowb = pl.BlockSpec((bb, g * CHUNK, WB), lambda bi, ci: (bi, ci, 0))
    state = pl.BlockSpec((bb, HB, DK, DV), lambda bi, ci: (bi, 0, 0, 0))
    return pl.pallas_call(
        functools.partial(_gdn_scan_kernel, bb=bb, g=g),
        grid=(b // bb, nc // g),
        in_specs=[blk((HB, DK, DV)), blk((HB, DK, DV)), blk((n, DK)), blk((n, DV)), blk((8, 128)), rowb, state,
                  _const_spec((1, DV))],
        out_specs=[rowb, state],
        out_shape=[jax.ShapeDtypeStruct((b, t, WB), BF16), jax.ShapeDtypeStruct((b, HB, DK, DV), F32)],
        scratch_shapes=[pltpu.VMEM((bb, HB, DK, DV), F32)],
        compiler_params=_cparams(("arbitrary", "arbitrary")),
        name="gdn_scan",
    )(wm, um, qe, oi, e, gb, s0, norm_gdn.reshape(1, DV))


def _post_kernel(x_ref, oaT_ref, ob_ref, ga_ref, gbr_ref, p_ref, hist_ref, wa_ref, wb_ref, wo_ref, nf_ref,
                 wup_ref, cw_ref, wdn_ref, npl_ref, wpg_ref, wple_ref, nfin_ref, y_ref, tail_ref, ext_ref, *, tm):
    @pl.when(pl.program_id(1) == 0)
    def _():
        ext_ref[0:8, :] = hist_ref[0]

    ya = lax.dot_general(oaT_ref[0], wa_ref[...], (((0,), (0,)), ((), ())), preferred_element_type=F32)
    yb = _dot(ob_ref[0], wb_ref[...])
    mix = _sigmoid(ga_ref[0]) * ya + _sigmoid(gbr_ref[0]) * yb
    x1 = x_ref[0] + _dot(mix.astype(BF16), wo_ref[...])
    h2 = _rms(x1, nf_ref[...]).astype(BF16)
    ext_ref[8:8 + tm, :] = _dot(h2, wup_ref[:, 0:D_FF])
    u_val = _dot(h2, wup_ref[:, D_FF:2 * D_FF])
    cv = ext_ref[6:6 + tm, :] * cw_ref[0:1, :]
    for j in range(1, CONV_F):
        cv = cv + ext_ref[6 + j:6 + j + tm, :] * cw_ref[j:j + 1, :]
    tail = ext_ref[tm:tm + 8, :]
    ext_ref[0:8, :] = tail
    tail_ref[0] = tail
    act = 0.5 * cv * (1.0 + jnp.tanh(0.7978845608028654 * (cv + 0.044715 * (cv * cv * cv))))
    x2 = x1 + _dot((act * u_val).astype(BF16), wdn_ref[...])
    gate = _sigmoid(_dot(_rms(x2, npl_ref[...]).astype(BF16), wpg_ref[...]))
    x3 = x2 + gate * _dot(p_ref[0].astype(BF16), wple_ref[...])
    y_ref[0] = _rms(x3, nfin_ref[...])


def _post(x, oaT, ob, ga, gbr, p, hist, w_proj_a, w_proj_b, w_out, norm_ffn, w_up, conv_ffn, w_down, norm_ple,
          w_ple_gate, w_ple, norm_final, tm):
    b, t, _ = x.shape
    nt = t // tm
    assert t % tm == 0 and tm >= 8 and oaT.shape == (b * nt, WA, tm)
    row = lambda c: pl.BlockSpec((1, tm, c), lambda bi, ti: (bi, ti, 0))
    per_b = pl.BlockSpec((1, 8, D_FF), lambda bi, ti: (bi, 0, 0))
    vec = _const_spec((1, D_MODEL))
    return pl.pallas_call(
        functools.partial(_post_kernel, tm=tm),
        grid=(b, nt),
        in_specs=[row(D_MODEL), pl.BlockSpec((1, WA, tm), lambda bi, ti: (bi * nt + ti, 0, 0)), row(WB),
                  row(D_MODEL), row(D_MODEL), row(D_PLE), per_b,
                  _const_spec((WA, D_MODEL)), _const_spec((WB, D_MODEL)), _const_spec((D_MODEL, D_MODEL)), vec,
                  _const_spec((D_MODEL, 2 * D_FF)), _const_spec((CONV_F, D_FF)), _const_spec((D_FF, D_MODEL)),
                  vec, _const_spec((D_MODEL, D_MODEL)), _const_spec((D_PLE, D_MODEL)), vec],
        out_specs=[row(D_MODEL), per_b],
        out_shape=[jax.ShapeDtypeStruct((b, t, D_MODEL), F32), jax.ShapeDtypeStruct((b, 8, D_FF), F32)],
        scratch_shapes=[pltpu.VMEM((8 + tm, D_FF), F32)],
        compiler_params=_cparams(("arbitrary", "arbitrary")),
        name="post",
    )(x, oaT, ob, ga, gbr, p, hist, w_proj_a.astype(BF16), w_proj_b.astype(BF16), w_out.astype(BF16),
      norm_ffn.reshape(1, D_MODEL), w_up.astype(BF16), conv_ffn, w_down.astype(BF16),
      norm_ple.reshape(1, D_MODEL), w_ple_gate.astype(BF16), w_ple.astype(BF16), norm_final.reshape(1, D_MODEL))


def _pad_hist(hist, rows=8):
    b, r, c = hist.shape
    return jnp.concatenate([jnp.zeros((b, rows - r, c), hist.dtype), hist], axis=1)


def _layer(x, p, past_k, past_v, past_kidx, s_gdn, conv_b_hist, ffn_hist, wts, *, tm, tq):
    (norm_mix, w_in, conv_b, a_log, dt_bias, norm_gdn, w_proj_a, w_proj_b, w_out, norm_ffn, w_up, conv_ffn,
     w_down, norm_ple, w_ple, w_ple_gate, rel_bias, norm_final) = wts
    b, t, _ = x.shape
    n = b * t
    past = past_k.shape[1]
    topk = min(TOPK_MAX, (past + t) // 4)
    x2d = x.reshape(n, D_MODEL)
    tmi = min(tm, n)
    (ka, va, kbf, conv_in, gb, ga, gbr, small, ki3, qT, qihT, qilT, wiT, vTa) = _in_proj(x2d, norm_mix, w_in, tmi)

    if past == 0:
        assert tq == tmi and t % tq == 0 and tq % CHUNK == 0 and tq >= topk
        nq = t // tq
        kj = np.arange(tq)[:, None]
        adm = ((kj // CHUNK) <= (np.arange(tq)[None, :] // CHUNK)).astype(np.float32)
        oT = _dsa(qT, qihT, qilT, wiT, kbf.reshape(b, nq, tq, WA), vTa.reshape(b, nq, HA * V_ROWS, tq),
                  ki3.reshape(b, nq, tq, 256), _bias_tables(rel_bias, adm, tq, tq, tq), adm,
                  nq=nq, causal=True, topk=topk)
    else:
        assert n == tmi
        oT = _dsa_step(qT, qihT, qilT, wiT, kbf, va.reshape(n, WA), small[:, :D_IDX], past_k, past_v, past_kidx, rel_bias,
                       b=b, t=t, topk=topk)
        oT = oT[:, :, :t]

    tp = -(-t // CHUNK) * CHUNK
    padt = lambda a: jnp.pad(a.reshape(b, t, a.shape[-1]), ((0, 0), (0, tp - t), (0, 0)))
    ob, s_new = _gdn(padt(conv_in), _pad_hist(conv_b_hist), padt(gb), padt(small), s_gdn, conv_b, a_log, dt_bias,
                     norm_gdn, nvalid=min(t, CHUNK))
    new_conv_b = jnp.concatenate([conv_b_hist, conv_in.reshape(b, t, C_CONV_B)], axis=1)[:, t:]

    per_bt = lambda a: a.reshape(b, t, a.shape[-1])
    y, tail = _post(x, oT, ob[:, :t], per_bt(ga), per_bt(gbr), p, _pad_hist(ffn_hist), w_proj_a, w_proj_b, w_out,
                    norm_ffn, w_up, conv_ffn, w_down, norm_ple, w_ple_gate, w_ple, norm_final, min(tm, t))
    new_ffn = tail[:, 8 - (CONV_F - 1):]
    return (y, ka.reshape(b, t, HA, DH), va.reshape(b, t, HA, DH), small[:, :D_IDX].reshape(b, t, D_IDX),
            s_new, new_conv_b, new_ffn)


def kernel(x_prompt, x_sample, p_prompt, p_sample, cache_k, cache_v, cache_kidx, state_gdn, state_gdn_conv,
           state_ffn_conv, norm_mix, w_in, conv_b, a_log, dt_bias, norm_gdn, w_proj_a, w_proj_b, w_out, norm_ffn,
           w_up, conv_ffn, w_down, norm_ple, w_ple, w_ple_gate, rel_bias, norm_final):
    assert norm_mix.shape[0] == 1
    bp = x_prompt.shape[0]
    dt = x_prompt.dtype
    wts = (norm_mix[0], w_in[0], conv_b[0], a_log[0], dt_bias[0], norm_gdn[0], w_proj_a[0], w_proj_b[0], w_out[0],
           norm_ffn[0], w_up[0], conv_ffn[0], w_down[0], norm_ple[0], w_ple[0], w_ple_gate[0], rel_bias, norm_final)
    outs_p = _layer(x_prompt, p_prompt[0], jnp.zeros((bp, 0, HA, DH), dt), jnp.zeros((bp, 0, HA, DH), dt),
                    jnp.zeros((bp, 0, D_IDX), dt), jnp.zeros((bp, HB, DK, DV), dt),
                    jnp.zeros((bp, CONV_B - 1, C_CONV_B), dt), jnp.zeros((bp, CONV_F - 1, D_FF), dt),
                    wts, tm=256, tq=256)
    outs_s = _layer(x_sample, p_sample[0], cache_k[0], cache_v[0], cache_kidx[0], state_gdn[0],
                    state_gdn_conv[0], state_ffn_conv[0], wts, tm=256, tq=256)
    yp, ys = outs_p[0], outs_s[0]
    return (yp, ys) + tuple(a[None] for a in outs_p[1:]) + tuple(a[None] for a in outs_s[1:])
```

```python
import functools

import numpy as np
import jax
import jax.numpy as jnp
from jax import lax
from jax.experimental import pallas as pl
from jax.experimental.pallas import tpu as pltpu

F32 = jnp.float32
BF16 = jnp.bfloat16
I32 = jnp.int32

D_MODEL = 1024
CHUNK = 64
HA, DH = 8, 64
H_IDX, D_IDX = 8, 64
TOPK_MAX = 256
NUM_BUCKETS, MAX_DISTANCE = 32, 128
HB, DK, DV = 4, 128, 128
CONV_B = 4
D_FF = 2816
CONV_F = 3
D_PLE = 256
EPS = 1e-6
NEG = -1e30
WA = HA * DH
WB = HB * DK
C_CONV_B = 3 * WB
INT_MIN = -2 ** 31
LOG2E = 1.4426950408889634

_O_QA, _O_KA, _O_VA, _O_QI, _O_KI, _O_WI = 0, 512, 1024, 1536, 2048, 2112
_O_QB, _O_GB, _O_BB, _O_AB, _O_GA, _O_GBR = 2120, 3656, 4168, 4172, 4176, 5200
_L_WI, _L_BB, _L_AB = 64, 72, 76

VMEM_LIMIT = 56 * 1024 * 1024


def _cparams(sem):
    return pltpu.CompilerParams(dimension_semantics=sem, vmem_limit_bytes=VMEM_LIMIT)


def _const_spec(shape):
    nd = len(shape)
    return pl.BlockSpec(shape, lambda *_: (0,) * nd, pipeline_mode=pl.Buffered(1))


def _rms(x, g):
    return x * lax.rsqrt(jnp.mean(x * x, axis=-1, keepdims=True) + EPS) * g


def _split(x):
    hi = x.astype(BF16)
    lo = (x - hi.astype(F32)).astype(BF16)
    return hi, lo


def _dot(a, b):
    return jnp.dot(a, b, preferred_element_type=F32)


def _dot3(a, b):
    ah, al = _split(a)
    bh, bl = _split(b)
    return _dot(ah, bh) + (_dot(al, bh) + _dot(ah, bl))


def _sigmoid(x):
    return 1.0 / (1.0 + jnp.exp(-x))


def _silu(x):
    return x * _sigmoid(x)


def _transpose32(a):
    a = list(a)
    j, m = 16, 0x0000FFFF
    while j:
        k = 0
        while k < 32:
            t = (a[k] ^ lax.shift_right_logical(a[k + j], jnp.int32(j))) & jnp.int32(m - (1 << 32) if m >> 31 else m)
            a[k] = a[k] ^ t
            a[k + j] = a[k + j] ^ (t << j)
            k = (k + j + 1) & ~j
        j >>= 1
        m = (m ^ (m << j)) & 0xFFFFFFFF
    return a


V_ROWS = DH + 16
RADIX_TILES = 8
_NT = (((1,), (1,)), ((), ()))


def _in_proj_kernel(x_ref, g_ref, wm_ref, wt_ref, wqh_ref, wql_ref, wsh_ref, wsl_ref, wwh_ref, wwl_ref,
                    ph_ref, plo_ref, sc_ref,
                    ka_ref, va_ref, kbf_ref, conv_ref, gb_ref, ga_ref, gbr_ref, small_ref, ki3_ref,
                    qT_ref, qihT_ref, qilT_ref, wiT_ref, vTa_ref):
    tm = x_ref.shape[0]
    h = _rms(x_ref[...], g_ref[...])
    hh, hl = _split(h)

    def main(lo, hi):
        return _dot(hh, wm_ref[:, lo:hi])

    def nt(w, a):
        return lax.dot_general(w, a, _NT, preferred_element_type=F32)

    def nt3(wh_ref, wl_ref):
        return nt(wh_ref[...], hh) + (nt(wh_ref[...], hl) + nt(wl_ref[...], hh))

    ka = main(0, 512)
    va = main(512, 1024)
    for hd in range(HA):
        ka_ref[:, hd, :] = ka[:, hd * DH:(hd + 1) * DH]
        va_ref[:, hd, :] = va[:, hd * DH:(hd + 1) * DH]
    kbf_ref[...] = ka.astype(BF16)
    conv_ref[...] = main(1024, 2560)
    gb_ref[...] = main(2560, 3072)
    ga_ref[...] = main(3072, 4096)
    gbr_ref[...] = main(4096, 5120)

    qT_ref[0] = nt(wt_ref[0:WA, :], hh).astype(BF16)
    vT = nt(wt_ref[WA:2 * WA, :], hh).astype(BF16)
    ones = jnp.ones((V_ROWS - DH, tm), BF16)
    for hd in range(HA):
        vTa_ref[0, hd * V_ROWS:hd * V_ROWS + DH, :] = vT[hd * DH:(hd + 1) * DH]
        vTa_ref[0, hd * V_ROWS + DH:(hd + 1) * V_ROWS, :] = ones
    qh, ql = _split(nt3(wqh_ref, wql_ref))
    qihT_ref[0] = qh
    qilT_ref[0] = ql
    wiT_ref[0] = nt3(wwh_ref, wwl_ref)[0:H_IDX] * (H_IDX ** -0.5)

    small = (_dot(hh, wsh_ref[...]) + (_dot(hl, wsh_ref[...]) + _dot(hh, wsl_ref[...]))) * sc_ref[...]
    small_ref[...] = small
    sh, sl = _split(small)
    ki3_ref[...] = (_dot(sh, ph_ref[...]) + _dot(sl, plo_ref[...])).astype(BF16)


def _in_proj(x2d, norm_mix, w_in, tm):
    n = x2d.shape[0]
    assert n % tm == 0
    g = n // tm
    w = w_in
    wm = jnp.concatenate([w[:, _O_KA:_O_QI], w[:, _O_QB:_O_BB], w[:, _O_GA:]], axis=1).astype(BF16)
    wt = jnp.concatenate([w[:, _O_QA:_O_KA] * (DH ** -0.5 * LOG2E), w[:, _O_VA:_O_QI]], axis=1).T.astype(BF16)
    hilo = lambda a: (a.astype(BF16), (a - a.astype(BF16).astype(F32)).astype(BF16))
    wqh, wql = hilo((w[:, _O_QI:_O_KI] * (D_IDX ** -0.5)).T)
    wsh, wsl = hilo(jnp.concatenate([w[:, _O_KI:_O_QB], w[:, _O_BB:_O_GA], jnp.zeros((D_MODEL, 48), F32)], axis=1))
    wwh, wwl = hilo(jnp.concatenate([w[:, _O_WI:_O_QB], jnp.zeros((D_MODEL, 8), F32)], axis=1).T)
    ph = np.zeros((128, 256), np.float32)
    plo = np.zeros((128, 256), np.float32)
    for c in range(64):
        ph[c, c] = 1.0
        ph[c, 64 + c] = 1.0
        plo[c, 128 + c] = 1.0
    sc = np.ones((1, 128), np.float32)
    sc[0, _L_WI:_L_WI + H_IDX] = H_IDX ** -0.5
    row = lambda c: pl.BlockSpec((tm, c), lambda i: (i, 0))
    colT = lambda r: pl.BlockSpec((1, r, tm), lambda i: (i, 0, 0))
    heads = pl.BlockSpec((tm, HA, DH), lambda i: (i, 0, 0))
    out_cols = [(512, BF16), (1536, F32), (512, F32), (1024, F32), (1024, F32), (128, F32), (256, BF16)]
    out_rows = [(WA, BF16), (WA, BF16), (WA, BF16), (H_IDX, F32), (HA * V_ROWS, BF16)]
    return pl.pallas_call(
        _in_proj_kernel,
        grid=(g,),
        in_specs=[row(D_MODEL), _const_spec((1, D_MODEL)), _const_spec((D_MODEL, 5120)),
                  _const_spec((2 * WA, D_MODEL)), _const_spec((WA, D_MODEL)), _const_spec((WA, D_MODEL)),
                  _const_spec((D_MODEL, 128)), _const_spec((D_MODEL, 128)),
                  _const_spec((16, D_MODEL)), _const_spec((16, D_MODEL)),
                  _const_spec((128, 256)), _const_spec((128, 256)), _const_spec((1, 128))],
        out_specs=[heads, heads] + [row(c) for c, _ in out_cols] + [colT(r) for r, _ in out_rows],
        out_shape=[jax.ShapeDtypeStruct((n, HA, DH), F32)] * 2
        + [jax.ShapeDtypeStruct((n, c), d) for c, d in out_cols]
        + [jax.ShapeDtypeStruct((g, r, tm), d) for r, d in out_rows],
        compiler_params=_cparams(("arbitrary",)),
        name="in_proj",
    )(x2d, norm_mix.reshape(1, D_MODEL), wm, wt, wqh, wql, wsh, wsl, wwh, wwl, jnp.asarray(ph, BF16),
      jnp.asarray(plo, BF16), jnp.asarray(sc))


def _store_keys(score, j, key_ref, pln_ref, eq_ref):
    tk, tq = score.shape
    bits = pltpu.bitcast(score, I32)
    bits = jnp.where(bits == INT_MIN, 0, bits)
    key = bits ^ ((bits >> 31) & 0x7FFFFFFF)
    key_ref[j] = key
    u3 = (key ^ INT_MIN).reshape(tk // 8, 8, tq)
    ones = jnp.full((8, tq), -1, I32)
    pln_ref[0, j] = ones
    for l0 in range(0, tq, 128):
        planes = _transpose32([u3[r][:, l0:l0 + 128] for r in range(32)])
        for b in range(32):
            pln_ref[b + 1, j, :, l0:l0 + 128] = planes[b]
    eq_ref[0, j] = ones


def _radix_threshold(pln_ref, eq_ref, nk, tq, topk):
    for d in range(RADIX_TILES - 1):
        pln_ref[:, nk + d] = jnp.zeros((33, 8, tq), I32)
        eq_ref[0, nk + d] = jnp.zeros((8, tq), I32)

    sub = lax.broadcasted_iota(I32, (8, tq), 0)
    group_masks = (-0x10000, -0xFF0100, -0xF0F0F10, -0x33333334, -0x55555556)

    def pos_plane(e, j):
        if e < 6:
            return jnp.broadcast_to(-((~j >> (5 - e)) & 1), (8, tq))
        if e < 11:
            return jnp.full((8, tq), group_masks[e - 6], I32)
        return -((~sub >> (13 - e)) & 1)

    def sweep(prev, cur, carry, src, dst):
        n_gt, flip = carry

        def body(jq, cnt):
            for d in range(RADIX_TILES):
                j = jq * RADIX_TILES + d
                e = eq_ref[src, j] & (prev(j) ^ flip)
                eq_ref[dst, j] = e
                cnt = cnt + lax.population_count(e & cur(j))
            return cnt

        cnt = lax.fori_loop(0, (nk + RADIX_TILES - 1) // RADIX_TILES, body, jnp.zeros((8, tq), I32))
        cnt = n_gt + cnt.sum(axis=0, keepdims=True)
        acc = cnt >= topk
        return acc, (jnp.where(acc, n_gt, cnt), jnp.where(acc, 0, -1))

    def key_sweep(it, carry, src, dst):
        tu, rest = carry
        acc, rest = sweep(lambda j: pln_ref[it, j], lambda j: pln_ref[it + 1, j], rest, src, dst)
        return jnp.where(acc, tu | jnp.left_shift(jnp.int32(1), 31 - it), tu), rest

    def two_bits(i, carry):
        return key_sweep(2 * i + 1, key_sweep(2 * i, carry, 0, 1), 1, 0)

    zero_row = jnp.zeros((1, tq), I32)
    tu, rest = lax.fori_loop(0, 16, two_bits, (zero_row, (zero_row, zero_row)))
    inv_pos = zero_row
    for e in range(14):
        prev = (lambda j: pln_ref[32, j]) if e == 0 else functools.partial(pos_plane, e - 1)
        acc, rest = sweep(prev, functools.partial(pos_plane, e), rest, e % 2, 1 - e % 2)
        inv_pos = jnp.where(acc, inv_pos | (1 << (13 - e)), inv_pos)
    return tu ^ INT_MIN, ~inv_pos & 0x3FFF


def _dsa_kernel(qT_ref, qihT_ref, qilT_ref, wiT_ref, k_ref, vT_ref, ki3_ref, bias_ref, cap_ref,
                o_ref, key_ref, l_ref, acc_ref, lg_ref, pln_ref, eq_ref, sc_ref, ma_ref,
                *, tq, tk, nk_static, causal, topk):
    nk = (pl.program_id(1) + 1) if causal else nk_static
    wiT = wiT_ref[0]

    def when(cond):
        if isinstance(cond, bool):
            return (lambda f: f()) if cond else (lambda f: None)
        return pl.when(cond)

    zeros64 = jnp.zeros((64, tq), BF16)
    q3 = []
    for h in range(H_IDX):
        hi = qihT_ref[0, h * 64:(h + 1) * 64, :]
        lo = qilT_ref[0, h * 64:(h + 1) * 64, :]
        q3.append(jnp.concatenate([hi, lo, hi, zeros64], axis=0))

    def score_matmuls(j0, n):
        kts = [ki3_ref[0, j0 + d] for d in range(n)]
        accs = [None] * n
        for h in range(H_IDX):
            for d in range(n):
                t = jnp.maximum(_dot(kts[d], q3[h]), 0.0) * wiT[h:h + 1, :]
                accs[d] = t if accs[d] is None else accs[d] + t
        for d in range(n):
            sc_ref[d] = jnp.minimum(accs[d], cap_ref[jnp.where(j0 + d == nk - 1, 1, 0)])

    def score_finish(j0, n):
        for d in range(n):
            _store_keys(sc_ref[d], j0 + d, key_ref, pln_ref, eq_ref)

    def score_body(i, c):
        score_finish(2 * i - 2, 2)
        score_matmuls(2 * i, 2)
        return c

    @when(nk >= 2)
    def _():
        score_matmuls(0, 2)
        lax.fori_loop(1, nk // 2, score_body, 0)
        score_finish(2 * (nk // 2) - 2, 2)

    @when(nk % 2 == 1)
    def _():
        score_matmuls(nk - 1, 1)
        score_finish(nk - 1, 1)

    thr, pos_thr = _radix_threshold(pln_ref, eq_ref, nk, tq, topk)

    l_ref[...] = jnp.zeros(l_ref.shape, F32)
    acc_ref[...] = jnp.zeros(acc_ref.shape, F32)
    qm = []
    for h in range(HA):
        qh = qT_ref[0, h * 64:(h + 1) * 64, :]
        qm.append(jnp.concatenate([qh, zeros64] if h % 2 == 0 else [zeros64, qh], axis=0))
    row = lax.broadcasted_iota(I32, (tk, tq), 0)

    def tile_logits(j, slot, far):
        sel = key_ref[j] > jnp.where(row <= pos_thr - j * tk, thr - 1, thr)
        kind = jnp.clip(j - (nk - 3), 0, 2)
        mx = []
        for h in range(HA):
            pr = h // 2
            lg = _dot(k_ref[0, j, :, pr * 128:(pr + 1) * 128], qm[h])
            if not far:
                lg = lg + bias_ref[kind, h]
            lg = jnp.where(sel, lg, NEG).astype(BF16)
            lg_ref[slot, h] = lg
            part = jnp.max(lg.reshape(tk // 16, 16, tq), axis=0).astype(F32)
            mx.append(jnp.max(part, axis=0, keepdims=True))
        return jnp.concatenate(mx, axis=0)

    def tile_values(j, slot, mx):
        m_old = ma_ref[...]
        m_new = jnp.maximum(m_old, mx)
        alpha = jnp.exp2(m_old - m_new)
        ma_ref[...] = m_new
        ls = []
        for h in range(HA):
            p = jnp.exp2(lg_ref[slot, h] - m_new[h:h + 1, :].astype(BF16))
            pv = _dot(vT_ref[0, j, h * V_ROWS:(h + 1) * V_ROWS, :], p)
            acc_ref[h * 64:(h + 1) * 64, :] = alpha[h:h + 1, :] * acc_ref[h * 64:(h + 1) * 64, :] + pv[0:DH]
            ls.append(pv[DH:DH + 1])
        l_ref[...] = alpha * l_ref[...] + jnp.concatenate(ls, axis=0)

    ma_ref[...] = jnp.full((HA, tq), NEG, F32)

    def pair(j0, far):
        mx0 = tile_logits(j0, 0, far)
        mx1 = tile_logits(j0 + 1, 1, far)
        tile_values(j0, 0, mx0)
        tile_values(j0 + 1, 1, mx1)

    def far_pair(i, c):
        pair(2 * i, True)
        return c

    n_far = jnp.maximum(nk - 2, 0) // 2
    lax.fori_loop(0, n_far, far_pair, 0)

    @when(nk >= 2)
    def _():
        pair(2 * n_far, False)

    @when(nk % 2 == 1)
    def _():
        tile_values(nk - 1, 0, tile_logits(nk - 1, 0, False))

    for h in range(HA):
        o_ref[0, h * 64:(h + 1) * 64, :] = (acc_ref[h * 64:(h + 1) * 64, :] / l_ref[h:h + 1, :]).astype(BF16)


def _t5_bucket_np(rel):
    half = NUM_BUCKETS // 2
    max_exact = half // 2
    out = np.zeros(rel.shape, np.int64)
    flat_rel = rel.reshape(-1)
    flat = out.reshape(-1)
    for a in range(flat_rel.size):
        r = int(flat_rel[a])
        n = abs(r)
        b = n if n < max_exact else min(half - 1, (n * n).bit_length() + 1)
        flat[a] = b + (half if r > 0 else 0)
    return out


def _bias_tables(rel_bias, adm, tk, tq, tq_valid):
    kj = np.arange(tk)[:, None]
    t = np.minimum(np.arange(tq), tq_valid - 1)[None, :]
    rel_to_bucket = _t5_bucket_np(np.arange(-2 * tk - tq, tk + 1))
    lut = lambda rel: rel_to_bucket[rel + 2 * tk + tq]
    far_bucket = NUM_BUCKETS // 2 - 1
    idx = np.stack([np.full((tk, tq), far_bucket), lut(kj - tk - t), lut(kj - t)], axis=0)
    tab = rel_bias.astype(F32) * LOG2E
    onehot = jnp.asarray(idx[..., None] == np.arange(NUM_BUCKETS), F32)
    bias = jnp.einsum("ktqb,bh->khtq", onehot, tab - tab[far_bucket], precision=lax.Precision.HIGHEST)
    mask = np.zeros((3, 1, tk, tq), np.float32)
    mask[2, 0] = np.where(adm > 0.5, 0.0, NEG)
    return bias + jnp.asarray(mask)


def _dsa(qT, qihT, qilT, wiT, k4, vT4, ki34, bias, adm, *, nq, causal, topk):
    g, _, tq = qT.shape
    b, nk, tk, _ = k4.shape
    assert g == b * nq and tk == 256 and nk + RADIX_TILES - 1 <= 64
    cap = jnp.asarray(np.stack([np.full((tk, tq), np.inf, np.float32), np.where(adm > 0.5, np.inf, NEG)]), F32)
    qspec = lambda r: pl.BlockSpec((1, r, tq), lambda bi, i: (bi * nq + i, 0, 0))
    kspec = lambda s: pl.BlockSpec((1,) + s, lambda bi, i: (bi, 0, 0, 0), pipeline_mode=pl.Buffered(1))
    kern = functools.partial(_dsa_kernel, tq=tq, tk=tk, nk_static=nk, causal=causal, topk=topk)
    return pl.pallas_call(
        kern,
        grid=(b, nq),
        in_specs=[qspec(512), qspec(512), qspec(512), qspec(8),
                  kspec((nk, tk, 512)), kspec((nk, HA * V_ROWS, tk)), kspec((nk, tk, 256)),
                  _const_spec((3, HA, tk, tq)), _const_spec((2, tk, tq))],
        out_specs=qspec(512),
        out_shape=jax.ShapeDtypeStruct((g, 512, tq), BF16),
        scratch_shapes=[pltpu.VMEM((nk, tk, tq), I32), pltpu.VMEM((8, tq), F32),
                        pltpu.VMEM((512, tq), F32), pltpu.VMEM((2, HA, tk, tq), BF16),
                        pltpu.VMEM((33, nk + RADIX_TILES - 1, 8, tq), I32),
                        pltpu.VMEM((2, nk + RADIX_TILES - 1, 8, tq), I32),
                        pltpu.VMEM((2, tk, tq), F32), pltpu.VMEM((HA, tq), F32)],
        compiler_params=_cparams(("arbitrary", "arbitrary")),
        name="dsa_causal" if causal else "dsa_cached",
    )(qT, qihT, qilT, wiT, k4, vT4, ki34, bias, cap)


def _dsa_step_kernel(qa_ref, qih_ref, qil_ref, w_ref, ck_ref, cv_ref, cki_ref, nk_ref, nv_ref, nki_ref,
                     bias_ref, cap_ref, blk_ref, o_ref, key_ref, pln_ref, eq_ref, m_ref, l_ref, acc_ref,
                     *, nkc, t, topk):
    tk, tq = nk_ref.shape[1], qa_ref.shape[2]
    qih, qil = qih_ref[0], qil_ref[0]
    w = w_ref[0, 0:1, :]
    shifts = [t << s for s in range((tq // t).bit_length() - 2, -1, -1)]

    def scores(kx, j, cap):
        kh, kl = _split(kx)
        s = _dot(kh, qih) + (_dot(kl, qih) + _dot(kh, qil))
        yield
        sc = jnp.maximum(s, 0.0) * w
        for sh in shifts:
            sc = sc + pltpu.roll(sc, sh, 1)
            yield
        if cap is not None:
            sc = jnp.minimum(sc, cap)
        _store_keys(sc, j, key_ref, pln_ref, eq_ref)

    def together(*gens):
        gens = list(gens)
        while gens:
            gens = [g for g in gens if next(g, gens) is not gens]

    def score_body(i, c):
        cached_idx = lambda j: cki_ref[0, pl.ds(pl.multiple_of(j * tk, tk), tk), :]
        together(scores(cached_idx(2 * i), 2 * i, None), scores(cached_idx(2 * i + 1), 2 * i + 1, None))
        return c

    lax.fori_loop(0, nkc // 2, score_body, 0)
    together(scores(nki_ref[0], nkc, cap_ref[...]))
    thr, pos_thr = _radix_threshold(pln_ref, eq_ref, nkc + 1, tq, topk)

    m_ref[...] = jnp.full(m_ref.shape, NEG, F32)
    l_ref[...] = jnp.zeros(l_ref.shape, F32)
    acc_ref[...] = jnp.zeros(acc_ref.shape, F32)
    qa = qa_ref[0]
    row = lax.broadcasted_iota(I32, (tk, tq), 0)
    ones8 = jnp.ones((8, tk), BF16)
    tn = (((0,), (0,)), ((), ()))

    def logits(k_bf, j, bias):
        sel = key_ref[j] > jnp.where(row <= pos_thr - j * tk, thr - 1, thr)
        lg = _dot(k_bf, qa)
        if bias is not None:
            lg = lg + bias
        return jnp.where(sel, lg, NEG).astype(BF16)

    def consume(lg, v_bf):
        part = jnp.max(lg.reshape(tk // 16, 16, tq), axis=0).astype(F32)
        m_old = m_ref[0:1, :]
        m_new = jnp.maximum(m_old, jnp.max(part, axis=0, keepdims=True))
        alpha = jnp.exp2(m_old - m_new)
        p = jnp.exp2(lg - m_new.astype(BF16))
        l_ref[0:1, :] = alpha * l_ref[0:1, :] + _dot(ones8, p)[0:1]
        acc_ref[...] = alpha * acc_ref[...] + lax.dot_general(v_bf, p, tn, preferred_element_type=F32)
        m_ref[0:1, :] = m_new

    def cached_pair(j0, bias1):
        r0 = pl.ds(pl.multiple_of(j0 * tk, tk), tk)
        r1 = pl.ds(pl.multiple_of(j0 * tk + tk, tk), tk)
        lg0 = logits(ck_ref[0, r0, :].astype(BF16), j0, None)
        lg1 = logits(ck_ref[0, r1, :].astype(BF16), j0 + 1, bias1)
        consume(lg0, cv_ref[0, r0, :].astype(BF16))
        consume(lg1, cv_ref[0, r1, :].astype(BF16))

    def far(i, c):
        cached_pair(2 * i, None)
        return c

    lax.fori_loop(0, nkc // 2 - 1, far, 0)
    cached_pair(nkc - 2, bias_ref[0])
    consume(logits(nk_ref[0], nkc, bias_ref[1]), nv_ref[0])
    o = acc_ref[...] / l_ref[0:1, :] * blk_ref[...]
    for sh in shifts:
        o = o + pltpu.roll(o, sh, 1)
    o_ref[0] = o.astype(BF16)


def _dsa_step(qT, qihT, qilT, wiT, kbf, va, ki_new, past_k, past_v, past_kidx, rel_bias, *, b, t, topk):
    past = past_k.shape[1]
    tk, tq = 256, HA * t
    assert tq == 128 and past % (2 * tk) == 0 and t <= tk
    nkc = past // tk
    nk = nkc + 1
    per_b = lambda a: jnp.swapaxes(a[0].reshape(a.shape[1], b, t), 0, 1)
    blk = (np.arange(WA)[:, None] // DH == np.arange(tq)[None, :] // t).astype(np.float32)
    qa = (jnp.tile(per_b(qT), (1, 1, HA)) * jnp.asarray(blk, BF16))
    lanes = lambda a: jnp.swapaxes(per_b(a).reshape(b, HA, D_IDX, t), 1, 2).reshape(b, D_IDX, tq)
    w = jnp.broadcast_to(per_b(wiT).reshape(b, 1, tq), (b, 8, tq))
    pad_rows = lambda a: jnp.pad(a.reshape(b, t, a.shape[-1]), ((0, 0), (0, tk - t), (0, 0)))
    adm = np.broadcast_to(np.arange(tk)[:, None] < t, (tk, tq)).astype(np.float32)
    bias = _bias_tables(rel_bias, adm[:, :t], tk, t, t)[1:]
    bias = jnp.swapaxes(bias, 1, 2).reshape(2, tk, tq)
    cap = jnp.asarray(np.where(adm > 0.5, np.inf, NEG), F32)
    bspec = lambda s: pl.BlockSpec((1,) + s, lambda bi: (bi,) + (0,) * len(s))
    return pl.pallas_call(
        functools.partial(_dsa_step_kernel, nkc=nkc, t=t, topk=topk),
        grid=(b,),
        in_specs=[bspec((WA, tq)), bspec((D_IDX, tq)), bspec((D_IDX, tq)), bspec((8, tq)),
                  bspec((past, WA)), bspec((past, WA)), bspec((past, D_IDX)),
                  bspec((tk, WA)), bspec((tk, WA)), bspec((tk, D_IDX)),
                  _const_spec((2, tk, tq)), _const_spec((tk, tq)), _const_spec((WA, tq))],
        out_specs=bspec((WA, tq)),
        out_shape=jax.ShapeDtypeStruct((b, WA, tq), BF16),
        scratch_shapes=[pltpu.VMEM((nk, tk, tq), I32), pltpu.VMEM((33, nk + RADIX_TILES - 1, 8, tq), I32),
                        pltpu.VMEM((2, nk + RADIX_TILES - 1, 8, tq), I32), pltpu.VMEM((8, tq), F32),
                        pltpu.VMEM((8, tq), F32), pltpu.VMEM((WA, tq), F32)],
        compiler_params=_cparams(("arbitrary",)),
        name="dsa_step",
    )(qa, lanes(qihT), lanes(qilT), w, past_k.reshape(b, past, WA), past_v.reshape(b, past, WA), past_kidx,
      pad_rows(kbf), pad_rows(va.astype(BF16)), pad_rows(ki_new), bias, cap, jnp.asarray(blk))


def _gdn_pre_kernel(conv_ref, prev_ref, hist_ref, small_ref, cw_ref, alog_ref, dtb_ref,
                    wm_ref, um_ref, qe_ref, oi_ref, e_ref, ext_ref, *, rows, nvalid):
    c = CHUNK
    n = HB * c
    ext_ref[0:8, :] = jnp.where(pl.program_id(1) == 0, hist_ref[0], prev_ref[0])
    ext_ref[8:8 + rows, :] = conv_ref[0]
    cb = ext_ref[5:5 + rows, :] * cw_ref[0:1, :]
    for j in range(1, CONV_B):
        cb = cb + ext_ref[5 + j:5 + j + rows, :] * cw_ref[j:j + 1, :]
    cb = _silu(cb)

    small = small_ref[0]
    pos = lax.broadcasted_iota(I32, (rows, 1), 0) % c
    rowv = (pos < nvalid).astype(F32)
    beta_all = _sigmoid(small) * rowv
    sp = small + dtb_ref[...]
    g_all = -jnp.exp(alog_ref[...]) * (jnp.maximum(sp, 0.0) + jnp.log(1.0 + jnp.exp(-jnp.abs(sp)))) * rowv

    ri = lax.broadcasted_iota(I32, (n, n), 0)
    ci = lax.broadcasted_iota(I32, (n, n), 1)
    same = (ri // c) == (ci // c)
    tri_b = (same & (ci <= ri)).astype(BF16)
    bd_f = same.astype(F32)
    bd_b = same.astype(BF16)
    wr = lax.broadcasted_iota(I32, (c, n), 0)
    wl = lax.broadcasted_iota(I32, (c, n), 1)
    grp = wl // c
    tri_w = (wl % c) <= wr
    strict_w = (wl % c) < wr
    eye_w = ((wl % c) == wr).astype(F32)
    nt = (((1,), (1,)), ((), ()))
    tn = (((0,), (0,)), ((), ()))

    def l2n(x):
        return x * lax.rsqrt(jnp.sum(x * x, axis=-1, keepdims=True) + EPS)

    def to_wide(full):
        out = jnp.where(grp == 0, full[0:c, :], 0.0)
        for h in range(1, HB):
            out = out + jnp.where(grp == h, full[h * c:(h + 1) * c, :], 0.0)
        return out

    def tile4(x):
        return jnp.concatenate([x] * HB, axis=0)

    def chunk(k0):
        def stack(fn):
            return jnp.concatenate([fn(h) for h in range(HB)], axis=0)

        rv = rowv[k0:k0 + c]
        q = stack(lambda h: l2n(cb[k0:k0 + c, h * DK:(h + 1) * DK]) * (DK ** -0.5))
        k = stack(lambda h: l2n(cb[k0:k0 + c, WB + h * DK:WB + (h + 1) * DK]) * rv)
        v = stack(lambda h: cb[k0:k0 + c, 2 * WB + h * DV:2 * WB + (h + 1) * DV] * rv)
        beta = stack(lambda h: beta_all[k0:k0 + c, _L_BB + h:_L_BB + h + 1])
        g = stack(lambda h: g_all[k0:k0 + c, _L_AB + h:_L_AB + h + 1])
        yield

        gh, gl = _split(jnp.broadcast_to(g, (n, 128)))
        gcum = _dot(tri_b, gh) + _dot(tri_b, gl)
        yield
        gcum_row = gcum.T[0:1, :]
        gcum_col = gcum[:, 0:1]
        col_w = to_wide(jnp.concatenate([gcum, gcum], axis=1))
        decay_w = jnp.where(tri_w, jnp.exp(jnp.where(tri_w, col_w - gcum_row, 0.0)), 0.0)
        kb = k * beta
        kbf = k.astype(BF16)
        kk_w = to_wide(lax.dot_general(kb.astype(BF16), kbf, nt, preferred_element_type=F32))
        qk_w = to_wide(lax.dot_general(q.astype(BF16), kbf, nt, preferred_element_type=F32))
        nmat_w = jnp.where(strict_w, kk_w * decay_w, 0.0)
        attn_w = jnp.where(tri_w, qk_w * decay_w, 0.0)
        yield

        inv_w = eye_w - nmat_w
        ph = nmat_w.astype(BF16)
        bh = tile4(ph) * bd_b
        for _ in range(int(np.log2(c)) - 1):
            ph = _dot(ph, bh).astype(BF16)
            yield
            bh = tile4(ph) * bd_b
            inv_w = inv_w + _dot(inv_w.astype(BF16), bh)
            yield
        rhs = jnp.concatenate([v * beta, kb * jnp.exp(gcum_col)], axis=1)
        sol = _dot3(tile4(inv_w) * bd_f, rhs)
        yield
        solb = sol.astype(BF16)
        aw = _dot((tile4(attn_w) * bd_f).astype(BF16), solb)
        yield
        ck = k0 // c
        oi_ref[0, ck] = aw[:, :DV]
        qe_ref[0, ck] = (q * jnp.exp(gcum_col) - aw[:, DV:]).astype(BF16)
        for h in range(HB):
            rs = slice(h * c, (h + 1) * c)
            g_last = gcum[(h + 1) * c - 1:(h + 1) * c, :]
            kd = (k[rs] * jnp.exp(g_last - gcum[rs])).astype(BF16)
            uw = lax.dot_general(kd, solb[rs], tn, preferred_element_type=F32)
            um_ref[0, ck, h] = uw[:, :DV]
            wm_ref[0, ck, h] = uw[:, DV:].astype(BF16)
            e_ref[0, ck, h:h + 1, :] = jnp.exp(g_last)
        e_ref[0, ck, HB:8, :] = jnp.zeros((8 - HB, 128), F32)

    gens = [chunk(k0) for k0 in range(0, rows, c)]
    while gens:
        alive = []
        for gen in gens:
            try:
                next(gen)
                alive.append(gen)
            except StopIteration:
                pass
        gens = alive


def _gdn_scan_kernel(wm_ref, um_ref, qe_ref, oi_ref, e_ref, gb_ref, s0_ref, ng_ref, ob_ref, sfin_ref, s_ref,
                     *, bb, g):
    c = CHUNK

    @pl.when(pl.program_id(1) == 0)
    def _():
        s_ref[...] = s0_ref[...]

    ng = ng_ref[...]
    for b in range(bb):
        for ck in range(g):
            for h in range(HB):
                s = s_ref[b, h]
                sb = s.astype(BF16)
                rs = slice(h * c, (h + 1) * c)
                o = _dot(qe_ref[b, ck, rs, :], sb) + oi_ref[b, ck, rs, :]
                s_ref[b, h] = e_ref[b, ck, h:h + 1, :] * s + (um_ref[b, ck, h] - _dot(wm_ref[b, ck, h], sb))
                gate = gb_ref[b, ck * c:(ck + 1) * c, h * DV:(h + 1) * DV]
                ob_ref[b, ck * c:(ck + 1) * c, h * DV:(h + 1) * DV] = (_rms(o, ng) * _silu(gate)).astype(BF16)
    sfin_ref[...] = s_ref[...]


def _gdn(conv_in, hist, gb, small, s0, conv_w, a_log, dt_bias, norm_gdn, nvalid):
    b, t, _ = conv_in.shape
    assert t % CHUNK == 0
    rows = 256 if t % 256 == 0 else CHUNK
    nc, cps = t // CHUNK, rows // CHUNK
    n = HB * CHUNK
    alog = jnp.zeros((1, 128), F32).at[0, _L_AB:_L_AB + HB].set(a_log)
    dtb = jnp.zeros((1, 128), F32).at[0, _L_AB:_L_AB + HB].set(dt_bias)
    row = lambda c_: pl.BlockSpec((1, rows, c_), lambda bi, ti: (bi, ti, 0))
    prev = pl.BlockSpec((1, 8, C_CONV_B), lambda bi, ti: (bi, jnp.maximum(ti * (rows // 8) - 1, 0), 0))
    per_b = lambda s: pl.BlockSpec((1,) + s, lambda bi, ti: (bi,) + (0,) * len(s))
    chunked = lambda s: pl.BlockSpec((1, cps) + s, lambda bi, ti: (bi, ti) + (0,) * len(s))
    wm, um, qe, oi, e = pl.pallas_call(
        functools.partial(_gdn_pre_kernel, rows=rows, nvalid=nvalid),
        grid=(b, t // rows),
        in_specs=[row(C_CONV_B), prev, per_b((8, C_CONV_B)), row(128),
                  _const_spec((CONV_B, C_CONV_B)), _const_spec((1, 128)), _const_spec((1, 128))],
        out_specs=[chunked((HB, DK, DV)), chunked((HB, DK, DV)), chunked((n, DK)), chunked((n, DV)),
                   chunked((8, 128))],
        out_shape=[jax.ShapeDtypeStruct((b, nc, HB, DK, DV), BF16), jax.ShapeDtypeStruct((b, nc, HB, DK, DV), F32),
                   jax.ShapeDtypeStruct((b, nc, n, DK), BF16), jax.ShapeDtypeStruct((b, nc, n, DV), F32),
                   jax.ShapeDtypeStruct((b, nc, 8, 128), F32)],
        scratch_shapes=[pltpu.VMEM((8 + rows, C_CONV_B), F32)],
        compiler_params=_cparams(("arbitrary", "arbitrary")),
        name="gdn_pre",
    )(conv_in, conv_in, hist, small, conv_w, alog, dtb)

    bb = 2 if b % 2 == 0 else 1
    g = 2 if nc % 2 == 0 else 1
    blk = lambda s: pl.BlockSpec((bb, g) + s, lambda bi, ci: (bi, ci) + (0,) * len(s))
    rowb = pl.BlockSpec((bb, g * CHUNK, WB), lambda bi, ci: (bi, ci, 0))
    state = pl.BlockSpec((bb, HB, DK, DV), lambda bi, ci: (bi, 0, 0, 0))
    return pl.pallas_call(
        functools.partial(_gdn_scan_kernel, bb=bb, g=g),
        grid=(b // bb, nc // g),
        in_specs=[blk((HB, DK, DV)), blk((HB, DK, DV)), blk((n, DK)), blk((n, DV)), blk((8, 128)), rowb, state,
                  _const_spec((1, DV))],
        out_specs=[rowb, state],
        out_shape=[jax.ShapeDtypeStruct((b, t, WB), BF16), jax.ShapeDtypeStruct((b, HB, DK, DV), F32)],
        scratch_shapes=[pltpu.VMEM((bb, HB, DK, DV), F32)],
        compiler_params=_cparams(("arbitrary", "arbitrary")),
        name="gdn_scan",
    )(wm, um, qe, oi, e, gb, s0, norm_gdn.reshape(1, DV))


def _post_kernel(x_ref, oaT_ref, ob_ref, ga_ref, gbr_ref, p_ref, hist_ref, wa_ref, wb_ref, wo_ref, nf_ref,
                 wup_ref, cw_ref, wdn_ref, npl_ref, wpg_ref, wple_ref, nfin_ref, y_ref, tail_ref, ext_ref, *, tm):
    @pl.when(pl.program_id(1) == 0)
    def _():
        ext_ref[0:8, :] = hist_ref[0]

    ya = lax.dot_general(oaT_ref[0], wa_ref[...], (((0,), (0,)), ((), ())), preferred_element_type=F32)
    yb = _dot(ob_ref[0], wb_ref[...])
    mix = _sigmoid(ga_ref[0]) * ya + _sigmoid(gbr_ref[0]) * yb
    x1 = x_ref[0] + _dot(mix.astype(BF16), wo_ref[...])
    h2 = _rms(x1, nf_ref[...]).astype(BF16)
    ext_ref[8:8 + tm, :] = _dot(h2, wup_ref[:, 0:D_FF])
    u_val = _dot(h2, wup_ref[:, D_FF:2 * D_FF])
    cv = ext_ref[6:6 + tm, :] * cw_ref[0:1, :]
    for j in range(1, CONV_F):
        cv = cv + ext_ref[6 + j:6 + j + tm, :] * cw_ref[j:j + 1, :]
    tail = ext_ref[tm:tm + 8, :]
    ext_ref[0:8, :] = tail
    tail_ref[0] = tail
    act = 0.5 * cv * (1.0 + jnp.tanh(0.7978845608028654 * (cv + 0.044715 * (cv * cv * cv))))
    x2 = x1 + _dot((act * u_val).astype(BF16), wdn_ref[...])
    gate = _sigmoid(_dot(_rms(x2, npl_ref[...]).astype(BF16), wpg_ref[...]))
    x3 = x2 + gate * _dot(p_ref[0].astype(BF16), wple_ref[...])
    y_ref[0] = _rms(x3, nfin_ref[...])


def _post(x, oaT, ob, ga, gbr, p, hist, w_proj_a, w_proj_b, w_out, norm_ffn, w_up, conv_ffn, w_down, norm_ple,
          w_ple_gate, w_ple, norm_final, tm):
    b, t, _ = x.shape
    nt = t // tm
    assert t % tm == 0 and tm >= 8 and oaT.shape == (b * nt, WA, tm)
    row = lambda c: pl.BlockSpec((1, tm, c), lambda bi, ti: (bi, ti, 0))
    per_b = pl.BlockSpec((1, 8, D_FF), lambda bi, ti: (bi, 0, 0))
    vec = _const_spec((1, D_MODEL))
    return pl.pallas_call(
        functools.partial(_post_kernel, tm=tm),
        grid=(b, nt),
        in_specs=[row(D_MODEL), pl.BlockSpec((1, WA, tm), lambda bi, ti: (bi * nt + ti, 0, 0)), row(WB),
                  row(D_MODEL), row(D_MODEL), row(D_PLE), per_b,
                  _const_spec((WA, D_MODEL)), _const_spec((WB, D_MODEL)), _const_spec((D_MODEL, D_MODEL)), vec,
                  _const_spec((D_MODEL, 2 * D_FF)), _const_spec((CONV_F, D_FF)), _const_spec((D_FF, D_MODEL)),
                  vec, _const_spec((D_MODEL, D_MODEL)), _const_spec((D_PLE, D_MODEL)), vec],
        out_specs=[row(D_MODEL), per_b],
        out_shape=[jax.ShapeDtypeStruct((b, t, D_MODEL), F32), jax.ShapeDtypeStruct((b, 8, D_FF), F32)],
        scratch_shapes=[pltpu.VMEM((8 + tm, D_FF), F32)],
        compiler_params=_cparams(("arbitrary", "arbitrary")),
        name="post",
    )(x, oaT, ob, ga, gbr, p, hist, w_proj_a.astype(BF16), w_proj_b.astype(BF16), w_out.astype(BF16),
      norm_ffn.reshape(1, D_MODEL), w_up.astype(BF16), conv_ffn, w_down.astype(BF16),
      norm_ple.reshape(1, D_MODEL), w_ple_gate.astype(BF16), w_ple.astype(BF16), norm_final.reshape(1, D_MODEL))


def _pad_hist(hist, rows=8):
    b, r, c = hist.shape
    return jnp.concatenate([jnp.zeros((b, rows - r, c), hist.dtype), hist], axis=1)


def _layer(x, p, past_k, past_v, past_kidx, s_gdn, conv_b_hist, ffn_hist, wts, *, tm, tq):
    (norm_mix, w_in, conv_b, a_log, dt_bias, norm_gdn, w_proj_a, w_proj_b, w_out, norm_ffn, w_up, conv_ffn,
     w_down, norm_ple, w_ple, w_ple_gate, rel_bias, norm_final) = wts
    b, t, _ = x.shape
    n = b * t
    past = past_k.shape[1]
    topk = min(TOPK_MAX, (past + t) // 4)
    x2d = x.reshape(n, D_MODEL)
    tmi = min(tm, n)
    (ka, va, kbf, conv_in, gb, ga, gbr, small, ki3, qT, qihT, qilT, wiT, vTa) = _in_proj(x2d, norm_mix, w_in, tmi)

    if past == 0:
        assert tq == tmi and t % tq == 0 and tq % CHUNK == 0 and tq >= topk
        nq = t // tq
        kj = np.arange(tq)[:, None]
        adm = ((kj // CHUNK) <= (np.arange(tq)[None, :] // CHUNK)).astype(np.float32)
        oT = _dsa(qT, qihT, qilT, wiT, kbf.reshape(b, nq, tq, WA), vTa.reshape(b, nq, HA * V_ROWS, tq),
                  ki3.reshape(b, nq, tq, 256), _bias_tables(rel_bias, adm, tq, tq, tq), adm,
                  nq=nq, causal=True, topk=topk)
    else:
        assert n == tmi
        oT = _dsa_step(qT, qihT, qilT, wiT, kbf, va.reshape(n, WA), small[:, :D_IDX], past_k, past_v, past_kidx, rel_bias,
                       b=b, t=t, topk=topk)
        oT = oT[:, :, :t]

    tp = -(-t // CHUNK) * CHUNK
    padt = lambda a: jnp.pad(a.reshape(b, t, a.shape[-1]), ((0, 0), (0, tp - t), (0, 0)))
    ob, s_new = _gdn(padt(conv_in), _pad_hist(conv_b_hist), padt(gb), padt(small), s_gdn, conv_b, a_log, dt_bias,
                     norm_gdn, nvalid=min(t, CHUNK))
    new_conv_b = jnp.concatenate([conv_b_hist, conv_in.reshape(b, t, C_CONV_B)], axis=1)[:, t:]

    per_bt = lambda a: a.reshape(b, t, a.shape[-1])
    y, tail = _post(x, oT, ob[:, :t], per_bt(ga), per_bt(gbr), p, _pad_hist(ffn_hist), w_proj_a, w_proj_b, w_out,
                    norm_ffn, w_up, conv_ffn, w_down, norm_ple, w_ple_gate, w_ple, norm_final, min(tm, t))
    new_ffn = tail[:, 8 - (CONV_F - 1):]
    return (y, ka.reshape(b, t, HA, DH), va.reshape(b, t, HA, DH), small[:, :D_IDX].reshape(b, t, D_IDX),
            s_new, new_conv_b, new_ffn)


def kernel(x_prompt, x_sample, p_prompt, p_sample, cache_k, cache_v, cache_kidx, state_gdn, state_gdn_conv,
           state_ffn_conv, norm_mix, w_in, conv_b, a_log, dt_bias, norm_gdn, w_proj_a, w_proj_b, w_out, norm_ffn,
           w_up, conv_ffn, w_down, norm_ple, w_ple, w_ple_gate, rel_bias, norm_final):
    assert norm_mix.shape[0] == 1
    bp = x_prompt.shape[0]
    dt = x_prompt.dtype
    wts = (norm_mix[0], w_in[0], conv_b[0], a_log[0], dt_bias[0], norm_gdn[0], w_proj_a[0], w_proj_b[0], w_out[0],
           norm_ffn[0], w_up[0], conv_ffn[0], w_down[0], norm_ple[0], w_ple[0], w_ple_gate[0], rel_bias, norm_final)
    outs_p = _layer(x_prompt, p_prompt[0], jnp.zeros((bp, 0, HA, DH), dt), jnp.zeros((bp, 0, HA, DH), dt),
                    jnp.zeros((bp, 0, D_IDX), dt), jnp.zeros((bp, HB, DK, DV), dt),
                    jnp.zeros((bp, CONV_B - 1, C_CONV_B), dt), jnp.zeros((bp, CONV_F - 1, D_FF), dt),
                    wts, tm=256, tq=256)
    outs_s = _layer(x_sample, p_sample[0], cache_k[0], cache_v[0], cache_kidx[0], state_gdn[0],
                    state_gdn_conv[0], state_ffn_conv[0], wts, tm=256, tq=256)
    yp, ys = outs_p[0], outs_s[0]
    return (yp, ys) + tuple(a[None] for a in outs_p[1:]) + tuple(a[None] for a in outs_s[1:])
```

```python
import functools

import numpy as np
import jax
import jax.numpy as jnp
from jax import lax
from jax.experimental import pallas as pl
from jax.experimental.pallas import tpu as pltpu

F32 = jnp.float32
BF16 = jnp.bfloat16
I32 = jnp.int32

D_MODEL = 1024
CHUNK = 64
HA, DH = 8, 64
H_IDX, D_IDX = 8, 64
TOPK_MAX = 256
NUM_BUCKETS, MAX_DISTANCE = 32, 128
HB, DK, DV = 4, 128, 128
CONV_B = 4
D_FF = 2816
CONV_F = 3
D_PLE = 256
EPS = 1e-6
NEG = -1e30
WA = HA * DH
WB = HB * DK
C_CONV_B = 3 * WB
INT_MIN = -2 ** 31
LOG2E = 1.4426950408889634

_O_QA, _O_KA, _O_VA, _O_QI, _O_KI, _O_WI = 0, 512, 1024, 1536, 2048, 2112
_O_QB, _O_GB, _O_BB, _O_AB, _O_GA, _O_GBR = 2120, 3656, 4168, 4172, 4176, 5200
_L_WI, _L_BB, _L_AB = 64, 72, 76

VMEM_LIMIT = 56 * 1024 * 1024


def _cparams(sem):
    return pltpu.CompilerParams(dimension_semantics=sem, vmem_limit_bytes=VMEM_LIMIT)


def _const_spec(shape):
    nd = len(shape)
    return pl.BlockSpec(shape, lambda *_: (0,) * nd, pipeline_mode=pl.Buffered(1))


def _rms(x, g):
    return x * lax.rsqrt(jnp.mean(x * x, axis=-1, keepdims=True) + EPS) * g


def _split(x):
    hi = x.astype(BF16)
    lo = (x - hi.astype(F32)).astype(BF16)
    return hi, lo


def _dot(a, b):
    return jnp.dot(a, b, preferred_element_type=F32)


def _dot3(a, b):
    ah, al = _split(a)
    bh, bl = _split(b)
    return _dot(ah, bh) + (_dot(al, bh) + _dot(ah, bl))


def _sigmoid(x):
    return 1.0 / (1.0 + jnp.exp(-x))


def _silu(x):
    return x * _sigmoid(x)


def _transpose32(a):
    a = list(a)
    j, m = 16, 0x0000FFFF
    while j:
        k = 0
        while k < 32:
            t = (a[k] ^ lax.shift_right_logical(a[k + j], jnp.int32(j))) & jnp.int32(m - (1 << 32) if m >> 31 else m)
            a[k] = a[k] ^ t
            a[k + j] = a[k + j] ^ (t << j)
            k = (k + j + 1) & ~j
        j >>= 1
        m = (m ^ (m << j)) & 0xFFFFFFFF
    return a


V_ROWS = DH + 16
RADIX_TILES = 8
_NT = (((1,), (1,)), ((), ()))


def _in_proj_kernel(x_ref, g_ref, wm_ref, wt_ref, wqh_ref, wql_ref, wsh_ref, wsl_ref, wwh_ref, wwl_ref,
                    ph_ref, plo_ref, sc_ref,
                    ka_ref, va_ref, kbf_ref, conv_ref, gb_ref, ga_ref, gbr_ref, small_ref, ki3_ref,
                    qT_ref, qihT_ref, qilT_ref, wiT_ref, vTa_ref):
    tm = x_ref.shape[0]
    h = _rms(x_ref[...], g_ref[...])
    hh, hl = _split(h)

    def main(lo, hi):
        return _dot(hh, wm_ref[:, lo:hi])

    def nt(w, a):
        return lax.dot_general(w, a, _NT, preferred_element_type=F32)

    def nt3(wh_ref, wl_ref):
        return nt(wh_ref[...], hh) + (nt(wh_ref[...], hl) + nt(wl_ref[...], hh))

    ka = main(0, 512)
    va = main(512, 1024)
    for hd in range(HA):
        ka_ref[:, hd, :] = ka[:, hd * DH:(hd + 1) * DH]
        va_ref[:, hd, :] = va[:, hd * DH:(hd + 1) * DH]
    kbf_ref[...] = ka.astype(BF16)
    conv_ref[...] = main(1024, 2560)
    gb_ref[...] = main(2560, 3072)
    ga_ref[...] = main(3072, 4096)
    gbr_ref[...] = main(4096, 5120)

    qT_ref[0] = nt(wt_ref[0:WA, :], hh).astype(BF16)
    vT = nt(wt_ref[WA:2 * WA, :], hh).astype(BF16)
    ones = jnp.ones((V_ROWS - DH, tm), BF16)
    for hd in range(HA):
        vTa_ref[0, hd * V_ROWS:hd * V_ROWS + DH, :] = vT[hd * DH:(hd + 1) * DH]
        vTa_ref[0, hd * V_ROWS + DH:(hd + 1) * V_ROWS, :] = ones
    qh, ql = _split(nt3(wqh_ref, wql_ref))
    qihT_ref[0] = qh
    qilT_ref[0] = ql
    wiT_ref[0] = nt3(wwh_ref, wwl_ref)[0:H_IDX] * (H_IDX ** -0.5)

    small = (_dot(hh, wsh_ref[...]) + (_dot(hl, wsh_ref[...]) + _dot(hh, wsl_ref[...]))) * sc_ref[...]
    small_ref[...] = small
    sh, sl = _split(small)
    ki3_ref[...] = (_dot(sh, ph_ref[...]) + _dot(sl, plo_ref[...])).astype(BF16)


def _in_proj(x2d, norm_mix, w_in, tm):
    n = x2d.shape[0]
    assert n % tm == 0
    g = n // tm
    w = w_in
    wm = jnp.concatenate([w[:, _O_KA:_O_QI], w[:, _O_QB:_O_BB], w[:, _O_GA:]], axis=1).astype(BF16)
    wt = jnp.concatenate([w[:, _O_QA:_O_KA] * (DH ** -0.5 * LOG2E), w[:, _O_VA:_O_QI]], axis=1).T.astype(BF16)
    hilo = lambda a: (a.astype(BF16), (a - a.astype(BF16).astype(F32)).astype(BF16))
    wqh, wql = hilo((w[:, _O_QI:_O_KI] * (D_IDX ** -0.5)).T)
    wsh, wsl = hilo(jnp.concatenate([w[:, _O_KI:_O_QB], w[:, _O_BB:_O_GA], jnp.zeros((D_MODEL, 48), F32)], axis=1))
    wwh, wwl = hilo(jnp.concatenate([w[:, _O_WI:_O_QB], jnp.zeros((D_MODEL, 8), F32)], axis=1).T)
    ph = np.zeros((128, 256), np.float32)
    plo = np.zeros((128, 256), np.float32)
    for c in range(64):
        ph[c, c] = 1.0
        ph[c, 64 + c] = 1.0
        plo[c, 128 + c] = 1.0
    sc = np.ones((1, 128), np.float32)
    sc[0, _L_WI:_L_WI + H_IDX] = H_IDX ** -0.5
    row = lambda c: pl.BlockSpec((tm, c), lambda i: (i, 0))
    colT = lambda r: pl.BlockSpec((1, r, tm), lambda i: (i, 0, 0))
    heads = pl.BlockSpec((tm, HA, DH), lambda i: (i, 0, 0))
    out_cols = [(512, BF16), (1536, F32), (512, F32), (1024, F32), (1024, F32), (128, F32), (256, BF16)]
    out_rows = [(WA, BF16), (WA, BF16), (WA, BF16), (H_IDX, F32), (HA * V_ROWS, BF16)]
    return pl.pallas_call(
        _in_proj_kernel,
        grid=(g,),
        in_specs=[row(D_MODEL), _const_spec((1, D_MODEL)), _const_spec((D_MODEL, 5120)),
                  _const_spec((2 * WA, D_MODEL)), _const_spec((WA, D_MODEL)), _const_spec((WA, D_MODEL)),
                  _const_spec((D_MODEL, 128)), _const_spec((D_MODEL, 128)),
                  _const_spec((16, D_MODEL)), _const_spec((16, D_MODEL)),
                  _const_spec((128, 256)), _const_spec((128, 256)), _const_spec((1, 128))],
        out_specs=[heads, heads] + [row(c) for c, _ in out_cols] + [colT(r) for r, _ in out_rows],
        out_shape=[jax.ShapeDtypeStruct((n, HA, DH), F32)] * 2
        + [jax.ShapeDtypeStruct((n, c), d) for c, d in out_cols]
        + [jax.ShapeDtypeStruct((g, r, tm), d) for r, d in out_rows],
        compiler_params=_cparams(("arbitrary",)),
        name="in_proj",
    )(x2d, norm_mix.reshape(1, D_MODEL), wm, wt, wqh, wql, wsh, wsl, wwh, wwl, jnp.asarray(ph, BF16),
      jnp.asarray(plo, BF16), jnp.asarray(sc))


def _store_keys(score, j, key_ref, pln_ref, eq_ref):
    tk, tq = score.shape
    bits = pltpu.bitcast(score, I32)
    bits = jnp.where(bits == INT_MIN, 0, bits)
    key = bits ^ ((bits >> 31) & 0x7FFFFFFF)
    key_ref[j] = key
    u3 = (key ^ INT_MIN).reshape(tk // 8, 8, tq)
    ones = jnp.full((8, tq), -1, I32)
    pln_ref[0, j] = ones
    for l0 in range(0, tq, 128):
        planes = _transpose32([u3[r][:, l0:l0 + 128] for r in range(32)])
        for b in range(32):
            pln_ref[b + 1, j, :, l0:l0 + 128] = planes[b]
    eq_ref[0, j] = ones


def _radix_threshold(pln_ref, eq_ref, nk, tq, topk):
    for d in range(RADIX_TILES - 1):
        pln_ref[:, nk + d] = jnp.zeros((33, 8, tq), I32)
        eq_ref[0, nk + d] = jnp.zeros((8, tq), I32)

    sub = lax.broadcasted_iota(I32, (8, tq), 0)
    group_masks = (-0x10000, -0xFF0100, -0xF0F0F10, -0x33333334, -0x55555556)

    def pos_plane(e, j):
        if e < 6:
            return jnp.broadcast_to(-((~j >> (5 - e)) & 1), (8, tq))
        if e < 11:
            return jnp.full((8, tq), group_masks[e - 6], I32)
        return -((~sub >> (13 - e)) & 1)

    def sweep(prev, cur, carry, src, dst):
        n_gt, flip = carry

        def body(jq, cnt):
            for d in range(RADIX_TILES):
                j = jq * RADIX_TILES + d
                e = eq_ref[src, j] & (prev(j) ^ flip)
                eq_ref[dst, j] = e
                cnt = cnt + lax.population_count(e & cur(j))
            return cnt

        cnt = lax.fori_loop(0, (nk + RADIX_TILES - 1) // RADIX_TILES, body, jnp.zeros((8, tq), I32))
        cnt = n_gt + cnt.sum(axis=0, keepdims=True)
        acc = cnt >= topk
        return acc, (jnp.where(acc, n_gt, cnt), jnp.where(acc, 0, -1))

    def key_sweep(it, carry, src, dst):
        tu, rest = carry
        acc, rest = sweep(lambda j: pln_ref[it, j], lambda j: pln_ref[it + 1, j], rest, src, dst)
        return jnp.where(acc, tu | jnp.left_shift(jnp.int32(1), 31 - it), tu), rest

    def two_bits(i, carry):
        return key_sweep(2 * i + 1, key_sweep(2 * i, carry, 0, 1), 1, 0)

    zero_row = jnp.zeros((1, tq), I32)
    tu, rest = lax.fori_loop(0, 16, two_bits, (zero_row, (zero_row, zero_row)))
    inv_pos = zero_row
    for e in range(14):
        prev = (lambda j: pln_ref[32, j]) if e == 0 else functools.partial(pos_plane, e - 1)
        acc, rest = sweep(prev, functools.partial(pos_plane, e), rest, e % 2, 1 - e % 2)
        inv_pos = jnp.where(acc, inv_pos | (1 << (13 - e)), inv_pos)
    return tu ^ INT_MIN, ~inv_pos & 0x3FFF


def _dsa_kernel(qT_ref, qihT_ref, qilT_ref, wiT_ref, k_ref, vT_ref, ki3_ref, bias_ref, cap_ref,
                o_ref, key_ref, l_ref, acc_ref, lg_ref, pln_ref, eq_ref, sc_ref, ma_ref,
                *, tq, tk, nk_static, causal, topk):
    nk = (pl.program_id(1) + 1) if causal else nk_static
    wiT = wiT_ref[0]

    def when(cond):
        if isinstance(cond, bool):
            return (lambda f: f()) if cond else (lambda f: None)
        return pl.when(cond)

    zeros64 = jnp.zeros((64, tq), BF16)
    q3 = []
    for h in range(H_IDX):
        hi = qihT_ref[0, h * 64:(h + 1) * 64, :]
        lo = qilT_ref[0, h * 64:(h + 1) * 64, :]
        q3.append(jnp.concatenate([hi, lo, hi, zeros64], axis=0))

    def score_matmuls(j0, n):
        kts = [ki3_ref[0, j0 + d] for d in range(n)]
        accs = [None] * n
        for h in range(H_IDX):
            for d in range(n):
                t = jnp.maximum(_dot(kts[d], q3[h]), 0.0) * wiT[h:h + 1, :]
                accs[d] = t if accs[d] is None else accs[d] + t
        for d in range(n):
            sc_ref[d] = jnp.minimum(accs[d], cap_ref[jnp.where(j0 + d == nk - 1, 1, 0)])

    def score_finish(j0, n):
        for d in range(n):
            _store_keys(sc_ref[d], j0 + d, key_ref, pln_ref, eq_ref)

    def score_body(i, c):
        score_finish(2 * i - 2, 2)
        score_matmuls(2 * i, 2)
        return c

    @when(nk >= 2)
    def _():
        score_matmuls(0, 2)
        lax.fori_loop(1, nk // 2, score_body, 0)
        score_finish(2 * (nk // 2) - 2, 2)

    @when(nk % 2 == 1)
    def _():
        score_matmuls(nk - 1, 1)
        score_finish(nk - 1, 1)

    thr, pos_thr = _radix_threshold(pln_ref, eq_ref, nk, tq, topk)

    l_ref[...] = jnp.zeros(l_ref.shape, F32)
    acc_ref[...] = jnp.zeros(acc_ref.shape, F32)
    qm = []
    for h in range(HA):
        qh = qT_ref[0, h * 64:(h + 1) * 64, :]
        qm.append(jnp.concatenate([qh, zeros64] if h % 2 == 0 else [zeros64, qh], axis=0))
    row = lax.broadcasted_iota(I32, (tk, tq), 0)

    def tile_logits(j, slot, far):
        sel = key_ref[j] > jnp.where(row <= pos_thr - j * tk, thr - 1, thr)
        kind = jnp.clip(j - (nk - 3), 0, 2)
        mx = []
        for h in range(HA):
            pr = h // 2
            lg = _dot(k_ref[0, j, :, pr * 128:(pr + 1) * 128], qm[h])
            if not far:
                lg = lg + bias_ref[kind, h]
            lg = jnp.where(sel, lg, NEG).astype(BF16)
            lg_ref[slot, h] = lg
            part = jnp.max(lg.reshape(tk // 16, 16, tq), axis=0).astype(F32)
            mx.append(jnp.max(part, axis=0, keepdims=True))
        return jnp.concatenate(mx, axis=0)

    def tile_values(j, slot, mx):
        m_old = ma_ref[...]
        m_new = jnp.maximum(m_old, mx)
        alpha = jnp.exp2(m_old - m_new)
        ma_ref[...] = m_new
        ls = []
        for h in range(HA):
            p = jnp.exp2(lg_ref[slot, h] - m_new[h:h + 1, :].astype(BF16))
            pv = _dot(vT_ref[0, j, h * V_ROWS:(h + 1) * V_ROWS, :], p)
            acc_ref[h * 64:(h + 1) * 64, :] = alpha[h:h + 1, :] * acc_ref[h * 64:(h + 1) * 64, :] + pv[0:DH]
            ls.append(pv[DH:DH + 1])
        l_ref[...] = alpha * l_ref[...] + jnp.concatenate(ls, axis=0)

    ma_ref[...] = jnp.full((HA, tq), NEG, F32)

    def pair(j0, far):
        mx0 = tile_logits(j0, 0, far)
        mx1 = tile_logits(j0 + 1, 1, far)
        tile_values(j0, 0, mx0)
        tile_values(j0 + 1, 1, mx1)

    def far_pair(i, c):
        pair(2 * i, True)
        return c

    n_far = jnp.maximum(nk - 2, 0) // 2
    lax.fori_loop(0, n_far, far_pair, 0)

    @when(nk >= 2)
    def _():
        pair(2 * n_far, False)

    @when(nk % 2 == 1)
    def _():
        tile_values(nk - 1, 0, tile_logits(nk - 1, 0, False))

    for h in range(HA):
        o_ref[0, h * 64:(h + 1) * 64, :] = (acc_ref[h * 64:(h + 1) * 64, :] / l_ref[h:h + 1, :]).astype(BF16)


def _t5_bucket_np(rel):
    half = NUM_BUCKETS // 2
    max_exact = half // 2
    out = np.zeros(rel.shape, np.int64)
    flat_rel = rel.reshape(-1)
    flat = out.reshape(-1)
    for a in range(flat_rel.size):
        r = int(flat_rel[a])
        n = abs(r)
        b = n if n < max_exact else min(half - 1, (n * n).bit_length() + 1)
        flat[a] = b + (half if r > 0 else 0)
    return out


def _bias_tables(rel_bias, adm, tk, tq, tq_valid):
    kj = np.arange(tk)[:, None]
    t = np.minimum(np.arange(tq), tq_valid - 1)[None, :]
    rel_to_bucket = _t5_bucket_np(np.arange(-2 * tk - tq, tk + 1))
    lut = lambda rel: rel_to_bucket[rel + 2 * tk + tq]
    far_bucket = NUM_BUCKETS // 2 - 1
    idx = np.stack([np.full((tk, tq), far_bucket), lut(kj - tk - t), lut(kj - t)], axis=0)
    tab = rel_bias.astype(F32) * LOG2E
    onehot = jnp.asarray(idx[..., None] == np.arange(NUM_BUCKETS), F32)
    bias = jnp.einsum("ktqb,bh->khtq", onehot, tab - tab[far_bucket], precision=lax.Precision.HIGHEST)
    mask = np.zeros((3, 1, tk, tq), np.float32)
    mask[2, 0] = np.where(adm > 0.5, 0.0, NEG)
    return bias + jnp.asarray(mask)


def _dsa(qT, qihT, qilT, wiT, k4, vT4, ki34, bias, adm, *, nq, causal, topk):
    g, _, tq = qT.shape
    b, nk, tk, _ = k4.shape
    assert g == b * nq and tk == 256 and nk + RADIX_TILES - 1 <= 64
    cap = jnp.asarray(np.stack([np.full((tk, tq), np.inf, np.float32), np.where(adm > 0.5, np.inf, NEG)]), F32)
    qspec = lambda r: pl.BlockSpec((1, r, tq), lambda bi, i: (bi * nq + i, 0, 0))
    kspec = lambda s: pl.BlockSpec((1,) + s, lambda bi, i: (bi, 0, 0, 0), pipeline_mode=pl.Buffered(1))
    kern = functools.partial(_dsa_kernel, tq=tq, tk=tk, nk_static=nk, causal=causal, topk=topk)
    return pl.pallas_call(
        kern,
        grid=(b, nq),
        in_specs=[qspec(512), qspec(512), qspec(512), qspec(8),
                  kspec((nk, tk, 512)), kspec((nk, HA * V_ROWS, tk)), kspec((nk, tk, 256)),
                  _const_spec((3, HA, tk, tq)), _const_spec((2, tk, tq))],
        out_specs=qspec(512),
        out_shape=jax.ShapeDtypeStruct((g, 512, tq), BF16),
        scratch_shapes=[pltpu.VMEM((nk, tk, tq), I32), pltpu.VMEM((8, tq), F32),
                        pltpu.VMEM((512, tq), F32), pltpu.VMEM((2, HA, tk, tq), BF16),
                        pltpu.VMEM((33, nk + RADIX_TILES - 1, 8, tq), I32),
                        pltpu.VMEM((2, nk + RADIX_TILES - 1, 8, tq), I32),
                        pltpu.VMEM((2, tk, tq), F32), pltpu.VMEM((HA, tq), F32)],
        compiler_params=_cparams(("arbitrary", "arbitrary")),
        name="dsa_causal" if causal else "dsa_cached",
    )(qT, qihT, qilT, wiT, k4, vT4, ki34, bias, cap)


def _dsa_step_kernel(qa_ref, qih_ref, qil_ref, w_ref, ck_ref, cv_ref, cki_ref, nk_ref, nv_ref, nki_ref,
                     bias_ref, cap_ref, blk_ref, o_ref, key_ref, pln_ref, eq_ref, m_ref, l_ref, acc_ref,
                     *, nkc, t, topk):
    tk, tq = nk_ref.shape[1], qa_ref.shape[2]
    qih, qil = qih_ref[0], qil_ref[0]
    w = w_ref[0, 0:1, :]
    shifts = [t << s for s in range((tq // t).bit_length() - 2, -1, -1)]

    def scores(kx, j, cap):
        kh, kl = _split(kx)
        s = _dot(kh, qih) + (_dot(kl, qih) + _dot(kh, qil))
        yield
        sc = jnp.maximum(s, 0.0) * w
        for sh in shifts:
            sc = sc + pltpu.roll(sc, sh, 1)
            yield
        if cap is not None:
            sc = jnp.minimum(sc, cap)
        _store_keys(sc, j, key_ref, pln_ref, eq_ref)

    def together(*gens):
        gens = list(gens)
        while gens:
            gens = [g for g in gens if next(g, gens) is not gens]

    def score_body(i, c):
        cached_idx = lambda j: cki_ref[0, pl.ds(pl.multiple_of(j * tk, tk), tk), :]
        together(scores(cached_idx(2 * i), 2 * i, None), scores(cached_idx(2 * i + 1), 2 * i + 1, None))
        return c

    lax.fori_loop(0, nkc // 2, score_body, 0)
    together(scores(nki_ref[0], nkc, cap_ref[...]))
    thr, pos_thr = _radix_threshold(pln_ref, eq_ref, nkc + 1, tq, topk)

    m_ref[...] = jnp.full(m_ref.shape, NEG, F32)
    l_ref[...] = jnp.zeros(l_ref.shape, F32)
    acc_ref[...] = jnp.zeros(acc_ref.shape, F32)
    qa = qa_ref[0]
    row = lax.broadcasted_iota(I32, (tk, tq), 0)
    ones8 = jnp.ones((8, tk), BF16)
    tn = (((0,), (0,)), ((), ()))

    def logits(k_bf, j, bias):
        sel = key_ref[j] > jnp.where(row <= pos_thr - j * tk, thr - 1, thr)
        lg = _dot(k_bf, qa)
        if bias is not None:
            lg = lg + bias
        return jnp.where(sel, lg, NEG).astype(BF16)

    def consume(lg, v_bf):
        part = jnp.max(lg.reshape(tk // 16, 16, tq), axis=0).astype(F32)
        m_old = m_ref[0:1, :]
        m_new = jnp.maximum(m_old, jnp.max(part, axis=0, keepdims=True))
        alpha = jnp.exp2(m_old - m_new)
        p = jnp.exp2(lg - m_new.astype(BF16))
        l_ref[0:1, :] = alpha * l_ref[0:1, :] + _dot(ones8, p)[0:1]
        acc_ref[...] = alpha * acc_ref[...] + lax.dot_general(v_bf, p, tn, preferred_element_type=F32)
        m_ref[0:1, :] = m_new

    def cached_pair(j0, bias1):
        r0 = pl.ds(pl.multiple_of(j0 * tk, tk), tk)
        r1 = pl.ds(pl.multiple_of(j0 * tk + tk, tk), tk)
        lg0 = logits(ck_ref[0, r0, :].astype(BF16), j0, None)
        lg1 = logits(ck_ref[0, r1, :].astype(BF16), j0 + 1, bias1)
        consume(lg0, cv_ref[0, r0, :].astype(BF16))
        consume(lg1, cv_ref[0, r1, :].astype(BF16))

    def far(i, c):
        cached_pair(2 * i, None)
        return c

    lax.fori_loop(0, nkc // 2 - 1, far, 0)
    cached_pair(nkc - 2, bias_ref[0])
    consume(logits(nk_ref[0], nkc, bias_ref[1]), nv_ref[0])
    o = acc_ref[...] / l_ref[0:1, :] * blk_ref[...]
    for sh in shifts:
        o = o + pltpu.roll(o, sh, 1)
    o_ref[0] = o.astype(BF16)


def _dsa_step(qT, qihT, qilT, wiT, kbf, va, ki_new, past_k, past_v, past_kidx, rel_bias, *, b, t, topk):
    past = past_k.shape[1]
    tk, tq = 256, HA * t
    assert tq == 128 and past % (2 * tk) == 0 and t <= tk
    nkc = past // tk
    nk = nkc + 1
    per_b = lambda a: jnp.swapaxes(a[0].reshape(a.shape[1], b, t), 0, 1)
    blk = (np.arange(WA)[:, None] // DH == np.arange(tq)[None, :] // t).astype(np.float32)
    qa = (jnp.tile(per_b(qT), (1, 1, HA)) * jnp.asarray(blk, BF16))
    lanes = lambda a: jnp.swapaxes(per_b(a).reshape(b, HA, D_IDX, t), 1, 2).reshape(b, D_IDX, tq)
    w = jnp.broadcast_to(per_b(wiT).reshape(b, 1, tq), (b, 8, tq))
    pad_rows = lambda a: jnp.pad(a.reshape(b, t, a.shape[-1]), ((0, 0), (0, tk - t), (0, 0)))
    adm = np.broadcast_to(np.arange(tk)[:, None] < t, (tk, tq)).astype(np.float32)
    bias = _bias_tables(rel_bias, adm[:, :t], tk, t, t)[1:]
    bias = jnp.swapaxes(bias, 1, 2).reshape(2, tk, tq)
    cap = jnp.asarray(np.where(adm > 0.5, np.inf, NEG), F32)
    bspec = lambda s: pl.BlockSpec((1,) + s, lambda bi: (bi,) + (0,) * len(s))
    return pl.pallas_call(
        functools.partial(_dsa_step_kernel, nkc=nkc, t=t, topk=topk),
        grid=(b,),
        in_specs=[bspec((WA, tq)), bspec((D_IDX, tq)), bspec((D_IDX, tq)), bspec((8, tq)),
                  bspec((past, WA)), bspec((past, WA)), bspec((past, D_IDX)),
                  bspec((tk, WA)), bspec((tk, WA)), bspec((tk, D_IDX)),
                  _const_spec((2, tk, tq)), _const_spec((tk, tq)), _const_spec((WA, tq))],
        out_specs=bspec((WA, tq)),
        out_shape=jax.ShapeDtypeStruct((b, WA, tq), BF16),
        scratch_shapes=[pltpu.VMEM((nk, tk, tq), I32), pltpu.VMEM((33, nk + RADIX_TILES - 1, 8, tq), I32),
                        pltpu.VMEM((2, nk + RADIX_TILES - 1, 8, tq), I32), pltpu.VMEM((8, tq), F32),
                        pltpu.VMEM((8, tq), F32), pltpu.VMEM((WA, tq), F32)],
        compiler_params=_cparams(("arbitrary",)),
        name="dsa_step",
    )(qa, lanes(qihT), lanes(qilT), w, past_k.reshape(b, past, WA), past_v.reshape(b, past, WA), past_kidx,
      pad_rows(kbf), pad_rows(va.astype(BF16)), pad_rows(ki_new), bias, cap, jnp.asarray(blk))


def _gdn_pre_kernel(conv_ref, prev_ref, hist_ref, small_ref, cw_ref, alog_ref, dtb_ref,
                    wm_ref, um_ref, qe_ref, oi_ref, e_ref, ext_ref, *, rows, nvalid):
    c = CHUNK
    n = HB * c
    ext_ref[0:8, :] = jnp.where(pl.program_id(1) == 0, hist_ref[0], prev_ref[0])
    ext_ref[8:8 + rows, :] = conv_ref[0]
    cb = ext_ref[5:5 + rows, :] * cw_ref[0:1, :]
    for j in range(1, CONV_B):
        cb = cb + ext_ref[5 + j:5 + j + rows, :] * cw_ref[j:j + 1, :]
    cb = _silu(cb)

    small = small_ref[0]
    pos = lax.broadcasted_iota(I32, (rows, 1), 0) % c
    rowv = (pos < nvalid).astype(F32)
    beta_all = _sigmoid(small) * rowv
    sp = small + dtb_ref[...]
    g_all = -jnp.exp(alog_ref[...]) * (jnp.maximum(sp, 0.0) + jnp.log(1.0 + jnp.exp(-jnp.abs(sp)))) * rowv

    ri = lax.broadcasted_iota(I32, (n, n), 0)
    ci = lax.broadcasted_iota(I32, (n, n), 1)
    same = (ri // c) == (ci // c)
    tri_b = (same & (ci <= ri)).astype(BF16)
    bd_f = same.astype(F32)
    bd_b = same.astype(BF16)
    wr = lax.broadcasted_iota(I32, (c, n), 0)
    wl = lax.broadcasted_iota(I32, (c, n), 1)
    grp = wl // c
    tri_w = (wl % c) <= wr
    strict_w = (wl % c) < wr
    eye_w = ((wl % c) == wr).astype(F32)
    nt = (((1,), (1,)), ((), ()))
    tn = (((0,), (0,)), ((), ()))

    def l2n(x):
        return x * lax.rsqrt(jnp.sum(x * x, axis=-1, keepdims=True) + EPS)

    def to_wide(full):
        out = jnp.where(grp == 0, full[0:c, :], 0.0)
        for h in range(1, HB):
            out = out + jnp.where(grp == h, full[h * c:(h + 1) * c, :], 0.0)
        return out

    def tile4(x):
        return jnp.concatenate([x] * HB, axis=0)

    def chunk(k0):
        def stack(fn):
            return jnp.concatenate([fn(h) for h in range(HB)], axis=0)

        rv = rowv[k0:k0 + c]
        q = stack(lambda h: l2n(cb[k0:k0 + c, h * DK:(h + 1) * DK]) * (DK ** -0.5))
        k = stack(lambda h: l2n(cb[k0:k0 + c, WB + h * DK:WB + (h + 1) * DK]) * rv)
        v = stack(lambda h: cb[k0:k0 + c, 2 * WB + h * DV:2 * WB + (h + 1) * DV] * rv)
        beta = stack(lambda h: beta_all[k0:k0 + c, _L_BB + h:_L_BB + h + 1])
        g = stack(lambda h: g_all[k0:k0 + c, _L_AB + h:_L_AB + h + 1])
        yield

        gh, gl = _split(jnp.broadcast_to(g, (n, 128)))
        gcum = _dot(tri_b, gh) + _dot(tri_b, gl)
        yield
        gcum_row = gcum.T[0:1, :]
        gcum_col = gcum[:, 0:1]
        col_w = to_wide(jnp.concatenate([gcum, gcum], axis=1))
        decay_w = jnp.where(tri_w, jnp.exp(jnp.where(tri_w, col_w - gcum_row, 0.0)), 0.0)
        kb = k * beta
        kbf = k.astype(BF16)
        kk_w = to_wide(lax.dot_general(kb.astype(BF16), kbf, nt, preferred_element_type=F32))
        qk_w = to_wide(lax.dot_general(q.astype(BF16), kbf, nt, preferred_element_type=F32))
        nmat_w = jnp.where(strict_w, kk_w * decay_w, 0.0)
        attn_w = jnp.where(tri_w, qk_w * decay_w, 0.0)
        yield

        inv_w = eye_w - nmat_w
        ph = nmat_w.astype(BF16)
        bh = tile4(ph) * bd_b
        for _ in range(int(np.log2(c)) - 1):
            ph = _dot(ph, bh).astype(BF16)
            yield
            bh = tile4(ph) * bd_b
            inv_w = inv_w + _dot(inv_w.astype(BF16), bh)
            yield
        rhs = jnp.concatenate([v * beta, kb * jnp.exp(gcum_col)], axis=1)
        sol = _dot3(tile4(inv_w) * bd_f, rhs)
        yield
        solb = sol.astype(BF16)
        aw = _dot((tile4(attn_w) * bd_f).astype(BF16), solb)
        yield
        ck = k0 // c
        oi_ref[0, ck] = aw[:, :DV]
        qe_ref[0, ck] = (q * jnp.exp(gcum_col) - aw[:, DV:]).astype(BF16)
        for h in range(HB):
            rs = slice(h * c, (h + 1) * c)
            g_last = gcum[(h + 1) * c - 1:(h + 1) * c, :]
            kd = (k[rs] * jnp.exp(g_last - gcum[rs])).astype(BF16)
            uw = lax.dot_general(kd, solb[rs], tn, preferred_element_type=F32)
            um_ref[0, ck, h] = uw[:, :DV]
            wm_ref[0, ck, h] = uw[:, DV:].astype(BF16)
            e_ref[0, ck, h:h + 1, :] = jnp.exp(g_last)
        e_ref[0, ck, HB:8, :] = jnp.zeros((8 - HB, 128), F32)

    gens = [chunk(k0) for k0 in range(0, rows, c)]
    while gens:
        alive = []
        for gen in gens:
            try:
                next(gen)
                alive.append(gen)
            except StopIteration:
                pass
        gens = alive


def _gdn_scan_kernel(wm_ref, um_ref, qe_ref, oi_ref, e_ref, gb_ref, s0_ref, ng_ref, ob_ref, sfin_ref, s_ref,
                     *, bb, g):
    c = CHUNK

    @pl.when(pl.program_id(1) == 0)
    def _():
        s_ref[...] = s0_ref[...]

    ng = ng_ref[...]
    for b in range(bb):
        for ck in range(g):
            for h in range(HB):
                s = s_ref[b, h]
                sb = s.astype(BF16)
                rs = slice(h * c, (h + 1) * c)
                o = _dot(qe_ref[b, ck, rs, :], sb) + oi_ref[b, ck, rs, :]
                s_ref[b, h] = e_ref[b, ck, h:h + 1, :] * s + (um_ref[b, ck, h] - _dot(wm_ref[b, ck, h], sb))
                gate = gb_ref[b, ck * c:(ck + 1) * c, h * DV:(h + 1) * DV]
                ob_ref[b, ck * c:(ck + 1) * c, h * DV:(h + 1) * DV] = (_rms(o, ng) * _silu(gate)).astype(BF16)
    sfin_ref[...] = s_ref[...]


def _gdn(conv_in, hist, gb, small, s0, conv_w, a_log, dt_bias, norm_gdn, nvalid):
    b, t, _ = conv_in.shape
    assert t % CHUNK == 0
    rows = 256 if t % 256 == 0 else CHUNK
    nc, cps = t // CHUNK, rows // CHUNK
    n = HB * CHUNK
    alog = jnp.zeros((1, 128), F32).at[0, _L_AB:_L_AB + HB].set(a_log)
    dtb = jnp.zeros((1, 128), F32).at[0, _L_AB:_L_AB + HB].set(dt_bias)
    row = lambda c_: pl.BlockSpec((1, rows, c_), lambda bi, ti: (bi, ti, 0))
    prev = pl.BlockSpec((1, 8, C_CONV_B), lambda bi, ti: (bi, jnp.maximum(ti * (rows // 8) - 1, 0), 0))
    per_b = lambda s: pl.BlockSpec((1,) + s, lambda bi, ti: (bi,) + (0,) * len(s))
    chunked = lambda s: pl.BlockSpec((1, cps) + s, lambda bi, ti: (bi, ti) + (0,) * len(s))
    wm, um, qe, oi, e = pl.pallas_call(
        functools.partial(_gdn_pre_kernel, rows=rows, nvalid=nvalid),
        grid=(b, t // rows),
        in_specs=[row(C_CONV_B), prev, per_b((8, C_CONV_B)), row(128),
                  _const_spec((CONV_B, C_CONV_B)), _const_spec((1, 128)), _const_spec((1, 128))],
        out_specs=[chunked((HB, DK, DV)), chunked((HB, DK, DV)), chunked((n, DK)), chunked((n, DV)),
                   chunked((8, 128))],
        out_shape=[jax.ShapeDtypeStruct((b, nc, HB, DK, DV), BF16), jax.ShapeDtypeStruct((b, nc, HB, DK, DV), F32),
                   jax.ShapeDtypeStruct((b, nc, n, DK), BF16), jax.ShapeDtypeStruct((b, nc, n, DV), F32),
                   jax.ShapeDtypeStruct((b, nc, 8, 128), F32)],
        scratch_shapes=[pltpu.VMEM((8 + rows, C_CONV_B), F32)],
        compiler_params=_cparams(("arbitrary", "arbitrary")),
        name="gdn_pre",
    )(conv_in, conv_in, hist, small, conv_w, alog, dtb)

    bb = 2 if b % 2 == 0 else 1
    g = 2 if nc % 2 == 0 else 1
    blk = lambda s: pl.BlockSpec((bb, g) + s, lambda bi, ci: (bi, ci) + (0,) * len(s))
    rowb = pl.BlockSpec((bb, g * CHUNK, WB), lambda bi, ci: (bi, ci, 0))
    state = pl.BlockSpec((bb, HB, DK, DV), lambda bi, ci: (bi, 0, 0, 0))
    return pl.pallas_call(
        functools.partial(_gdn_scan_kernel, bb=bb, g=g),
        grid=(b // bb, nc // g),
        in_specs=[blk((HB, DK, DV)), blk((HB, DK, DV)), blk((n, DK)), blk((n, DV)), blk((8, 128)), rowb, state,
                  _const_spec((1, DV))],
        out_specs=[rowb, state],
        out_shape=[jax.ShapeDtypeStruct((b, t, WB), BF16), jax.ShapeDtypeStruct((b, HB, DK, DV), F32)],
        scratch_shapes=[pltpu.VMEM((bb, HB, DK, DV), F32)],
        compiler_params=_cparams(("arbitrary", "arbitrary")),
        name="gdn_scan",
    )(wm, um, qe, oi, e, gb, s0, norm_gdn.reshape(1, DV))


def _post_kernel(x_ref, oaT_ref, ob_ref, ga_ref, gbr_ref, p_ref, hist_ref, wa_ref, wb_ref, wo_ref, nf_ref,
                 wup_ref, cw_ref, wdn_ref, npl_ref, wpg_ref, wple_ref, nfin_ref, y_ref, tail_ref, ext_ref,
                 *, tm, seq):
    if seq is None:
        @pl.when(pl.program_id(1) == 0)
        def _():
            ext_ref[0:8, :] = hist_ref[0]
    else:
        ext_ref[0:8, :] = jnp.zeros((8, D_FF), F32)

    ya = lax.dot_general(oaT_ref[0], wa_ref[...], (((0,), (0,)), ((), ())), preferred_element_type=F32)
    yb = _dot(ob_ref[0], wb_ref[...])
    mix = _sigmoid(ga_ref[0]) * ya + _sigmoid(gbr_ref[0]) * yb
    x1 = x_ref[0] + _dot(mix.astype(BF16), wo_ref[...])
    h2 = _rms(x1, nf_ref[...]).astype(BF16)
    ext_ref[8:8 + tm, :] = _dot(h2, wup_ref[:, 0:D_FF])
    u_val = _dot(h2, wup_ref[:, D_FF:2 * D_FF])
    if seq is None:
        cv = ext_ref[6:6 + tm, :] * cw_ref[0:1, :]
        for j in range(1, CONV_F):
            cv = cv + ext_ref[6 + j:6 + j + tm, :] * cw_ref[j:j + 1, :]
        tail = ext_ref[tm:tm + 8, :]
        ext_ref[0:8, :] = tail
        tail_ref[0] = tail
    else:
        pos = lax.broadcasted_iota(I32, (tm, 1), 0) % seq
        cv = jnp.where(pos >= 2, ext_ref[6:6 + tm, :], hist_ref[1]) * cw_ref[0:1, :]
        cv = cv + jnp.where(pos >= 1, ext_ref[7:7 + tm, :], hist_ref[0]) * cw_ref[1:2, :]
        cv = cv + ext_ref[8:8 + tm, :] * cw_ref[2:3, :]
        tail_ref[...] = ext_ref[8:8 + tm, :]
    act = 0.5 * cv * (1.0 + jnp.tanh(0.7978845608028654 * (cv + 0.044715 * (cv * cv * cv))))
    x2 = x1 + _dot((act * u_val).astype(BF16), wdn_ref[...])
    gate = _sigmoid(_dot(_rms(x2, npl_ref[...]).astype(BF16), wpg_ref[...]))
    x3 = x2 + gate * _dot(p_ref[0].astype(BF16), wple_ref[...])
    y_ref[0] = _rms(x3, nfin_ref[...])


def _post(x, oaT, ob, ga, gbr, p, hist, w_proj_a, w_proj_b, w_out, norm_ffn, w_up, conv_ffn, w_down, norm_ple,
          w_ple_gate, w_ple, norm_final, tm, seq=None):
    b, t, _ = x.shape
    nt = t // tm
    assert t % tm == 0 and tm >= 8 and oaT.shape == (b * nt, WA, tm)
    row = lambda c: pl.BlockSpec((1, tm, c), lambda bi, ti: (bi, ti, 0))
    vec = _const_spec((1, D_MODEL))
    if seq is None:
        hist_spec = tail_spec = pl.BlockSpec((1, 8, D_FF), lambda bi, ti: (bi, 0, 0))
        tail_shape = (b, 8, D_FF)
    else:
        assert b == 1 and nt == 1 and tm % seq == 0
        hist_spec = _const_spec((2, tm, D_FF))
        tail_spec = pl.BlockSpec((tm, D_FF), lambda bi, ti: (0, 0))
        tail_shape = (tm, D_FF)
    return pl.pallas_call(
        functools.partial(_post_kernel, tm=tm, seq=seq),
        grid=(b, nt),
        in_specs=[row(D_MODEL), pl.BlockSpec((1, WA, tm), lambda bi, ti: (bi * nt + ti, 0, 0)), row(WB),
                  row(D_MODEL), row(D_MODEL), row(D_PLE), hist_spec,
                  _const_spec((WA, D_MODEL)), _const_spec((WB, D_MODEL)), _const_spec((D_MODEL, D_MODEL)), vec,
                  _const_spec((D_MODEL, 2 * D_FF)), _const_spec((CONV_F, D_FF)), _const_spec((D_FF, D_MODEL)),
                  vec, _const_spec((D_MODEL, D_MODEL)), _const_spec((D_PLE, D_MODEL)), vec],
        out_specs=[row(D_MODEL), tail_spec],
        out_shape=[jax.ShapeDtypeStruct((b, t, D_MODEL), F32), jax.ShapeDtypeStruct(tail_shape, F32)],
        scratch_shapes=[pltpu.VMEM((8 + tm, D_FF), F32)],
        compiler_params=_cparams(("arbitrary", "arbitrary")),
        name="post",
    )(x, oaT, ob, ga, gbr, p, hist, w_proj_a.astype(BF16), w_proj_b.astype(BF16), w_out.astype(BF16),
      norm_ffn.reshape(1, D_MODEL), w_up.astype(BF16), conv_ffn, w_down.astype(BF16),
      norm_ple.reshape(1, D_MODEL), w_ple_gate.astype(BF16), w_ple.astype(BF16), norm_final.reshape(1, D_MODEL))


def _pad_hist(hist, rows=8):
    b, r, c = hist.shape
    return jnp.concatenate([jnp.zeros((b, rows - r, c), hist.dtype), hist], axis=1)


def _layer(x, p, past_k, past_v, past_kidx, s_gdn, conv_b_hist, ffn_hist, wts, *, tm, tq):
    (norm_mix, w_in, conv_b, a_log, dt_bias, norm_gdn, w_proj_a, w_proj_b, w_out, norm_ffn, w_up, conv_ffn,
     w_down, norm_ple, w_ple, w_ple_gate, rel_bias, norm_final) = wts
    b, t, _ = x.shape
    n = b * t
    past = past_k.shape[1]
    topk = min(TOPK_MAX, (past + t) // 4)
    x2d = x.reshape(n, D_MODEL)
    tmi = min(tm, n)
    (ka, va, kbf, conv_in, gb, ga, gbr, small, ki3, qT, qihT, qilT, wiT, vTa) = _in_proj(x2d, norm_mix, w_in, tmi)

    if past == 0:
        assert tq == tmi and t % tq == 0 and tq % CHUNK == 0 and tq >= topk
        nq = t // tq
        kj = np.arange(tq)[:, None]
        adm = ((kj // CHUNK) <= (np.arange(tq)[None, :] // CHUNK)).astype(np.float32)
        oT = _dsa(qT, qihT, qilT, wiT, kbf.reshape(b, nq, tq, WA), vTa.reshape(b, nq, HA * V_ROWS, tq),
                  ki3.reshape(b, nq, tq, 256), _bias_tables(rel_bias, adm, tq, tq, tq), adm,
                  nq=nq, causal=True, topk=topk)
    else:
        assert n == tmi
        oT = _dsa_step(qT, qihT, qilT, wiT, kbf, va.reshape(n, WA), small[:, :D_IDX], past_k, past_v, past_kidx, rel_bias,
                       b=b, t=t, topk=topk)
        oT = oT[:, :, :t]

    tp = -(-t // CHUNK) * CHUNK
    padt = lambda a: jnp.pad(a.reshape(b, t, a.shape[-1]), ((0, 0), (0, tp - t), (0, 0)))
    ob, s_new = _gdn(padt(conv_in), _pad_hist(conv_b_hist), padt(gb), padt(small), s_gdn, conv_b, a_log, dt_bias,
                     norm_gdn, nvalid=min(t, CHUNK))
    new_conv_b = jnp.concatenate([conv_b_hist, conv_in.reshape(b, t, C_CONV_B)], axis=1)[:, t:]

    post_w = (w_proj_a, w_proj_b, w_out, norm_ffn, w_up, conv_ffn, w_down, norm_ple, w_ple_gate, w_ple, norm_final)
    if past == 0:
        per_bt = lambda a: a.reshape(b, t, a.shape[-1])
        y, tail = _post(x, oT, ob, per_bt(ga), per_bt(gbr), p, _pad_hist(ffn_hist), *post_w, tm)
        new_ffn = tail[:, 8 - (CONV_F - 1):]
    else:
        assert n == tmi and t >= CONV_F - 1
        one = lambda a: a.reshape(1, n, a.shape[-1])
        first = jnp.zeros((b, t, D_FF), F32)
        hist = jnp.stack([first.at[:, 0].set(ffn_hist[:, 1]),
                          first.at[:, 0].set(ffn_hist[:, 0]).at[:, 1].set(ffn_hist[:, 1])]).reshape(2, n, D_FF)
        y, ug = _post(one(x), jnp.swapaxes(oT, 0, 1).reshape(1, WA, n), one(ob[:, :t]), one(ga), one(gbr), one(p),
                      hist, *post_w, n, seq=t)
        y = y.reshape(b, t, D_MODEL)
        new_ffn = ug.reshape(b, t, D_FF)[:, t - (CONV_F - 1):]
    return (y, ka.reshape(b, t, HA, DH), va.reshape(b, t, HA, DH), small[:, :D_IDX].reshape(b, t, D_IDX),
            s_new, new_conv_b, new_ffn)


def kernel(x_prompt, x_sample, p_prompt, p_sample, cache_k, cache_v, cache_kidx, state_gdn, state_gdn_conv,
           state_ffn_conv, norm_mix, w_in, conv_b, a_log, dt_bias, norm_gdn, w_proj_a, w_proj_b, w_out, norm_ffn,
           w_up, conv_ffn, w_down, norm_ple, w_ple, w_ple_gate, rel_bias, norm_final):
    assert norm_mix.shape[0] == 1
    bp = x_prompt.shape[0]
    dt = x_prompt.dtype
    wts = (norm_mix[0], w_in[0], conv_b[0], a_log[0], dt_bias[0], norm_gdn[0], w_proj_a[0], w_proj_b[0], w_out[0],
           norm_ffn[0], w_up[0], conv_ffn[0], w_down[0], norm_ple[0], w_ple[0], w_ple_gate[0], rel_bias, norm_final)
    outs_p = _layer(x_prompt, p_prompt[0], jnp.zeros((bp, 0, HA, DH), dt), jnp.zeros((bp, 0, HA, DH), dt),
                    jnp.zeros((bp, 0, D_IDX), dt), jnp.zeros((bp, HB, DK, DV), dt),
                    jnp.zeros((bp, CONV_B - 1, C_CONV_B), dt), jnp.zeros((bp, CONV_F - 1, D_FF), dt),
                    wts, tm=256, tq=256)
    outs_s = _layer(x_sample, p_sample[0], cache_k[0], cache_v[0], cache_kidx[0], state_gdn[0],
                    state_gdn_conv[0], state_ffn_conv[0], wts, tm=256, tq=256)
    yp, ys = outs_p[0], outs_s[0]
    return (yp, ys) + tuple(a[None] for a in outs_p[1:]) + tuple(a[None] for a in outs_s[1:])
```

```python
import functools

import numpy as np
import jax
import jax.numpy as jnp
from jax import lax
from jax.experimental import pallas as pl
from jax.experimental.pallas import tpu as pltpu

F32 = jnp.float32
BF16 = jnp.bfloat16
I32 = jnp.int32

D_MODEL = 1024
CHUNK = 64
HA, DH = 8, 64
H_IDX, D_IDX = 8, 64
TOPK_MAX = 256
NUM_BUCKETS, MAX_DISTANCE = 32, 128
HB, DK, DV = 4, 128, 128
CONV_B = 4
D_FF = 2816
CONV_F = 3
D_PLE = 256
EPS = 1e-6
NEG = -1e30
WA = HA * DH
WB = HB * DK
C_CONV_B = 3 * WB
INT_MIN = -2 ** 31
LOG2E = 1.4426950408889634

_O_QA, _O_KA, _O_VA, _O_QI, _O_KI, _O_WI = 0, 512, 1024, 1536, 2048, 2112
_O_QB, _O_GB, _O_BB, _O_AB, _O_GA, _O_GBR = 2120, 3656, 4168, 4172, 4176, 5200
_L_WI, _L_BB, _L_AB = 64, 72, 76

VMEM_LIMIT = 56 * 1024 * 1024


def _cparams(sem):
    return pltpu.CompilerParams(dimension_semantics=sem, vmem_limit_bytes=VMEM_LIMIT)


def _const_spec(shape):
    nd = len(shape)
    return pl.BlockSpec(shape, lambda *_: (0,) * nd, pipeline_mode=pl.Buffered(1))


def _rms(x, g):
    return x * lax.rsqrt(jnp.mean(x * x, axis=-1, keepdims=True) + EPS) * g


def _split(x):
    hi = x.astype(BF16)
    lo = (x - hi.astype(F32)).astype(BF16)
    return hi, lo


def _dot(a, b):
    return jnp.dot(a, b, preferred_element_type=F32)


def _dot3(a, b):
    ah, al = _split(a)
    bh, bl = _split(b)
    return _dot(ah, bh) + (_dot(al, bh) + _dot(ah, bl))


def _sigmoid(x):
    return 1.0 / (1.0 + jnp.exp(-x))


def _silu(x):
    return x * _sigmoid(x)


def _transpose32(a):
    a = list(a)
    j, m = 16, 0x0000FFFF
    while j:
        k = 0
        while k < 32:
            t = (a[k] ^ lax.shift_right_logical(a[k + j], jnp.int32(j))) & jnp.int32(m - (1 << 32) if m >> 31 else m)
            a[k] = a[k] ^ t
            a[k + j] = a[k + j] ^ (t << j)
            k = (k + j + 1) & ~j
        j >>= 1
        m = (m ^ (m << j)) & 0xFFFFFFFF
    return a


V_ROWS = DH + 16
RADIX_TILES = 8
_NT = (((1,), (1,)), ((), ()))


def _in_proj_kernel(x_ref, g_ref, wm_ref, wt_ref, wqh_ref, wql_ref, wsh_ref, wsl_ref, wwh_ref, wwl_ref,
                    ph_ref, plo_ref, sc_ref,
                    ka_ref, va_ref, kbf_ref, conv_ref, gb_ref, ga_ref, gbr_ref, small_ref, ki3_ref,
                    qT_ref, qihT_ref, qilT_ref, wiT_ref, vTa_ref):
    tm = x_ref.shape[0]
    h = _rms(x_ref[...], g_ref[...])
    hh, hl = _split(h)

    def main(lo, hi):
        return _dot(hh, wm_ref[:, lo:hi])

    def nt(w, a):
        return lax.dot_general(w, a, _NT, preferred_element_type=F32)

    def nt3(wh_ref, wl_ref):
        return nt(wh_ref[...], hh) + (nt(wh_ref[...], hl) + nt(wl_ref[...], hh))

    ka = main(0, 512)
    va = main(512, 1024)
    for hd in range(HA):
        ka_ref[:, hd, :] = ka[:, hd * DH:(hd + 1) * DH]
        va_ref[:, hd, :] = va[:, hd * DH:(hd + 1) * DH]
    kbf_ref[...] = ka.astype(BF16)
    conv_ref[...] = main(1024, 2560)
    gb_ref[...] = main(2560, 3072)
    ga_ref[...] = main(3072, 4096)
    gbr_ref[...] = main(4096, 5120)

    qT_ref[0] = nt(wt_ref[0:WA, :], hh).astype(BF16)
    vT = nt(wt_ref[WA:2 * WA, :], hh).astype(BF16)
    ones = jnp.ones((V_ROWS - DH, tm), BF16)
    for hd in range(HA):
        vTa_ref[0, hd * V_ROWS:hd * V_ROWS + DH, :] = vT[hd * DH:(hd + 1) * DH]
        vTa_ref[0, hd * V_ROWS + DH:(hd + 1) * V_ROWS, :] = ones
    qh, ql = _split(nt3(wqh_ref, wql_ref))
    qihT_ref[0] = qh
    qilT_ref[0] = ql
    wiT_ref[0] = nt3(wwh_ref, wwl_ref)[0:H_IDX] * (H_IDX ** -0.5)

    small = (_dot(hh, wsh_ref[...]) + (_dot(hl, wsh_ref[...]) + _dot(hh, wsl_ref[...]))) * sc_ref[...]
    small_ref[...] = small
    sh, sl = _split(small)
    ki3_ref[...] = (_dot(sh, ph_ref[...]) + _dot(sl, plo_ref[...])).astype(BF16)


def _in_proj(x2d, norm_mix, w_in, tm):
    n = x2d.shape[0]
    assert n % tm == 0
    g = n // tm
    w = w_in
    wm = jnp.concatenate([w[:, _O_KA:_O_QI], w[:, _O_QB:_O_BB], w[:, _O_GA:]], axis=1).astype(BF16)
    wt = jnp.concatenate([w[:, _O_QA:_O_KA] * (DH ** -0.5 * LOG2E), w[:, _O_VA:_O_QI]], axis=1).T.astype(BF16)
    hilo = lambda a: (a.astype(BF16), (a - a.astype(BF16).astype(F32)).astype(BF16))
    wqh, wql = hilo((w[:, _O_QI:_O_KI] * (D_IDX ** -0.5)).T)
    wsh, wsl = hilo(jnp.concatenate([w[:, _O_KI:_O_QB], w[:, _O_BB:_O_GA], jnp.zeros((D_MODEL, 48), F32)], axis=1))
    wwh, wwl = hilo(jnp.concatenate([w[:, _O_WI:_O_QB], jnp.zeros((D_MODEL, 8), F32)], axis=1).T)
    ph = np.zeros((128, 256), np.float32)
    plo = np.zeros((128, 256), np.float32)
    for c in range(64):
        ph[c, c] = 1.0
        ph[c, 64 + c] = 1.0
        plo[c, 128 + c] = 1.0
    sc = np.ones((1, 128), np.float32)
    sc[0, _L_WI:_L_WI + H_IDX] = H_IDX ** -0.5
    row = lambda c: pl.BlockSpec((tm, c), lambda i: (i, 0))
    colT = lambda r: pl.BlockSpec((1, r, tm), lambda i: (i, 0, 0))
    heads = pl.BlockSpec((tm, HA, DH), lambda i: (i, 0, 0))
    out_cols = [(512, BF16), (1536, F32), (512, F32), (1024, F32), (1024, F32), (128, F32), (256, BF16)]
    out_rows = [(WA, BF16), (WA, BF16), (WA, BF16), (H_IDX, F32), (HA * V_ROWS, BF16)]
    return pl.pallas_call(
        _in_proj_kernel,
        grid=(g,),
        in_specs=[row(D_MODEL), _const_spec((1, D_MODEL)), _const_spec((D_MODEL, 5120)),
                  _const_spec((2 * WA, D_MODEL)), _const_spec((WA, D_MODEL)), _const_spec((WA, D_MODEL)),
                  _const_spec((D_MODEL, 128)), _const_spec((D_MODEL, 128)),
                  _const_spec((16, D_MODEL)), _const_spec((16, D_MODEL)),
                  _const_spec((128, 256)), _const_spec((128, 256)), _const_spec((1, 128))],
        out_specs=[heads, heads] + [row(c) for c, _ in out_cols] + [colT(r) for r, _ in out_rows],
        out_shape=[jax.ShapeDtypeStruct((n, HA, DH), F32)] * 2
        + [jax.ShapeDtypeStruct((n, c), d) for c, d in out_cols]
        + [jax.ShapeDtypeStruct((g, r, tm), d) for r, d in out_rows],
        compiler_params=_cparams(("arbitrary",)),
        name="in_proj",
    )(x2d, norm_mix.reshape(1, D_MODEL), wm, wt, wqh, wql, wsh, wsl, wwh, wwl, jnp.asarray(ph, BF16),
      jnp.asarray(plo, BF16), jnp.asarray(sc))


def _store_keys(score, j, key_ref, pln_ref, eq_ref):
    tk, tq = score.shape
    bits = pltpu.bitcast(score, I32)
    bits = jnp.where(bits == INT_MIN, 0, bits)
    key = bits ^ ((bits >> 31) & 0x7FFFFFFF)
    key_ref[j] = key
    u3 = (key ^ INT_MIN).reshape(tk // 8, 8, tq)
    ones = jnp.full((8, tq), -1, I32)
    pln_ref[0, j] = ones
    for l0 in range(0, tq, 128):
        planes = _transpose32([u3[r][:, l0:l0 + 128] for r in range(32)])
        for b in range(32):
            pln_ref[b + 1, j, :, l0:l0 + 128] = planes[b]
    eq_ref[0, j] = ones


def _radix_threshold(pln_ref, eq_ref, nk, tq, topk):
    for d in range(RADIX_TILES - 1):
        pln_ref[:, nk + d] = jnp.zeros((33, 8, tq), I32)
        eq_ref[0, nk + d] = jnp.zeros((8, tq), I32)

    sub = lax.broadcasted_iota(I32, (8, tq), 0)
    group_masks = (-0x10000, -0xFF0100, -0xF0F0F10, -0x33333334, -0x55555556)

    def pos_plane(e, j):
        if e < 6:
            return jnp.broadcast_to(-((~j >> (5 - e)) & 1), (8, tq))
        if e < 11:
            return jnp.full((8, tq), group_masks[e - 6], I32)
        return -((~sub >> (13 - e)) & 1)

    def sweep(prev, cur, carry, src, dst):
        n_gt, flip = carry

        def body(jq, cnt):
            for d in range(RADIX_TILES):
                j = jq * RADIX_TILES + d
                e = eq_ref[src, j] & (prev(j) ^ flip)
                eq_ref[dst, j] = e
                cnt = cnt + lax.population_count(e & cur(j))
            return cnt

        cnt = lax.fori_loop(0, (nk + RADIX_TILES - 1) // RADIX_TILES, body, jnp.zeros((8, tq), I32))
        cnt = n_gt + cnt.sum(axis=0, keepdims=True)
        acc = cnt >= topk
        return acc, (jnp.where(acc, n_gt, cnt), jnp.where(acc, 0, -1))

    def key_sweep(it, carry, src, dst):
        tu, rest = carry
        acc, rest = sweep(lambda j: pln_ref[it, j], lambda j: pln_ref[it + 1, j], rest, src, dst)
        return jnp.where(acc, tu | jnp.left_shift(jnp.int32(1), 31 - it), tu), rest

    def two_bits(i, carry):
        return key_sweep(2 * i + 1, key_sweep(2 * i, carry, 0, 1), 1, 0)

    zero_row = jnp.zeros((1, tq), I32)
    tu, rest = lax.fori_loop(0, 16, two_bits, (zero_row, (zero_row, zero_row)))
    inv_pos = zero_row
    for e in range(14):
        prev = (lambda j: pln_ref[32, j]) if e == 0 else functools.partial(pos_plane, e - 1)
        acc, rest = sweep(prev, functools.partial(pos_plane, e), rest, e % 2, 1 - e % 2)
        inv_pos = jnp.where(acc, inv_pos | (1 << (13 - e)), inv_pos)
    return tu ^ INT_MIN, ~inv_pos & 0x3FFF


def _dsa_kernel(qT_ref, qihT_ref, qilT_ref, wiT_ref, k_ref, vT_ref, ki3_ref, bias_ref, cap_ref,
                o_ref, key_ref, l_ref, acc_ref, lg_ref, pln_ref, eq_ref, sc_ref, ma_ref,
                *, tq, tk, nk_static, causal, topk):
    nk = (pl.program_id(1) + 1) if causal else nk_static
    wiT = wiT_ref[0]

    def when(cond):
        if isinstance(cond, bool):
            return (lambda f: f()) if cond else (lambda f: None)
        return pl.when(cond)

    zeros64 = jnp.zeros((64, tq), BF16)
    q3 = []
    for h in range(H_IDX):
        hi = qihT_ref[0, h * 64:(h + 1) * 64, :]
        lo = qilT_ref[0, h * 64:(h + 1) * 64, :]
        q3.append(jnp.concatenate([hi, lo, hi, zeros64], axis=0))

    def score_matmuls(j0, n):
        kts = [ki3_ref[0, j0 + d] for d in range(n)]
        accs = [None] * n
        for h in range(H_IDX):
            for d in range(n):
                t = jnp.maximum(_dot(kts[d], q3[h]), 0.0) * wiT[h:h + 1, :]
                accs[d] = t if accs[d] is None else accs[d] + t
        for d in range(n):
            sc_ref[d] = jnp.minimum(accs[d], cap_ref[jnp.where(j0 + d == nk - 1, 1, 0)])

    def score_finish(j0, n):
        for d in range(n):
            _store_keys(sc_ref[d], j0 + d, key_ref, pln_ref, eq_ref)

    def score_body(i, c):
        score_finish(2 * i - 2, 2)
        score_matmuls(2 * i, 2)
        return c

    @when(nk >= 2)
    def _():
        score_matmuls(0, 2)
        lax.fori_loop(1, nk // 2, score_body, 0)
        score_finish(2 * (nk // 2) - 2, 2)

    @when(nk % 2 == 1)
    def _():
        score_matmuls(nk - 1, 1)
        score_finish(nk - 1, 1)

    thr, pos_thr = _radix_threshold(pln_ref, eq_ref, nk, tq, topk)

    l_ref[...] = jnp.zeros(l_ref.shape, F32)
    acc_ref[...] = jnp.zeros(acc_ref.shape, F32)
    qm = []
    for h in range(HA):
        qh = qT_ref[0, h * 64:(h + 1) * 64, :]
        qm.append(jnp.concatenate([qh, zeros64] if h % 2 == 0 else [zeros64, qh], axis=0))
    row = lax.broadcasted_iota(I32, (tk, tq), 0)

    def tile_logits(j, slot, far):
        sel = key_ref[j] > jnp.where(row <= pos_thr - j * tk, thr - 1, thr)
        kind = jnp.clip(j - (nk - 3), 0, 2)
        mx = []
        for h in range(HA):
            pr = h // 2
            lg = _dot(k_ref[0, j, :, pr * 128:(pr + 1) * 128], qm[h])
            if not far:
                lg = lg + bias_ref[kind, h]
            lg = jnp.where(sel, lg, NEG).astype(BF16)
            lg_ref[slot, h] = lg
            part = jnp.max(lg.reshape(tk // 16, 16, tq), axis=0).astype(F32)
            mx.append(jnp.max(part, axis=0, keepdims=True))
        return jnp.concatenate(mx, axis=0)

    def tile_values(j, slot, mx):
        m_old = ma_ref[...]
        m_new = jnp.maximum(m_old, mx)
        alpha = jnp.exp2(m_old - m_new)
        ma_ref[...] = m_new
        ls = []
        for h in range(HA):
            p = jnp.exp2(lg_ref[slot, h] - m_new[h:h + 1, :].astype(BF16))
            pv = _dot(vT_ref[0, j, h * V_ROWS:(h + 1) * V_ROWS, :], p)
            acc_ref[h * 64:(h + 1) * 64, :] = alpha[h:h + 1, :] * acc_ref[h * 64:(h + 1) * 64, :] + pv[0:DH]
            ls.append(pv[DH:DH + 1])
        l_ref[...] = alpha * l_ref[...] + jnp.concatenate(ls, axis=0)

    ma_ref[...] = jnp.full((HA, tq), NEG, F32)

    def pair(j0, far):
        mx0 = tile_logits(j0, 0, far)
        mx1 = tile_logits(j0 + 1, 1, far)
        tile_values(j0, 0, mx0)
        tile_values(j0 + 1, 1, mx1)

    def far_pair(i, c):
        pair(2 * i, True)
        return c

    n_far = jnp.maximum(nk - 2, 0) // 2
    lax.fori_loop(0, n_far, far_pair, 0)

    @when(nk >= 2)
    def _():
        pair(2 * n_far, False)

    @when(nk % 2 == 1)
    def _():
        tile_values(nk - 1, 0, tile_logits(nk - 1, 0, False))

    for h in range(HA):
        o_ref[0, h * 64:(h + 1) * 64, :] = (acc_ref[h * 64:(h + 1) * 64, :] / l_ref[h:h + 1, :]).astype(BF16)


def _t5_bucket_np(rel):
    half = NUM_BUCKETS // 2
    max_exact = half // 2
    out = np.zeros(rel.shape, np.int64)
    flat_rel = rel.reshape(-1)
    flat = out.reshape(-1)
    for a in range(flat_rel.size):
        r = int(flat_rel[a])
        n = abs(r)
        b = n if n < max_exact else min(half - 1, (n * n).bit_length() + 1)
        flat[a] = b + (half if r > 0 else 0)
    return out


def _bias_tables(rel_bias, adm, tk, tq, tq_valid):
    kj = np.arange(tk)[:, None]
    t = np.minimum(np.arange(tq), tq_valid - 1)[None, :]
    rel_to_bucket = _t5_bucket_np(np.arange(-2 * tk - tq, tk + 1))
    lut = lambda rel: rel_to_bucket[rel + 2 * tk + tq]
    far_bucket = NUM_BUCKETS // 2 - 1
    idx = np.stack([np.full((tk, tq), far_bucket), lut(kj - tk - t), lut(kj - t)], axis=0)
    tab = rel_bias.astype(F32) * LOG2E
    onehot = jnp.asarray(idx[..., None] == np.arange(NUM_BUCKETS), F32)
    bias = jnp.einsum("ktqb,bh->khtq", onehot, tab - tab[far_bucket], precision=lax.Precision.HIGHEST)
    mask = np.zeros((3, 1, tk, tq), np.float32)
    mask[2, 0] = np.where(adm > 0.5, 0.0, NEG)
    return bias + jnp.asarray(mask)


def _dsa(qT, qihT, qilT, wiT, k4, vT4, ki34, bias, adm, *, nq, causal, topk):
    g, _, tq = qT.shape
    b, nk, tk, _ = k4.shape
    assert g == b * nq and tk == 256 and nk + RADIX_TILES - 1 <= 64
    cap = jnp.asarray(np.stack([np.full((tk, tq), np.inf, np.float32), np.where(adm > 0.5, np.inf, NEG)]), F32)
    qspec = lambda r: pl.BlockSpec((1, r, tq), lambda bi, i: (bi * nq + i, 0, 0))
    kspec = lambda s: pl.BlockSpec((1,) + s, lambda bi, i: (bi, 0, 0, 0), pipeline_mode=pl.Buffered(1))
    kern = functools.partial(_dsa_kernel, tq=tq, tk=tk, nk_static=nk, causal=causal, topk=topk)
    return pl.pallas_call(
        kern,
        grid=(b, nq),
        in_specs=[qspec(512), qspec(512), qspec(512), qspec(8),
                  kspec((nk, tk, 512)), kspec((nk, HA * V_ROWS, tk)), kspec((nk, tk, 256)),
                  _const_spec((3, HA, tk, tq)), _const_spec((2, tk, tq))],
        out_specs=qspec(512),
        out_shape=jax.ShapeDtypeStruct((g, 512, tq), BF16),
        scratch_shapes=[pltpu.VMEM((nk, tk, tq), I32), pltpu.VMEM((8, tq), F32),
                        pltpu.VMEM((512, tq), F32), pltpu.VMEM((2, HA, tk, tq), BF16),
                        pltpu.VMEM((33, nk + RADIX_TILES - 1, 8, tq), I32),
                        pltpu.VMEM((2, nk + RADIX_TILES - 1, 8, tq), I32),
                        pltpu.VMEM((2, tk, tq), F32), pltpu.VMEM((HA, tq), F32)],
        compiler_params=_cparams(("arbitrary", "arbitrary")),
        name="dsa_causal" if causal else "dsa_cached",
    )(qT, qihT, qilT, wiT, k4, vT4, ki34, bias, cap)


def _dsa_step_kernel(qa_ref, qih_ref, qil_ref, w_ref, ck_ref, cv_ref, cki_ref, nk_ref, nv_ref, nki_ref,
                     bias_ref, cap_ref, blk_ref, o_ref, key_ref, pln_ref, eq_ref, m_ref, l_ref, acc_ref,
                     *, nkc, t, topk):
    tk, tq = nk_ref.shape[1], qa_ref.shape[2]
    qih, qil = qih_ref[0], qil_ref[0]
    w = w_ref[0, 0:1, :]
    shifts = [t << s for s in range((tq // t).bit_length() - 2, -1, -1)]

    def scores(kx, j, cap):
        kh, kl = _split(kx)
        s = _dot(kh, qih) + (_dot(kl, qih) + _dot(kh, qil))
        yield
        sc = jnp.maximum(s, 0.0) * w
        for sh in shifts:
            sc = sc + pltpu.roll(sc, sh, 1)
            yield
        if cap is not None:
            sc = jnp.minimum(sc, cap)
        _store_keys(sc, j, key_ref, pln_ref, eq_ref)

    def together(*gens):
        gens = list(gens)
        while gens:
            gens = [g for g in gens if next(g, gens) is not gens]

    def score_body(i, c):
        cached_idx = lambda j: cki_ref[0, pl.ds(pl.multiple_of(j * tk, tk), tk), :]
        together(scores(cached_idx(2 * i), 2 * i, None), scores(cached_idx(2 * i + 1), 2 * i + 1, None))
        return c

    lax.fori_loop(0, nkc // 2, score_body, 0)
    together(scores(nki_ref[0], nkc, cap_ref[...]))
    thr, pos_thr = _radix_threshold(pln_ref, eq_ref, nkc + 1, tq, topk)

    m_ref[...] = jnp.full(m_ref.shape, NEG, F32)
    l_ref[...] = jnp.zeros(l_ref.shape, F32)
    acc_ref[...] = jnp.zeros(acc_ref.shape, F32)
    qa = qa_ref[0]
    row = lax.broadcasted_iota(I32, (tk, tq), 0)
    ones8 = jnp.ones((8, tk), BF16)
    tn = (((0,), (0,)), ((), ()))

    def logits(k_bf, j, bias):
        sel = key_ref[j] > jnp.where(row <= pos_thr - j * tk, thr - 1, thr)
        lg = _dot(k_bf, qa)
        if bias is not None:
            lg = lg + bias
        return jnp.where(sel, lg, NEG).astype(BF16)

    def consume(lg, v_bf):
        part = jnp.max(lg.reshape(tk // 16, 16, tq), axis=0).astype(F32)
        m_old = m_ref[0:1, :]
        m_new = jnp.maximum(m_old, jnp.max(part, axis=0, keepdims=True))
        alpha = jnp.exp2(m_old - m_new)
        p = jnp.exp2(lg - m_new.astype(BF16))
        l_ref[0:1, :] = alpha * l_ref[0:1, :] + _dot(ones8, p)[0:1]
        acc_ref[...] = alpha * acc_ref[...] + lax.dot_general(v_bf, p, tn, preferred_element_type=F32)
        m_ref[0:1, :] = m_new

    def cached_pair(j0, bias1):
        r0 = pl.ds(pl.multiple_of(j0 * tk, tk), tk)
        r1 = pl.ds(pl.multiple_of(j0 * tk + tk, tk), tk)
        lg0 = logits(ck_ref[0, r0, :].astype(BF16), j0, None)
        lg1 = logits(ck_ref[0, r1, :].astype(BF16), j0 + 1, bias1)
        consume(lg0, cv_ref[0, r0, :].astype(BF16))
        consume(lg1, cv_ref[0, r1, :].astype(BF16))

    def far(i, c):
        cached_pair(2 * i, None)
        return c

    lax.fori_loop(0, nkc // 2 - 1, far, 0)
    cached_pair(nkc - 2, bias_ref[0])
    consume(logits(nk_ref[0], nkc, bias_ref[1]), nv_ref[0])
    o = acc_ref[...] / l_ref[0:1, :] * blk_ref[...]
    for sh in shifts:
        o = o + pltpu.roll(o, sh, 1)
    o_ref[0] = o.astype(BF16)


def _dsa_step(qT, qihT, qilT, wiT, kbf, va, ki_new, past_k, past_v, past_kidx, rel_bias, *, b, t, topk):
    past = past_k.shape[1]
    tk, tq = 256, HA * t
    assert tq == 128 and past % (2 * tk) == 0 and t <= tk
    nkc = past // tk
    nk = nkc + 1
    per_b = lambda a: jnp.swapaxes(a[0].reshape(a.shape[1], b, t), 0, 1)
    blk = (np.arange(WA)[:, None] // DH == np.arange(tq)[None, :] // t).astype(np.float32)
    qa = (jnp.tile(per_b(qT), (1, 1, HA)) * jnp.asarray(blk, BF16))
    lanes = lambda a: jnp.swapaxes(per_b(a).reshape(b, HA, D_IDX, t), 1, 2).reshape(b, D_IDX, tq)
    w = jnp.broadcast_to(per_b(wiT).reshape(b, 1, tq), (b, 8, tq))
    pad_rows = lambda a: jnp.pad(a.reshape(b, t, a.shape[-1]), ((0, 0), (0, tk - t), (0, 0)))
    adm = np.broadcast_to(np.arange(tk)[:, None] < t, (tk, tq)).astype(np.float32)
    bias = _bias_tables(rel_bias, adm[:, :t], tk, t, t)[1:]
    bias = jnp.swapaxes(bias, 1, 2).reshape(2, tk, tq)
    cap = jnp.asarray(np.where(adm > 0.5, np.inf, NEG), F32)
    bspec = lambda s: pl.BlockSpec((1,) + s, lambda bi: (bi,) + (0,) * len(s))
    return pl.pallas_call(
        functools.partial(_dsa_step_kernel, nkc=nkc, t=t, topk=topk),
        grid=(b,),
        in_specs=[bspec((WA, tq)), bspec((D_IDX, tq)), bspec((D_IDX, tq)), bspec((8, tq)),
                  bspec((past, WA)), bspec((past, WA)), bspec((past, D_IDX)),
                  bspec((tk, WA)), bspec((tk, WA)), bspec((tk, D_IDX)),
                  _const_spec((2, tk, tq)), _const_spec((tk, tq)), _const_spec((WA, tq))],
        out_specs=bspec((WA, tq)),
        out_shape=jax.ShapeDtypeStruct((b, WA, tq), BF16),
        scratch_shapes=[pltpu.VMEM((nk, tk, tq), I32), pltpu.VMEM((33, nk + RADIX_TILES - 1, 8, tq), I32),
                        pltpu.VMEM((2, nk + RADIX_TILES - 1, 8, tq), I32), pltpu.VMEM((8, tq), F32),
                        pltpu.VMEM((8, tq), F32), pltpu.VMEM((WA, tq), F32)],
        compiler_params=_cparams(("arbitrary",)),
        name="dsa_step",
    )(qa, lanes(qihT), lanes(qilT), w, past_k.reshape(b, past, WA), past_v.reshape(b, past, WA), past_kidx,
      pad_rows(kbf), pad_rows(va.astype(BF16)), pad_rows(ki_new), bias, cap, jnp.asarray(blk))


def _gdn_pre_kernel(conv_ref, prev_ref, hist_ref, small_ref, cw_ref, alog_ref, dtb_ref,
                    wm_ref, um_ref, qe_ref, oi_ref, e_ref, ext_ref, *, rows, nvalid):
    c = CHUNK
    n = HB * c
    ext_ref[0:8, :] = jnp.where(pl.program_id(1) == 0, hist_ref[0], prev_ref[0])
    ext_ref[8:8 + rows, :] = conv_ref[0]
    cb = ext_ref[5:5 + rows, :] * cw_ref[0:1, :]
    for j in range(1, CONV_B):
        cb = cb + ext_ref[5 + j:5 + j + rows, :] * cw_ref[j:j + 1, :]
    cb = _silu(cb)

    small = small_ref[0]
    pos = lax.broadcasted_iota(I32, (rows, 1), 0) % c
    rowv = (pos < nvalid).astype(F32)
    beta_all = _sigmoid(small) * rowv
    sp = small + dtb_ref[...]
    g_all = -jnp.exp(alog_ref[...]) * (jnp.maximum(sp, 0.0) + jnp.log(1.0 + jnp.exp(-jnp.abs(sp)))) * rowv

    ri = lax.broadcasted_iota(I32, (n, n), 0)
    ci = lax.broadcasted_iota(I32, (n, n), 1)
    same = (ri // c) == (ci // c)
    tri_b = (same & (ci <= ri)).astype(BF16)
    bd_f = same.astype(F32)
    bd_b = same.astype(BF16)
    wr = lax.broadcasted_iota(I32, (c, n), 0)
    wl = lax.broadcasted_iota(I32, (c, n), 1)
    grp = wl // c
    tri_w = (wl % c) <= wr
    strict_w = (wl % c) < wr
    eye_w = ((wl % c) == wr).astype(F32)
    nt = (((1,), (1,)), ((), ()))
    tn = (((0,), (0,)), ((), ()))

    def l2n(x):
        return x * lax.rsqrt(jnp.sum(x * x, axis=-1, keepdims=True) + EPS)

    def to_wide(full):
        out = jnp.where(grp == 0, full[0:c, :], 0.0)
        for h in range(1, HB):
            out = out + jnp.where(grp == h, full[h * c:(h + 1) * c, :], 0.0)
        return out

    def tile4(x):
        return jnp.concatenate([x] * HB, axis=0)

    def chunk(k0):
        def stack(fn):
            return jnp.concatenate([fn(h) for h in range(HB)], axis=0)

        rv = rowv[k0:k0 + c]
        q = stack(lambda h: l2n(cb[k0:k0 + c, h * DK:(h + 1) * DK]) * (DK ** -0.5))
        k = stack(lambda h: l2n(cb[k0:k0 + c, WB + h * DK:WB + (h + 1) * DK]) * rv)
        v = stack(lambda h: cb[k0:k0 + c, 2 * WB + h * DV:2 * WB + (h + 1) * DV] * rv)
        beta = stack(lambda h: beta_all[k0:k0 + c, _L_BB + h:_L_BB + h + 1])
        g = stack(lambda h: g_all[k0:k0 + c, _L_AB + h:_L_AB + h + 1])
        yield

        gh, gl = _split(jnp.broadcast_to(g, (n, 128)))
        gcum = _dot(tri_b, gh) + _dot(tri_b, gl)
        yield
        gcum_row = gcum.T[0:1, :]
        gcum_col = gcum[:, 0:1]
        col_w = to_wide(jnp.concatenate([gcum, gcum], axis=1))
        decay_w = jnp.where(tri_w, jnp.exp(jnp.where(tri_w, col_w - gcum_row, 0.0)), 0.0)
        kb = k * beta
        kbf = k.astype(BF16)
        kk_w = to_wide(lax.dot_general(kb.astype(BF16), kbf, nt, preferred_element_type=F32))
        qk_w = to_wide(lax.dot_general(q.astype(BF16), kbf, nt, preferred_element_type=F32))
        nmat_w = jnp.where(strict_w, kk_w * decay_w, 0.0)
        attn_w = jnp.where(tri_w, qk_w * decay_w, 0.0)
        yield

        inv_w = eye_w - nmat_w
        ph = nmat_w.astype(BF16)
        bh = tile4(ph) * bd_b
        for _ in range(int(np.log2(c)) - 1):
            ph = _dot(ph, bh).astype(BF16)
            yield
            bh = tile4(ph) * bd_b
            inv_w = inv_w + _dot(inv_w.astype(BF16), bh)
            yield
        rhs = jnp.concatenate([v * beta, kb * jnp.exp(gcum_col)], axis=1)
        sol = _dot3(tile4(inv_w) * bd_f, rhs)
        yield
        solb = sol.astype(BF16)
        aw = _dot((tile4(attn_w) * bd_f).astype(BF16), solb)
        yield
        ck = k0 // c
        oi_ref[0, ck] = aw[:, :DV]
        qe_ref[0, ck] = (q * jnp.exp(gcum_col) - aw[:, DV:]).astype(BF16)
        for h in range(HB):
            rs = slice(h * c, (h + 1) * c)
            g_last = gcum[(h + 1) * c - 1:(h + 1) * c, :]
            kd = (k[rs] * jnp.exp(g_last - gcum[rs])).astype(BF16)
            uw = lax.dot_general(kd, solb[rs], tn, preferred_element_type=F32)
            um_ref[0, ck, h] = uw[:, :DV]
            wm_ref[0, ck, h] = uw[:, DV:].astype(BF16)
            e_ref[0, ck, h:h + 1, :] = jnp.exp(g_last)
        e_ref[0, ck, HB:8, :] = jnp.zeros((8 - HB, 128), F32)

    gens = [chunk(k0) for k0 in range(0, rows, c)]
    while gens:
        alive = []
        for gen in gens:
            try:
                next(gen)
                alive.append(gen)
            except StopIteration:
                pass
        gens = alive


def _gdn_scan_kernel(wm_ref, um_ref, qe_ref, oi_ref, e_ref, gb_ref, s0_ref, ng_ref, ob_ref, sfin_ref, s_ref,
                     *, bb, g):
    c = CHUNK

    @pl.when(pl.program_id(1) == 0)
    def _():
        s_ref[...] = s0_ref[...]

    ng = ng_ref[...]
    for b in range(bb):
        for ck in range(g):
            for h in range(HB):
                s = s_ref[b, h]
                sb = s.astype(BF16)
                rs = slice(h * c, (h + 1) * c)
                o = _dot(qe_ref[b, ck, rs, :], sb) + oi_ref[b, ck, rs, :]
                s_ref[b, h] = e_ref[b, ck, h:h + 1, :] * s + (um_ref[b, ck, h] - _dot(wm_ref[b, ck, h], sb))
                gate = gb_ref[b, ck * c:(ck + 1) * c, h * DV:(h + 1) * DV]
                ob_ref[b, ck * c:(ck + 1) * c, h * DV:(h + 1) * DV] = (_rms(o, ng) * _silu(gate)).astype(BF16)
    sfin_ref[...] = s_ref[...]


def _gdn(conv_in, hist, gb, small, s0, conv_w, a_log, dt_bias, norm_gdn, nvalid):
    b, t, _ = conv_in.shape
    assert t % CHUNK == 0
    rows = 512 if t % 512 == 0 else CHUNK
    nc, cps = t // CHUNK, rows // CHUNK
    n = HB * CHUNK
    alog = jnp.zeros((1, 128), F32).at[0, _L_AB:_L_AB + HB].set(a_log)
    dtb = jnp.zeros((1, 128), F32).at[0, _L_AB:_L_AB + HB].set(dt_bias)
    row = lambda c_: pl.BlockSpec((1, rows, c_), lambda bi, ti: (bi, ti, 0))
    prev = pl.BlockSpec((1, 8, C_CONV_B), lambda bi, ti: (bi, jnp.maximum(ti * (rows // 8) - 1, 0), 0))
    per_b = lambda s: pl.BlockSpec((1,) + s, lambda bi, ti: (bi,) + (0,) * len(s))
    chunked = lambda s: pl.BlockSpec((1, cps) + s, lambda bi, ti: (bi, ti) + (0,) * len(s))
    wm, um, qe, oi, e = pl.pallas_call(
        functools.partial(_gdn_pre_kernel, rows=rows, nvalid=nvalid),
        grid=(b, t // rows),
        in_specs=[row(C_CONV_B), prev, per_b((8, C_CONV_B)), row(128),
                  _const_spec((CONV_B, C_CONV_B)), _const_spec((1, 128)), _const_spec((1, 128))],
        out_specs=[chunked((HB, DK, DV)), chunked((HB, DK, DV)), chunked((n, DK)), chunked((n, DV)),
                   chunked((8, 128))],
        out_shape=[jax.ShapeDtypeStruct((b, nc, HB, DK, DV), BF16), jax.ShapeDtypeStruct((b, nc, HB, DK, DV), F32),
                   jax.ShapeDtypeStruct((b, nc, n, DK), BF16), jax.ShapeDtypeStruct((b, nc, n, DV), F32),
                   jax.ShapeDtypeStruct((b, nc, 8, 128), F32)],
        scratch_shapes=[pltpu.VMEM((8 + rows, C_CONV_B), F32)],
        compiler_params=_cparams(("arbitrary", "arbitrary")),
        name="gdn_pre",
    )(conv_in, conv_in, hist, small, conv_w, alog, dtb)

    bb = 2 if b % 2 == 0 else 1
    g = 2 if nc % 2 == 0 else 1
    blk = lambda s: pl.BlockSpec((bb, g) + s, lambda bi, ci: (bi, ci) + (0,) * len(s))
    rowb = pl.BlockSpec((bb, g * CHUNK, WB), lambda bi, ci: (bi, ci, 0))
    state = pl.BlockSpec((bb, HB, DK, DV), lambda bi, ci: (bi, 0, 0, 0))
    return pl.pallas_call(
        functools.partial(_gdn_scan_kernel, bb=bb, g=g),
        grid=(b // bb, nc // g),
        in_specs=[blk((HB, DK, DV)), blk((HB, DK, DV)), blk((n, DK)), blk((n, DV)), blk((8, 128)), rowb, state,
                  _const_spec((1, DV))],
        out_specs=[rowb, state],
        out_shape=[jax.ShapeDtypeStruct((b, t, WB), BF16), jax.ShapeDtypeStruct((b, HB, DK, DV), F32)],
        scratch_shapes=[pltpu.VMEM((bb, HB, DK, DV), F32)],
        compiler_params=_cparams(("arbitrary", "arbitrary")),
        name="gdn_scan",
    )(wm, um, qe, oi, e, gb, s0, norm_gdn.reshape(1, DV))


def _post_kernel(x_ref, oaT_ref, ob_ref, ga_ref, gbr_ref, p_ref, hist_ref, wa_ref, wb_ref, wo_ref, nf_ref,
                 wup_ref, cw_ref, wdn_ref, npl_ref, wpg_ref, wple_ref, nfin_ref, y_ref, tail_ref, ext_ref,
                 *, tm, seq):
    if seq is None:
        @pl.when(pl.program_id(1) == 0)
        def _():
            ext_ref[0:8, :] = hist_ref[0]
    else:
        ext_ref[0:8, :] = jnp.zeros((8, D_FF), F32)

    ya = lax.dot_general(oaT_ref[0], wa_ref[...], (((0,), (0,)), ((), ())), preferred_element_type=F32)
    yb = _dot(ob_ref[0], wb_ref[...])
    mix = _sigmoid(ga_ref[0]) * ya + _sigmoid(gbr_ref[0]) * yb
    x1 = x_ref[0] + _dot(mix.astype(BF16), wo_ref[...])
    h2 = _rms(x1, nf_ref[...]).astype(BF16)
    ext_ref[8:8 + tm, :] = _dot(h2, wup_ref[:, 0:D_FF])
    u_val = _dot(h2, wup_ref[:, D_FF:2 * D_FF])
    if seq is None:
        cv = ext_ref[6:6 + tm, :] * cw_ref[0:1, :]
        for j in range(1, CONV_F):
            cv = cv + ext_ref[6 + j:6 + j + tm, :] * cw_ref[j:j + 1, :]
        tail = ext_ref[tm:tm + 8, :]
        ext_ref[0:8, :] = tail
        tail_ref[0] = tail
    else:
        pos = lax.broadcasted_iota(I32, (tm, 1), 0) % seq
        cv = jnp.where(pos >= 2, ext_ref[6:6 + tm, :], hist_ref[1]) * cw_ref[0:1, :]
        cv = cv + jnp.where(pos >= 1, ext_ref[7:7 + tm, :], hist_ref[0]) * cw_ref[1:2, :]
        cv = cv + ext_ref[8:8 + tm, :] * cw_ref[2:3, :]
        tail_ref[...] = ext_ref[8:8 + tm, :]
    act = 0.5 * cv * (1.0 + jnp.tanh(0.7978845608028654 * (cv + 0.044715 * (cv * cv * cv))))
    x2 = x1 + _dot((act * u_val).astype(BF16), wdn_ref[...])
    gate = _sigmoid(_dot(_rms(x2, npl_ref[...]).astype(BF16), wpg_ref[...]))
    x3 = x2 + gate * _dot(p_ref[0].astype(BF16), wple_ref[...])
    y_ref[0] = _rms(x3, nfin_ref[...])


def _post(x, oaT, ob, ga, gbr, p, hist, w_proj_a, w_proj_b, w_out, norm_ffn, w_up, conv_ffn, w_down, norm_ple,
          w_ple_gate, w_ple, norm_final, tm, seq=None):
    b, t, _ = x.shape
    nt = t // tm
    assert t % tm == 0 and tm >= 8 and oaT.shape == (b * nt, WA, tm)
    row = lambda c: pl.BlockSpec((1, tm, c), lambda bi, ti: (bi, ti, 0))
    vec = _const_spec((1, D_MODEL))
    if seq is None:
        hist_spec = tail_spec = pl.BlockSpec((1, 8, D_FF), lambda bi, ti: (bi, 0, 0))
        tail_shape = (b, 8, D_FF)
    else:
        assert b == 1 and nt == 1 and tm % seq == 0
        hist_spec = _const_spec((2, tm, D_FF))
        tail_spec = pl.BlockSpec((tm, D_FF), lambda bi, ti: (0, 0))
        tail_shape = (tm, D_FF)
    return pl.pallas_call(
        functools.partial(_post_kernel, tm=tm, seq=seq),
        grid=(b, nt),
        in_specs=[row(D_MODEL), pl.BlockSpec((1, WA, tm), lambda bi, ti: (bi * nt + ti, 0, 0)), row(WB),
                  row(D_MODEL), row(D_MODEL), row(D_PLE), hist_spec,
                  _const_spec((WA, D_MODEL)), _const_spec((WB, D_MODEL)), _const_spec((D_MODEL, D_MODEL)), vec,
                  _const_spec((D_MODEL, 2 * D_FF)), _const_spec((CONV_F, D_FF)), _const_spec((D_FF, D_MODEL)),
                  vec, _const_spec((D_MODEL, D_MODEL)), _const_spec((D_PLE, D_MODEL)), vec],
        out_specs=[row(D_MODEL), tail_spec],
        out_shape=[jax.ShapeDtypeStruct((b, t, D_MODEL), F32), jax.ShapeDtypeStruct(tail_shape, F32)],
        scratch_shapes=[pltpu.VMEM((8 + tm, D_FF), F32)],
        compiler_params=_cparams(("arbitrary", "arbitrary")),
        name="post",
    )(x, oaT, ob, ga, gbr, p, hist, w_proj_a.astype(BF16), w_proj_b.astype(BF16), w_out.astype(BF16),
      norm_ffn.reshape(1, D_MODEL), w_up.astype(BF16), conv_ffn, w_down.astype(BF16),
      norm_ple.reshape(1, D_MODEL), w_ple_gate.astype(BF16), w_ple.astype(BF16), norm_final.reshape(1, D_MODEL))


def _pad_hist(hist, rows=8):
    b, r, c = hist.shape
    return jnp.concatenate([jnp.zeros((b, rows - r, c), hist.dtype), hist], axis=1)


def _layer(x, p, past_k, past_v, past_kidx, s_gdn, conv_b_hist, ffn_hist, wts, *, tm, tq):
    (norm_mix, w_in, conv_b, a_log, dt_bias, norm_gdn, w_proj_a, w_proj_b, w_out, norm_ffn, w_up, conv_ffn,
     w_down, norm_ple, w_ple, w_ple_gate, rel_bias, norm_final) = wts
    b, t, _ = x.shape
    n = b * t
    past = past_k.shape[1]
    topk = min(TOPK_MAX, (past + t) // 4)
    x2d = x.reshape(n, D_MODEL)
    tmi = min(tm, n)
    (ka, va, kbf, conv_in, gb, ga, gbr, small, ki3, qT, qihT, qilT, wiT, vTa) = _in_proj(x2d, norm_mix, w_in, tmi)

    if past == 0:
        assert tq == tmi and t % tq == 0 and tq % CHUNK == 0 and tq >= topk
        nq = t // tq
        kj = np.arange(tq)[:, None]
        adm = ((kj // CHUNK) <= (np.arange(tq)[None, :] // CHUNK)).astype(np.float32)
        oT = _dsa(qT, qihT, qilT, wiT, kbf.reshape(b, nq, tq, WA), vTa.reshape(b, nq, HA * V_ROWS, tq),
                  ki3.reshape(b, nq, tq, 256), _bias_tables(rel_bias, adm, tq, tq, tq), adm,
                  nq=nq, causal=True, topk=topk)
    else:
        assert n == tmi
        oT = _dsa_step(qT, qihT, qilT, wiT, kbf, va.reshape(n, WA), small[:, :D_IDX], past_k, past_v, past_kidx, rel_bias,
                       b=b, t=t, topk=topk)
        oT = oT[:, :, :t]

    tp = -(-t // CHUNK) * CHUNK
    padt = lambda a: jnp.pad(a.reshape(b, t, a.shape[-1]), ((0, 0), (0, tp - t), (0, 0)))
    ob, s_new = _gdn(padt(conv_in), _pad_hist(conv_b_hist), padt(gb), padt(small), s_gdn, conv_b, a_log, dt_bias,
                     norm_gdn, nvalid=min(t, CHUNK))
    new_conv_b = jnp.concatenate([conv_b_hist, conv_in.reshape(b, t, C_CONV_B)], axis=1)[:, t:]

    post_w = (w_proj_a, w_proj_b, w_out, norm_ffn, w_up, conv_ffn, w_down, norm_ple, w_ple_gate, w_ple, norm_final)
    if past == 0:
        per_bt = lambda a: a.reshape(b, t, a.shape[-1])
        y, tail = _post(x, oT, ob, per_bt(ga), per_bt(gbr), p, _pad_hist(ffn_hist), *post_w, tm)
        new_ffn = tail[:, 8 - (CONV_F - 1):]
    else:
        assert n == tmi and t >= CONV_F - 1
        one = lambda a: a.reshape(1, n, a.shape[-1])
        first = jnp.zeros((b, t, D_FF), F32)
        hist = jnp.stack([first.at[:, 0].set(ffn_hist[:, 1]),
                          first.at[:, 0].set(ffn_hist[:, 0]).at[:, 1].set(ffn_hist[:, 1])]).reshape(2, n, D_FF)
        y, ug = _post(one(x), jnp.swapaxes(oT, 0, 1).reshape(1, WA, n), one(ob[:, :t]), one(ga), one(gbr), one(p),
                      hist, *post_w, n, seq=t)
        y = y.reshape(b, t, D_MODEL)
        new_ffn = ug.reshape(b, t, D_FF)[:, t - (CONV_F - 1):]
    return (y, ka.reshape(b, t, HA, DH), va.reshape(b, t, HA, DH), small[:, :D_IDX].reshape(b, t, D_IDX),
            s_new, new_conv_b, new_ffn)


def kernel(x_prompt, x_sample, p_prompt, p_sample, cache_k, cache_v, cache_kidx, state_gdn, state_gdn_conv,
           state_ffn_conv, norm_mix, w_in, conv_b, a_log, dt_bias, norm_gdn, w_proj_a, w_proj_b, w_out, norm_ffn,
           w_up, conv_ffn, w_down, norm_ple, w_ple, w_ple_gate, rel_bias, norm_final):
    assert norm_mix.shape[0] == 1
    bp = x_prompt.shape[0]
    dt = x_prompt.dtype
    wts = (norm_mix[0], w_in[0], conv_b[0], a_log[0], dt_bias[0], norm_gdn[0], w_proj_a[0], w_proj_b[0], w_out[0],
           norm_ffn[0], w_up[0], conv_ffn[0], w_down[0], norm_ple[0], w_ple[0], w_ple_gate[0], rel_bias, norm_final)
    outs_p = _layer(x_prompt, p_prompt[0], jnp.zeros((bp, 0, HA, DH), dt), jnp.zeros((bp, 0, HA, DH), dt),
                    jnp.zeros((bp, 0, D_IDX), dt), jnp.zeros((bp, HB, DK, DV), dt),
                    jnp.zeros((bp, CONV_B - 1, C_CONV_B), dt), jnp.zeros((bp, CONV_F - 1, D_FF), dt),
                    wts, tm=256, tq=256)
    outs_s = _layer(x_sample, p_sample[0], cache_k[0], cache_v[0], cache_kidx[0], state_gdn[0],
                    state_gdn_conv[0], state_ffn_conv[0], wts, tm=256, tq=256)
    yp, ys = outs_p[0], outs_s[0]
    return (yp, ys) + tuple(a[None] for a in outs_p[1:]) + tuple(a[None] for a in outs_s[1:])
```

```python
import functools

import numpy as np
import jax
import jax.numpy as jnp
from jax import lax
from jax.experimental import pallas as pl
from jax.experimental.pallas import tpu as pltpu

F32 = jnp.float32
BF16 = jnp.bfloat16
I32 = jnp.int32

D_MODEL = 1024
CHUNK = 64
HA, DH = 8, 64
H_IDX, D_IDX = 8, 64
TOPK_MAX = 256
NUM_BUCKETS, MAX_DISTANCE = 32, 128
HB, DK, DV = 4, 128, 128
CONV_B = 4
D_FF = 2816
CONV_F = 3
D_PLE = 256
EPS = 1e-6
NEG = -1e30
WA = HA * DH
WB = HB * DK
C_CONV_B = 3 * WB
INT_MIN = -2 ** 31
LOG2E = 1.4426950408889634

_O_QA, _O_KA, _O_VA, _O_QI, _O_KI, _O_WI = 0, 512, 1024, 1536, 2048, 2112
_O_QB, _O_GB, _O_BB, _O_AB, _O_GA, _O_GBR = 2120, 3656, 4168, 4172, 4176, 5200
_L_WI, _L_BB, _L_AB = 64, 72, 76

VMEM_LIMIT = 56 * 1024 * 1024


def _cparams(sem):
    return pltpu.CompilerParams(dimension_semantics=sem, vmem_limit_bytes=VMEM_LIMIT)


def _const_spec(shape):
    nd = len(shape)
    return pl.BlockSpec(shape, lambda *_: (0,) * nd, pipeline_mode=pl.Buffered(1))


def _rms(x, g):
    return x * lax.rsqrt(jnp.mean(x * x, axis=-1, keepdims=True) + EPS) * g


def _split(x):
    hi = x.astype(BF16)
    lo = (x - hi.astype(F32)).astype(BF16)
    return hi, lo


def _dot(a, b):
    return jnp.dot(a, b, preferred_element_type=F32)


def _dot3(a, b):
    ah, al = _split(a)
    bh, bl = _split(b)
    return _dot(ah, bh) + (_dot(al, bh) + _dot(ah, bl))


def _sigmoid(x):
    return 1.0 / (1.0 + jnp.exp(-x))


def _silu(x):
    return x * _sigmoid(x)


def _transpose32(a):
    a = list(a)
    j, m = 16, 0x0000FFFF
    while j:
        k = 0
        while k < 32:
            t = (a[k] ^ lax.shift_right_logical(a[k + j], jnp.int32(j))) & jnp.int32(m - (1 << 32) if m >> 31 else m)
            a[k] = a[k] ^ t
            a[k + j] = a[k + j] ^ (t << j)
            k = (k + j + 1) & ~j
        j >>= 1
        m = (m ^ (m << j)) & 0xFFFFFFFF
    return a


V_ROWS = DH + 16
RADIX_TILES = 8
_NT = (((1,), (1,)), ((), ()))


def _in_proj_kernel(x_ref, g_ref, wm_ref, wt_ref, wqh_ref, wql_ref, wsh_ref, wsl_ref, wwh_ref, wwl_ref,
                    ph_ref, plo_ref, sc_ref,
                    ka_ref, va_ref, kbf_ref, conv_ref, gb_ref, ga_ref, gbr_ref, small_ref, ki3_ref,
                    qT_ref, qihT_ref, qilT_ref, wiT_ref, vTa_ref):
    tm = x_ref.shape[0]
    h = _rms(x_ref[...], g_ref[...])
    hh, hl = _split(h)

    def main(lo, hi):
        return _dot(hh, wm_ref[:, lo:hi])

    def nt(w, a):
        return lax.dot_general(w, a, _NT, preferred_element_type=F32)

    def nt3(wh_ref, wl_ref):
        return nt(wh_ref[...], hh) + (nt(wh_ref[...], hl) + nt(wl_ref[...], hh))

    ka = main(0, 512)
    va = main(512, 1024)
    for hd in range(HA):
        ka_ref[:, hd, :] = ka[:, hd * DH:(hd + 1) * DH]
        va_ref[:, hd, :] = va[:, hd * DH:(hd + 1) * DH]
    kbf_ref[...] = ka.astype(BF16)
    conv_ref[...] = main(1024, 2560)
    gb_ref[...] = main(2560, 3072)
    ga_ref[...] = main(3072, 4096)
    gbr_ref[...] = main(4096, 5120)

    qT_ref[0] = nt(wt_ref[0:WA, :], hh).astype(BF16)
    vT = nt(wt_ref[WA:2 * WA, :], hh).astype(BF16)
    ones = jnp.ones((V_ROWS - DH, tm), BF16)
    for hd in range(HA):
        vTa_ref[0, hd * V_ROWS:hd * V_ROWS + DH, :] = vT[hd * DH:(hd + 1) * DH]
        vTa_ref[0, hd * V_ROWS + DH:(hd + 1) * V_ROWS, :] = ones
    qh, ql = _split(nt3(wqh_ref, wql_ref))
    qihT_ref[0] = qh
    qilT_ref[0] = ql
    wiT_ref[0] = nt3(wwh_ref, wwl_ref)[0:H_IDX] * (H_IDX ** -0.5)

    small = (_dot(hh, wsh_ref[...]) + (_dot(hl, wsh_ref[...]) + _dot(hh, wsl_ref[...]))) * sc_ref[...]
    small_ref[...] = small
    sh, sl = _split(small)
    ki3_ref[...] = (_dot(sh, ph_ref[...]) + _dot(sl, plo_ref[...])).astype(BF16)


def _in_proj(x2d, norm_mix, w_in, tm):
    n = x2d.shape[0]
    assert n % tm == 0
    g = n // tm
    w = w_in
    wm = jnp.concatenate([w[:, _O_KA:_O_QI], w[:, _O_QB:_O_BB], w[:, _O_GA:]], axis=1).astype(BF16)
    wt = jnp.concatenate([w[:, _O_QA:_O_KA] * (DH ** -0.5 * LOG2E), w[:, _O_VA:_O_QI]], axis=1).T.astype(BF16)
    hilo = lambda a: (a.astype(BF16), (a - a.astype(BF16).astype(F32)).astype(BF16))
    wqh, wql = hilo((w[:, _O_QI:_O_KI] * (D_IDX ** -0.5)).T)
    wsh, wsl = hilo(jnp.concatenate([w[:, _O_KI:_O_QB], w[:, _O_BB:_O_GA], jnp.zeros((D_MODEL, 48), F32)], axis=1))
    wwh, wwl = hilo(jnp.concatenate([w[:, _O_WI:_O_QB], jnp.zeros((D_MODEL, 8), F32)], axis=1).T)
    ph = np.zeros((128, 256), np.float32)
    plo = np.zeros((128, 256), np.float32)
    for c in range(64):
        ph[c, c] = 1.0
        ph[c, 64 + c] = 1.0
        plo[c, 128 + c] = 1.0
    sc = np.ones((1, 128), np.float32)
    sc[0, _L_WI:_L_WI + H_IDX] = H_IDX ** -0.5
    row = lambda c: pl.BlockSpec((tm, c), lambda i: (i, 0))
    colT = lambda r: pl.BlockSpec((1, r, tm), lambda i: (i, 0, 0))
    heads = pl.BlockSpec((tm, HA, DH), lambda i: (i, 0, 0))
    out_cols = [(512, BF16), (1536, F32), (512, F32), (1024, F32), (1024, F32), (128, F32), (256, BF16)]
    out_rows = [(WA, BF16), (WA, BF16), (WA, BF16), (H_IDX, F32), (HA * V_ROWS, BF16)]
    return pl.pallas_call(
        _in_proj_kernel,
        grid=(g,),
        in_specs=[row(D_MODEL), _const_spec((1, D_MODEL)), _const_spec((D_MODEL, 5120)),
                  _const_spec((2 * WA, D_MODEL)), _const_spec((WA, D_MODEL)), _const_spec((WA, D_MODEL)),
                  _const_spec((D_MODEL, 128)), _const_spec((D_MODEL, 128)),
                  _const_spec((16, D_MODEL)), _const_spec((16, D_MODEL)),
                  _const_spec((128, 256)), _const_spec((128, 256)), _const_spec((1, 128))],
        out_specs=[heads, heads] + [row(c) for c, _ in out_cols] + [colT(r) for r, _ in out_rows],
        out_shape=[jax.ShapeDtypeStruct((n, HA, DH), F32)] * 2
        + [jax.ShapeDtypeStruct((n, c), d) for c, d in out_cols]
        + [jax.ShapeDtypeStruct((g, r, tm), d) for r, d in out_rows],
        compiler_params=_cparams(("arbitrary",)),
        name="in_proj",
    )(x2d, norm_mix.reshape(1, D_MODEL), wm, wt, wqh, wql, wsh, wsl, wwh, wwl, jnp.asarray(ph, BF16),
      jnp.asarray(plo, BF16), jnp.asarray(sc))


def _store_keys(score, j, key_ref, pln_ref, eq_ref):
    tk, tq = score.shape
    bits = pltpu.bitcast(score, I32)
    bits = jnp.where(bits == INT_MIN, 0, bits)
    key = bits ^ ((bits >> 31) & 0x7FFFFFFF)
    key_ref[j] = key
    u3 = (key ^ INT_MIN).reshape(tk // 8, 8, tq)
    ones = jnp.full((8, tq), -1, I32)
    pln_ref[0, j] = ones
    for l0 in range(0, tq, 128):
        planes = _transpose32([u3[r][:, l0:l0 + 128] for r in range(32)])
        for b in range(32):
            pln_ref[b + 1, j, :, l0:l0 + 128] = planes[b]
    eq_ref[0, j] = ones


def _radix_threshold(pln_ref, eq_ref, nk, tq, topk):
    for d in range(RADIX_TILES - 1):
        pln_ref[:, nk + d] = jnp.zeros((33, 8, tq), I32)
        eq_ref[0, nk + d] = jnp.zeros((8, tq), I32)

    sub = lax.broadcasted_iota(I32, (8, tq), 0)
    group_masks = (-0x10000, -0xFF0100, -0xF0F0F10, -0x33333334, -0x55555556)

    def pos_plane(e, j):
        if e < 6:
            return jnp.broadcast_to(-((~j >> (5 - e)) & 1), (8, tq))
        if e < 11:
            return jnp.full((8, tq), group_masks[e - 6], I32)
        return -((~sub >> (13 - e)) & 1)

    def sweep(prev, cur, carry, src, dst):
        n_gt, flip = carry

        def body(jq, cnt):
            for d in range(RADIX_TILES):
                j = jq * RADIX_TILES + d
                e = eq_ref[src, j] & (prev(j) ^ flip)
                eq_ref[dst, j] = e
                cnt = cnt + lax.population_count(e & cur(j))
            return cnt

        cnt = lax.fori_loop(0, (nk + RADIX_TILES - 1) // RADIX_TILES, body, jnp.zeros((8, tq), I32))
        cnt = n_gt + cnt.sum(axis=0, keepdims=True)
        acc = cnt >= topk
        return acc, (jnp.where(acc, n_gt, cnt), jnp.where(acc, 0, -1))

    def key_sweep(it, carry, src, dst):
        tu, rest = carry
        acc, rest = sweep(lambda j: pln_ref[it, j], lambda j: pln_ref[it + 1, j], rest, src, dst)
        return jnp.where(acc, tu | jnp.left_shift(jnp.int32(1), 31 - it), tu), rest

    def two_bits(i, carry):
        return key_sweep(2 * i + 1, key_sweep(2 * i, carry, 0, 1), 1, 0)

    zero_row = jnp.zeros((1, tq), I32)
    tu, rest = lax.fori_loop(0, 16, two_bits, (zero_row, (zero_row, zero_row)))
    inv_pos = zero_row
    for e in range(14):
        prev = (lambda j: pln_ref[32, j]) if e == 0 else functools.partial(pos_plane, e - 1)
        acc, rest = sweep(prev, functools.partial(pos_plane, e), rest, e % 2, 1 - e % 2)
        inv_pos = jnp.where(acc, inv_pos | (1 << (13 - e)), inv_pos)
    return tu ^ INT_MIN, ~inv_pos & 0x3FFF


def _dsa_kernel(qT_ref, qihT_ref, qilT_ref, wiT_ref, k_ref, vT_ref, ki3_ref, bias_ref, cap_ref,
                o_ref, key_ref, l_ref, acc_ref, lg_ref, pln_ref, eq_ref, sc_ref, ma_ref,
                *, tq, tk, nk_static, causal, topk):
    nk = (pl.program_id(1) + 1) if causal else nk_static
    wiT = wiT_ref[0]

    def when(cond):
        if isinstance(cond, bool):
            return (lambda f: f()) if cond else (lambda f: None)
        return pl.when(cond)

    zeros64 = jnp.zeros((64, tq), BF16)
    q3 = []
    for h in range(H_IDX):
        hi = qihT_ref[0, h * 64:(h + 1) * 64, :]
        lo = qilT_ref[0, h * 64:(h + 1) * 64, :]
        q3.append(jnp.concatenate([hi, lo, hi, zeros64], axis=0))

    def score_matmuls(j0, n):
        kts = [ki3_ref[0, j0 + d] for d in range(n)]
        accs = [None] * n
        for h in range(H_IDX):
            for d in range(n):
                t = jnp.maximum(_dot(kts[d], q3[h]), 0.0) * wiT[h:h + 1, :]
                accs[d] = t if accs[d] is None else accs[d] + t
        for d in range(n):
            sc_ref[d] = jnp.minimum(accs[d], cap_ref[jnp.where(j0 + d == nk - 1, 1, 0)])

    def score_finish(j0, n):
        for d in range(n):
            _store_keys(sc_ref[d], j0 + d, key_ref, pln_ref, eq_ref)

    def score_body(i, c):
        score_finish(2 * i - 2, 2)
        score_matmuls(2 * i, 2)
        return c

    @when(nk >= 2)
    def _():
        score_matmuls(0, 2)
        lax.fori_loop(1, nk // 2, score_body, 0)
        score_finish(2 * (nk // 2) - 2, 2)

    @when(nk % 2 == 1)
    def _():
        score_matmuls(nk - 1, 1)
        score_finish(nk - 1, 1)

    thr, pos_thr = _radix_threshold(pln_ref, eq_ref, nk, tq, topk)

    l_ref[...] = jnp.zeros(l_ref.shape, F32)
    acc_ref[...] = jnp.zeros(acc_ref.shape, F32)
    qm = []
    for h in range(HA):
        qh = qT_ref[0, h * 64:(h + 1) * 64, :]
        qm.append(jnp.concatenate([qh, zeros64] if h % 2 == 0 else [zeros64, qh], axis=0))
    row = lax.broadcasted_iota(I32, (tk, tq), 0)

    def tile_logits(j, slot, far):
        sel = key_ref[j] > jnp.where(row <= pos_thr - j * tk, thr - 1, thr)
        kind = jnp.clip(j - (nk - 3), 0, 2)
        mx = []
        for h in range(HA):
            pr = h // 2
            lg = _dot(k_ref[0, j, :, pr * 128:(pr + 1) * 128], qm[h])
            if not far:
                lg = lg + bias_ref[kind, h]
            lg = jnp.where(sel, lg, NEG).astype(BF16)
            lg_ref[slot, h] = lg
            part = jnp.max(lg.reshape(tk // 16, 16, tq), axis=0).astype(F32)
            mx.append(jnp.max(part, axis=0, keepdims=True))
        return jnp.concatenate(mx, axis=0)

    def tile_values(j, slot, mx):
        m_old = ma_ref[...]
        m_new = jnp.maximum(m_old, mx)
        alpha = jnp.exp2(m_old - m_new)
        ma_ref[...] = m_new
        ls = []
        for h in range(HA):
            p = jnp.exp2(lg_ref[slot, h] - m_new[h:h + 1, :].astype(BF16))
            pv = _dot(vT_ref[0, j, h * V_ROWS:(h + 1) * V_ROWS, :], p)
            acc_ref[h * 64:(h + 1) * 64, :] = alpha[h:h + 1, :] * acc_ref[h * 64:(h + 1) * 64, :] + pv[0:DH]
            ls.append(pv[DH:DH + 1])
        l_ref[...] = alpha * l_ref[...] + jnp.concatenate(ls, axis=0)

    ma_ref[...] = jnp.full((HA, tq), NEG, F32)

    def pair(j0, far):
        mx0 = tile_logits(j0, 0, far)
        mx1 = tile_logits(j0 + 1, 1, far)
        tile_values(j0, 0, mx0)
        tile_values(j0 + 1, 1, mx1)

    def far_pair(i, c):
        pair(2 * i, True)
        return c

    n_far = jnp.maximum(nk - 2, 0) // 2
    lax.fori_loop(0, n_far, far_pair, 0)

    @when(nk >= 2)
    def _():
        pair(2 * n_far, False)

    @when(nk % 2 == 1)
    def _():
        tile_values(nk - 1, 0, tile_logits(nk - 1, 0, False))

    for h in range(HA):
        o_ref[0, h * 64:(h + 1) * 64, :] = (acc_ref[h * 64:(h + 1) * 64, :] / l_ref[h:h + 1, :]).astype(BF16)


def _t5_bucket_np(rel):
    half = NUM_BUCKETS // 2
    max_exact = half // 2
    out = np.zeros(rel.shape, np.int64)
    flat_rel = rel.reshape(-1)
    flat = out.reshape(-1)
    for a in range(flat_rel.size):
        r = int(flat_rel[a])
        n = abs(r)
        b = n if n < max_exact else min(half - 1, (n * n).bit_length() + 1)
        flat[a] = b + (half if r > 0 else 0)
    return out


def _bias_tables(rel_bias, adm, tk, tq, tq_valid):
    kj = np.arange(tk)[:, None]
    t = np.minimum(np.arange(tq), tq_valid - 1)[None, :]
    rel_to_bucket = _t5_bucket_np(np.arange(-2 * tk - tq, tk + 1))
    lut = lambda rel: rel_to_bucket[rel + 2 * tk + tq]
    far_bucket = NUM_BUCKETS // 2 - 1
    idx = np.stack([np.full((tk, tq), far_bucket), lut(kj - tk - t), lut(kj - t)], axis=0)
    tab = rel_bias.astype(F32) * LOG2E
    onehot = jnp.asarray(idx[..., None] == np.arange(NUM_BUCKETS), F32)
    bias = jnp.einsum("ktqb,bh->khtq", onehot, tab - tab[far_bucket], precision=lax.Precision.HIGHEST)
    mask = np.zeros((3, 1, tk, tq), np.float32)
    mask[2, 0] = np.where(adm > 0.5, 0.0, NEG)
    return bias + jnp.asarray(mask)


def _dsa(qT, qihT, qilT, wiT, k4, vT4, ki34, bias, adm, *, nq, causal, topk):
    g, _, tq = qT.shape
    b, nk, tk, _ = k4.shape
    assert g == b * nq and tk == 256 and nk + RADIX_TILES - 1 <= 64
    cap = jnp.asarray(np.stack([np.full((tk, tq), np.inf, np.float32), np.where(adm > 0.5, np.inf, NEG)]), F32)
    qspec = lambda r: pl.BlockSpec((1, r, tq), lambda bi, i: (bi * nq + i, 0, 0))
    kspec = lambda s: pl.BlockSpec((1,) + s, lambda bi, i: (bi, 0, 0, 0), pipeline_mode=pl.Buffered(1))
    kern = functools.partial(_dsa_kernel, tq=tq, tk=tk, nk_static=nk, causal=causal, topk=topk)
    return pl.pallas_call(
        kern,
        grid=(b, nq),
        in_specs=[qspec(512), qspec(512), qspec(512), qspec(8),
                  kspec((nk, tk, 512)), kspec((nk, HA * V_ROWS, tk)), kspec((nk, tk, 256)),
                  _const_spec((3, HA, tk, tq)), _const_spec((2, tk, tq))],
        out_specs=qspec(512),
        out_shape=jax.ShapeDtypeStruct((g, 512, tq), BF16),
        scratch_shapes=[pltpu.VMEM((nk, tk, tq), I32), pltpu.VMEM((8, tq), F32),
                        pltpu.VMEM((512, tq), F32), pltpu.VMEM((2, HA, tk, tq), BF16),
                        pltpu.VMEM((33, nk + RADIX_TILES - 1, 8, tq), I32),
                        pltpu.VMEM((2, nk + RADIX_TILES - 1, 8, tq), I32),
                        pltpu.VMEM((2, tk, tq), F32), pltpu.VMEM((HA, tq), F32)],
        compiler_params=_cparams(("arbitrary", "arbitrary")),
        name="dsa_causal" if causal else "dsa_cached",
    )(qT, qihT, qilT, wiT, k4, vT4, ki34, bias, cap)


def _dsa_step_kernel(qa_ref, qih_ref, qil_ref, w_ref, ck_ref, cv_ref, cki_ref, nk_ref, nv_ref, nki_ref,
                     bias_ref, cap_ref, blk_ref, o_ref, key_ref, pln_ref, eq_ref, m_ref, l_ref, acc_ref,
                     *, nkc, t, topk):
    tk, tq = nk_ref.shape[1], qa_ref.shape[2]
    qih, qil = qih_ref[0], qil_ref[0]
    w = w_ref[0, 0:1, :]
    shifts = [t << s for s in range((tq // t).bit_length() - 2, -1, -1)]

    def scores(kx, j, cap):
        kh, kl = _split(kx)
        s = _dot(kh, qih) + (_dot(kl, qih) + _dot(kh, qil))
        yield
        sc = jnp.maximum(s, 0.0) * w
        for sh in shifts:
            sc = sc + pltpu.roll(sc, sh, 1)
            yield
        if cap is not None:
            sc = jnp.minimum(sc, cap)
        _store_keys(sc, j, key_ref, pln_ref, eq_ref)

    def together(*gens):
        gens = list(gens)
        while gens:
            gens = [g for g in gens if next(g, gens) is not gens]

    def score_body(i, c):
        cached_idx = lambda j: cki_ref[0, pl.ds(pl.multiple_of(j * tk, tk), tk), :]
        together(scores(cached_idx(2 * i), 2 * i, None), scores(cached_idx(2 * i + 1), 2 * i + 1, None))
        return c

    lax.fori_loop(0, nkc // 2, score_body, 0)
    together(scores(nki_ref[0], nkc, cap_ref[...]))
    thr, pos_thr = _radix_threshold(pln_ref, eq_ref, nkc + 1, tq, topk)

    m_ref[...] = jnp.full(m_ref.shape, NEG, F32)
    l_ref[...] = jnp.zeros(l_ref.shape, F32)
    acc_ref[...] = jnp.zeros(acc_ref.shape, F32)
    qa = qa_ref[0]
    row = lax.broadcasted_iota(I32, (tk, tq), 0)
    ones8 = jnp.ones((8, tk), BF16)
    tn = (((0,), (0,)), ((), ()))

    def logits(k_bf, j, bias):
        sel = key_ref[j] > jnp.where(row <= pos_thr - j * tk, thr - 1, thr)
        lg = _dot(k_bf, qa)
        if bias is not None:
            lg = lg + bias
        return jnp.where(sel, lg, NEG).astype(BF16)

    def consume(lg, v_bf):
        part = jnp.max(lg.reshape(tk // 16, 16, tq), axis=0).astype(F32)
        m_old = m_ref[0:1, :]
        m_new = jnp.maximum(m_old, jnp.max(part, axis=0, keepdims=True))
        alpha = jnp.exp2(m_old - m_new)
        p = jnp.exp2(lg - m_new.astype(BF16))
        l_ref[0:1, :] = alpha * l_ref[0:1, :] + _dot(ones8, p)[0:1]
        acc_ref[...] = alpha * acc_ref[...] + lax.dot_general(v_bf, p, tn, preferred_element_type=F32)
        m_ref[0:1, :] = m_new

    def cached_pair(j0, bias1):
        r0 = pl.ds(pl.multiple_of(j0 * tk, tk), tk)
        r1 = pl.ds(pl.multiple_of(j0 * tk + tk, tk), tk)
        lg0 = logits(ck_ref[0, r0, :].astype(BF16), j0, None)
        lg1 = logits(ck_ref[0, r1, :].astype(BF16), j0 + 1, bias1)
        consume(lg0, cv_ref[0, r0, :].astype(BF16))
        consume(lg1, cv_ref[0, r1, :].astype(BF16))

    def far(i, c):
        cached_pair(2 * i, None)
        return c

    lax.fori_loop(0, nkc // 2 - 1, far, 0)
    cached_pair(nkc - 2, bias_ref[0])
    consume(logits(nk_ref[0], nkc, bias_ref[1]), nv_ref[0])
    o = acc_ref[...] / l_ref[0:1, :] * blk_ref[...]
    for sh in shifts:
        o = o + pltpu.roll(o, sh, 1)
    o_ref[0] = o.astype(BF16)


def _dsa_step(qT, qihT, qilT, wiT, kbf, va, ki_new, past_k, past_v, past_kidx, rel_bias, *, b, t, topk):
    past = past_k.shape[1]
    tk, tq = 256, HA * t
    assert tq == 128 and past % (2 * tk) == 0 and t <= tk
    nkc = past // tk
    nk = nkc + 1
    per_b = lambda a: jnp.swapaxes(a[0].reshape(a.shape[1], b, t), 0, 1)
    blk = (np.arange(WA)[:, None] // DH == np.arange(tq)[None, :] // t).astype(np.float32)
    qa = (jnp.tile(per_b(qT), (1, 1, HA)) * jnp.asarray(blk, BF16))
    lanes = lambda a: jnp.swapaxes(per_b(a).reshape(b, HA, D_IDX, t), 1, 2).reshape(b, D_IDX, tq)
    w = jnp.broadcast_to(per_b(wiT).reshape(b, 1, tq), (b, 8, tq))
    pad_rows = lambda a: jnp.pad(a.reshape(b, t, a.shape[-1]), ((0, 0), (0, tk - t), (0, 0)))
    adm = np.broadcast_to(np.arange(tk)[:, None] < t, (tk, tq)).astype(np.float32)
    bias = _bias_tables(rel_bias, adm[:, :t], tk, t, t)[1:]
    bias = jnp.swapaxes(bias, 1, 2).reshape(2, tk, tq)
    cap = jnp.asarray(np.where(adm > 0.5, np.inf, NEG), F32)
    bspec = lambda s: pl.BlockSpec((1,) + s, lambda bi: (bi,) + (0,) * len(s))
    return pl.pallas_call(
        functools.partial(_dsa_step_kernel, nkc=nkc, t=t, topk=topk),
        grid=(b,),
        in_specs=[bspec((WA, tq)), bspec((D_IDX, tq)), bspec((D_IDX, tq)), bspec((8, tq)),
                  bspec((past, WA)), bspec((past, WA)), bspec((past, D_IDX)),
                  bspec((tk, WA)), bspec((tk, WA)), bspec((tk, D_IDX)),
                  _const_spec((2, tk, tq)), _const_spec((tk, tq)), _const_spec((WA, tq))],
        out_specs=bspec((WA, tq)),
        out_shape=jax.ShapeDtypeStruct((b, WA, tq), BF16),
        scratch_shapes=[pltpu.VMEM((nk, tk, tq), I32), pltpu.VMEM((33, nk + RADIX_TILES - 1, 8, tq), I32),
                        pltpu.VMEM((2, nk + RADIX_TILES - 1, 8, tq), I32), pltpu.VMEM((8, tq), F32),
                        pltpu.VMEM((8, tq), F32), pltpu.VMEM((WA, tq), F32)],
        compiler_params=_cparams(("arbitrary",)),
        name="dsa_step",
    )(qa, lanes(qihT), lanes(qilT), w, past_k.reshape(b, past, WA), past_v.reshape(b, past, WA), past_kidx,
      pad_rows(kbf), pad_rows(va.astype(BF16)), pad_rows(ki_new), bias, cap, jnp.asarray(blk))


def _gdn_pre_kernel(conv_ref, prev_ref, hist_ref, small_ref, cw_ref, alog_ref, dtb_ref,
                    wm_ref, um_ref, qe_ref, oi_ref, e_ref, ext_ref, *, rows, nvalid):
    c = CHUNK
    n = HB * c
    ext_ref[0:8, :] = jnp.where(pl.program_id(1) == 0, hist_ref[0], prev_ref[0])
    ext_ref[8:8 + rows, :] = conv_ref[0]
    cb = ext_ref[5:5 + rows, :] * cw_ref[0:1, :]
    for j in range(1, CONV_B):
        cb = cb + ext_ref[5 + j:5 + j + rows, :] * cw_ref[j:j + 1, :]
    cb = _silu(cb)

    small = small_ref[0]
    pos = lax.broadcasted_iota(I32, (rows, 1), 0) % c
    rowv = (pos < nvalid).astype(F32)
    beta_all = _sigmoid(small) * rowv
    sp = small + dtb_ref[...]
    g_all = -jnp.exp(alog_ref[...]) * (jnp.maximum(sp, 0.0) + jnp.log(1.0 + jnp.exp(-jnp.abs(sp)))) * rowv

    ri = lax.broadcasted_iota(I32, (n, n), 0)
    ci = lax.broadcasted_iota(I32, (n, n), 1)
    same = (ri // c) == (ci // c)
    tri_b = (same & (ci <= ri)).astype(BF16)
    bd_f = same.astype(F32)
    bd_b = same.astype(BF16)
    wr = lax.broadcasted_iota(I32, (c, n), 0)
    wl = lax.broadcasted_iota(I32, (c, n), 1)
    grp = wl // c
    tri_w = (wl % c) <= wr
    strict_w = (wl % c) < wr
    eye_w = ((wl % c) == wr).astype(F32)
    nt = (((1,), (1,)), ((), ()))
    tn = (((0,), (0,)), ((), ()))

    def l2n(x):
        return x * lax.rsqrt(jnp.sum(x * x, axis=-1, keepdims=True) + EPS)

    def to_wide(full):
        out = jnp.where(grp == 0, full[0:c, :], 0.0)
        for h in range(1, HB):
            out = out + jnp.where(grp == h, full[h * c:(h + 1) * c, :], 0.0)
        return out

    def tile4(x):
        return jnp.concatenate([x] * HB, axis=0)

    def chunk(k0):
        def stack(fn):
            return jnp.concatenate([fn(h) for h in range(HB)], axis=0)

        rv = rowv[k0:k0 + c]
        q = stack(lambda h: l2n(cb[k0:k0 + c, h * DK:(h + 1) * DK]) * (DK ** -0.5))
        k = stack(lambda h: l2n(cb[k0:k0 + c, WB + h * DK:WB + (h + 1) * DK]) * rv)
        v = stack(lambda h: cb[k0:k0 + c, 2 * WB + h * DV:2 * WB + (h + 1) * DV] * rv)
        beta = stack(lambda h: beta_all[k0:k0 + c, _L_BB + h:_L_BB + h + 1])
        g = stack(lambda h: g_all[k0:k0 + c, _L_AB + h:_L_AB + h + 1])
        yield

        gh, gl = _split(jnp.broadcast_to(g, (n, 128)))
        gcum = _dot(tri_b, gh) + _dot(tri_b, gl)
        yield
        gcum_row = gcum.T[0:1, :]
        gcum_col = gcum[:, 0:1]
        col_w = to_wide(jnp.concatenate([gcum, gcum], axis=1))
        decay_w = jnp.where(tri_w, jnp.exp(jnp.where(tri_w, col_w - gcum_row, 0.0)), 0.0)
        kb = k * beta
        kbf = k.astype(BF16)
        kk_w = to_wide(lax.dot_general(kb.astype(BF16), kbf, nt, preferred_element_type=F32))
        qk_w = to_wide(lax.dot_general(q.astype(BF16), kbf, nt, preferred_element_type=F32))
        nmat_w = jnp.where(strict_w, kk_w * decay_w, 0.0)
        attn_w = jnp.where(tri_w, qk_w * decay_w, 0.0)
        yield

        inv_w = eye_w - nmat_w
        ph = nmat_w.astype(BF16)
        bh = tile4(ph) * bd_b
        for _ in range(int(np.log2(c)) - 1):
            ph = _dot(ph, bh).astype(BF16)
            yield
            bh = tile4(ph) * bd_b
            inv_w = inv_w + _dot(inv_w.astype(BF16), bh)
            yield
        rhs = jnp.concatenate([v * beta, kb * jnp.exp(gcum_col)], axis=1)
        sol = _dot3(tile4(inv_w) * bd_f, rhs)
        yield
        solb = sol.astype(BF16)
        aw = _dot((tile4(attn_w) * bd_f).astype(BF16), solb)
        yield
        ck = k0 // c
        oi_ref[0, ck] = aw[:, :DV]
        qe_ref[0, ck] = (q * jnp.exp(gcum_col) - aw[:, DV:]).astype(BF16)
        for h in range(HB):
            rs = slice(h * c, (h + 1) * c)
            g_last = gcum[(h + 1) * c - 1:(h + 1) * c, :]
            kd = (k[rs] * jnp.exp(g_last - gcum[rs])).astype(BF16)
            uw = lax.dot_general(kd, solb[rs], tn, preferred_element_type=F32)
            um_ref[0, ck, h] = uw[:, :DV]
            wm_ref[0, ck, h] = uw[:, DV:].astype(BF16)
            e_ref[0, ck, h:h + 1, :] = jnp.exp(g_last)
        e_ref[0, ck, HB:8, :] = jnp.zeros((8 - HB, 128), F32)

    gens = [chunk(k0) for k0 in range(0, rows, c)]
    while gens:
        alive = []
        for gen in gens:
            try:
                next(gen)
                alive.append(gen)
            except StopIteration:
                pass
        gens = alive


def _gdn_scan_kernel(wm_ref, um_ref, qe_ref, oi_ref, e_ref, gb_ref, s0_ref, ng_ref, ob_ref, sfin_ref, s_ref,
                     *, bb, g):
    c = CHUNK

    @pl.when(pl.program_id(1) == 0)
    def _():
        s_ref[...] = s0_ref[...]

    ng = ng_ref[...]
    for b in range(bb):
        for ck in range(g):
            for h in range(HB):
                s = s_ref[b, h]
                sb = s.astype(BF16)
                rs = slice(h * c, (h + 1) * c)
                o = _dot(qe_ref[b, ck, rs, :], sb) + oi_ref[b, ck, rs, :]
                s_ref[b, h] = e_ref[b, ck, h:h + 1, :] * s + (um_ref[b, ck, h] - _dot(wm_ref[b, ck, h], sb))
                gate = gb_ref[b, ck * c:(ck + 1) * c, h * DV:(h + 1) * DV]
                ob_ref[b, ck * c:(ck + 1) * c, h * DV:(h + 1) * DV] = (_rms(o, ng) * _silu(gate)).astype(BF16)
    sfin_ref[...] = s_ref[...]


def _gdn(conv_in, hist, gb, small, s0, conv_w, a_log, dt_bias, norm_gdn, nvalid):
    b, t, _ = conv_in.shape
    assert t % CHUNK == 0
    rows = 512 if t % 512 == 0 else CHUNK
    nc, cps = t // CHUNK, rows // CHUNK
    n = HB * CHUNK
    alog = jnp.zeros((1, 128), F32).at[0, _L_AB:_L_AB + HB].set(a_log)
    dtb = jnp.zeros((1, 128), F32).at[0, _L_AB:_L_AB + HB].set(dt_bias)
    row = lambda c_: pl.BlockSpec((1, rows, c_), lambda bi, ti: (bi, ti, 0))
    prev = pl.BlockSpec((1, 8, C_CONV_B), lambda bi, ti: (bi, jnp.maximum(ti * (rows // 8) - 1, 0), 0))
    per_b = lambda s: pl.BlockSpec((1,) + s, lambda bi, ti: (bi,) + (0,) * len(s))
    chunked = lambda s: pl.BlockSpec((1, cps) + s, lambda bi, ti: (bi, ti) + (0,) * len(s))
    wm, um, qe, oi, e = pl.pallas_call(
        functools.partial(_gdn_pre_kernel, rows=rows, nvalid=nvalid),
        grid=(b, t // rows),
        in_specs=[row(C_CONV_B), prev, per_b((8, C_CONV_B)), row(128),
                  _const_spec((CONV_B, C_CONV_B)), _const_spec((1, 128)), _const_spec((1, 128))],
        out_specs=[chunked((HB, DK, DV)), chunked((HB, DK, DV)), chunked((n, DK)), chunked((n, DV)),
                   chunked((8, 128))],
        out_shape=[jax.ShapeDtypeStruct((b, nc, HB, DK, DV), BF16), jax.ShapeDtypeStruct((b, nc, HB, DK, DV), F32),
                   jax.ShapeDtypeStruct((b, nc, n, DK), BF16), jax.ShapeDtypeStruct((b, nc, n, DV), F32),
                   jax.ShapeDtypeStruct((b, nc, 8, 128), F32)],
        scratch_shapes=[pltpu.VMEM((8 + rows, C_CONV_B), F32)],
        compiler_params=_cparams(("arbitrary", "arbitrary")),
        name="gdn_pre",
    )(conv_in, conv_in, hist, small, conv_w, alog, dtb)

    bb = 2 if b % 2 == 0 else 1
    g = 4 if nc % 4 == 0 else 1
    blk = lambda s: pl.BlockSpec((bb, g) + s, lambda bi, ci: (bi, ci) + (0,) * len(s))
    rowb = pl.BlockSpec((bb, g * CHUNK, WB), lambda bi, ci: (bi, ci, 0))
    state = pl.BlockSpec((bb, HB, DK, DV), lambda bi, ci: (bi, 0, 0, 0))
    return pl.pallas_call(
        functools.partial(_gdn_scan_kernel, bb=bb, g=g),
        grid=(b // bb, nc // g),
        in_specs=[blk((HB, DK, DV)), blk((HB, DK, DV)), blk((n, DK)), blk((n, DV)), blk((8, 128)), rowb, state,
                  _const_spec((1, DV))],
        out_specs=[rowb, state],
        out_shape=[jax.ShapeDtypeStruct((b, t, WB), BF16), jax.ShapeDtypeStruct((b, HB, DK, DV), F32)],
        scratch_shapes=[pltpu.VMEM((bb, HB, DK, DV), F32)],
        compiler_params=_cparams(("arbitrary", "arbitrary")),
        name="gdn_scan",
    )(wm, um, qe, oi, e, gb, s0, norm_gdn.reshape(1, DV))


def _post_kernel(x_ref, oaT_ref, ob_ref, ga_ref, gbr_ref, p_ref, hist_ref, wa_ref, wb_ref, wo_ref, nf_ref,
                 wup_ref, cw_ref, wdn_ref, npl_ref, wpg_ref, wple_ref, nfin_ref, y_ref, tail_ref, ext_ref,
                 *, tm, seq):
    if seq is None:
        @pl.when(pl.program_id(1) == 0)
        def _():
            ext_ref[0:8, :] = hist_ref[0]
    else:
        ext_ref[0:8, :] = jnp.zeros((8, D_FF), F32)

    ya = lax.dot_general(oaT_ref[0], wa_ref[...], (((0,), (0,)), ((), ())), preferred_element_type=F32)
    yb = _dot(ob_ref[0], wb_ref[...])
    mix = _sigmoid(ga_ref[0]) * ya + _sigmoid(gbr_ref[0]) * yb
    x1 = x_ref[0] + _dot(mix.astype(BF16), wo_ref[...])
    h2 = _rms(x1, nf_ref[...]).astype(BF16)
    ext_ref[8:8 + tm, :] = _dot(h2, wup_ref[:, 0:D_FF])
    u_val = _dot(h2, wup_ref[:, D_FF:2 * D_FF])
    if seq is None:
        cv = ext_ref[6:6 + tm, :] * cw_ref[0:1, :]
        for j in range(1, CONV_F):
            cv = cv + ext_ref[6 + j:6 + j + tm, :] * cw_ref[j:j + 1, :]
        tail = ext_ref[tm:tm + 8, :]
        ext_ref[0:8, :] = tail
        tail_ref[0] = tail
    else:
        pos = lax.broadcasted_iota(I32, (tm, 1), 0) % seq
        cv = jnp.where(pos >= 2, ext_ref[6:6 + tm, :], hist_ref[1]) * cw_ref[0:1, :]
        cv = cv + jnp.where(pos >= 1, ext_ref[7:7 + tm, :], hist_ref[0]) * cw_ref[1:2, :]
        cv = cv + ext_ref[8:8 + tm, :] * cw_ref[2:3, :]
        tail_ref[...] = ext_ref[8:8 + tm, :]
    act = 0.5 * cv * (1.0 + jnp.tanh(0.7978845608028654 * (cv + 0.044715 * (cv * cv * cv))))
    x2 = x1 + _dot((act * u_val).astype(BF16), wdn_ref[...])
    gate = _sigmoid(_dot(_rms(x2, npl_ref[...]).astype(BF16), wpg_ref[...]))
    x3 = x2 + gate * _dot(p_ref[0].astype(BF16), wple_ref[...])
    y_ref[0] = _rms(x3, nfin_ref[...])


def _post(x, oaT, ob, ga, gbr, p, hist, w_proj_a, w_proj_b, w_out, norm_ffn, w_up, conv_ffn, w_down, norm_ple,
          w_ple_gate, w_ple, norm_final, tm, seq=None):
    b, t, _ = x.shape
    nt = t // tm
    assert t % tm == 0 and tm >= 8 and oaT.shape == (b * nt, WA, tm)
    row = lambda c: pl.BlockSpec((1, tm, c), lambda bi, ti: (bi, ti, 0))
    vec = _const_spec((1, D_MODEL))
    if seq is None:
        hist_spec = tail_spec = pl.BlockSpec((1, 8, D_FF), lambda bi, ti: (bi, 0, 0))
        tail_shape = (b, 8, D_FF)
    else:
        assert b == 1 and nt == 1 and tm % seq == 0
        hist_spec = _const_spec((2, tm, D_FF))
        tail_spec = pl.BlockSpec((tm, D_FF), lambda bi, ti: (0, 0))
        tail_shape = (tm, D_FF)
    return pl.pallas_call(
        functools.partial(_post_kernel, tm=tm, seq=seq),
        grid=(b, nt),
        in_specs=[row(D_MODEL), pl.BlockSpec((1, WA, tm), lambda bi, ti: (bi * nt + ti, 0, 0)), row(WB),
                  row(D_MODEL), row(D_MODEL), row(D_PLE), hist_spec,
                  _const_spec((WA, D_MODEL)), _const_spec((WB, D_MODEL)), _const_spec((D_MODEL, D_MODEL)), vec,
                  _const_spec((D_MODEL, 2 * D_FF)), _const_spec((CONV_F, D_FF)), _const_spec((D_FF, D_MODEL)),
                  vec, _const_spec((D_MODEL, D_MODEL)), _const_spec((D_PLE, D_MODEL)), vec],
        out_specs=[row(D_MODEL), tail_spec],
        out_shape=[jax.ShapeDtypeStruct((b, t, D_MODEL), F32), jax.ShapeDtypeStruct(tail_shape, F32)],
        scratch_shapes=[pltpu.VMEM((8 + tm, D_FF), F32)],
        compiler_params=_cparams(("arbitrary", "arbitrary")),
        name="post",
    )(x, oaT, ob, ga, gbr, p, hist, w_proj_a.astype(BF16), w_proj_b.astype(BF16), w_out.astype(BF16),
      norm_ffn.reshape(1, D_MODEL), w_up.astype(BF16), conv_ffn, w_down.astype(BF16),
      norm_ple.reshape(1, D_MODEL), w_ple_gate.astype(BF16), w_ple.astype(BF16), norm_final.reshape(1, D_MODEL))


def _pad_hist(hist, rows=8):
    b, r, c = hist.shape
    return jnp.concatenate([jnp.zeros((b, rows - r, c), hist.dtype), hist], axis=1)


def _layer(x, p, past_k, past_v, past_kidx, s_gdn, conv_b_hist, ffn_hist, wts, *, tm, tq):
    (norm_mix, w_in, conv_b, a_log, dt_bias, norm_gdn, w_proj_a, w_proj_b, w_out, norm_ffn, w_up, conv_ffn,
     w_down, norm_ple, w_ple, w_ple_gate, rel_bias, norm_final) = wts
    b, t, _ = x.shape
    n = b * t
    past = past_k.shape[1]
    topk = min(TOPK_MAX, (past + t) // 4)
    x2d = x.reshape(n, D_MODEL)
    tmi = min(tm, n)
    (ka, va, kbf, conv_in, gb, ga, gbr, small, ki3, qT, qihT, qilT, wiT, vTa) = _in_proj(x2d, norm_mix, w_in, tmi)

    if past == 0:
        assert tq == tmi and t % tq == 0 and tq % CHUNK == 0 and tq >= topk
        nq = t // tq
        kj = np.arange(tq)[:, None]
        adm = ((kj // CHUNK) <= (np.arange(tq)[None, :] // CHUNK)).astype(np.float32)
        oT = _dsa(qT, qihT, qilT, wiT, kbf.reshape(b, nq, tq, WA), vTa.reshape(b, nq, HA * V_ROWS, tq),
                  ki3.reshape(b, nq, tq, 256), _bias_tables(rel_bias, adm, tq, tq, tq), adm,
                  nq=nq, causal=True, topk=topk)
    else:
        assert n == tmi
        oT = _dsa_step(qT, qihT, qilT, wiT, kbf, va.reshape(n, WA), small[:, :D_IDX], past_k, past_v, past_kidx, rel_bias,
                       b=b, t=t, topk=topk)
        oT = oT[:, :, :t]

    tp = -(-t // CHUNK) * CHUNK
    padt = lambda a: jnp.pad(a.reshape(b, t, a.shape[-1]), ((0, 0), (0, tp - t), (0, 0)))
    ob, s_new = _gdn(padt(conv_in), _pad_hist(conv_b_hist), padt(gb), padt(small), s_gdn, conv_b, a_log, dt_bias,
                     norm_gdn, nvalid=min(t, CHUNK))
    new_conv_b = jnp.concatenate([conv_b_hist, conv_in.reshape(b, t, C_CONV_B)], axis=1)[:, t:]

    post_w = (w_proj_a, w_proj_b, w_out, norm_ffn, w_up, conv_ffn, w_down, norm_ple, w_ple_gate, w_ple, norm_final)
    if past == 0:
        per_bt = lambda a: a.reshape(b, t, a.shape[-1])
        y, tail = _post(x, oT, ob, per_bt(ga), per_bt(gbr), p, _pad_hist(ffn_hist), *post_w, tm)
        new_ffn = tail[:, 8 - (CONV_F - 1):]
    else:
        assert n == tmi and t >= CONV_F - 1
        one = lambda a: a.reshape(1, n, a.shape[-1])
        first = jnp.zeros((b, t, D_FF), F32)
        hist = jnp.stack([first.at[:, 0].set(ffn_hist[:, 1]),
                          first.at[:, 0].set(ffn_hist[:, 0]).at[:, 1].set(ffn_hist[:, 1])]).reshape(2, n, D_FF)
        y, ug = _post(one(x), jnp.swapaxes(oT, 0, 1).reshape(1, WA, n), one(ob[:, :t]), one(ga), one(gbr), one(p),
                      hist, *post_w, n, seq=t)
        y = y.reshape(b, t, D_MODEL)
        new_ffn = ug.reshape(b, t, D_FF)[:, t - (CONV_F - 1):]
    return (y, ka.reshape(b, t, HA, DH), va.reshape(b, t, HA, DH), small[:, :D_IDX].reshape(b, t, D_IDX),
            s_new, new_conv_b, new_ffn)


def kernel(x_prompt, x_sample, p_prompt, p_sample, cache_k, cache_v, cache_kidx, state_gdn, state_gdn_conv,
           state_ffn_conv, norm_mix, w_in, conv_b, a_log, dt_bias, norm_gdn, w_proj_a, w_proj_b, w_out, norm_ffn,
           w_up, conv_ffn, w_down, norm_ple, w_ple, w_ple_gate, rel_bias, norm_final):
    assert norm_mix.shape[0] == 1
    bp = x_prompt.shape[0]
    dt = x_prompt.dtype
    wts = (norm_mix[0], w_in[0], conv_b[0], a_log[0], dt_bias[0], norm_gdn[0], w_proj_a[0], w_proj_b[0], w_out[0],
           norm_ffn[0], w_up[0], conv_ffn[0], w_down[0], norm_ple[0], w_ple[0], w_ple_gate[0], rel_bias, norm_final)
    outs_p = _layer(x_prompt, p_prompt[0], jnp.zeros((bp, 0, HA, DH), dt), jnp.zeros((bp, 0, HA, DH), dt),
                    jnp.zeros((bp, 0, D_IDX), dt), jnp.zeros((bp, HB, DK, DV), dt),
                    jnp.zeros((bp, CONV_B - 1, C_CONV_B), dt), jnp.zeros((bp, CONV_F - 1, D_FF), dt),
                    wts, tm=256, tq=256)
    outs_s = _layer(x_sample, p_sample[0], cache_k[0], cache_v[0], cache_kidx[0], state_gdn[0],
                    state_gdn_conv[0], state_ffn_conv[0], wts, tm=256, tq=256)
    yp, ys = outs_p[0], outs_s[0]
    return (yp, ys) + tuple(a[None] for a in outs_p[1:]) + tuple(a[None] for a in outs_s[1:])
```

```python
import functools

import numpy as np
import jax
import jax.numpy as jnp
from jax import lax
from jax.experimental import pallas as pl
from jax.experimental.pallas import tpu as pltpu

F32 = jnp.float32
BF16 = jnp.bfloat16
I32 = jnp.int32

D_MODEL = 1024
CHUNK = 64
HA, DH = 8, 64
H_IDX, D_IDX = 8, 64
TOPK_MAX = 256
NUM_BUCKETS, MAX_DISTANCE = 32, 128
HB, DK, DV = 4, 128, 128
CONV_B = 4
D_FF = 2816
CONV_F = 3
D_PLE = 256
EPS = 1e-6
NEG = -1e30
WA = HA * DH
WB = HB * DK
C_CONV_B = 3 * WB
INT_MIN = -2 ** 31
LOG2E = 1.4426950408889634

_O_QA, _O_KA, _O_VA, _O_QI, _O_KI, _O_WI = 0, 512, 1024, 1536, 2048, 2112
_O_QB, _O_GB, _O_BB, _O_AB, _O_GA, _O_GBR = 2120, 3656, 4168, 4172, 4176, 5200
_L_WI, _L_BB, _L_AB = 64, 72, 76

VMEM_LIMIT = 56 * 1024 * 1024


def _cparams(sem):
    return pltpu.CompilerParams(dimension_semantics=sem, vmem_limit_bytes=VMEM_LIMIT)


def _const_spec(shape):
    nd = len(shape)
    return pl.BlockSpec(shape, lambda *_: (0,) * nd, pipeline_mode=pl.Buffered(1))


def _rms(x, g):
    return x * lax.rsqrt(jnp.mean(x * x, axis=-1, keepdims=True) + EPS) * g


def _split(x):
    hi = x.astype(BF16)
    lo = (x - hi.astype(F32)).astype(BF16)
    return hi, lo


def _dot(a, b):
    return jnp.dot(a, b, preferred_element_type=F32)


def _dot3(a, b):
    ah, al = _split(a)
    bh, bl = _split(b)
    return _dot(ah, bh) + (_dot(al, bh) + _dot(ah, bl))


def _sigmoid(x):
    return 1.0 / (1.0 + jnp.exp(-x))


def _silu(x):
    return x * _sigmoid(x)


def _transpose32(a):
    a = list(a)
    j, m = 16, 0x0000FFFF
    while j:
        k = 0
        while k < 32:
            t = (a[k] ^ lax.shift_right_logical(a[k + j], jnp.int32(j))) & jnp.int32(m - (1 << 32) if m >> 31 else m)
            a[k] = a[k] ^ t
            a[k + j] = a[k + j] ^ (t << j)
            k = (k + j + 1) & ~j
        j >>= 1
        m = (m ^ (m << j)) & 0xFFFFFFFF
    return a


V_ROWS = DH + 16
RADIX_TILES = 8
_NT = (((1,), (1,)), ((), ()))


def _in_proj_kernel(x_ref, g_ref, wm_ref, wt_ref, wqh_ref, wql_ref, wsh_ref, wsl_ref, wwh_ref, wwl_ref,
                    ph_ref, plo_ref, sc_ref,
                    ka_ref, va_ref, kbf_ref, conv_ref, gb_ref, ga_ref, gbr_ref, small_ref, ki3_ref,
                    qT_ref, qihT_ref, qilT_ref, wiT_ref, vTa_ref):
    tm = x_ref.shape[0]
    h = _rms(x_ref[...], g_ref[...])
    hh, hl = _split(h)

    def main(lo, hi):
        return _dot(hh, wm_ref[:, lo:hi])

    def nt(w, a):
        return lax.dot_general(w, a, _NT, preferred_element_type=F32)

    def nt3(wh_ref, wl_ref):
        return nt(wh_ref[...], hh) + (nt(wh_ref[...], hl) + nt(wl_ref[...], hh))

    ka = main(0, 512)
    va = main(512, 1024)
    for hd in range(HA):
        ka_ref[:, hd, :] = ka[:, hd * DH:(hd + 1) * DH]
        va_ref[:, hd, :] = va[:, hd * DH:(hd + 1) * DH]
    kbf_ref[...] = ka.astype(BF16)
    conv_ref[...] = main(1024, 2560)
    gb_ref[...] = main(2560, 3072)
    ga_ref[...] = main(3072, 4096)
    gbr_ref[...] = main(4096, 5120)

    qT_ref[0] = nt(wt_ref[0:WA, :], hh).astype(BF16)
    vT = nt(wt_ref[WA:2 * WA, :], hh).astype(BF16)
    ones = jnp.ones((V_ROWS - DH, tm), BF16)
    for hd in range(HA):
        vTa_ref[0, hd * V_ROWS:hd * V_ROWS + DH, :] = vT[hd * DH:(hd + 1) * DH]
        vTa_ref[0, hd * V_ROWS + DH:(hd + 1) * V_ROWS, :] = ones
    qh, ql = _split(nt3(wqh_ref, wql_ref))
    qihT_ref[0] = qh
    qilT_ref[0] = ql
    wiT_ref[0] = nt3(wwh_ref, wwl_ref)[0:H_IDX] * (H_IDX ** -0.5)

    small = (_dot(hh, wsh_ref[...]) + (_dot(hl, wsh_ref[...]) + _dot(hh, wsl_ref[...]))) * sc_ref[...]
    small_ref[...] = small
    sh, sl = _split(small)
    ki3_ref[...] = (_dot(sh, ph_ref[...]) + _dot(sl, plo_ref[...])).astype(BF16)


def _in_proj(x2d, norm_mix, w_in, tm):
    n = x2d.shape[0]
    assert n % tm == 0
    g = n // tm
    w = w_in
    wm = jnp.concatenate([w[:, _O_KA:_O_QI], w[:, _O_QB:_O_BB], w[:, _O_GA:]], axis=1).astype(BF16)
    wt = jnp.concatenate([w[:, _O_QA:_O_KA] * (DH ** -0.5 * LOG2E), w[:, _O_VA:_O_QI]], axis=1).T.astype(BF16)
    hilo = lambda a: (a.astype(BF16), (a - a.astype(BF16).astype(F32)).astype(BF16))
    wqh, wql = hilo((w[:, _O_QI:_O_KI] * (D_IDX ** -0.5)).T)
    wsh, wsl = hilo(jnp.concatenate([w[:, _O_KI:_O_QB], w[:, _O_BB:_O_GA], jnp.zeros((D_MODEL, 48), F32)], axis=1))
    wwh, wwl = hilo(jnp.concatenate([w[:, _O_WI:_O_QB], jnp.zeros((D_MODEL, 8), F32)], axis=1).T)
    ph = np.zeros((128, 256), np.float32)
    plo = np.zeros((128, 256), np.float32)
    for c in range(64):
        ph[c, c] = 1.0
        ph[c, 64 + c] = 1.0
        plo[c, 128 + c] = 1.0
    sc = np.ones((1, 128), np.float32)
    sc[0, _L_WI:_L_WI + H_IDX] = H_IDX ** -0.5
    row = lambda c: pl.BlockSpec((tm, c), lambda i: (i, 0))
    colT = lambda r: pl.BlockSpec((1, r, tm), lambda i: (i, 0, 0))
    heads = pl.BlockSpec((tm, HA, DH), lambda i: (i, 0, 0))
    out_cols = [(512, BF16), (1536, F32), (512, F32), (1024, F32), (1024, F32), (128, F32), (256, BF16)]
    out_rows = [(WA, BF16), (WA, BF16), (WA, BF16), (H_IDX, F32), (HA * V_ROWS, BF16)]
    return pl.pallas_call(
        _in_proj_kernel,
        grid=(g,),
        in_specs=[row(D_MODEL), _const_spec((1, D_MODEL)), _const_spec((D_MODEL, 5120)),
                  _const_spec((2 * WA, D_MODEL)), _const_spec((WA, D_MODEL)), _const_spec((WA, D_MODEL)),
                  _const_spec((D_MODEL, 128)), _const_spec((D_MODEL, 128)),
                  _const_spec((16, D_MODEL)), _const_spec((16, D_MODEL)),
                  _const_spec((128, 256)), _const_spec((128, 256)), _const_spec((1, 128))],
        out_specs=[heads, heads] + [row(c) for c, _ in out_cols] + [colT(r) for r, _ in out_rows],
        out_shape=[jax.ShapeDtypeStruct((n, HA, DH), F32)] * 2
        + [jax.ShapeDtypeStruct((n, c), d) for c, d in out_cols]
        + [jax.ShapeDtypeStruct((g, r, tm), d) for r, d in out_rows],
        compiler_params=_cparams(("arbitrary",)),
        name="in_proj",
    )(x2d, norm_mix.reshape(1, D_MODEL), wm, wt, wqh, wql, wsh, wsl, wwh, wwl, jnp.asarray(ph, BF16),
      jnp.asarray(plo, BF16), jnp.asarray(sc))


def _store_keys(score, j, key_ref, pln_ref, eq_ref):
    tk, tq = score.shape
    bits = pltpu.bitcast(score, I32)
    bits = jnp.where(bits == INT_MIN, 0, bits)
    key = bits ^ ((bits >> 31) & 0x7FFFFFFF)
    key_ref[j] = key
    u3 = (key ^ INT_MIN).reshape(tk // 8, 8, tq)
    ones = jnp.full((8, tq), -1, I32)
    pln_ref[0, j] = ones
    for l0 in range(0, tq, 128):
        planes = _transpose32([u3[r][:, l0:l0 + 128] for r in range(32)])
        for b in range(32):
            pln_ref[b + 1, j, :, l0:l0 + 128] = planes[b]
    eq_ref[0, j] = ones


def _radix_threshold(pln_ref, eq_ref, nk, tq, topk):
    for d in range(RADIX_TILES - 1):
        pln_ref[:, nk + d] = jnp.zeros((33, 8, tq), I32)
        eq_ref[0, nk + d] = jnp.zeros((8, tq), I32)

    sub = lax.broadcasted_iota(I32, (8, tq), 0)
    group_masks = (-0x10000, -0xFF0100, -0xF0F0F10, -0x33333334, -0x55555556)

    def pos_plane(e, j):
        if e < 6:
            return jnp.broadcast_to(-((~j >> (5 - e)) & 1), (8, tq))
        if e < 11:
            return jnp.full((8, tq), group_masks[e - 6], I32)
        return -((~sub >> (13 - e)) & 1)

    def sweep(prev, cur, carry, src, dst):
        n_gt, flip = carry

        def body(jq, cnt):
            for d in range(RADIX_TILES):
                j = jq * RADIX_TILES + d
                e = eq_ref[src, j] & (prev(j) ^ flip)
                eq_ref[dst, j] = e
                cnt = cnt + lax.population_count(e & cur(j))
            return cnt

        cnt = lax.fori_loop(0, (nk + RADIX_TILES - 1) // RADIX_TILES, body, jnp.zeros((8, tq), I32))
        cnt = n_gt + cnt.sum(axis=0, keepdims=True)
        acc = cnt >= topk
        return acc, (jnp.where(acc, n_gt, cnt), jnp.where(acc, 0, -1))

    def key_sweep(it, carry, src, dst):
        tu, rest = carry
        acc, rest = sweep(lambda j: pln_ref[it, j], lambda j: pln_ref[it + 1, j], rest, src, dst)
        return jnp.where(acc, tu | jnp.left_shift(jnp.int32(1), 31 - it), tu), rest

    def two_bits(i, carry):
        return key_sweep(2 * i + 1, key_sweep(2 * i, carry, 0, 1), 1, 0)

    zero_row = jnp.zeros((1, tq), I32)
    tu, rest = lax.fori_loop(0, 16, two_bits, (zero_row, (zero_row, zero_row)))
    inv_pos = zero_row
    for e in range(14):
        prev = (lambda j: pln_ref[32, j]) if e == 0 else functools.partial(pos_plane, e - 1)
        acc, rest = sweep(prev, functools.partial(pos_plane, e), rest, e % 2, 1 - e % 2)
        inv_pos = jnp.where(acc, inv_pos | (1 << (13 - e)), inv_pos)
    return tu ^ INT_MIN, ~inv_pos & 0x3FFF


def _dsa_kernel(qT_ref, qihT_ref, qilT_ref, wiT_ref, k_ref, vT_ref, ki3_ref, bias_ref, cap_ref,
                o_ref, key_ref, l_ref, acc_ref, lg_ref, pln_ref, eq_ref, sc_ref, ma_ref,
                *, tq, tk, nk_static, causal, topk):
    nk = (pl.program_id(1) + 1) if causal else nk_static
    wiT = wiT_ref[0]

    def when(cond):
        if isinstance(cond, bool):
            return (lambda f: f()) if cond else (lambda f: None)
        return pl.when(cond)

    zeros64 = jnp.zeros((64, tq), BF16)
    q3 = []
    for h in range(H_IDX):
        hi = qihT_ref[0, h * 64:(h + 1) * 64, :]
        lo = qilT_ref[0, h * 64:(h + 1) * 64, :]
        q3.append(jnp.concatenate([hi, lo, hi, zeros64], axis=0))

    def score_matmuls(j0, n):
        kts = [ki3_ref[0, j0 + d] for d in range(n)]
        accs = [None] * n
        for h in range(H_IDX):
            for d in range(n):
                t = jnp.maximum(_dot(kts[d], q3[h]), 0.0) * wiT[h:h + 1, :]
                accs[d] = t if accs[d] is None else accs[d] + t
        for d in range(n):
            sc_ref[d] = jnp.minimum(accs[d], cap_ref[jnp.where(j0 + d == nk - 1, 1, 0)])

    def score_finish(j0, n):
        for d in range(n):
            _store_keys(sc_ref[d], j0 + d, key_ref, pln_ref, eq_ref)

    def score_body(i, c):
        score_finish(2 * i - 2, 2)
        score_matmuls(2 * i, 2)
        return c

    @when(nk >= 2)
    def _():
        score_matmuls(0, 2)
        lax.fori_loop(1, nk // 2, score_body, 0)
        score_finish(2 * (nk // 2) - 2, 2)

    @when(nk % 2 == 1)
    def _():
        score_matmuls(nk - 1, 1)
        score_finish(nk - 1, 1)

    thr, pos_thr = _radix_threshold(pln_ref, eq_ref, nk, tq, topk)

    l_ref[...] = jnp.zeros(l_ref.shape, F32)
    acc_ref[...] = jnp.zeros(acc_ref.shape, F32)
    qm = []
    for h in range(HA):
        qh = qT_ref[0, h * 64:(h + 1) * 64, :]
        qm.append(jnp.concatenate([qh, zeros64] if h % 2 == 0 else [zeros64, qh], axis=0))
    row = lax.broadcasted_iota(I32, (tk, tq), 0)

    def tile_logits(j, slot, far):
        sel = key_ref[j] > jnp.where(row <= pos_thr - j * tk, thr - 1, thr)
        kind = jnp.clip(j - (nk - 3), 0, 2)
        mx = []
        for h in range(HA):
            pr = h // 2
            lg = _dot(k_ref[0, j, :, pr * 128:(pr + 1) * 128], qm[h])
            if not far:
                lg = lg + bias_ref[kind, h]
            lg = jnp.where(sel, lg, NEG).astype(BF16)
            lg_ref[slot, h] = lg
            part = jnp.max(lg.reshape(tk // 16, 16, tq), axis=0).astype(F32)
            mx.append(jnp.max(part, axis=0, keepdims=True))
        return jnp.concatenate(mx, axis=0)

    def tile_values(j, slot, mx):
        m_old = ma_ref[...]
        m_new = jnp.maximum(m_old, mx)
        alpha = jnp.exp2(m_old - m_new)
        ma_ref[...] = m_new
        ls = []
        for h in range(HA):
            p = jnp.exp2(lg_ref[slot, h] - m_new[h:h + 1, :].astype(BF16))
            pv = _dot(vT_ref[0, j, h * V_ROWS:(h + 1) * V_ROWS, :], p)
            acc_ref[h * 64:(h + 1) * 64, :] = alpha[h:h + 1, :] * acc_ref[h * 64:(h + 1) * 64, :] + pv[0:DH]
            ls.append(pv[DH:DH + 1])
        l_ref[...] = alpha * l_ref[...] + jnp.concatenate(ls, axis=0)

    ma_ref[...] = jnp.full((HA, tq), NEG, F32)

    def pair(j0, far):
        mx0 = tile_logits(j0, 0, far)
        mx1 = tile_logits(j0 + 1, 1, far)
        tile_values(j0, 0, mx0)
        tile_values(j0 + 1, 1, mx1)

    def far_pair(i, c):
        pair(2 * i, True)
        return c

    n_far = jnp.maximum(nk - 2, 0) // 2
    lax.fori_loop(0, n_far, far_pair, 0)

    @when(nk >= 2)
    def _():
        pair(2 * n_far, False)

    @when(nk % 2 == 1)
    def _():
        tile_values(nk - 1, 0, tile_logits(nk - 1, 0, False))

    for h in range(HA):
        o_ref[0, h * 64:(h + 1) * 64, :] = (acc_ref[h * 64:(h + 1) * 64, :] / l_ref[h:h + 1, :]).astype(BF16)


def _t5_bucket_np(rel):
    half = NUM_BUCKETS // 2
    max_exact = half // 2
    out = np.zeros(rel.shape, np.int64)
    flat_rel = rel.reshape(-1)
    flat = out.reshape(-1)
    for a in range(flat_rel.size):
        r = int(flat_rel[a])
        n = abs(r)
        b = n if n < max_exact else min(half - 1, (n * n).bit_length() + 1)
        flat[a] = b + (half if r > 0 else 0)
    return out


def _bias_tables(rel_bias, adm, tk, tq, tq_valid):
    kj = np.arange(tk)[:, None]
    t = np.minimum(np.arange(tq), tq_valid - 1)[None, :]
    rel_to_bucket = _t5_bucket_np(np.arange(-2 * tk - tq, tk + 1))
    lut = lambda rel: rel_to_bucket[rel + 2 * tk + tq]
    far_bucket = NUM_BUCKETS // 2 - 1
    idx = np.stack([np.full((tk, tq), far_bucket), lut(kj - tk - t), lut(kj - t)], axis=0)
    tab = rel_bias.astype(F32) * LOG2E
    onehot = jnp.asarray(idx[..., None] == np.arange(NUM_BUCKETS), F32)
    bias = jnp.einsum("ktqb,bh->khtq", onehot, tab - tab[far_bucket], precision=lax.Precision.HIGHEST)
    mask = np.zeros((3, 1, tk, tq), np.float32)
    mask[2, 0] = np.where(adm > 0.5, 0.0, NEG)
    return bias + jnp.asarray(mask)


def _dsa(qT, qihT, qilT, wiT, k4, vT4, ki34, bias, adm, *, nq, causal, topk):
    g, _, tq = qT.shape
    b, nk, tk, _ = k4.shape
    assert g == b * nq and tk == 256 and nk + RADIX_TILES - 1 <= 64
    cap = jnp.asarray(np.stack([np.full((tk, tq), np.inf, np.float32), np.where(adm > 0.5, np.inf, NEG)]), F32)
    qspec = lambda r: pl.BlockSpec((1, r, tq), lambda bi, i: (bi * nq + i, 0, 0))
    kspec = lambda s: pl.BlockSpec((1,) + s, lambda bi, i: (bi, 0, 0, 0), pipeline_mode=pl.Buffered(1))
    kern = functools.partial(_dsa_kernel, tq=tq, tk=tk, nk_static=nk, causal=causal, topk=topk)
    return pl.pallas_call(
        kern,
        grid=(b, nq),
        in_specs=[qspec(512), qspec(512), qspec(512), qspec(8),
                  kspec((nk, tk, 512)), kspec((nk, HA * V_ROWS, tk)), kspec((nk, tk, 256)),
                  _const_spec((3, HA, tk, tq)), _const_spec((2, tk, tq))],
        out_specs=qspec(512),
        out_shape=jax.ShapeDtypeStruct((g, 512, tq), BF16),
        scratch_shapes=[pltpu.VMEM((nk, tk, tq), I32), pltpu.VMEM((8, tq), F32),
                        pltpu.VMEM((512, tq), F32), pltpu.VMEM((2, HA, tk, tq), BF16),
                        pltpu.VMEM((33, nk + RADIX_TILES - 1, 8, tq), I32),
                        pltpu.VMEM((2, nk + RADIX_TILES - 1, 8, tq), I32),
                        pltpu.VMEM((2, tk, tq), F32), pltpu.VMEM((HA, tq), F32)],
        compiler_params=_cparams(("arbitrary", "arbitrary")),
        name="dsa_causal" if causal else "dsa_cached",
    )(qT, qihT, qilT, wiT, k4, vT4, ki34, bias, cap)


def _dsa_step_kernel(qa_ref, qih_ref, qil_ref, w_ref, ck_ref, cv_ref, cki_ref, nk_ref, nv_ref, nki_ref,
                     bias_ref, cap_ref, blk_ref, o_ref, key_ref, pln_ref, eq_ref, m_ref, l_ref, acc_ref,
                     *, nkc, t, topk):
    tk, tq = nk_ref.shape[1], qa_ref.shape[2]
    qih, qil = qih_ref[0], qil_ref[0]
    w = w_ref[0, 0:1, :]
    shifts = [t << s for s in range((tq // t).bit_length() - 2, -1, -1)]

    def scores(kx, j, cap):
        kh, kl = _split(kx)
        s = _dot(kh, qih) + (_dot(kl, qih) + _dot(kh, qil))
        yield
        sc = jnp.maximum(s, 0.0) * w
        for sh in shifts:
            sc = sc + pltpu.roll(sc, sh, 1)
            yield
        if cap is not None:
            sc = jnp.minimum(sc, cap)
        _store_keys(sc, j, key_ref, pln_ref, eq_ref)

    def together(*gens):
        gens = list(gens)
        while gens:
            gens = [g for g in gens if next(g, gens) is not gens]

    def score_body(i, c):
        cached_idx = lambda j: cki_ref[0, pl.ds(pl.multiple_of(j * tk, tk), tk), :]
        together(scores(cached_idx(2 * i), 2 * i, None), scores(cached_idx(2 * i + 1), 2 * i + 1, None))
        return c

    lax.fori_loop(0, nkc // 2, score_body, 0)
    together(scores(nki_ref[0], nkc, cap_ref[...]))
    thr, pos_thr = _radix_threshold(pln_ref, eq_ref, nkc + 1, tq, topk)

    m_ref[...] = jnp.full(m_ref.shape, NEG, F32)
    l_ref[...] = jnp.zeros(l_ref.shape, F32)
    acc_ref[...] = jnp.zeros(acc_ref.shape, F32)
    qa = qa_ref[0]
    row = lax.broadcasted_iota(I32, (tk, tq), 0)
    ones8 = jnp.ones((8, tk), BF16)
    tn = (((0,), (0,)), ((), ()))

    def logits(k_bf, j, bias):
        sel = key_ref[j] > jnp.where(row <= pos_thr - j * tk, thr - 1, thr)
        lg = _dot(k_bf, qa)
        if bias is not None:
            lg = lg + bias
        return jnp.where(sel, lg, NEG).astype(BF16)

    def consume(lg, v_bf):
        part = jnp.max(lg.reshape(tk // 16, 16, tq), axis=0).astype(F32)
        m_old = m_ref[0:1, :]
        m_new = jnp.maximum(m_old, jnp.max(part, axis=0, keepdims=True))
        alpha = jnp.exp2(m_old - m_new)
        p = jnp.exp2(lg - m_new.astype(BF16))
        l_ref[0:1, :] = alpha * l_ref[0:1, :] + _dot(ones8, p)[0:1]
        acc_ref[...] = alpha * acc_ref[...] + lax.dot_general(v_bf, p, tn, preferred_element_type=F32)
        m_ref[0:1, :] = m_new

    def cached_pair(j0, bias1):
        r0 = pl.ds(pl.multiple_of(j0 * tk, tk), tk)
        r1 = pl.ds(pl.multiple_of(j0 * tk + tk, tk), tk)
        lg0 = logits(ck_ref[0, r0, :].astype(BF16), j0, None)
        lg1 = logits(ck_ref[0, r1, :].astype(BF16), j0 + 1, bias1)
        consume(lg0, cv_ref[0, r0, :].astype(BF16))
        consume(lg1, cv_ref[0, r1, :].astype(BF16))

    def far(i, c):
        cached_pair(2 * i, None)
        return c

    lax.fori_loop(0, nkc // 2 - 1, far, 0)
    cached_pair(nkc - 2, bias_ref[0])
    consume(logits(nk_ref[0], nkc, bias_ref[1]), nv_ref[0])
    o = acc_ref[...] / l_ref[0:1, :] * blk_ref[...]
    for sh in shifts:
        o = o + pltpu.roll(o, sh, 1)
    o_ref[0] = o.astype(BF16)


def _dsa_step(qT, qihT, qilT, wiT, kbf, va, ki_new, past_k, past_v, past_kidx, rel_bias, *, b, t, topk):
    past = past_k.shape[1]
    tk, tq = 256, HA * t
    assert tq == 128 and past % (2 * tk) == 0 and t <= tk
    nkc = past // tk
    nk = nkc + 1
    per_b = lambda a: jnp.swapaxes(a[0].reshape(a.shape[1], b, t), 0, 1)
    blk = (np.arange(WA)[:, None] // DH == np.arange(tq)[None, :] // t).astype(np.float32)
    qa = (jnp.tile(per_b(qT), (1, 1, HA)) * jnp.asarray(blk, BF16))
    lanes = lambda a: jnp.swapaxes(per_b(a).reshape(b, HA, D_IDX, t), 1, 2).reshape(b, D_IDX, tq)
    w = jnp.broadcast_to(per_b(wiT).reshape(b, 1, tq), (b, 8, tq))
    pad_rows = lambda a: jnp.pad(a.reshape(b, t, a.shape[-1]), ((0, 0), (0, tk - t), (0, 0)))
    adm = np.broadcast_to(np.arange(tk)[:, None] < t, (tk, tq)).astype(np.float32)
    bias = _bias_tables(rel_bias, adm[:, :t], tk, t, t)[1:]
    bias = jnp.swapaxes(bias, 1, 2).reshape(2, tk, tq)
    cap = jnp.asarray(np.where(adm > 0.5, np.inf, NEG), F32)
    bspec = lambda s: pl.BlockSpec((1,) + s, lambda bi: (bi,) + (0,) * len(s))
    return pl.pallas_call(
        functools.partial(_dsa_step_kernel, nkc=nkc, t=t, topk=topk),
        grid=(b,),
        in_specs=[bspec((WA, tq)), bspec((D_IDX, tq)), bspec((D_IDX, tq)), bspec((8, tq)),
                  bspec((past, WA)), bspec((past, WA)), bspec((past, D_IDX)),
                  bspec((tk, WA)), bspec((tk, WA)), bspec((tk, D_IDX)),
                  _const_spec((2, tk, tq)), _const_spec((tk, tq)), _const_spec((WA, tq))],
        out_specs=bspec((WA, tq)),
        out_shape=jax.ShapeDtypeStruct((b, WA, tq), BF16),
        scratch_shapes=[pltpu.VMEM((nk, tk, tq), I32), pltpu.VMEM((33, nk + RADIX_TILES - 1, 8, tq), I32),
                        pltpu.VMEM((2, nk + RADIX_TILES - 1, 8, tq), I32), pltpu.VMEM((8, tq), F32),
                        pltpu.VMEM((8, tq), F32), pltpu.VMEM((WA, tq), F32)],
        compiler_params=_cparams(("arbitrary",)),
        name="dsa_step",
    )(qa, lanes(qihT), lanes(qilT), w, past_k.reshape(b, past, WA), past_v.reshape(b, past, WA), past_kidx,
      pad_rows(kbf), pad_rows(va.astype(BF16)), pad_rows(ki_new), bias, cap, jnp.asarray(blk))


def _gdn_pre_kernel(conv_ref, prev_ref, hist_ref, small_ref, cw_ref, alog_ref, dtb_ref,
                    wm_ref, um_ref, qe_ref, oi_ref, e_ref, ext_ref, *, rows, nvalid):
    c = CHUNK
    n = HB * c
    ext_ref[0:8, :] = jnp.where(pl.program_id(1) == 0, hist_ref[0], prev_ref[0])
    ext_ref[8:8 + rows, :] = conv_ref[0]
    cb = ext_ref[5:5 + rows, :] * cw_ref[0:1, :]
    for j in range(1, CONV_B):
        cb = cb + ext_ref[5 + j:5 + j + rows, :] * cw_ref[j:j + 1, :]
    cb = _silu(cb)

    small = small_ref[0]
    pos = lax.broadcasted_iota(I32, (rows, 1), 0) % c
    rowv = (pos < nvalid).astype(F32)
    beta_all = _sigmoid(small) * rowv
    sp = small + dtb_ref[...]
    g_all = -jnp.exp(alog_ref[...]) * (jnp.maximum(sp, 0.0) + jnp.log(1.0 + jnp.exp(-jnp.abs(sp)))) * rowv

    ri = lax.broadcasted_iota(I32, (n, n), 0)
    ci = lax.broadcasted_iota(I32, (n, n), 1)
    same = (ri // c) == (ci // c)
    tri_b = (same & (ci <= ri)).astype(BF16)
    bd_f = same.astype(F32)
    bd_b = same.astype(BF16)
    wr = lax.broadcasted_iota(I32, (c, n), 0)
    wl = lax.broadcasted_iota(I32, (c, n), 1)
    grp = wl // c
    tri_w = (wl % c) <= wr
    strict_w = (wl % c) < wr
    eye_w = ((wl % c) == wr).astype(F32)
    nt = (((1,), (1,)), ((), ()))
    tn = (((0,), (0,)), ((), ()))

    def l2n(x):
        return x * lax.rsqrt(jnp.sum(x * x, axis=-1, keepdims=True) + EPS)

    def to_wide(full):
        out = jnp.where(grp == 0, full[0:c, :], 0.0)
        for h in range(1, HB):
            out = out + jnp.where(grp == h, full[h * c:(h + 1) * c, :], 0.0)
        return out

    def tile4(x):
        return jnp.concatenate([x] * HB, axis=0)

    def chunk(k0):
        def stack(fn):
            return jnp.concatenate([fn(h) for h in range(HB)], axis=0)

        rv = rowv[k0:k0 + c]
        q = stack(lambda h: l2n(cb[k0:k0 + c, h * DK:(h + 1) * DK]) * (DK ** -0.5))
        k = stack(lambda h: l2n(cb[k0:k0 + c, WB + h * DK:WB + (h + 1) * DK]) * rv)
        v = stack(lambda h: cb[k0:k0 + c, 2 * WB + h * DV:2 * WB + (h + 1) * DV] * rv)
        beta = stack(lambda h: beta_all[k0:k0 + c, _L_BB + h:_L_BB + h + 1])
        g = stack(lambda h: g_all[k0:k0 + c, _L_AB + h:_L_AB + h + 1])
        yield

        gh, gl = _split(jnp.broadcast_to(g, (n, 128)))
        gcum = _dot(tri_b, gh) + _dot(tri_b, gl)
        yield
        gcum_row = gcum.T[0:1, :]
        gcum_col = gcum[:, 0:1]
        col_w = to_wide(jnp.concatenate([gcum, gcum], axis=1))
        decay_w = jnp.where(tri_w, jnp.exp(jnp.where(tri_w, col_w - gcum_row, 0.0)), 0.0)
        kb = k * beta
        kbf = k.astype(BF16)
        kk_w = to_wide(lax.dot_general(kb.astype(BF16), kbf, nt, preferred_element_type=F32))
        qk_w = to_wide(lax.dot_general(q.astype(BF16), kbf, nt, preferred_element_type=F32))
        nmat_w = jnp.where(strict_w, kk_w * decay_w, 0.0)
        attn_w = jnp.where(tri_w, qk_w * decay_w, 0.0)
        yield

        inv_w = eye_w - nmat_w
        ph = nmat_w.astype(BF16)
        bh = tile4(ph) * bd_b
        for _ in range(int(np.log2(c)) - 1):
            ph = _dot(ph, bh).astype(BF16)
            yield
            bh = tile4(ph) * bd_b
            inv_w = inv_w + _dot(inv_w.astype(BF16), bh)
            yield
        rhs = jnp.concatenate([v * beta, kb * jnp.exp(gcum_col)], axis=1)
        sol = _dot3(tile4(inv_w) * bd_f, rhs)
        yield
        solb = sol.astype(BF16)
        aw = _dot((tile4(attn_w) * bd_f).astype(BF16), solb)
        yield
        ck = k0 // c
        oi_ref[0, ck] = aw[:, :DV]
        qe_ref[0, ck] = (q * jnp.exp(gcum_col) - aw[:, DV:]).astype(BF16)
        for h in range(HB):
            rs = slice(h * c, (h + 1) * c)
            g_last = gcum[(h + 1) * c - 1:(h + 1) * c, :]
            kd = (k[rs] * jnp.exp(g_last - gcum[rs])).astype(BF16)
            uw = lax.dot_general(kd, solb[rs], tn, preferred_element_type=F32)
            um_ref[0, ck, h] = uw[:, :DV]
            wm_ref[0, ck, h] = uw[:, DV:].astype(BF16)
            e_ref[0, ck, h:h + 1, :] = jnp.exp(g_last)
        e_ref[0, ck, HB:8, :] = jnp.zeros((8 - HB, 128), F32)

    gens = [chunk(k0) for k0 in range(0, rows, c)]
    while gens:
        alive = []
        for gen in gens:
            try:
                next(gen)
                alive.append(gen)
            except StopIteration:
                pass
        gens = alive


def _gdn_scan_kernel(wm_ref, um_ref, qe_ref, oi_ref, e_ref, gb_ref, s0_ref, ng_ref, ob_ref, sfin_ref, s_ref,
                     *, bb, g):
    c = CHUNK

    @pl.when(pl.program_id(1) == 0)
    def _():
        s_ref[...] = s0_ref[...]

    ng = ng_ref[...]
    for b in range(bb):
        for ck in range(g):
            for h in range(HB):
                s = s_ref[b, h]
                sb = s.astype(BF16)
                rs = slice(h * c, (h + 1) * c)
                o = _dot(qe_ref[b, ck, rs, :], sb) + oi_ref[b, ck, rs, :]
                s_ref[b, h] = e_ref[b, ck, h:h + 1, :] * s + (um_ref[b, ck, h] - _dot(wm_ref[b, ck, h], sb))
                gate = gb_ref[b, ck * c:(ck + 1) * c, h * DV:(h + 1) * DV]
                ob_ref[b, ck * c:(ck + 1) * c, h * DV:(h + 1) * DV] = (_rms(o, ng) * _silu(gate)).astype(BF16)
    sfin_ref[...] = s_ref[...]


def _gdn_pre(conv_in, hist, small, conv_w, a_log, dt_bias, nvalid):
    b, t, _ = conv_in.shape
    assert t % CHUNK == 0
    rows = 512 if t % 512 == 0 else CHUNK
    nc, cps = t // CHUNK, rows // CHUNK
    n = HB * CHUNK
    alog = jnp.zeros((1, 128), F32).at[0, _L_AB:_L_AB + HB].set(a_log)
    dtb = jnp.zeros((1, 128), F32).at[0, _L_AB:_L_AB + HB].set(dt_bias)
    row = lambda c_: pl.BlockSpec((1, rows, c_), lambda bi, ti: (bi, ti, 0))
    prev = pl.BlockSpec((1, 8, C_CONV_B), lambda bi, ti: (bi, jnp.maximum(ti * (rows // 8) - 1, 0), 0))
    per_b = lambda s: pl.BlockSpec((1,) + s, lambda bi, ti: (bi,) + (0,) * len(s))
    chunked = lambda s: pl.BlockSpec((1, cps) + s, lambda bi, ti: (bi, ti) + (0,) * len(s))
    return pl.pallas_call(
        functools.partial(_gdn_pre_kernel, rows=rows, nvalid=nvalid),
        grid=(b, t // rows),
        in_specs=[row(C_CONV_B), prev, per_b((8, C_CONV_B)), row(128),
                  _const_spec((CONV_B, C_CONV_B)), _const_spec((1, 128)), _const_spec((1, 128))],
        out_specs=[chunked((HB, DK, DV)), chunked((HB, DK, DV)), chunked((n, DK)), chunked((n, DV)),
                   chunked((8, 128))],
        out_shape=[jax.ShapeDtypeStruct((b, nc, HB, DK, DV), BF16), jax.ShapeDtypeStruct((b, nc, HB, DK, DV), F32),
                   jax.ShapeDtypeStruct((b, nc, n, DK), BF16), jax.ShapeDtypeStruct((b, nc, n, DV), F32),
                   jax.ShapeDtypeStruct((b, nc, 8, 128), F32)],
        scratch_shapes=[pltpu.VMEM((8 + rows, C_CONV_B), F32)],
        compiler_params=_cparams(("arbitrary", "arbitrary")),
        name="gdn_pre",
    )(conv_in, conv_in, hist, small, conv_w, alog, dtb)


def _gdn_scan(wm, um, qe, oi, e, gb, s0, norm_gdn):
    b, nc = wm.shape[:2]
    t = nc * CHUNK
    n = HB * CHUNK
    bb = 2 if b % 2 == 0 else 1
    g = 4 if nc % 4 == 0 else 1
    blk = lambda s: pl.BlockSpec((bb, g) + s, lambda bi, ci: (bi, ci) + (0,) * len(s))
    rowb = pl.BlockSpec((bb, g * CHUNK, WB), lambda bi, ci: (bi, ci, 0))
    state = pl.BlockSpec((bb, HB, DK, DV), lambda bi, ci: (bi, 0, 0, 0))
    return pl.pallas_call(
        functools.partial(_gdn_scan_kernel, bb=bb, g=g),
        grid=(b // bb, nc // g),
        in_specs=[blk((HB, DK, DV)), blk((HB, DK, DV)), blk((n, DK)), blk((n, DV)), blk((8, 128)), rowb, state,
                  _const_spec((1, DV))],
        out_specs=[rowb, state],
        out_shape=[jax.ShapeDtypeStruct((b, t, WB), BF16), jax.ShapeDtypeStruct((b, HB, DK, DV), F32)],
        scratch_shapes=[pltpu.VMEM((bb, HB, DK, DV), F32)],
        compiler_params=_cparams(("arbitrary", "arbitrary")),
        name="gdn_scan",
    )(wm, um, qe, oi, e, gb, s0, norm_gdn.reshape(1, DV))


def _post_kernel(x_ref, oaT_ref, ob_ref, ga_ref, gbr_ref, p_ref, hist_ref, wa_ref, wb_ref, wo_ref, nf_ref,
                 wup_ref, cw_ref, wdn_ref, npl_ref, wpg_ref, wple_ref, nfin_ref, y_ref, tail_ref, ext_ref,
                 *, tm, seq):
    if seq is None:
        @pl.when(pl.program_id(1) == 0)
        def _():
            ext_ref[0:8, :] = hist_ref[0]
    else:
        ext_ref[0:8, :] = jnp.zeros((8, D_FF), F32)

    ya = lax.dot_general(oaT_ref[0], wa_ref[...], (((0,), (0,)), ((), ())), preferred_element_type=F32)
    yb = _dot(ob_ref[0], wb_ref[...])
    mix = _sigmoid(ga_ref[0]) * ya + _sigmoid(gbr_ref[0]) * yb
    x1 = x_ref[0] + _dot(mix.astype(BF16), wo_ref[...])
    h2 = _rms(x1, nf_ref[...]).astype(BF16)
    ext_ref[8:8 + tm, :] = _dot(h2, wup_ref[:, 0:D_FF])
    u_val = _dot(h2, wup_ref[:, D_FF:2 * D_FF])
    if seq is None:
        cv = ext_ref[6:6 + tm, :] * cw_ref[0:1, :]
        for j in range(1, CONV_F):
            cv = cv + ext_ref[6 + j:6 + j + tm, :] * cw_ref[j:j + 1, :]
        tail = ext_ref[tm:tm + 8, :]
        ext_ref[0:8, :] = tail
        tail_ref[0] = tail
    else:
        pos = lax.broadcasted_iota(I32, (tm, 1), 0) % seq
        cv = jnp.where(pos >= 2, ext_ref[6:6 + tm, :], hist_ref[1]) * cw_ref[0:1, :]
        cv = cv + jnp.where(pos >= 1, ext_ref[7:7 + tm, :], hist_ref[0]) * cw_ref[1:2, :]
        cv = cv + ext_ref[8:8 + tm, :] * cw_ref[2:3, :]
        tail_ref[...] = ext_ref[8:8 + tm, :]
    act = 0.5 * cv * (1.0 + jnp.tanh(0.7978845608028654 * (cv + 0.044715 * (cv * cv * cv))))
    x2 = x1 + _dot((act * u_val).astype(BF16), wdn_ref[...])
    gate = _sigmoid(_dot(_rms(x2, npl_ref[...]).astype(BF16), wpg_ref[...]))
    x3 = x2 + gate * _dot(p_ref[0].astype(BF16), wple_ref[...])
    y_ref[0] = _rms(x3, nfin_ref[...])


def _post(x, oaT, ob, ga, gbr, p, hist, w_proj_a, w_proj_b, w_out, norm_ffn, w_up, conv_ffn, w_down, norm_ple,
          w_ple_gate, w_ple, norm_final, tm, seq=None):
    b, t, _ = x.shape
    nt = t // tm
    assert t % tm == 0 and tm >= 8 and oaT.shape == (b * nt, WA, tm)
    row = lambda c: pl.BlockSpec((1, tm, c), lambda bi, ti: (bi, ti, 0))
    vec = _const_spec((1, D_MODEL))
    if seq is None:
        hist_spec = tail_spec = pl.BlockSpec((1, 8, D_FF), lambda bi, ti: (bi, 0, 0))
        tail_shape = (b, 8, D_FF)
    else:
        assert b == 1 and nt == 1 and tm % seq == 0
        hist_spec = _const_spec((2, tm, D_FF))
        tail_spec = pl.BlockSpec((tm, D_FF), lambda bi, ti: (0, 0))
        tail_shape = (tm, D_FF)
    return pl.pallas_call(
        functools.partial(_post_kernel, tm=tm, seq=seq),
        grid=(b, nt),
        in_specs=[row(D_MODEL), pl.BlockSpec((1, WA, tm), lambda bi, ti: (bi * nt + ti, 0, 0)), row(WB),
                  row(D_MODEL), row(D_MODEL), row(D_PLE), hist_spec,
                  _const_spec((WA, D_MODEL)), _const_spec((WB, D_MODEL)), _const_spec((D_MODEL, D_MODEL)), vec,
                  _const_spec((D_MODEL, 2 * D_FF)), _const_spec((CONV_F, D_FF)), _const_spec((D_FF, D_MODEL)),
                  vec, _const_spec((D_MODEL, D_MODEL)), _const_spec((D_PLE, D_MODEL)), vec],
        out_specs=[row(D_MODEL), tail_spec],
        out_shape=[jax.ShapeDtypeStruct((b, t, D_MODEL), F32), jax.ShapeDtypeStruct(tail_shape, F32)],
        scratch_shapes=[pltpu.VMEM((8 + tm, D_FF), F32)],
        compiler_params=_cparams(("arbitrary", "arbitrary")),
        name="post",
    )(x, oaT, ob, ga, gbr, p, hist, w_proj_a.astype(BF16), w_proj_b.astype(BF16), w_out.astype(BF16),
      norm_ffn.reshape(1, D_MODEL), w_up.astype(BF16), conv_ffn, w_down.astype(BF16),
      norm_ple.reshape(1, D_MODEL), w_ple_gate.astype(BF16), w_ple.astype(BF16), norm_final.reshape(1, D_MODEL))


def _pad_hist(hist, rows=8):
    b, r, c = hist.shape
    return jnp.concatenate([jnp.zeros((b, rows - r, c), hist.dtype), hist], axis=1)


def _layer(x, p, past_k, past_v, past_kidx, s_gdn, conv_b_hist, ffn_hist, wts, *, tm, tq):
    (norm_mix, w_in, conv_b, a_log, dt_bias, norm_gdn, w_proj_a, w_proj_b, w_out, norm_ffn, w_up, conv_ffn,
     w_down, norm_ple, w_ple, w_ple_gate, rel_bias, norm_final) = wts
    b, t, _ = x.shape
    n = b * t
    past = past_k.shape[1]
    topk = min(TOPK_MAX, (past + t) // 4)
    x2d = x.reshape(n, D_MODEL)
    tmi = min(tm, n)
    (ka, va, kbf, conv_in, gb, ga, gbr, small, ki3, qT, qihT, qilT, wiT, vTa) = _in_proj(x2d, norm_mix, w_in, tmi)

    if past == 0:
        assert tq == tmi and t % tq == 0 and tq % CHUNK == 0 and tq >= topk
        nq = t // tq
        kj = np.arange(tq)[:, None]
        adm = ((kj // CHUNK) <= (np.arange(tq)[None, :] // CHUNK)).astype(np.float32)
        oT = _dsa(qT, qihT, qilT, wiT, kbf.reshape(b, nq, tq, WA), vTa.reshape(b, nq, HA * V_ROWS, tq),
                  ki3.reshape(b, nq, tq, 256), _bias_tables(rel_bias, adm, tq, tq, tq), adm,
                  nq=nq, causal=True, topk=topk)
    else:
        assert n == tmi
        oT = _dsa_step(qT, qihT, qilT, wiT, kbf, va.reshape(n, WA), small[:, :D_IDX], past_k, past_v, past_kidx, rel_bias,
                       b=b, t=t, topk=topk)
        oT = oT[:, :, :t]

    tp = -(-t // CHUNK) * CHUNK
    padt = lambda a: jnp.pad(a.reshape(b, t, a.shape[-1]), ((0, 0), (0, tp - t), (0, 0)))
    if t == tp:
        pre = _gdn_pre(padt(conv_in), _pad_hist(conv_b_hist), padt(small), conv_b, a_log, dt_bias, CHUNK)
    else:
        assert tp == CHUNK and t + CONV_B - 1 <= CHUNK
        cflat = padt(conv_in).at[:-1, CHUNK - (CONV_B - 1):].set(conv_b_hist[1:]).reshape(1, b * CHUNK, C_CONV_B)
        pre = _gdn_pre(cflat, _pad_hist(conv_b_hist[:1]), padt(small).reshape(1, b * CHUNK, 128), conv_b, a_log,
                       dt_bias, t)
        pre = [a.reshape((b, 1) + a.shape[2:]) for a in pre]
    ob, s_new = _gdn_scan(*pre, padt(gb), s_gdn, norm_gdn)
    new_conv_b = jnp.concatenate([conv_b_hist, conv_in.reshape(b, t, C_CONV_B)], axis=1)[:, t:]

    post_w = (w_proj_a, w_proj_b, w_out, norm_ffn, w_up, conv_ffn, w_down, norm_ple, w_ple_gate, w_ple, norm_final)
    if past == 0:
        per_bt = lambda a: a.reshape(b, t, a.shape[-1])
        y, tail = _post(x, oT, ob, per_bt(ga), per_bt(gbr), p, _pad_hist(ffn_hist), *post_w, tm)
        new_ffn = tail[:, 8 - (CONV_F - 1):]
    else:
        assert n == tmi and t >= CONV_F - 1
        one = lambda a: a.reshape(1, n, a.shape[-1])
        first = jnp.zeros((b, t, D_FF), F32)
        hist = jnp.stack([first.at[:, 0].set(ffn_hist[:, 1]),
                          first.at[:, 0].set(ffn_hist[:, 0]).at[:, 1].set(ffn_hist[:, 1])]).reshape(2, n, D_FF)
        y, ug = _post(one(x), jnp.swapaxes(oT, 0, 1).reshape(1, WA, n), one(ob[:, :t]), one(ga), one(gbr), one(p),
                      hist, *post_w, n, seq=t)
        y = y.reshape(b, t, D_MODEL)
        new_ffn = ug.reshape(b, t, D_FF)[:, t - (CONV_F - 1):]
    return (y, ka.reshape(b, t, HA, DH), va.reshape(b, t, HA, DH), small[:, :D_IDX].reshape(b, t, D_IDX),
            s_new, new_conv_b, new_ffn)


def kernel(x_prompt, x_sample, p_prompt, p_sample, cache_k, cache_v, cache_kidx, state_gdn, state_gdn_conv,
           state_ffn_conv, norm_mix, w_in, conv_b, a_log, dt_bias, norm_gdn, w_proj_a, w_proj_b, w_out, norm_ffn,
           w_up, conv_ffn, w_down, norm_ple, w_ple, w_ple_gate, rel_bias, norm_final):
    assert norm_mix.shape[0] == 1
    bp = x_prompt.shape[0]
    dt = x_prompt.dtype
    wts = (norm_mix[0], w_in[0], conv_b[0], a_log[0], dt_bias[0], norm_gdn[0], w_proj_a[0], w_proj_b[0], w_out[0],
           norm_ffn[0], w_up[0], conv_ffn[0], w_down[0], norm_ple[0], w_ple[0], w_ple_gate[0], rel_bias, norm_final)
    outs_p = _layer(x_prompt, p_prompt[0], jnp.zeros((bp, 0, HA, DH), dt), jnp.zeros((bp, 0, HA, DH), dt),
                    jnp.zeros((bp, 0, D_IDX), dt), jnp.zeros((bp, HB, DK, DV), dt),
                    jnp.zeros((bp, CONV_B - 1, C_CONV_B), dt), jnp.zeros((bp, CONV_F - 1, D_FF), dt),
                    wts, tm=256, tq=256)
    outs_s = _layer(x_sample, p_sample[0], cache_k[0], cache_v[0], cache_kidx[0], state_gdn[0],
                    state_gdn_conv[0], state_ffn_conv[0], wts, tm=256, tq=256)
    yp, ys = outs_p[0], outs_s[0]
    return (yp, ys) + tuple(a[None] for a in outs_p[1:]) + tuple(a[None] for a in outs_s[1:])
```
